```python
import math
import jax, jax.numpy as jnp
from jax import lax
import numpy as np

D_MODEL = 1024
BATCH = 8
SEQ = 2048
DEPTH = 1
DEC_BATCH = 128
DEC_SEQ = 1
PAST_LEN = 16384
PAGE_SIZE = 128

N_META = 16
H_A = 8
N_A = 64
W_A = H_A * N_A
D_DECAY = 32
D_AAA = 32
D_GATE = 96
N_A_IN = 3 * W_A + D_DECAY + D_AAA + D_GATE
SPLIT_A = (W_A, 2 * W_A, 3 * W_A, 3 * W_A + D_DECAY, 3 * W_A + D_DECAY + D_AAA)
LNX_EPS = 64e-5
H_B = 4
DK_B = 128
DV_B = 128
W_B = H_B * DV_B
CONV_W = 4
GDN_CHUNK = 64
N_B_IN = 4 * W_B + 2 * H_B
N_IN = N_A_IN + N_B_IN
W_MIX = W_A + W_B
D_FF = 2816
NORM_EPS = 1e-6

kernel_name = "hymba_rwkv7_gdn_macaron_step"


def rms_norm(x, gain):
    xf = x.astype(jnp.float32)
    return xf * lax.rsqrt(jnp.mean(xf * xf, axis=-1, keepdims=True) + NORM_EPS) * gain


def l2norm(x):
    return x * lax.rsqrt(jnp.sum(x * x, axis=-1, keepdims=True) + 1e-12)


def swiglu(x, w_gate, w_up, w_down):
    return (jax.nn.silu(x @ w_gate) * (x @ w_up)) @ w_down


def rwkv7_mixer(pa, shift_prev, s0, mu, w0, w_decay_up, a0, w_a_up, w_g_up, k_k, k_a, r_k, lnx_w, lnx_b):
    B, L, _ = pa.shape
    prev = jnp.concatenate([shift_prev[:, None, :].astype(jnp.float32), pa[:, :-1]], axis=1)
    xm = pa + (prev - pa) * mu
    r, k, v, wd, ad, gd = jnp.split(xm, SPLIT_A, axis=-1)
    w = -jax.nn.softplus(-(w0 + jnp.tanh(wd) @ w_decay_up)) - 0.5
    decay = jnp.exp(-jnp.exp(w))
    a = jax.nn.sigmoid(a0 + ad @ w_a_up)
    g = jax.nn.sigmoid(gd) @ w_g_up
    hd = lambda t: t.reshape(B, L, H_A, N_A)
    kk = l2norm(hd(k * k_k))
    k = k * (1.0 + (a - 1.0) * k_a)
    r_h, k_h, v_h, a_h, w_h = hd(r), hd(k), hd(v), hd(a), hd(decay)
    tm = lambda t: jnp.moveaxis(t, 1, 0)

    def step(S, inp):
        r_t, w_t, k_t, v_t, kk_t, kka_t = inp
        S = (S * w_t[:, :, None, :]
             - jnp.einsum('bhvk,bhk->bhv', S, kk_t)[..., None] * kka_t[:, :, None, :]
             + v_t[..., None] * k_t[:, :, None, :])
        return S, jnp.einsum('bhvk,bhk->bhv', S, r_t)

    s_final, o = lax.scan(step, s0.astype(jnp.float32),
                          (tm(r_h), tm(w_h), tm(k_h), tm(v_h), tm(kk), tm(kk * a_h)))
    o = jnp.moveaxis(o, 0, 1)
    mean = jnp.mean(o, axis=-1, keepdims=True)
    var = jnp.mean(jnp.square(o - mean), axis=-1, keepdims=True)
    o = ((o - mean) * lax.rsqrt(var + LNX_EPS)).reshape(B, L, W_A) * lnx_w + lnx_b
    bonus = jnp.sum(r_h * k_h * r_k, axis=-1, keepdims=True) * v_h
    out = (o + bonus.reshape(B, L, W_A)) * g
    return out, pa[:, -1], s_final


def gdn_chunked(q, k, v, logdecay, beta, s0, chunk):
    B, L, H, _ = q.shape
    n = L // chunk

    def blk(t):
        t = t.reshape((B, n, chunk) + t.shape[2:])
        return jnp.moveaxis(jnp.moveaxis(t, 1, 0), 2, 3)

    qc, kc, vc, gc, bc = blk(q), blk(k), blk(v), blk(logdecay), blk(beta)
    G = jnp.cumsum(gc, axis=-1)
    idx = jnp.arange(chunk)
    causal = idx[:, None] >= idx[None, :]
    strict = idx[:, None] > idx[None, :]
    dmat = jnp.exp(jnp.where(causal, G[..., :, None] - G[..., None, :], -jnp.inf))
    kb = kc * bc[..., None]
    lower = jnp.where(strict, jnp.einsum('nbhid,nbhjd->nbhij', kb, kc) * dmat, 0.0)
    amat = lower + jnp.eye(chunk, dtype=lower.dtype)
    solve = lambda rhs: lax.linalg.triangular_solve(amat, rhs, left_side=True, lower=True,
                                                    unit_diagonal=True)
    u = solve(vc * bc[..., None])
    wk = solve(kb * jnp.exp(G)[..., None])
    qk = jnp.einsum('nbhid,nbhjd->nbhij', qc, kc) * dmat

    def step(S, inp):
        q_i, k_i, u_i, w_i, G_i, qk_i = inp
        v_new = u_i - jnp.einsum('bhcd,bhde->bhce', w_i, S)
        o = (jnp.einsum('bhcd,bhde->bhce', q_i * jnp.exp(G_i)[..., None], S)
             + jnp.einsum('bhij,bhje->bhie', qk_i, v_new))
        g_last = G_i[..., -1:]
        S = (S * jnp.exp(g_last)[..., None]
             + jnp.einsum('bhcd,bhce->bhde', k_i * jnp.exp(g_last - G_i)[..., None], v_new))
        return S, o

    s_final, o = lax.scan(step, s0, (qc, kc, u, wk, G, qk))
    o = jnp.moveaxis(jnp.moveaxis(o, 2, 3), 0, 1).reshape(B, L, H, -1)
    return o, s_final


def gdn_mixer(pb, conv_prev, s0, segments, conv_w, a_log, dt_bias, norm_w):
    B, L, _ = pb.shape
    qkv_raw, z, a_raw, b_raw = jnp.split(pb, (3 * W_B, 4 * W_B, 4 * W_B + H_B), axis=-1)
    full = jnp.concatenate([conv_prev.astype(jnp.float32), qkv_raw], axis=1)
    conv = full[:, 0:L] * conv_w[0]
    for j in range(1, CONV_W):
        conv = conv + full[:, j:j + L] * conv_w[j]
    q, k, v = jnp.split(jax.nn.silu(conv), 3, axis=-1)
    q = l2norm(q.reshape(B, L, H_B, DK_B)) * (DK_B ** -0.5)
    k = l2norm(k.reshape(B, L, H_B, DK_B))
    v = v.reshape(B, L, H_B, DV_B)
    beta = jax.nn.sigmoid(b_raw)
    logdecay = -jnp.exp(a_log) * jax.nn.softplus(a_raw + dt_bias)
    S = s0.astype(jnp.float32)
    outs = []
    start = 0
    for length, chunk in segments:
        seg = slice(start, start + length)
        o_seg, S = gdn_chunked(q[:, seg], k[:, seg], v[:, seg], logdecay[:, seg], beta[:, seg], S, chunk)
        outs.append(o_seg)
        start += length
    o = jnp.concatenate(outs, axis=1)
    o = o * lax.rsqrt(jnp.mean(o * o, axis=-1, keepdims=True) + NORM_EPS) * norm_w
    o = o * jax.nn.silu(z.reshape(B, L, H_B, DV_B))
    return o.reshape(B, L, W_B), full[:, -(CONV_W - 1):], S


def trunk_layer(h, shift_prev, conv_prev, s_rwkv, s_gdn, segments, p):
    h = h + 0.5 * swiglu(rms_norm(h, p['g_ffn1']), p['w_gate1'], p['w_up1'], p['w_down1'])
    proj = rms_norm(h, p['g_mix']) @ p['w_in']
    pa, pb = proj[..., :N_A_IN], proj[..., N_A_IN:]
    o_a, shift_new, rwkv_new = rwkv7_mixer(pa, shift_prev, s_rwkv, p['mu_shift'], p['w0'], p['w_decay_up'],
                                           p['a0'], p['w_a_up'], p['w_g_up'], p['k_k'], p['k_a'],
                                           p['r_k'], p['lnx_w'], p['lnx_b'])
    o_b, conv_new, gdn_new = gdn_mixer(pb, conv_prev, s_gdn, segments, p['conv_w'], p['a_log'],
                                       p['dt_bias'], p['gdn_norm_w'])
    h = h + jnp.concatenate([o_a, o_b], axis=-1) @ p['w_out']
    h = h + 0.5 * swiglu(rms_norm(h, p['g_ffn2']), p['w_gate2'], p['w_up2'], p['w_down2'])
    return h, rwkv_new, shift_new, gdn_new, conv_new


def setup_inputs(seed: int = 0) -> dict:
    key = jax.random.key(seed)
    ks = iter(jax.random.split(key, 48))
    f32 = jnp.float32
    nrm = lambda shape, s: jax.random.normal(next(ks), shape, f32) * s
    uni = lambda shape, lo, hi: jax.random.uniform(next(ks), shape, f32, lo, hi)
    D = D_MODEL
    dt = jnp.exp(uni((DEPTH, H_B), math.log(1e-3), math.log(1e-1)))
    return {
        'x_prompt': nrm((BATCH, SEQ, D), 1.0),
        'x_sample': nrm((DEC_BATCH, DEC_SEQ, D), 1.0),
        'state_rwkv': nrm((DEPTH, DEC_BATCH, H_A, N_A, N_A), 0.3),
        'state_shift': nrm((DEPTH, DEC_BATCH, N_A_IN), 1.0),
        'state_gdn': nrm((DEPTH, DEC_BATCH, H_B, DK_B, DV_B), 0.3),
        'state_conv': nrm((DEPTH, DEC_BATCH, CONV_W - 1, 3 * W_B), 1.0),
        'meta_tokens': nrm((N_META, D), 1.0),
        'g_ffn1': 1.0 + nrm((DEPTH, D), 0.05),
        'w_gate1': nrm((DEPTH, D, D_FF), D ** -0.5),
        'w_up1': nrm((DEPTH, D, D_FF), D ** -0.5),
        'w_down1': nrm((DEPTH, D_FF, D), D_FF ** -0.5),
        'g_mix': 1.0 + nrm((DEPTH, D), 0.05),
        'w_in': nrm((DEPTH, D, N_IN), D ** -0.5),
        'mu_shift': uni((DEPTH, N_A_IN), 0.0, 1.0),
        'w0': uni((DEPTH, W_A), -6.0, -1.0),
        'w_decay_up': nrm((DEPTH, D_DECAY, W_A), 0.1 * D_DECAY ** -0.5),
        'a0': nrm((DEPTH, W_A), 0.1),
        'w_a_up': nrm((DEPTH, D_AAA, W_A), 0.1 * D_AAA ** -0.5),
        'w_g_up': nrm((DEPTH, D_GATE, W_A), D_GATE ** -0.5),
        'k_k': 0.85 + nrm((DEPTH, W_A), 0.05),
        'k_a': 1.0 + nrm((DEPTH, W_A), 0.05),
        'r_k': nrm((DEPTH, H_A, N_A), 0.1),
        'lnx_w': 1.0 + nrm((DEPTH, W_A), 0.05),
        'lnx_b': nrm((DEPTH, W_A), 0.01),
        'conv_w': nrm((DEPTH, CONV_W, 3 * W_B), CONV_W ** -0.5),
        'a_log': jnp.log(uni((DEPTH, H_B), 1.0, 16.0)),
        'dt_bias': dt + jnp.log(-jnp.expm1(-dt)),
        'gdn_norm_w': 1.0 + nrm((DEPTH, DV_B), 0.05),
        'w_out': nrm((DEPTH, W_MIX, D), W_MIX ** -0.5),
        'g_ffn2': 1.0 + nrm((DEPTH, D), 0.05),
        'w_gate2': nrm((DEPTH, D, D_FF), D ** -0.5),
        'w_up2': nrm((DEPTH, D, D_FF), D ** -0.5),
        'w_down2': nrm((DEPTH, D_FF, D), D_FF ** -0.5),
        'g_final': 1.0 + nrm((D,), 0.05),
    }


def reference(x_prompt, x_sample, state_rwkv, state_shift, state_gdn, state_conv, meta_tokens,
              g_ffn1, w_gate1, w_up1, w_down1, g_mix, w_in, mu_shift, w0, w_decay_up, a0, w_a_up,
              w_g_up, k_k, k_a, r_k, lnx_w, lnx_b, conv_w, a_log, dt_bias, gdn_norm_w, w_out,
              g_ffn2, w_gate2, w_up2, w_down2, g_final):
    f32 = jnp.float32
    bp, sp = x_prompt.shape[0], x_prompt.shape[1]
    ss = x_sample.shape[1]
    hp = jnp.concatenate([jnp.broadcast_to(meta_tokens.astype(f32)[None], (bp, N_META, D_MODEL)),
                          x_prompt.astype(f32)], axis=1)
    hs = x_sample.astype(f32)
    prompt_segments = ((N_META, N_META), (sp, GDN_CHUNK))
    sample_segments = ((ss, math.gcd(ss, GDN_CHUNK)),)
    zero_shift = jnp.zeros((bp, N_A_IN), f32)
    zero_conv = jnp.zeros((bp, CONV_W - 1, 3 * W_B), f32)
    zero_rwkv = jnp.zeros((bp, H_A, N_A, N_A), f32)
    zero_gdn = jnp.zeros((bp, H_B, DK_B, DV_B), f32)
    p_rwkv, p_shift, p_gdn, p_conv = [], [], [], []
    s_rwkv, s_shift, s_gdn, s_conv = [], [], [], []
    for l in range(DEPTH):
        lp = dict(g_ffn1=g_ffn1[l], w_gate1=w_gate1[l], w_up1=w_up1[l], w_down1=w_down1[l],
                  g_mix=g_mix[l], w_in=w_in[l], mu_shift=mu_shift[l], w0=w0[l],
                  w_decay_up=w_decay_up[l], a0=a0[l], w_a_up=w_a_up[l], w_g_up=w_g_up[l],
                  k_k=k_k[l], k_a=k_a[l], r_k=r_k[l], lnx_w=lnx_w[l], lnx_b=lnx_b[l],
                  conv_w=conv_w[l], a_log=a_log[l], dt_bias=dt_bias[l], gdn_norm_w=gdn_norm_w[l],
                  w_out=w_out[l], g_ffn2=g_ffn2[l], w_gate2=w_gate2[l], w_up2=w_up2[l],
                  w_down2=w_down2[l])
        hp, r1, sh1, g1, c1 = trunk_layer(hp, zero_shift, zero_conv, zero_rwkv, zero_gdn,
                                          prompt_segments, lp)
        hs, r2, sh2, g2, c2 = trunk_layer(hs, state_shift[l], state_conv[l], state_rwkv[l],
                                          state_gdn[l], sample_segments, lp)
        p_rwkv.append(r1); p_shift.append(sh1); p_gdn.append(g1); p_conv.append(c1)
        s_rwkv.append(r2); s_shift.append(sh2); s_gdn.append(g2); s_conv.append(c2)
    y_prompt = rms_norm(hp, g_final)[:, N_META:].astype(x_prompt.dtype)
    y_sample = rms_norm(hs, g_final).astype(x_sample.dtype)
    return (y_prompt, y_sample,
            jnp.stack(p_rwkv), jnp.stack(p_shift), jnp.stack(p_gdn), jnp.stack(p_conv),
            jnp.stack(s_rwkv), jnp.stack(s_shift), jnp.stack(s_gdn), jnp.stack(s_conv))
```

```python
import functools
import math

import jax
import jax.numpy as jnp
from jax import lax
from jax.experimental import pallas as pl
from jax.experimental.pallas import tpu as pltpu

F32 = jnp.float32
BF16 = jnp.bfloat16
HIGHEST = lax.Precision.HIGHEST

LANE = 128
SUBLANE = 8
VMEM_LIMIT = 56 * 1024 * 1024

D_MODEL = 1024
D_FF = 2816
N_META = 16
H_A, N_A = 8, 64
W_A = H_A * N_A
D_DECAY, D_AAA, D_GATE = 32, 32, 96
N_A_IN = 3 * W_A + D_DECAY + D_AAA + D_GATE
H_B, DK_B, DV_B = 4, 128, 128
W_B = H_B * DV_B
CONV_W = 4
N_B_IN = 4 * W_B + 2 * H_B
LNX_EPS = 64e-5
NORM_EPS = 1e-6

PA_W = 3 * W_A + 3 * LANE
PA_DECAY = 3 * W_A
PA_AAA = PA_DECAY + LANE
PA_GATE = PA_AAA + LANE
PB_W = 4 * W_B + LANE
PB_Z = 3 * W_B
PB_GATES = 4 * W_B

TM_DENSE = 256
CHUNK = 64
DEC_CHUNK = SUBLANE


def _dot(a, b, precision=None):
    return jnp.dot(a, b, preferred_element_type=F32, precision=precision)


def _dot_nt(a, b, precision=None):
    return lax.dot_general(a, b, (((1,), (1,)), ((), ())), preferred_element_type=F32,
                           precision=precision)


def _dot_tn(a, b, precision=None):
    return lax.dot_general(a, b, (((0,), (0,)), ((), ())), preferred_element_type=F32,
                           precision=precision)


def _sigmoid(x):
    return 1.0 / (1.0 + jnp.exp(-x))


def _softplus(x):
    return jnp.maximum(x, 0.0) + jnp.log(1.0 + jnp.exp(-jnp.abs(x)))


def _rms(x, g):
    return x * lax.rsqrt(jnp.mean(x * x, axis=-1, keepdims=True) + NORM_EPS) * g


def _iota2(shape, dim):
    return lax.broadcasted_iota(jnp.int32, shape, dim)


def _inv_unit_lower(low, n):
    eye = (_iota2((n, n), 0) == _iota2((n, n), 1)).astype(F32)
    p = -low
    inv = eye + p
    for _ in range(int(math.log2(n)) - 1):
        p = _dot(p, p, HIGHEST)
        inv = inv + _dot(inv, p, HIGHEST)
    return inv


def _swiglu(n, wg_ref, wu_ref, wd_ref):
    gate = _dot(n, wg_ref[...])
    up = _dot(n, wu_ref[...])
    act = (gate * _sigmoid(gate) * up).astype(BF16)
    return _dot(act, wd_ref[...])


def _dense_in_kernel(x_ref, g1_ref, wg_ref, wu_ref, wd_ref, gm_ref, wa_ref, wb_ref,
                     h_ref, pa_ref, pb_ref):
    x = x_ref[...]
    h = x + 0.5 * _swiglu(_rms(x, g1_ref[...]).astype(BF16), wg_ref, wu_ref, wd_ref)
    h_ref[...] = h
    n = _rms(h, gm_ref[...]).astype(BF16)
    pa_ref[...] = _dot(n, wa_ref[...])
    pb_ref[...] = _dot(n, wb_ref[...])


def _dense_out_kernel(h_ref, oa_ref, ob_ref, woa_ref, wob_ref, g2_ref, wg_ref, wu_ref, wd_ref,
                      gf_ref, y_ref):
    h = (h_ref[...] + _dot(oa_ref[...].astype(BF16), woa_ref[...])
         + _dot(ob_ref[...].astype(BF16), wob_ref[...]))
    h = h + 0.5 * _swiglu(_rms(h, g2_ref[...]).astype(BF16), wg_ref, wu_ref, wd_ref)
    y_ref[...] = _rms(h, gf_ref[...])


def _const_spec(shape):
    return pl.BlockSpec(shape, lambda *_: (0,) * len(shape), pipeline_mode=pl.Buffered(1))


def _row_spec(tm, width):
    return pl.BlockSpec((tm, width), lambda i: (i, 0))


def _dense_in(x, g1, wg, wu, wd, gm, wa, wb, tm):
    n = x.shape[0]
    consts = (g1, wg, wu, wd, gm, wa, wb)
    return pl.pallas_call(
        _dense_in_kernel,
        grid=(n // tm,),
        in_specs=[_row_spec(tm, D_MODEL)] + [_const_spec(c.shape) for c in consts],
        out_specs=[_row_spec(tm, D_MODEL), _row_spec(tm, PA_W), _row_spec(tm, PB_W)],
        out_shape=[jax.ShapeDtypeStruct((n, D_MODEL), F32),
                   jax.ShapeDtypeStruct((n, PA_W), F32),
                   jax.ShapeDtypeStruct((n, PB_W), F32)],
        compiler_params=pltpu.CompilerParams(dimension_semantics=("arbitrary",),
                                             vmem_limit_bytes=VMEM_LIMIT),
        name="dense_in",
    )(x, *consts)


def _dense_out(h, oa, ob, woa, wob, g2, wg, wu, wd, gf, tm):
    n = h.shape[0]
    consts = (woa, wob, g2, wg, wu, wd, gf)
    return pl.pallas_call(
        _dense_out_kernel,
        grid=(n // tm,),
        in_specs=[_row_spec(tm, D_MODEL), _row_spec(tm, W_A), _row_spec(tm, W_B)]
        + [_const_spec(c.shape) for c in consts],
        out_specs=_row_spec(tm, D_MODEL),
        out_shape=jax.ShapeDtypeStruct((n, D_MODEL), F32),
        compiler_params=pltpu.CompilerParams(dimension_semantics=("arbitrary",),
                                             vmem_limit_bytes=VMEM_LIMIT),
        name="dense_out",
    )(h, oa, ob, *consts)


def _rwkv_kernel(pa_ref, prev_ref, s0_ref, mu_ref, w0_ref, wdu_ref, a0_ref, wau_ref, wgu_ref,
                 kk_ref, ka_ref, rk_ref, lnw_ref, lnb_ref, o_ref, sout_ref, xbuf, s_scr,
                 *, chunk, nvalid):
    c = pl.program_id(1)
    C = chunk

    @pl.when(c == 0)
    def _():
        xbuf[0:SUBLANE, :] = prev_ref[0]
        s_scr[...] = s0_ref[0]

    x = pa_ref[0]
    xbuf[SUBLANE:SUBLANE + C, :] = x
    prev = xbuf[SUBLANE - 1:SUBLANE - 1 + C, :]
    xm = x + (prev - x) * mu_ref[...]
    xbuf[0:SUBLANE, :] = xbuf[C:C + SUBLANE, :]

    r = xm[:, 0:W_A]
    k = xm[:, W_A:2 * W_A]
    v = xm[:, 2 * W_A:3 * W_A]
    wd = xm[:, PA_DECAY:PA_DECAY + LANE]
    ad = xm[:, PA_AAA:PA_AAA + LANE]
    gd = xm[:, PA_GATE:PA_GATE + LANE]

    wl = w0_ref[...] + _dot(jnp.tanh(wd).astype(BF16), wdu_ref[...])
    lw = -jnp.exp(-_softplus(-wl) - 0.5)
    a = _sigmoid(a0_ref[...] + _dot(ad.astype(BF16), wau_ref[...]))
    g = _dot(_sigmoid(gd).astype(BF16), wgu_ref[...])
    kkr = k * kk_ref[...]
    k2 = k * (1.0 + (a - 1.0) * ka_ref[...])

    row = _iota2((C, C), 0)
    col = _iota2((C, C), 1)
    incl = row >= col
    strict = row > col
    if nvalid < C:
        valid = _iota2((C, 1), 0) < nvalid
        lw = jnp.where(valid, lw, 0.0)
        kkr = jnp.where(valid, kkr, 0.0)
        k2 = jnp.where(valid, k2, 0.0)

    G = _dot(incl.astype(F32), lw, HIGHEST)
    Gx = G - lw
    mid = C // 2 - 1
    Gm = G[mid:mid + 1, :]
    Gc = G[C - 1:C, :]

    lane = _iota2((1, LANE), 1)
    head_masks = ((lane < N_A).astype(F32), (lane >= N_A).astype(F32))
    seg = ((_iota2((LANE, LANE), 0) // N_A) == (_iota2((LANE, LANE), 1) // N_A)).astype(F32)

    for p in range(H_A // 2):
        sl = slice(p * LANE, (p + 1) * LANE)
        r_p, v_p, k2_p, kkr_p, a_p = r[:, sl], v[:, sl], k2[:, sl], kkr[:, sl], a[:, sl]
        G_p, Gx_p, Gm_p, Gc_p = G[:, sl], Gx[:, sl], Gm[:, sl], Gc[:, sl]
        S = s_scr[p]

        kk_p = kkr_p * lax.rsqrt(_dot(kkr_p * kkr_p, seg, HIGHEST) + 1e-12)
        kka_p = kk_p * a_p
        kd_rel = kk_p * jnp.exp(Gx_p - Gm_p)
        rd_rel = r_p * jnp.exp(G_p - Gm_p)
        inv_rel = jnp.exp(Gm_p - G_p)
        ai = kka_p * inv_rel
        ki = k2_p * inv_rel
        kd_abs = kk_p * jnp.exp(Gx_p)
        rd_abs = r_p * jnp.exp(G_p)
        dec = jnp.exp(Gc_p - G_p)
        adec = kka_p * dec
        kdec = k2_p * dec

        x0 = _dot_nt(kd_abs, S, HIGHEST)
        U = jnp.zeros((C, LANE), F32)
        inter = jnp.zeros((C, LANE), F32)
        mats = []
        for mh in head_masks:
            lhs = jnp.concatenate([kd_rel * mh, rd_rel * mh], axis=0)
            pa_ = _dot_nt(lhs, ai, HIGHEST)
            pk_ = _dot_nt(lhs, ki, HIGHEST)
            a_aa = jnp.where(strict, pa_[:C], 0.0)
            a_ak = jnp.where(strict, pk_[:C], 0.0)
            a_ra = jnp.where(incl, pa_[C:], 0.0)
            a_rk = jnp.where(incl, pk_[C:], 0.0)
            ainv = _inv_unit_lower(a_aa, C)
            z = x0 + _dot(a_ak, v_p, HIGHEST)
            U = U - mh * _dot(ainv, z, HIGHEST)
            mats.append((a_ra, a_rk))
        for mh, (a_ra, a_rk) in zip(head_masks, mats):
            inter = inter + mh * (_dot(a_ra, U, HIGHEST) + _dot(a_rk, v_p, HIGHEST))
        O = _dot_nt(rd_abs, S, HIGHEST) + inter
        s_scr[p] = S * jnp.exp(Gc_p) + seg * (_dot_tn(U, adec, HIGHEST) + _dot_tn(v_p, kdec, HIGHEST))

        mean = _dot(O, seg, HIGHEST) * (1.0 / N_A)
        dlt = O - mean
        var = _dot(dlt * dlt, seg, HIGHEST) * (1.0 / N_A)
        on = dlt * lax.rsqrt(var + LNX_EPS) * lnw_ref[:, sl] + lnb_ref[:, sl]
        bonus = _dot(r_p * k2_p * rk_ref[:, sl], seg, HIGHEST) * v_p
        o_ref[0, :, sl] = (on + bonus) * g[:, sl]

    @pl.when(c == pl.num_programs(1) - 1)
    def _():
        sout_ref[0] = s_scr[...]


def _rwkv(pa, prev8, s0, consts, chunk, nvalid):
    B, L, _ = pa.shape
    shared = prev8.shape[0] == 1 and B > 1
    bidx = (lambda b, c: (0, 0, 0)) if shared else (lambda b, c: (b, 0, 0))
    sidx = (lambda b, c: (0, 0, 0, 0)) if shared else (lambda b, c: (b, 0, 0, 0))
    return pl.pallas_call(
        functools.partial(_rwkv_kernel, chunk=chunk, nvalid=nvalid),
        grid=(B, L // chunk),
        in_specs=[pl.BlockSpec((1, chunk, PA_W), lambda b, c: (b, c, 0)),
                  pl.BlockSpec((1, SUBLANE, PA_W), bidx),
                  pl.BlockSpec((1, H_A // 2, LANE, LANE), sidx)]
        + [pl.BlockSpec(t.shape, lambda b, c: (0, 0)) for t in consts],
        out_specs=[pl.BlockSpec((1, chunk, W_A), lambda b, c: (b, c, 0)),
                   pl.BlockSpec((1, H_A // 2, LANE, LANE), lambda b, c: (b, 0, 0, 0))],
        out_shape=[jax.ShapeDtypeStruct((B, L, W_A), F32),
                   jax.ShapeDtypeStruct((B, H_A // 2, LANE, LANE), F32)],
        scratch_shapes=[pltpu.VMEM((SUBLANE + chunk, PA_W), F32),
                        pltpu.VMEM((H_A // 2, LANE, LANE), F32)],
        compiler_params=pltpu.CompilerParams(dimension_semantics=("arbitrary", "arbitrary"),
                                             vmem_limit_bytes=VMEM_LIMIT),
        name="rwkv_chunk%d" % chunk,
    )(pa, prev8, s0, *consts)


def _gdn_kernel(pb_ref, hist_ref, s0_ref, cw_ref, alog_ref, dtb_ref, nw_ref, o_ref, sout_ref,
                xbuf, s_scr, *, chunk, nvalid):
    c = pl.program_id(1)
    C = chunk

    @pl.when(c == 0)
    def _():
        xbuf[0:SUBLANE, :] = hist_ref[0]
        s_scr[...] = s0_ref[0]

    x = pb_ref[0]
    xbuf[SUBLANE:SUBLANE + C, :] = x[:, 0:3 * W_B]
    conv = xbuf[SUBLANE - 3:SUBLANE - 3 + C, :] * cw_ref[0:1, :]
    for j in range(1, CONV_W):
        off = SUBLANE - 3 + j
        conv = conv + xbuf[off:off + C, :] * cw_ref[j:j + 1, :]
    xbuf[0:SUBLANE, :] = xbuf[C:C + SUBLANE, :]
    cs = conv * _sigmoid(conv)

    z = x[:, PB_Z:PB_Z + W_B]
    gates = x[:, PB_GATES:PB_GATES + LANE]
    glog = -jnp.exp(alog_ref[...]) * _softplus(gates + dtb_ref[...])
    beta = _sigmoid(gates)
    if nvalid < C:
        valid = _iota2((C, 1), 0) < nvalid
        glog = jnp.where(valid, glog, 0.0)
        beta = jnp.where(valid, beta, 0.0)

    row = _iota2((C, C), 0)
    col = _iota2((C, C), 1)
    incl = row >= col
    strict = row > col
    G = _dot(incl.astype(F32), glog, HIGHEST)
    sel = (_iota2((SUBLANE, LANE), 0) == _iota2((SUBLANE, LANE), 1)).astype(F32)
    Gt = _dot_nt(sel, G, HIGHEST)

    for h in range(H_B):
        sl = slice(h * LANE, (h + 1) * LANE)
        q = cs[:, sl]
        k = cs[:, W_B + h * LANE:W_B + (h + 1) * LANE]
        v = cs[:, 2 * W_B + h * LANE:2 * W_B + (h + 1) * LANE]
        q = q * lax.rsqrt(jnp.sum(q * q, axis=-1, keepdims=True) + 1e-12) * (DK_B ** -0.5)
        k = k * lax.rsqrt(jnp.sum(k * k, axis=-1, keepdims=True) + 1e-12)
        gcol = G[:, h:h + 1]
        grow = Gt[h:h + 1, :]
        bcol = beta[:, H_B + h:H_B + h + 1]
        S = s_scr[h]

        dmat = jnp.where(incl, jnp.exp(jnp.where(incl, gcol - grow, 0.0)), 0.0)
        kb = k * bcol
        low = jnp.where(strict, _dot_nt(kb, k, HIGHEST) * dmat, 0.0)
        ainv = _inv_unit_lower(low, C)
        eg = jnp.exp(gcol)
        wm = _dot(ainv, kb * eg, HIGHEST)
        um = _dot(ainv, v * bcol, HIGHEST)
        qk = _dot_nt(q, k, HIGHEST) * dmat
        v_new = um - _dot(wm, S, HIGHEST)
        o = _dot(q * eg, S, HIGHEST) + _dot(qk, v_new, HIGHEST)
        glast = gcol[C - 1:C, :]
        s_scr[h] = S * jnp.exp(glast) + _dot_tn(k * jnp.exp(glast - gcol), v_new, HIGHEST)

        o = o * lax.rsqrt(jnp.mean(o * o, axis=-1, keepdims=True) + NORM_EPS) * nw_ref[...]
        zh = z[:, sl]
        o_ref[0, :, sl] = o * (zh * _sigmoid(zh))

    @pl.when(c == pl.num_programs(1) - 1)
    def _():
        sout_ref[0] = s_scr[...]


def _gdn(pb, hist8, s0, consts, chunk, nvalid):
    B, L, _ = pb.shape
    shared = hist8.shape[0] == 1 and B > 1
    bidx = (lambda b, c: (0, 0, 0)) if shared else (lambda b, c: (b, 0, 0))
    sidx = (lambda b, c: (0, 0, 0, 0)) if shared else (lambda b, c: (b, 0, 0, 0))
    return pl.pallas_call(
        functools.partial(_gdn_kernel, chunk=chunk, nvalid=nvalid),
        grid=(B, L // chunk),
        in_specs=[pl.BlockSpec((1, chunk, PB_W), lambda b, c: (b, c, 0)),
                  pl.BlockSpec((1, SUBLANE, 3 * W_B), bidx),
                  pl.BlockSpec((1, H_B, DK_B, DV_B), sidx)]
        + [pl.BlockSpec(t.shape, lambda b, c: (0, 0)) for t in consts],
        out_specs=[pl.BlockSpec((1, chunk, W_B), lambda b, c: (b, c, 0)),
                   pl.BlockSpec((1, H_B, DK_B, DV_B), lambda b, c: (b, 0, 0, 0))],
        out_shape=[jax.ShapeDtypeStruct((B, L, W_B), F32),
                   jax.ShapeDtypeStruct((B, H_B, DK_B, DV_B), F32)],
        scratch_shapes=[pltpu.VMEM((SUBLANE + chunk, 3 * W_B), F32),
                        pltpu.VMEM((H_B, DK_B, DV_B), F32)],
        compiler_params=pltpu.CompilerParams(dimension_semantics=("arbitrary", "arbitrary"),
                                             vmem_limit_bytes=VMEM_LIMIT),
        name="gdn_chunk%d" % chunk,
    )(pb, hist8, s0, *consts)


def _pad_cols(t, width):
    return jnp.pad(t, [(0, 0)] * (t.ndim - 1) + [(0, width - t.shape[-1])])


def _pa_layout(t):
    main = t[..., :3 * W_A]
    dec = t[..., 3 * W_A:3 * W_A + D_DECAY]
    aaa = t[..., 3 * W_A + D_DECAY:3 * W_A + D_DECAY + D_AAA]
    gate = t[..., 3 * W_A + D_DECAY + D_AAA:]
    return jnp.concatenate([main, _pad_cols(dec, LANE), _pad_cols(aaa, LANE), _pad_cols(gate, LANE)],
                           axis=-1)


def _pa_unlayout(t):
    return jnp.concatenate([t[..., :3 * W_A], t[..., PA_DECAY:PA_DECAY + D_DECAY],
                            t[..., PA_AAA:PA_AAA + D_AAA], t[..., PA_GATE:PA_GATE + D_GATE]], axis=-1)


def _pad_rows(t, rows):
    return jnp.pad(t, [(0, rows - t.shape[0]), (0, 0)])


def _pair_blockdiag(s):
    B = s.shape[0]
    s = s.reshape(B, H_A // 2, 2, N_A, N_A)
    zero = jnp.zeros_like(s[:, :, 0])
    top = jnp.concatenate([s[:, :, 0], zero], axis=-1)
    bot = jnp.concatenate([zero, s[:, :, 1]], axis=-1)
    return jnp.concatenate([top, bot], axis=-2)


def _pair_unblock(s):
    B = s.shape[0]
    return jnp.stack([s[:, :, :N_A, :N_A], s[:, :, N_A:, N_A:]], axis=2).reshape(B, H_A, N_A, N_A)


def _history_rows(rows):
    B, n, w = rows.shape
    return jnp.concatenate([jnp.zeros((B, SUBLANE - n, w), F32), rows], axis=1)


def kernel(x_prompt, x_sample, state_rwkv, state_shift, state_gdn, state_conv, meta_tokens,
           g_ffn1, w_gate1, w_up1, w_down1, g_mix, w_in, mu_shift, w0, w_decay_up, a0, w_a_up,
           w_g_up, k_k, k_a, r_k, lnx_w, lnx_b, conv_w, a_log, dt_bias, gdn_norm_w, w_out,
           g_ffn2, w_gate2, w_up2, w_down2, g_final):
    assert g_ffn1.shape[0] == 1, "single trunk layer"
    bp, sp, _ = x_prompt.shape
    bs = x_sample.shape[0]
    assert x_sample.shape[1] == 1 and sp % CHUNK == 0 and (bp * sp) % TM_DENSE == 0
    row = lambda t: t.reshape(1, -1).astype(F32)

    ffn1 = (w_gate1[0].astype(BF16), w_up1[0].astype(BF16), w_down1[0].astype(BF16))
    ffn2 = (w_gate2[0].astype(BF16), w_up2[0].astype(BF16), w_down2[0].astype(BF16))
    win_a = _pa_layout(w_in[0][:, :N_A_IN]).astype(BF16)
    win_b = _pad_cols(w_in[0][:, N_A_IN:], PB_W).astype(BF16)
    wo_a = w_out[0][:W_A].astype(BF16)
    wo_b = w_out[0][W_A:].astype(BF16)
    dense_in_consts = (row(g_ffn1[0]), *ffn1, row(g_mix[0]), win_a, win_b)
    dense_out_consts = (wo_a, wo_b, row(g_ffn2[0]), *ffn2, row(g_final))
    rwkv_consts = (_pa_layout(row(mu_shift[0])), row(w0[0]),
                   _pad_rows(w_decay_up[0], LANE).astype(BF16), row(a0[0]),
                   _pad_rows(w_a_up[0], LANE).astype(BF16), _pad_rows(w_g_up[0], LANE).astype(BF16),
                   row(k_k[0]), row(k_a[0]), row(r_k[0]), row(lnx_w[0]), row(lnx_b[0]))
    gdn_consts = (conv_w[0].astype(F32), _pad_cols(row(a_log[0]), LANE), _pad_cols(row(dt_bias[0]), LANE),
                  row(gdn_norm_w[0]))

    xs = jnp.concatenate([x_sample[:, 0, :].astype(F32), meta_tokens.astype(F32)], axis=0)
    hs, pas, pbs = _dense_in(xs, *dense_in_consts, tm=xs.shape[0])

    pa_meta = pas[bs:][None]
    pb_meta = pbs[bs:][None]
    _, rw_meta = _rwkv(pa_meta, jnp.zeros((1, SUBLANE, PA_W), F32),
                       jnp.zeros((1, H_A // 2, LANE, LANE), F32), rwkv_consts, N_META, N_META)
    _, gd_meta = _gdn(pb_meta, jnp.zeros((1, SUBLANE, 3 * W_B), F32),
                      jnp.zeros((1, H_B, DK_B, DV_B), F32), gdn_consts, N_META, N_META)

    hp, pap, pbp = _dense_in(x_prompt.reshape(bp * sp, D_MODEL).astype(F32), *dense_in_consts,
                             tm=TM_DENSE)
    pap3 = pap.reshape(bp, sp, PA_W)
    pbp3 = pbp.reshape(bp, sp, PB_W)
    oa_p, rw_p = _rwkv(pap3, _history_rows(pa_meta[:, -1:, :]), rw_meta, rwkv_consts, CHUNK, CHUNK)
    ob_p, gd_p = _gdn(pbp3, _history_rows(pb_meta[:, -(CONV_W - 1):, :3 * W_B]), gd_meta, gdn_consts,
                      CHUNK, CHUNK)
    y_p = _dense_out(hp, oa_p.reshape(bp * sp, W_A), ob_p.reshape(bp * sp, W_B), *dense_out_consts,
                     tm=TM_DENSE)

    pa_s = jnp.pad(pas[:bs, None, :], ((0, 0), (0, DEC_CHUNK - 1), (0, 0)))
    pb_s = jnp.pad(pbs[:bs, None, :], ((0, 0), (0, DEC_CHUNK - 1), (0, 0)))
    oa_s, rw_s = _rwkv(pa_s, _history_rows(_pa_layout(state_shift[0].astype(F32))[:, None, :]),
                       _pair_blockdiag(state_rwkv[0].astype(F32)), rwkv_consts, DEC_CHUNK, 1)
    ob_s, gd_s = _gdn(pb_s, _history_rows(state_conv[0].astype(F32)), state_gdn[0].astype(F32),
                      gdn_consts, DEC_CHUNK, 1)
    y_s = _dense_out(hs[:bs], oa_s[:, 0, :], ob_s[:, 0, :], *dense_out_consts, tm=bs)

    new_conv_s = jnp.concatenate([state_conv[0].astype(F32)[:, 1:, :], pbs[:bs, None, :3 * W_B]], axis=1)
    return (y_p.reshape(bp, sp, D_MODEL).astype(x_prompt.dtype),
            y_s.reshape(bs, 1, D_MODEL).astype(x_sample.dtype),
            _pair_unblock(rw_p)[None],
            _pa_unlayout(pap3[:, -1, :])[None],
            gd_p[None],
            pbp3[:, -(CONV_W - 1):, :3 * W_B][None],
            _pair_unblock(rw_s)[None],
            _pa_unlayout(pas[:bs])[None],
            gd_s[None],
            new_conv_s[None])
```

```python
import functools
import math

import jax
import jax.numpy as jnp
from jax import lax
from jax.experimental import pallas as pl
from jax.experimental.pallas import tpu as pltpu

F32 = jnp.float32
BF16 = jnp.bfloat16

LANE = 128
SUBLANE = 8
VMEM_LIMIT = 56 * 1024 * 1024

D_MODEL = 1024
D_FF = 2816
N_META = 16
H_A, N_A = 8, 64
W_A = H_A * N_A
D_DECAY, D_AAA, D_GATE = 32, 32, 96
N_A_IN = 3 * W_A + D_DECAY + D_AAA + D_GATE
H_B, DK_B, DV_B = 4, 128, 128
W_B = H_B * DV_B
CONV_W = 4
N_B_IN = 4 * W_B + 2 * H_B
LNX_EPS = 64e-5
NORM_EPS = 1e-6

PA_W = 3 * W_A + 3 * LANE
PA_DECAY = 3 * W_A
PA_AAA = PA_DECAY + LANE
PA_GATE = PA_AAA + LANE
PB_W = 4 * W_B + LANE
PB_Z = 3 * W_B
PB_GATES = 4 * W_B

TM_DENSE = 256
CHUNK_RWKV = 64
CHUNK_GDN = 128
DEC_CHUNK = SUBLANE

NN = (((1,), (0,)), ((), ()))
NT = (((1,), (1,)), ((), ()))
TN = (((0,), (0,)), ((), ()))


def _dot(a, b):
    return jnp.dot(a, b, preferred_element_type=F32)


def _sigmoid(x):
    return 1.0 / (1.0 + jnp.exp(-x))


def _softplus(x):
    return jnp.maximum(x, 0.0) + jnp.log(1.0 + jnp.exp(-jnp.abs(x)))


def _rms(x, g):
    return x * lax.rsqrt(jnp.mean(x * x, axis=-1, keepdims=True) + NORM_EPS) * g


def _iota2(shape, dim):
    return lax.broadcasted_iota(jnp.int32, shape, dim)


def _split(x):
    hi = x.astype(BF16)
    return hi, (x - hi.astype(F32)).astype(BF16)


def _dg(a, b, dims):
    return lax.dot_general(a, b, dims, preferred_element_type=F32)


def _mm(a, b, dims=NN, mode="b"):
    if mode == "b":
        return _dg(a.astype(BF16), b.astype(BF16), dims)
    if mode == "xl":
        ah, al = _split(a)
        bh = b.astype(BF16)
        return _dg(ah, bh, dims) + _dg(al, bh, dims)
    if mode == "xr":
        ah = a.astype(BF16)
        bh, bl = _split(b)
        return _dg(ah, bh, dims) + _dg(ah, bl, dims)
    assert mode == "x3"
    ah, al = _split(a)
    bh, bl = _split(b)
    return _dg(ah, bh, dims) + (_dg(ah, bl, dims) + _dg(al, bh, dims))


INV_BASE = SUBLANE


def _inv_unit_lower(lows, nil, base_mode, merge_mode):
    n = lows[0].shape[0]
    row = _iota2((n, n), 0)
    col = _iota2((n, n), 1)
    same_block = lambda s: (row // s) == (col // s)
    eye = (row == col).astype(F32)
    b = min(INV_BASE, nil)
    in_base = same_block(b)
    ps = [-jnp.where(in_base, low, 0.0) for low in lows]
    invs = [eye + p for p in ps]
    for _ in range(int(math.log2(b)) - 1):
        ps = [_mm(p, p, NN, base_mode) for p in ps]
        invs = [inv + _mm(inv, p, NN, base_mode) for inv, p in zip(invs, ps)]
    s = b
    while s < nil:
        newly = jnp.logical_and(same_block(2 * s), jnp.logical_not(same_block(s)))
        ts = [_mm(inv, jnp.where(newly, low, 0.0), NN, merge_mode) for inv, low in zip(invs, lows)]
        invs = [inv - _mm(t, inv, NN, merge_mode) for inv, t in zip(invs, ts)]
        s *= 2
    return invs


def _swiglu(n, wg_ref, wu_ref, wd_ref):
    gate = _dot(n, wg_ref[...])
    up = _dot(n, wu_ref[...])
    act = (gate * _sigmoid(gate) * up).astype(BF16)
    return _dot(act, wd_ref[...])


def _dense_in_kernel(x_ref, g1_ref, wg_ref, wu_ref, wd_ref, gm_ref, wa_ref, wb_ref,
                     h_ref, pa_ref, pb_ref):
    x = x_ref[...]
    h = x + 0.5 * _swiglu(_rms(x, g1_ref[...]).astype(BF16), wg_ref, wu_ref, wd_ref)
    h_ref[...] = h
    n = _rms(h, gm_ref[...]).astype(BF16)
    pa_ref[...] = _dot(n, wa_ref[...])
    pb_ref[...] = _dot(n, wb_ref[...])


def _dense_out_kernel(h_ref, oa_ref, ob_ref, woa_ref, wob_ref, g2_ref, wg_ref, wu_ref, wd_ref,
                      gf_ref, y_ref):
    h = (h_ref[...] + _dot(oa_ref[...].astype(BF16), woa_ref[...])
         + _dot(ob_ref[...].astype(BF16), wob_ref[...]))
    h = h + 0.5 * _swiglu(_rms(h, g2_ref[...]).astype(BF16), wg_ref, wu_ref, wd_ref)
    y_ref[...] = _rms(h, gf_ref[...])


def _const_spec(shape):
    return pl.BlockSpec(shape, lambda *_: (0,) * len(shape), pipeline_mode=pl.Buffered(1))


def _row_spec(tm, width):
    return pl.BlockSpec((tm, width), lambda i: (i, 0))


def _dense_in(x, g1, wg, wu, wd, gm, wa, wb, tm):
    n = x.shape[0]
    consts = (g1, wg, wu, wd, gm, wa, wb)
    return pl.pallas_call(
        _dense_in_kernel,
        grid=(n // tm,),
        in_specs=[_row_spec(tm, D_MODEL)] + [_const_spec(c.shape) for c in consts],
        out_specs=[_row_spec(tm, D_MODEL), _row_spec(tm, PA_W), _row_spec(tm, PB_W)],
        out_shape=[jax.ShapeDtypeStruct((n, D_MODEL), F32),
                   jax.ShapeDtypeStruct((n, PA_W), F32),
                   jax.ShapeDtypeStruct((n, PB_W), F32)],
        compiler_params=pltpu.CompilerParams(dimension_semantics=("arbitrary",),
                                             vmem_limit_bytes=VMEM_LIMIT),
        name="dense_in",
    )(x, *consts)


def _dense_out(h, oa, ob, woa, wob, g2, wg, wu, wd, gf, tm):
    n = h.shape[0]
    consts = (woa, wob, g2, wg, wu, wd, gf)
    return pl.pallas_call(
        _dense_out_kernel,
        grid=(n // tm,),
        in_specs=[_row_spec(tm, D_MODEL), _row_spec(tm, W_A), _row_spec(tm, W_B)]
        + [_const_spec(c.shape) for c in consts],
        out_specs=_row_spec(tm, D_MODEL),
        out_shape=jax.ShapeDtypeStruct((n, D_MODEL), F32),
        compiler_params=pltpu.CompilerParams(dimension_semantics=("arbitrary",),
                                             vmem_limit_bytes=VMEM_LIMIT),
        name="dense_out",
    )(h, oa, ob, *consts)


RWKV_PREC = dict(cum="xr", seg="xl", pair="b", invb="x3", invm="b", sread="x3", akv="b", solve="x3", inter="b",
                 state="x3")


def _rwkv_kernel(pa_ref, prev_ref, s0_ref, mu_ref, w0_ref, wdu_ref, a0_ref, wau_ref, wgu_ref,
                 kk_ref, ka_ref, rk_ref, lnw_ref, lnb_ref, o_ref, sout_ref, xbuf, s_scr,
                 *, chunk, nvalid):
    c = pl.program_id(1)
    C = chunk
    P = RWKV_PREC

    @pl.when(c == 0)
    def _():
        xbuf[0:SUBLANE, :] = prev_ref[0]
        s_scr[...] = s0_ref[0]

    x = pa_ref[0]
    xbuf[SUBLANE:SUBLANE + C, :] = x
    prev = xbuf[SUBLANE - 1:SUBLANE - 1 + C, :]
    xm = x + (prev - x) * mu_ref[...]
    xbuf[0:SUBLANE, :] = xbuf[C:C + SUBLANE, :]

    r = xm[:, 0:W_A]
    k = xm[:, W_A:2 * W_A]
    v = xm[:, 2 * W_A:3 * W_A]
    wd = xm[:, PA_DECAY:PA_DECAY + LANE]
    ad = xm[:, PA_AAA:PA_AAA + LANE]
    gd = xm[:, PA_GATE:PA_GATE + LANE]

    wl = w0_ref[...] + _dot(jnp.tanh(wd).astype(BF16), wdu_ref[...])
    lw = -jnp.exp(-_softplus(-wl) - 0.5)
    a = _sigmoid(a0_ref[...] + _dot(ad.astype(BF16), wau_ref[...]))
    g = _dot(_sigmoid(gd).astype(BF16), wgu_ref[...])
    kkr = k * kk_ref[...]
    k2 = k * (1.0 + (a - 1.0) * ka_ref[...])
    if nvalid < C:
        valid = _iota2((C, 1), 0) < nvalid
        lw = jnp.where(valid, lw, 0.0)
        kkr = jnp.where(valid, kkr, 0.0)
        k2 = jnp.where(valid, k2, 0.0)

    tri = (_iota2((C, C), 0) >= _iota2((C, C), 1)).astype(F32)
    G = _mm(tri, lw, NN, P["cum"])
    Gx = G - lw
    mid = C // 2 - 1
    Gm = G[mid:mid + 1, :]
    Gc = G[C - 1:C, :]

    C2 = 2 * C
    trow = _iota2((C2, C2), 0) % C
    tcol = _iota2((C2, C2), 1) % C
    incl = trow >= tcol
    strict = trow > tcol
    lane = _iota2((1, LANE), 1)
    m0 = (lane < N_A).astype(F32)
    m1 = (lane >= N_A).astype(F32)
    seg = ((_iota2((LANE, LANE), 0) // N_A) == (_iota2((LANE, LANE), 1) // N_A)).astype(F32)
    stack = lambda t: jnp.concatenate([t * m0, t * m1], axis=0)
    merged = C2 % LANE == 0

    pairs = range(H_A // 2)
    sls = [slice(p * LANE, (p + 1) * LANE) for p in pairs]
    rs, vs_, k2s = [r[:, s] for s in sls], [v[:, s] for s in sls], [k2[:, s] for s in sls]
    kkrs = [kkr[:, s] for s in sls]
    Gs, Gxs = [G[:, s] for s in sls], [Gx[:, s] for s in sls]
    Gms, Gcs = [Gm[:, s] for s in sls], [Gc[:, s] for s in sls]
    Ss = [s_scr[p] for p in pairs]

    ssq = [_mm(t * t, seg, NN, P["seg"]) for t in kkrs]
    kks = [t * lax.rsqrt(q + 1e-12) for t, q in zip(kkrs, ssq)]
    kkas = [kks[p] * a[:, sls[p]] for p in pairs]
    inv_rel = [jnp.exp(Gms[p] - Gs[p]) for p in pairs]
    dec = [jnp.exp(Gcs[p] - Gs[p]) for p in pairs]
    lhs = [jnp.concatenate([stack(kks[p] * jnp.exp(Gxs[p] - Gms[p])),
                            stack(rs[p] * jnp.exp(Gs[p] - Gms[p]))], axis=0) for p in pairs]
    ais = [stack(kkas[p] * inv_rel[p]) for p in pairs]
    kis = [stack(k2s[p] * inv_rel[p]) for p in pairs]
    if merged:
        pm = [_mm(lhs[p], jnp.concatenate([ais[p], kis[p]], axis=0), NT, P["pair"]) for p in pairs]
        pas_, pks_ = [t[:, :C2] for t in pm], [t[:, C2:] for t in pm]
    else:
        pas_ = [_mm(lhs[p], ais[p], NT, P["pair"]) for p in pairs]
        pks_ = [_mm(lhs[p], kis[p], NT, P["pair"]) for p in pairs]
    a_aa = [jnp.where(strict, t[:C2], 0.0) for t in pas_]
    a_ak = [jnp.where(strict, t[:C2], 0.0) for t in pks_]
    a_ra = [jnp.where(incl, t[C2:], 0.0) for t in pas_]
    a_rk = [jnp.where(incl, t[C2:], 0.0) for t in pks_]
    ainv = _inv_unit_lower(a_aa, C, P["invb"], P["invm"])

    vst = [stack(t) for t in vs_]
    akv = [_mm(a_ak[p], vst[p], NN, P["akv"]) for p in pairs]
    sx = [_mm(jnp.concatenate([kks[p] * jnp.exp(Gxs[p]), rs[p] * jnp.exp(Gs[p])], axis=0), Ss[p], NT,
              P["sread"]) for p in pairs]
    us = [-_mm(ainv[p], stack(sx[p][:C]) + akv[p], NN, P["solve"]) for p in pairs]
    uv = [jnp.concatenate([us[p], vst[p]], axis=0) for p in pairs]
    s_new = [Ss[p] * jnp.exp(Gcs[p])
             + _mm(uv[p], jnp.concatenate([stack(kkas[p] * dec[p]), stack(k2s[p] * dec[p])], axis=0),
                   TN, P["state"]) for p in pairs]
    for p in pairs:
        s_scr[p] = s_new[p]
    if merged:
        inter = [_mm(jnp.concatenate([a_ra[p], a_rk[p]], axis=1), uv[p], NN, P["inter"]) for p in pairs]
    else:
        inter = [_mm(a_ra[p], us[p], NN, P["inter"]) + _mm(a_rk[p], vst[p], NN, P["inter"])
                 for p in pairs]
    Os = [sx[p][C:] + inter[p][:C] + inter[p][C:] for p in pairs]
    means = [_mm(t, seg, NN, P["seg"]) * (1.0 / N_A) for t in Os]
    dlts = [t - m for t, m in zip(Os, means)]
    vars_ = [_mm(t * t, seg, NN, P["seg"]) * (1.0 / N_A) for t in dlts]
    bonus = [_mm(rs[p] * k2s[p] * rk_ref[:, sls[p]], seg, NN, P["seg"]) * vs_[p] for p in pairs]
    for p in pairs:
        on = dlts[p] * lax.rsqrt(vars_[p] + LNX_EPS) * lnw_ref[:, sls[p]] + lnb_ref[:, sls[p]]
        o_ref[0, :, sls[p]] = (on + bonus[p]) * g[:, sls[p]]


    @pl.when(c == pl.num_programs(1) - 1)
    def _():
        sout_ref[0] = s_scr[...]


def _rwkv(pa, prev8, s0, consts, chunk, nvalid):
    B, L, _ = pa.shape
    shared = prev8.shape[0] == 1 and B > 1
    bidx = (lambda b, c: (0, 0, 0)) if shared else (lambda b, c: (b, 0, 0))
    sidx = (lambda b, c: (0, 0, 0, 0)) if shared else (lambda b, c: (b, 0, 0, 0))
    return pl.pallas_call(
        functools.partial(_rwkv_kernel, chunk=chunk, nvalid=nvalid),
        grid=(B, L // chunk),
        in_specs=[pl.BlockSpec((1, chunk, PA_W), lambda b, c: (b, c, 0)),
                  pl.BlockSpec((1, SUBLANE, PA_W), bidx),
                  pl.BlockSpec((1, H_A // 2, LANE, LANE), sidx)]
        + [pl.BlockSpec(t.shape, lambda b, c: (0, 0)) for t in consts],
        out_specs=[pl.BlockSpec((1, chunk, W_A), lambda b, c: (b, c, 0)),
                   pl.BlockSpec((1, H_A // 2, LANE, LANE), lambda b, c: (b, 0, 0, 0))],
        out_shape=[jax.ShapeDtypeStruct((B, L, W_A), F32),
                   jax.ShapeDtypeStruct((B, H_A // 2, LANE, LANE), F32)],
        scratch_shapes=[pltpu.VMEM((SUBLANE + chunk, PA_W), F32),
                        pltpu.VMEM((H_A // 2, LANE, LANE), F32)],
        compiler_params=pltpu.CompilerParams(dimension_semantics=("arbitrary", "arbitrary"),
                                             vmem_limit_bytes=VMEM_LIMIT),
        name="rwkv_chunk%d" % chunk,
    )(pa, prev8, s0, *consts)


GDN_PREC = dict(cum="xr", tr="xr", kk="b", invb="x3", invm="b", wu="x3", qk="b", sread="x3", qkv="b", state="x3")


def _gdn_kernel(pb_ref, hist_ref, s0_ref, cw_ref, alog_ref, dtb_ref, nw_ref, o_ref, sout_ref,
                xbuf, s_scr, *, chunk, nvalid):
    c = pl.program_id(1)
    C = chunk
    P = GDN_PREC

    @pl.when(c == 0)
    def _():
        xbuf[0:SUBLANE, :] = hist_ref[0]
        s_scr[...] = s0_ref[0]

    x = pb_ref[0]
    xbuf[SUBLANE:SUBLANE + C, :] = x[:, 0:3 * W_B]
    conv = xbuf[SUBLANE - 3:SUBLANE - 3 + C, :] * cw_ref[0:1, :]
    for j in range(1, CONV_W):
        off = SUBLANE - 3 + j
        conv = conv + xbuf[off:off + C, :] * cw_ref[j:j + 1, :]
    xbuf[0:SUBLANE, :] = xbuf[C:C + SUBLANE, :]
    cs = conv * _sigmoid(conv)

    z = x[:, PB_Z:PB_Z + W_B]
    gates = x[:, PB_GATES:PB_GATES + LANE]
    glog = -jnp.exp(alog_ref[...]) * _softplus(gates + dtb_ref[...])
    beta = _sigmoid(gates)
    if nvalid < C:
        valid = _iota2((C, 1), 0) < nvalid
        glog = jnp.where(valid, glog, 0.0)
        beta = jnp.where(valid, beta, 0.0)

    row = _iota2((C, C), 0)
    col = _iota2((C, C), 1)
    incl = row >= col
    strict = row > col
    G = _mm(incl.astype(F32), glog, NN, P["cum"])
    sel = (_iota2((SUBLANE, LANE), 0) == _iota2((SUBLANE, LANE), 1)).astype(F32)
    Gt = _mm(sel, G, NT, P["tr"])

    heads = range(H_B)
    sls = [slice(h * LANE, (h + 1) * LANE) for h in heads]
    l2n = lambda t: t * lax.rsqrt(jnp.sum(t * t, axis=-1, keepdims=True) + 1e-12)
    qs = [l2n(cs[:, sls[h]]) * (DK_B ** -0.5) for h in heads]
    ks = [l2n(cs[:, W_B + h * LANE:W_B + (h + 1) * LANE]) for h in heads]
    vs = [cs[:, 2 * W_B + h * LANE:2 * W_B + (h + 1) * LANE] for h in heads]
    gcols = [G[:, h:h + 1] for h in heads]
    bcols = [beta[:, H_B + h:H_B + h + 1] for h in heads]
    Ss = [s_scr[h] for h in heads]

    dmats = [jnp.where(incl, jnp.exp(jnp.where(incl, gcols[h] - Gt[h:h + 1, :], 0.0)), 0.0) for h in heads]
    kbs = [ks[h] * bcols[h] for h in heads]
    lows = [jnp.where(strict, _mm(kbs[h], ks[h], NT, P["kk"]) * dmats[h], 0.0) for h in heads]
    ainv = _inv_unit_lower(lows, C, P["invb"], P["invm"])
    egs = [jnp.exp(t) for t in gcols]
    wus = [_mm(ainv[h], jnp.concatenate([kbs[h] * egs[h], vs[h] * bcols[h]], axis=1), NN, P["wu"])
           for h in heads]
    qks = [_mm(qs[h], ks[h], NT, P["qk"]) * dmats[h] for h in heads]
    srs = [_mm(jnp.concatenate([wus[h][:, :DK_B], qs[h] * egs[h]], axis=0), Ss[h], NN, P["sread"])
           for h in heads]
    v_new = [wus[h][:, DK_B:] - srs[h][:C] for h in heads]
    glast = [t[C - 1:C, :] for t in gcols]
    s_new = [Ss[h] * jnp.exp(glast[h])
             + _mm(ks[h] * jnp.exp(glast[h] - gcols[h]), v_new[h], TN, P["state"]) for h in heads]
    for h in heads:
        s_scr[h] = s_new[h]
    os_ = [srs[h][C:] + _mm(qks[h], v_new[h], NN, P["qkv"]) for h in heads]
    for h in heads:
        o = os_[h]
        o = o * lax.rsqrt(jnp.mean(o * o, axis=-1, keepdims=True) + NORM_EPS) * nw_ref[...]
        zh = z[:, sls[h]]
        o_ref[0, :, sls[h]] = o * (zh * _sigmoid(zh))


    @pl.when(c == pl.num_programs(1) - 1)
    def _():
        sout_ref[0] = s_scr[...]


def _gdn(pb, hist8, s0, consts, chunk, nvalid):
    B, L, _ = pb.shape
    shared = hist8.shape[0] == 1 and B > 1
    bidx = (lambda b, c: (0, 0, 0)) if shared else (lambda b, c: (b, 0, 0))
    sidx = (lambda b, c: (0, 0, 0, 0)) if shared else (lambda b, c: (b, 0, 0, 0))
    return pl.pallas_call(
        functools.partial(_gdn_kernel, chunk=chunk, nvalid=nvalid),
        grid=(B, L // chunk),
        in_specs=[pl.BlockSpec((1, chunk, PB_W), lambda b, c: (b, c, 0)),
                  pl.BlockSpec((1, SUBLANE, 3 * W_B), bidx),
                  pl.BlockSpec((1, H_B, DK_B, DV_B), sidx)]
        + [pl.BlockSpec(t.shape, lambda b, c: (0, 0)) for t in consts],
        out_specs=[pl.BlockSpec((1, chunk, W_B), lambda b, c: (b, c, 0)),
                   pl.BlockSpec((1, H_B, DK_B, DV_B), lambda b, c: (b, 0, 0, 0))],
        out_shape=[jax.ShapeDtypeStruct((B, L, W_B), F32),
                   jax.ShapeDtypeStruct((B, H_B, DK_B, DV_B), F32)],
        scratch_shapes=[pltpu.VMEM((SUBLANE + chunk, 3 * W_B), F32),
                        pltpu.VMEM((H_B, DK_B, DV_B), F32)],
        compiler_params=pltpu.CompilerParams(dimension_semantics=("arbitrary", "arbitrary"),
                                             vmem_limit_bytes=VMEM_LIMIT),
        name="gdn_chunk%d" % chunk,
    )(pb, hist8, s0, *consts)


def _pad_cols(t, width):
    return jnp.pad(t, [(0, 0)] * (t.ndim - 1) + [(0, width - t.shape[-1])])


def _pa_layout(t):
    main = t[..., :3 * W_A]
    dec = t[..., 3 * W_A:3 * W_A + D_DECAY]
    aaa = t[..., 3 * W_A + D_DECAY:3 * W_A + D_DECAY + D_AAA]
    gate = t[..., 3 * W_A + D_DECAY + D_AAA:]
    return jnp.concatenate([main, _pad_cols(dec, LANE), _pad_cols(aaa, LANE), _pad_cols(gate, LANE)],
                           axis=-1)


def _pa_unlayout(t):
    return jnp.concatenate([t[..., :3 * W_A], t[..., PA_DECAY:PA_DECAY + D_DECAY],
                            t[..., PA_AAA:PA_AAA + D_AAA], t[..., PA_GATE:PA_GATE + D_GATE]], axis=-1)


def _pad_rows(t, rows):
    return jnp.pad(t, [(0, rows - t.shape[0]), (0, 0)])


def _pair_blockdiag(s):
    B = s.shape[0]
    s = s.reshape(B, H_A // 2, 2, N_A, N_A)
    zero = jnp.zeros_like(s[:, :, 0])
    top = jnp.concatenate([s[:, :, 0], zero], axis=-1)
    bot = jnp.concatenate([zero, s[:, :, 1]], axis=-1)
    return jnp.concatenate([top, bot], axis=-2)


def _pair_unblock(s):
    B = s.shape[0]
    return jnp.stack([s[:, :, :N_A, :N_A], s[:, :, N_A:, N_A:]], axis=2).reshape(B, H_A, N_A, N_A)


def _history_rows(rows):
    B, n, w = rows.shape
    return jnp.concatenate([jnp.zeros((B, SUBLANE - n, w), F32), rows], axis=1)


def kernel(x_prompt, x_sample, state_rwkv, state_shift, state_gdn, state_conv, meta_tokens,
           g_ffn1, w_gate1, w_up1, w_down1, g_mix, w_in, mu_shift, w0, w_decay_up, a0, w_a_up,
           w_g_up, k_k, k_a, r_k, lnx_w, lnx_b, conv_w, a_log, dt_bias, gdn_norm_w, w_out,
           g_ffn2, w_gate2, w_up2, w_down2, g_final):
    assert g_ffn1.shape[0] == 1, "single trunk layer"
    bp, sp, _ = x_prompt.shape
    bs = x_sample.shape[0]
    assert x_sample.shape[1] == 1 and sp % CHUNK_RWKV == 0 and sp % CHUNK_GDN == 0
    assert (bp * sp) % TM_DENSE == 0
    row = lambda t: t.reshape(1, -1).astype(F32)

    ffn1 = (w_gate1[0].astype(BF16), w_up1[0].astype(BF16), w_down1[0].astype(BF16))
    ffn2 = (w_gate2[0].astype(BF16), w_up2[0].astype(BF16), w_down2[0].astype(BF16))
    win_a = _pa_layout(w_in[0][:, :N_A_IN]).astype(BF16)
    win_b = _pad_cols(w_in[0][:, N_A_IN:], PB_W).astype(BF16)
    wo_a = w_out[0][:W_A].astype(BF16)
    wo_b = w_out[0][W_A:].astype(BF16)
    dense_in_consts = (row(g_ffn1[0]), *ffn1, row(g_mix[0]), win_a, win_b)
    dense_out_consts = (wo_a, wo_b, row(g_ffn2[0]), *ffn2, row(g_final))
    rwkv_consts = (_pa_layout(row(mu_shift[0])), row(w0[0]),
                   _pad_rows(w_decay_up[0], LANE).astype(BF16), row(a0[0]),
                   _pad_rows(w_a_up[0], LANE).astype(BF16), _pad_rows(w_g_up[0], LANE).astype(BF16),
                   row(k_k[0]), row(k_a[0]), row(r_k[0]), row(lnx_w[0]), row(lnx_b[0]))
    gdn_consts = (conv_w[0].astype(F32), _pad_cols(row(a_log[0]), LANE), _pad_cols(row(dt_bias[0]), LANE),
                  row(gdn_norm_w[0]))

    xs = jnp.concatenate([x_sample[:, 0, :].astype(F32), meta_tokens.astype(F32)], axis=0)
    hs, pas, pbs = _dense_in(xs, *dense_in_consts, tm=xs.shape[0])

    pa_meta = pas[bs:][None]
    pb_meta = pbs[bs:][None]
    _, rw_meta = _rwkv(pa_meta, jnp.zeros((1, SUBLANE, PA_W), F32),
                       jnp.zeros((1, H_A // 2, LANE, LANE), F32), rwkv_consts, N_META, N_META)
    _, gd_meta = _gdn(pb_meta, jnp.zeros((1, SUBLANE, 3 * W_B), F32),
                      jnp.zeros((1, H_B, DK_B, DV_B), F32), gdn_consts, N_META, N_META)

    hp, pap, pbp = _dense_in(x_prompt.reshape(bp * sp, D_MODEL).astype(F32), *dense_in_consts,
                             tm=TM_DENSE)
    pap3 = pap.reshape(bp, sp, PA_W)
    pbp3 = pbp.reshape(bp, sp, PB_W)
    oa_p, rw_p = _rwkv(pap3, _history_rows(pa_meta[:, -1:, :]), rw_meta, rwkv_consts,
                       CHUNK_RWKV, CHUNK_RWKV)
    ob_p, gd_p = _gdn(pbp3, _history_rows(pb_meta[:, -(CONV_W - 1):, :3 * W_B]), gd_meta, gdn_consts,
                      CHUNK_GDN, CHUNK_GDN)
    y_p = _dense_out(hp, oa_p.reshape(bp * sp, W_A), ob_p.reshape(bp * sp, W_B), *dense_out_consts,
                     tm=TM_DENSE)

    pa_s = jnp.pad(pas[:bs, None, :], ((0, 0), (0, DEC_CHUNK - 1), (0, 0)))
    pb_s = jnp.pad(pbs[:bs, None, :], ((0, 0), (0, DEC_CHUNK - 1), (0, 0)))
    oa_s, rw_s = _rwkv(pa_s, _history_rows(_pa_layout(state_shift[0].astype(F32))[:, None, :]),
                       _pair_blockdiag(state_rwkv[0].astype(F32)), rwkv_consts, DEC_CHUNK, 1)
    ob_s, gd_s = _gdn(pb_s, _history_rows(state_conv[0].astype(F32)), state_gdn[0].astype(F32),
                      gdn_consts, DEC_CHUNK, 1)
    y_s = _dense_out(hs[:bs], oa_s[:, 0, :], ob_s[:, 0, :], *dense_out_consts, tm=bs)

    new_conv_s = jnp.concatenate([state_conv[0].astype(F32)[:, 1:, :], pbs[:bs, None, :3 * W_B]], axis=1)
    return (y_p.reshape(bp, sp, D_MODEL).astype(x_prompt.dtype),
            y_s.reshape(bs, 1, D_MODEL).astype(x_sample.dtype),
            _pair_unblock(rw_p)[None],
            _pa_unlayout(pap3[:, -1, :])[None],
            gd_p[None],
            pbp3[:, -(CONV_W - 1):, :3 * W_B][None],
            _pair_unblock(rw_s)[None],
            _pa_unlayout(pas[:bs])[None],
            gd_s[None],
            new_conv_s[None])
```

```python
import functools
import math

import jax
import jax.numpy as jnp
from jax import lax
from jax.experimental import pallas as pl
from jax.experimental.pallas import tpu as pltpu

F32 = jnp.float32
BF16 = jnp.bfloat16

LANE = 128
SUBLANE = 8
VMEM_LIMIT = 56 * 1024 * 1024

D_MODEL = 1024
D_FF = 2816
N_META = 16
H_A, N_A = 8, 64
W_A = H_A * N_A
D_DECAY, D_AAA, D_GATE = 32, 32, 96
N_A_IN = 3 * W_A + D_DECAY + D_AAA + D_GATE
H_B, DK_B, DV_B = 4, 128, 128
W_B = H_B * DV_B
CONV_W = 4
N_B_IN = 4 * W_B + 2 * H_B
LNX_EPS = 64e-5
NORM_EPS = 1e-6

PA_W = 3 * W_A + 3 * LANE
PA_DECAY = 3 * W_A
PA_AAA = PA_DECAY + LANE
PA_GATE = PA_AAA + LANE
PB_W = 4 * W_B + LANE
PB_Z = 3 * W_B
PB_GATES = 4 * W_B

TM_DENSE = 256
CHUNK_RWKV = 64
CHUNK_GDN = 128
DEC_CHUNK = SUBLANE
SEQS_PER_STEP = 2
DEC_SEQS_PER_STEP = 4

NN = (((1,), (0,)), ((), ()))
NT = (((1,), (1,)), ((), ()))
TN = (((0,), (0,)), ((), ()))


def _dot(a, b):
    return jnp.dot(a, b, preferred_element_type=F32)


def _sigmoid(x):
    return 1.0 / (1.0 + jnp.exp(-x))


def _softplus(x):
    return jnp.maximum(x, 0.0) + jnp.log(1.0 + jnp.exp(-jnp.abs(x)))


def _rms(x, g):
    return x * lax.rsqrt(jnp.mean(x * x, axis=-1, keepdims=True) + NORM_EPS) * g


def _iota2(shape, dim):
    return lax.broadcasted_iota(jnp.int32, shape, dim)


def _split(x):
    hi = x.astype(BF16)
    return hi, (x - hi.astype(F32)).astype(BF16)


def _dg(a, b, dims):
    return lax.dot_general(a, b, dims, preferred_element_type=F32)


def _mm(a, b, dims=NN, mode="b"):
    if mode == "b":
        return _dg(a.astype(BF16), b.astype(BF16), dims)
    if mode == "xl":
        ah, al = _split(a)
        bh = b.astype(BF16)
        return _dg(ah, bh, dims) + _dg(al, bh, dims)
    if mode == "xr":
        ah = a.astype(BF16)
        bh, bl = _split(b)
        return _dg(ah, bh, dims) + _dg(ah, bl, dims)
    assert mode == "x3"
    ah, al = _split(a)
    bh, bl = _split(b)
    return _dg(ah, bh, dims) + (_dg(ah, bl, dims) + _dg(al, bh, dims))


INV_BASE = SUBLANE


def _inv_unit_lower(lows, nil, base_mode, merge_mode):
    n = lows[0].shape[0]
    row = _iota2((n, n), 0)
    col = _iota2((n, n), 1)
    same_block = lambda s: (row // s) == (col // s)
    eye = (row == col).astype(F32)
    b = min(INV_BASE, nil)
    in_base = same_block(b)
    ps = [-jnp.where(in_base, low, 0.0) for low in lows]
    invs = [eye + p for p in ps]
    for _ in range(int(math.log2(b)) - 1):
        ps = [_mm(p, p, NN, base_mode) for p in ps]
        invs = [inv + _mm(inv, p, NN, base_mode) for inv, p in zip(invs, ps)]
    s = b
    while s < nil:
        newly = jnp.logical_and(same_block(2 * s), jnp.logical_not(same_block(s)))
        ts = [_mm(inv, jnp.where(newly, low, 0.0), NN, merge_mode) for inv, low in zip(invs, lows)]
        invs = [inv - _mm(t, inv, NN, merge_mode) for inv, t in zip(invs, ts)]
        s *= 2
    return invs


def _swiglu(n, wg_ref, wu_ref, wd_ref):
    gate = _dot(n, wg_ref[...])
    up = _dot(n, wu_ref[...])
    act = (gate * _sigmoid(gate) * up).astype(BF16)
    return _dot(act, wd_ref[...])


def _dense_in_kernel(x_ref, g1_ref, wg_ref, wu_ref, wd_ref, gm_ref, wa_ref, wb_ref,
                     h_ref, pa_ref, pb_ref):
    x = x_ref[...]
    h = x + 0.5 * _swiglu(_rms(x, g1_ref[...]).astype(BF16), wg_ref, wu_ref, wd_ref)
    h_ref[...] = h
    n = _rms(h, gm_ref[...]).astype(BF16)
    pa_ref[...] = _dot(n, wa_ref[...])
    pb_ref[...] = _dot(n, wb_ref[...])


def _dense_out_kernel(h_ref, oa_ref, ob_ref, woa_ref, wob_ref, g2_ref, wg_ref, wu_ref, wd_ref,
                      gf_ref, y_ref):
    h = (h_ref[...] + _dot(oa_ref[...].astype(BF16), woa_ref[...])
         + _dot(ob_ref[...].astype(BF16), wob_ref[...]))
    h = h + 0.5 * _swiglu(_rms(h, g2_ref[...]).astype(BF16), wg_ref, wu_ref, wd_ref)
    y_ref[...] = _rms(h, gf_ref[...])


def _const_spec(shape):
    return pl.BlockSpec(shape, lambda *_: (0,) * len(shape), pipeline_mode=pl.Buffered(1))


def _row_spec(tm, width):
    return pl.BlockSpec((tm, width), lambda i: (i, 0))


def _dense_in(x, g1, wg, wu, wd, gm, wa, wb, tm):
    n = x.shape[0]
    consts = (g1, wg, wu, wd, gm, wa, wb)
    return pl.pallas_call(
        _dense_in_kernel,
        grid=(n // tm,),
        in_specs=[_row_spec(tm, D_MODEL)] + [_const_spec(c.shape) for c in consts],
        out_specs=[_row_spec(tm, D_MODEL), _row_spec(tm, PA_W), _row_spec(tm, PB_W)],
        out_shape=[jax.ShapeDtypeStruct((n, D_MODEL), F32),
                   jax.ShapeDtypeStruct((n, PA_W), F32),
                   jax.ShapeDtypeStruct((n, PB_W), F32)],
        compiler_params=pltpu.CompilerParams(dimension_semantics=("arbitrary",),
                                             vmem_limit_bytes=VMEM_LIMIT),
        name="dense_in",
    )(x, *consts)


def _dense_out(h, oa, ob, woa, wob, g2, wg, wu, wd, gf, tm):
    n = h.shape[0]
    consts = (woa, wob, g2, wg, wu, wd, gf)
    return pl.pallas_call(
        _dense_out_kernel,
        grid=(n // tm,),
        in_specs=[_row_spec(tm, D_MODEL), _row_spec(tm, W_A), _row_spec(tm, W_B)]
        + [_const_spec(c.shape) for c in consts],
        out_specs=_row_spec(tm, D_MODEL),
        out_shape=jax.ShapeDtypeStruct((n, D_MODEL), F32),
        compiler_params=pltpu.CompilerParams(dimension_semantics=("arbitrary",),
                                             vmem_limit_bytes=VMEM_LIMIT),
        name="dense_out",
    )(h, oa, ob, *consts)


RWKV_PREC = dict(cum="xr", seg="xl", pair="b", invb="x3", invm="b", sread="x3", akv="b", solve="x3", inter="b",
                 state="x3")


def _rwkv_kernel(pa_ref, prev_ref, s0_ref, mu_ref, w0_ref, wdu_ref, a0_ref, wau_ref, wgu_ref,
                 kk_ref, ka_ref, rk_ref, lnw_ref, lnb_ref, o_ref, sout_ref, xbuf, s_scr,
                 *, chunk, nvalid, nseq, shared_init):
    c = pl.program_id(1)
    C = chunk
    P = RWKV_PREC

    @pl.when(c == 0)
    def _():
        for i in range(nseq):
            j = 0 if shared_init else i
            xbuf[i, 0:SUBLANE, :] = prev_ref[j]
            s_scr[i] = s0_ref[j]

    tri = (_iota2((C, C), 0) >= _iota2((C, C), 1)).astype(F32)
    mid = C // 2 - 1

    def prep(i):
        x = pa_ref[i]
        xbuf[i, SUBLANE:SUBLANE + C, :] = x
        prev = xbuf[i, SUBLANE - 1:SUBLANE - 1 + C, :]
        xm = x + (prev - x) * mu_ref[...]
        xbuf[i, 0:SUBLANE, :] = xbuf[i, C:C + SUBLANE, :]

        r = xm[:, 0:W_A]
        k = xm[:, W_A:2 * W_A]
        v = xm[:, 2 * W_A:3 * W_A]
        wd = xm[:, PA_DECAY:PA_DECAY + LANE]
        ad = xm[:, PA_AAA:PA_AAA + LANE]
        gd = xm[:, PA_GATE:PA_GATE + LANE]

        wl = w0_ref[...] + _dot(jnp.tanh(wd).astype(BF16), wdu_ref[...])
        lw = -jnp.exp(-_softplus(-wl) - 0.5)
        a = _sigmoid(a0_ref[...] + _dot(ad.astype(BF16), wau_ref[...]))
        g = _dot(_sigmoid(gd).astype(BF16), wgu_ref[...])
        kkr = k * kk_ref[...]
        k2 = k * (1.0 + (a - 1.0) * ka_ref[...])
        if nvalid < C:
            valid = _iota2((C, 1), 0) < nvalid
            lw = jnp.where(valid, lw, 0.0)
            kkr = jnp.where(valid, kkr, 0.0)
            k2 = jnp.where(valid, k2, 0.0)
        G = _mm(tri, lw, NN, P["cum"])
        return dict(r=r, v=v, a=a, g=g, kkr=kkr, k2=k2, G=G, Gx=G - lw, Gm=G[mid:mid + 1, :],
                    Gc=G[C - 1:C, :])

    seqs = [prep(i) for i in range(nseq)]

    C2 = 2 * C
    trow = _iota2((C2, C2), 0) % C
    tcol = _iota2((C2, C2), 1) % C
    incl = trow >= tcol
    strict = trow > tcol
    lane = _iota2((1, LANE), 1)
    m0 = (lane < N_A).astype(F32)
    m1 = (lane >= N_A).astype(F32)
    seg = ((_iota2((LANE, LANE), 0) // N_A) == (_iota2((LANE, LANE), 1) // N_A)).astype(F32)
    stack = lambda t: jnp.concatenate([t * m0, t * m1], axis=0)
    merged = C2 % LANE == 0

    inst = [(i, p) for i in range(nseq) for p in range(H_A // 2)]
    n = range(len(inst))
    sls = [slice(p * LANE, (p + 1) * LANE) for _, p in inst]
    take = lambda name: [seqs[i][name][:, sls[j]] for j, (i, _) in enumerate(inst)]
    rs, vs_, k2s, kkrs, as_, gs = take("r"), take("v"), take("k2"), take("kkr"), take("a"), take("g")
    Gs, Gxs, Gms, Gcs = take("G"), take("Gx"), take("Gm"), take("Gc")
    Ss = [s_scr[i, p] for i, p in inst]

    ssq = [_mm(t * t, seg, NN, P["seg"]) for t in kkrs]
    kks = [t * lax.rsqrt(q + 1e-12) for t, q in zip(kkrs, ssq)]
    kkas = [kks[j] * as_[j] for j in n]
    inv_rel = [jnp.exp(Gms[j] - Gs[j]) for j in n]
    dec = [jnp.exp(Gcs[j] - Gs[j]) for j in n]
    lhs = [jnp.concatenate([stack(kks[j] * jnp.exp(Gxs[j] - Gms[j])),
                            stack(rs[j] * jnp.exp(Gs[j] - Gms[j]))], axis=0) for j in n]
    ais = [stack(kkas[j] * inv_rel[j]) for j in n]
    kis = [stack(k2s[j] * inv_rel[j]) for j in n]
    if merged:
        pm = [_mm(lhs[j], jnp.concatenate([ais[j], kis[j]], axis=0), NT, P["pair"]) for j in n]
        pas_, pks_ = [t[:, :C2] for t in pm], [t[:, C2:] for t in pm]
    else:
        pas_ = [_mm(lhs[j], ais[j], NT, P["pair"]) for j in n]
        pks_ = [_mm(lhs[j], kis[j], NT, P["pair"]) for j in n]
    a_aa = [jnp.where(strict, t[:C2], 0.0) for t in pas_]
    a_ak = [jnp.where(strict, t[:C2], 0.0) for t in pks_]
    a_ra = [jnp.where(incl, t[C2:], 0.0) for t in pas_]
    a_rk = [jnp.where(incl, t[C2:], 0.0) for t in pks_]
    ainv = _inv_unit_lower(a_aa, C, P["invb"], P["invm"])

    vst = [stack(t) for t in vs_]
    akv = [_mm(a_ak[j], vst[j], NN, P["akv"]) for j in n]
    sx = [_mm(jnp.concatenate([kks[j] * jnp.exp(Gxs[j]), rs[j] * jnp.exp(Gs[j])], axis=0), Ss[j], NT,
              P["sread"]) for j in n]
    us = [-_mm(ainv[j], stack(sx[j][:C]) + akv[j], NN, P["solve"]) for j in n]
    uv = [jnp.concatenate([us[j], vst[j]], axis=0) for j in n]
    s_new = [Ss[j] * jnp.exp(Gcs[j])
             + _mm(uv[j], jnp.concatenate([stack(kkas[j] * dec[j]), stack(k2s[j] * dec[j])], axis=0),
                   TN, P["state"]) for j in n]
    for j, (i, p) in enumerate(inst):
        s_scr[i, p] = s_new[j]
    if merged:
        inter = [_mm(jnp.concatenate([a_ra[j], a_rk[j]], axis=1), uv[j], NN, P["inter"]) for j in n]
    else:
        inter = [_mm(a_ra[j], us[j], NN, P["inter"]) + _mm(a_rk[j], vst[j], NN, P["inter"]) for j in n]
    Os = [sx[j][C:] + inter[j][:C] + inter[j][C:] for j in n]
    means = [_mm(t, seg, NN, P["seg"]) * (1.0 / N_A) for t in Os]
    dlts = [t - m for t, m in zip(Os, means)]
    vars_ = [_mm(t * t, seg, NN, P["seg"]) * (1.0 / N_A) for t in dlts]
    bonus = [_mm(rs[j] * k2s[j] * rk_ref[:, sls[j]], seg, NN, P["seg"]) * vs_[j] for j in n]
    for j, (i, p) in enumerate(inst):
        on = dlts[j] * lax.rsqrt(vars_[j] + LNX_EPS) * lnw_ref[:, sls[j]] + lnb_ref[:, sls[j]]
        o_ref[i, :, sls[j]] = (on + bonus[j]) * gs[j]

    @pl.when(c == pl.num_programs(1) - 1)
    def _():
        sout_ref[...] = s_scr[...]


def _seq_specs(nseq, shared_init, hist_w, state_shape):
    zeros = (0,) * len(state_shape)
    if shared_init:
        return [pl.BlockSpec((1, SUBLANE, hist_w), lambda b, c: (0, 0, 0)),
                pl.BlockSpec((1,) + state_shape, lambda b, c: (0,) + zeros)]
    return [pl.BlockSpec((nseq, SUBLANE, hist_w), lambda b, c: (b, 0, 0)),
            pl.BlockSpec((nseq,) + state_shape, lambda b, c: (b,) + zeros)]


def _rwkv(pa, prev8, s0, consts, chunk, nvalid, nseq):
    B, L, _ = pa.shape
    shared_init = prev8.shape[0] == 1 and B > 1
    state_shape = (H_A // 2, LANE, LANE)
    return pl.pallas_call(
        functools.partial(_rwkv_kernel, chunk=chunk, nvalid=nvalid, nseq=nseq, shared_init=shared_init),
        grid=(B // nseq, L // chunk),
        in_specs=[pl.BlockSpec((nseq, chunk, PA_W), lambda b, c: (b, c, 0))]
        + _seq_specs(nseq, shared_init, PA_W, state_shape)
        + [pl.BlockSpec(t.shape, lambda b, c: (0, 0)) for t in consts],
        out_specs=[pl.BlockSpec((nseq, chunk, W_A), lambda b, c: (b, c, 0)),
                   pl.BlockSpec((nseq,) + state_shape, lambda b, c: (b, 0, 0, 0))],
        out_shape=[jax.ShapeDtypeStruct((B, L, W_A), F32),
                   jax.ShapeDtypeStruct((B,) + state_shape, F32)],
        scratch_shapes=[pltpu.VMEM((nseq, SUBLANE + chunk, PA_W), F32),
                        pltpu.VMEM((nseq,) + state_shape, F32)],
        compiler_params=pltpu.CompilerParams(dimension_semantics=("arbitrary", "arbitrary"),
                                             vmem_limit_bytes=VMEM_LIMIT),
        name="rwkv_chunk%d" % chunk,
    )(pa, prev8, s0, *consts)


GDN_PREC = dict(cum="xr", tr="xr", kk="b", invb="x3", invm="b", wu="x3", qk="b", sread="x3", qkv="b", state="x3")


def _gdn_kernel(pb_ref, hist_ref, s0_ref, cw_ref, alog_ref, dtb_ref, nw_ref, o_ref, sout_ref,
                xbuf, s_scr, *, chunk, nvalid, nseq, shared_init):
    c = pl.program_id(1)
    C = chunk
    P = GDN_PREC

    @pl.when(c == 0)
    def _():
        for i in range(nseq):
            j = 0 if shared_init else i
            xbuf[i, 0:SUBLANE, :] = hist_ref[j]
            s_scr[i] = s0_ref[j]

    row = _iota2((C, C), 0)
    col = _iota2((C, C), 1)
    incl = row >= col
    strict = row > col
    tri = incl.astype(F32)
    sel = (_iota2((SUBLANE, LANE), 0) == _iota2((SUBLANE, LANE), 1)).astype(F32)

    def prep(i):
        x = pb_ref[i]
        xbuf[i, SUBLANE:SUBLANE + C, :] = x[:, 0:3 * W_B]
        conv = xbuf[i, SUBLANE - 3:SUBLANE - 3 + C, :] * cw_ref[0:1, :]
        for j in range(1, CONV_W):
            off = SUBLANE - 3 + j
            conv = conv + xbuf[i, off:off + C, :] * cw_ref[j:j + 1, :]
        xbuf[i, 0:SUBLANE, :] = xbuf[i, C:C + SUBLANE, :]
        cs = conv * _sigmoid(conv)
        gates = x[:, PB_GATES:PB_GATES + LANE]
        glog = -jnp.exp(alog_ref[...]) * _softplus(gates + dtb_ref[...])
        beta = _sigmoid(gates)
        if nvalid < C:
            valid = _iota2((C, 1), 0) < nvalid
            glog = jnp.where(valid, glog, 0.0)
            beta = jnp.where(valid, beta, 0.0)
        G = _mm(tri, glog, NN, P["cum"])
        Gt = _mm(sel, G, NT, P["tr"])
        return dict(cs=cs, z=x[:, PB_Z:PB_Z + W_B], beta=beta, G=G, Gt=Gt)

    seqs = [prep(i) for i in range(nseq)]

    inst = [(i, h) for i in range(nseq) for h in range(H_B)]
    n = range(len(inst))
    l2n = lambda t: t * lax.rsqrt(jnp.sum(t * t, axis=-1, keepdims=True) + 1e-12)
    head = lambda i, h, part: seqs[i]["cs"][:, part * W_B + h * LANE:part * W_B + (h + 1) * LANE]
    qs = [l2n(head(i, h, 0)) * (DK_B ** -0.5) for i, h in inst]
    ks = [l2n(head(i, h, 1)) for i, h in inst]
    vs = [head(i, h, 2) for i, h in inst]
    gcols = [seqs[i]["G"][:, h:h + 1] for i, h in inst]
    grows = [seqs[i]["Gt"][h:h + 1, :] for i, h in inst]
    bcols = [seqs[i]["beta"][:, H_B + h:H_B + h + 1] for i, h in inst]
    Ss = [s_scr[i, h] for i, h in inst]

    dmats = [jnp.where(incl, jnp.exp(jnp.where(incl, gcols[j] - grows[j], 0.0)), 0.0) for j in n]
    kbs = [ks[j] * bcols[j] for j in n]
    lows = [jnp.where(strict, _mm(kbs[j], ks[j], NT, P["kk"]) * dmats[j], 0.0) for j in n]
    ainv = _inv_unit_lower(lows, C, P["invb"], P["invm"])
    egs = [jnp.exp(t) for t in gcols]
    wus = [_mm(ainv[j], jnp.concatenate([kbs[j] * egs[j], vs[j] * bcols[j]], axis=1), NN, P["wu"])
           for j in n]
    qks = [_mm(qs[j], ks[j], NT, P["qk"]) * dmats[j] for j in n]
    srs = [_mm(jnp.concatenate([wus[j][:, :DK_B], qs[j] * egs[j]], axis=0), Ss[j], NN, P["sread"])
           for j in n]
    v_new = [wus[j][:, DK_B:] - srs[j][:C] for j in n]
    glast = [t[C - 1:C, :] for t in gcols]
    s_new = [Ss[j] * jnp.exp(glast[j])
             + _mm(ks[j] * jnp.exp(glast[j] - gcols[j]), v_new[j], TN, P["state"]) for j in n]
    for j, (i, h) in enumerate(inst):
        s_scr[i, h] = s_new[j]
    os_ = [srs[j][C:] + _mm(qks[j], v_new[j], NN, P["qkv"]) for j in n]
    for j, (i, h) in enumerate(inst):
        o = os_[j]
        o = o * lax.rsqrt(jnp.mean(o * o, axis=-1, keepdims=True) + NORM_EPS) * nw_ref[...]
        zh = seqs[i]["z"][:, h * LANE:(h + 1) * LANE]
        o_ref[i, :, h * LANE:(h + 1) * LANE] = o * (zh * _sigmoid(zh))

    @pl.when(c == pl.num_programs(1) - 1)
    def _():
        sout_ref[...] = s_scr[...]


def _gdn(pb, hist8, s0, consts, chunk, nvalid, nseq):
    B, L, _ = pb.shape
    shared_init = hist8.shape[0] == 1 and B > 1
    state_shape = (H_B, DK_B, DV_B)
    return pl.pallas_call(
        functools.partial(_gdn_kernel, chunk=chunk, nvalid=nvalid, nseq=nseq, shared_init=shared_init),
        grid=(B // nseq, L // chunk),
        in_specs=[pl.BlockSpec((nseq, chunk, PB_W), lambda b, c: (b, c, 0))]
        + _seq_specs(nseq, shared_init, 3 * W_B, state_shape)
        + [pl.BlockSpec(t.shape, lambda b, c: (0, 0)) for t in consts],
        out_specs=[pl.BlockSpec((nseq, chunk, W_B), lambda b, c: (b, c, 0)),
                   pl.BlockSpec((nseq,) + state_shape, lambda b, c: (b, 0, 0, 0))],
        out_shape=[jax.ShapeDtypeStruct((B, L, W_B), F32),
                   jax.ShapeDtypeStruct((B,) + state_shape, F32)],
        scratch_shapes=[pltpu.VMEM((nseq, SUBLANE + chunk, 3 * W_B), F32),
                        pltpu.VMEM((nseq,) + state_shape, F32)],
        compiler_params=pltpu.CompilerParams(dimension_semantics=("arbitrary", "arbitrary"),
                                             vmem_limit_bytes=VMEM_LIMIT),
        name="gdn_chunk%d" % chunk,
    )(pb, hist8, s0, *consts)


def _pad_cols(t, width):
    return jnp.pad(t, [(0, 0)] * (t.ndim - 1) + [(0, width - t.shape[-1])])


def _pa_layout(t):
    main = t[..., :3 * W_A]
    dec = t[..., 3 * W_A:3 * W_A + D_DECAY]
    aaa = t[..., 3 * W_A + D_DECAY:3 * W_A + D_DECAY + D_AAA]
    gate = t[..., 3 * W_A + D_DECAY + D_AAA:]
    return jnp.concatenate([main, _pad_cols(dec, LANE), _pad_cols(aaa, LANE), _pad_cols(gate, LANE)],
                           axis=-1)


def _pa_unlayout(t):
    return jnp.concatenate([t[..., :3 * W_A], t[..., PA_DECAY:PA_DECAY + D_DECAY],
                            t[..., PA_AAA:PA_AAA + D_AAA], t[..., PA_GATE:PA_GATE + D_GATE]], axis=-1)


def _pad_rows(t, rows):
    return jnp.pad(t, [(0, rows - t.shape[0]), (0, 0)])


def _pair_blockdiag(s):
    B = s.shape[0]
    s = s.reshape(B, H_A // 2, 2, N_A, N_A)
    zero = jnp.zeros_like(s[:, :, 0])
    top = jnp.concatenate([s[:, :, 0], zero], axis=-1)
    bot = jnp.concatenate([zero, s[:, :, 1]], axis=-1)
    return jnp.concatenate([top, bot], axis=-2)


def _pair_unblock(s):
    B = s.shape[0]
    return jnp.stack([s[:, :, :N_A, :N_A], s[:, :, N_A:, N_A:]], axis=2).reshape(B, H_A, N_A, N_A)


def _history_rows(rows):
    B, n, w = rows.shape
    return jnp.concatenate([jnp.zeros((B, SUBLANE - n, w), F32), rows], axis=1)


def kernel(x_prompt, x_sample, state_rwkv, state_shift, state_gdn, state_conv, meta_tokens,
           g_ffn1, w_gate1, w_up1, w_down1, g_mix, w_in, mu_shift, w0, w_decay_up, a0, w_a_up,
           w_g_up, k_k, k_a, r_k, lnx_w, lnx_b, conv_w, a_log, dt_bias, gdn_norm_w, w_out,
           g_ffn2, w_gate2, w_up2, w_down2, g_final):
    assert g_ffn1.shape[0] == 1, "single trunk layer"
    bp, sp, _ = x_prompt.shape
    bs = x_sample.shape[0]
    assert x_sample.shape[1] == 1 and sp % CHUNK_RWKV == 0 and sp % CHUNK_GDN == 0
    assert (bp * sp) % TM_DENSE == 0 and bp % SEQS_PER_STEP == 0 and bs % DEC_SEQS_PER_STEP == 0
    row = lambda t: t.reshape(1, -1).astype(F32)

    ffn1 = (w_gate1[0].astype(BF16), w_up1[0].astype(BF16), w_down1[0].astype(BF16))
    ffn2 = (w_gate2[0].astype(BF16), w_up2[0].astype(BF16), w_down2[0].astype(BF16))
    win_a = _pa_layout(w_in[0][:, :N_A_IN]).astype(BF16)
    win_b = _pad_cols(w_in[0][:, N_A_IN:], PB_W).astype(BF16)
    wo_a = w_out[0][:W_A].astype(BF16)
    wo_b = w_out[0][W_A:].astype(BF16)
    dense_in_consts = (row(g_ffn1[0]), *ffn1, row(g_mix[0]), win_a, win_b)
    dense_out_consts = (wo_a, wo_b, row(g_ffn2[0]), *ffn2, row(g_final))
    rwkv_consts = (_pa_layout(row(mu_shift[0])), row(w0[0]),
                   _pad_rows(w_decay_up[0], LANE).astype(BF16), row(a0[0]),
                   _pad_rows(w_a_up[0], LANE).astype(BF16), _pad_rows(w_g_up[0], LANE).astype(BF16),
                   row(k_k[0]), row(k_a[0]), row(r_k[0]), row(lnx_w[0]), row(lnx_b[0]))
    gdn_consts = (conv_w[0].astype(F32), _pad_cols(row(a_log[0]), LANE), _pad_cols(row(dt_bias[0]), LANE),
                  row(gdn_norm_w[0]))

    xs = jnp.concatenate([x_sample[:, 0, :].astype(F32), meta_tokens.astype(F32)], axis=0)
    hs, pas, pbs = _dense_in(xs, *dense_in_consts, tm=xs.shape[0])

    pa_meta = pas[bs:][None]
    pb_meta = pbs[bs:][None]
    _, rw_meta = _rwkv(pa_meta, jnp.zeros((1, SUBLANE, PA_W), F32),
                       jnp.zeros((1, H_A // 2, LANE, LANE), F32), rwkv_consts, N_META, N_META, 1)
    _, gd_meta = _gdn(pb_meta, jnp.zeros((1, SUBLANE, 3 * W_B), F32),
                      jnp.zeros((1, H_B, DK_B, DV_B), F32), gdn_consts, N_META, N_META, 1)

    hp, pap, pbp = _dense_in(x_prompt.reshape(bp * sp, D_MODEL).astype(F32), *dense_in_consts,
                             tm=TM_DENSE)
    pap3 = pap.reshape(bp, sp, PA_W)
    pbp3 = pbp.reshape(bp, sp, PB_W)
    oa_p, rw_p = _rwkv(pap3, _history_rows(pa_meta[:, -1:, :]), rw_meta, rwkv_consts,
                       CHUNK_RWKV, CHUNK_RWKV, SEQS_PER_STEP)
    ob_p, gd_p = _gdn(pbp3, _history_rows(pb_meta[:, -(CONV_W - 1):, :3 * W_B]), gd_meta, gdn_consts,
                      CHUNK_GDN, CHUNK_GDN, SEQS_PER_STEP)
    y_p = _dense_out(hp, oa_p.reshape(bp * sp, W_A), ob_p.reshape(bp * sp, W_B), *dense_out_consts,
                     tm=TM_DENSE)

    pa_s = jnp.pad(pas[:bs, None, :], ((0, 0), (0, DEC_CHUNK - 1), (0, 0)))
    pb_s = jnp.pad(pbs[:bs, None, :], ((0, 0), (0, DEC_CHUNK - 1), (0, 0)))
    oa_s, rw_s = _rwkv(pa_s, _history_rows(_pa_layout(state_shift[0].astype(F32))[:, None, :]),
                       _pair_blockdiag(state_rwkv[0].astype(F32)), rwkv_consts, DEC_CHUNK, 1,
                       DEC_SEQS_PER_STEP)
    ob_s, gd_s = _gdn(pb_s, _history_rows(state_conv[0].astype(F32)), state_gdn[0].astype(F32),
                      gdn_consts, DEC_CHUNK, 1, DEC_SEQS_PER_STEP)
    y_s = _dense_out(hs[:bs], oa_s[:, 0, :], ob_s[:, 0, :], *dense_out_consts, tm=bs)

    new_conv_s = jnp.concatenate([state_conv[0].astype(F32)[:, 1:, :], pbs[:bs, None, :3 * W_B]], axis=1)
    return (y_p.reshape(bp, sp, D_MODEL).astype(x_prompt.dtype),
            y_s.reshape(bs, 1, D_MODEL).astype(x_sample.dtype),
            _pair_unblock(rw_p)[None],
            _pa_unlayout(pap3[:, -1, :])[None],
            gd_p[None],
            pbp3[:, -(CONV_W - 1):, :3 * W_B][None],
            _pair_unblock(rw_s)[None],
            _pa_unlayout(pas[:bs])[None],
            gd_s[None],
            new_conv_s[None])
```

```python
import functools

import jax
import jax.numpy as jnp
from jax import lax
from jax.experimental import pallas as pl
from jax.experimental.pallas import tpu as pltpu

F32 = jnp.float32
BF16 = jnp.bfloat16

LANE = 128
SUBLANE = 8
VMEM_LIMIT = 56 * 1024 * 1024

D_MODEL = 1024
D_FF = 2816
N_META = 16
H_A, N_A = 8, 64
W_A = H_A * N_A
D_DECAY, D_AAA, D_GATE = 32, 32, 96
N_A_IN = 3 * W_A + D_DECAY + D_AAA + D_GATE
H_B, DK_B, DV_B = 4, 128, 128
W_B = H_B * DV_B
CONV_W = 4
N_B_IN = 4 * W_B + 2 * H_B
LNX_EPS = 64e-5
NORM_EPS = 1e-6

PA_W = 3 * W_A + 3 * LANE
PA_DECAY = 3 * W_A
PA_AAA = PA_DECAY + LANE
PA_GATE = PA_AAA + LANE
PB_W = 4 * W_B + LANE
PB_Z = 3 * W_B
PB_GATES = 4 * W_B

TM_DENSE = 256
CHUNK_RWKV = 64
CHUNK_GDN = 128
DEC_CHUNK = SUBLANE
SEQS_PER_STEP = 4
DEC_SEQS_PER_STEP = 4

NN = (((1,), (0,)), ((), ()))
NT = (((1,), (1,)), ((), ()))
TN = (((0,), (0,)), ((), ()))


def _dot(a, b):
    return jnp.dot(a, b, preferred_element_type=F32)


def _sigmoid(x):
    return 1.0 / (1.0 + jnp.exp(-x))


def _softplus(x):
    return jnp.maximum(x, 0.0) + jnp.log(1.0 + jnp.exp(-jnp.abs(x)))


def _rms(x, g):
    return x * lax.rsqrt(jnp.mean(x * x, axis=-1, keepdims=True) + NORM_EPS) * g


def _iota2(shape, dim):
    return lax.broadcasted_iota(jnp.int32, shape, dim)


def _split(x):
    hi = x.astype(BF16)
    return hi, (x - hi.astype(F32)).astype(BF16)


def _dg(a, b, dims):
    return lax.dot_general(a, b, dims, preferred_element_type=F32)


def _mm(a, b, dims=NN, mode="b"):
    if mode == "b":
        return _dg(a.astype(BF16), b.astype(BF16), dims)
    if mode == "xl":
        ah, al = _split(a)
        bh = b.astype(BF16)
        return _dg(ah, bh, dims) + _dg(al, bh, dims)
    if mode == "xr":
        ah = a.astype(BF16)
        bh, bl = _split(b)
        return _dg(ah, bh, dims) + _dg(ah, bl, dims)
    assert mode == "x3"
    ah, al = _split(a)
    bh, bl = _split(b)
    return _dg(ah, bh, dims) + (_dg(ah, bl, dims) + _dg(al, bh, dims))


def _inv_unit_lower(lows, nil, merge_mode):
    n = lows[0].shape[0]
    row = _iota2((n, n), 0)
    col = _iota2((n, n), 1)
    same_block = lambda s: (row // s) == (col // s)
    eye = (row == col).astype(F32)
    s = 2
    in_base = same_block(s)
    invs = [eye - jnp.where(in_base, low, 0.0) for low in lows]
    while s < nil:
        newly = jnp.logical_and(same_block(2 * s), jnp.logical_not(same_block(s)))
        ts = [_mm(inv, jnp.where(newly, low, 0.0), NN, merge_mode) for inv, low in zip(invs, lows)]
        invs = [inv - _mm(t, inv, NN, merge_mode) for inv, t in zip(invs, ts)]
        s *= 2
    return invs


def _swiglu(n, wg_ref, wu_ref, wd_ref):
    gate = _dot(n, wg_ref[...])
    up = _dot(n, wu_ref[...])
    act = (gate * _sigmoid(gate) * up).astype(BF16)
    return _dot(act, wd_ref[...])


def _dense_in_kernel(x_ref, g1_ref, wg_ref, wu_ref, wd_ref, gm_ref, wa_ref, wb_ref,
                     h_ref, pa_ref, pb_ref):
    x = x_ref[...]
    h = x + 0.5 * _swiglu(_rms(x, g1_ref[...]).astype(BF16), wg_ref, wu_ref, wd_ref)
    h_ref[...] = h
    n = _rms(h, gm_ref[...]).astype(BF16)
    pa_ref[...] = _dot(n, wa_ref[...])
    pb_ref[...] = _dot(n, wb_ref[...])


def _dense_out_kernel(h_ref, oa_ref, ob_ref, woa_ref, wob_ref, g2_ref, wg_ref, wu_ref, wd_ref,
                      gf_ref, y_ref):
    h = (h_ref[...] + _dot(oa_ref[...].astype(BF16), woa_ref[...])
         + _dot(ob_ref[...].astype(BF16), wob_ref[...]))
    h = h + 0.5 * _swiglu(_rms(h, g2_ref[...]).astype(BF16), wg_ref, wu_ref, wd_ref)
    y_ref[...] = _rms(h, gf_ref[...])


def _const_spec(shape):
    return pl.BlockSpec(shape, lambda *_: (0,) * len(shape), pipeline_mode=pl.Buffered(1))


def _row_spec(tm, width):
    return pl.BlockSpec((tm, width), lambda i: (i, 0))


def _dense_in(x, g1, wg, wu, wd, gm, wa, wb, tm):
    n = x.shape[0]
    consts = (g1, wg, wu, wd, gm, wa, wb)
    return pl.pallas_call(
        _dense_in_kernel,
        grid=(n // tm,),
        in_specs=[_row_spec(tm, D_MODEL)] + [_const_spec(c.shape) for c in consts],
        out_specs=[_row_spec(tm, D_MODEL), _row_spec(tm, PA_W), _row_spec(tm, PB_W)],
        out_shape=[jax.ShapeDtypeStruct((n, D_MODEL), F32),
                   jax.ShapeDtypeStruct((n, PA_W), F32),
                   jax.ShapeDtypeStruct((n, PB_W), F32)],
        compiler_params=pltpu.CompilerParams(dimension_semantics=("arbitrary",),
                                             vmem_limit_bytes=VMEM_LIMIT),
        name="dense_in",
    )(x, *consts)


def _dense_out(h, oa, ob, woa, wob, g2, wg, wu, wd, gf, tm):
    n = h.shape[0]
    consts = (woa, wob, g2, wg, wu, wd, gf)
    return pl.pallas_call(
        _dense_out_kernel,
        grid=(n // tm,),
        in_specs=[_row_spec(tm, D_MODEL), _row_spec(tm, W_A), _row_spec(tm, W_B)]
        + [_const_spec(c.shape) for c in consts],
        out_specs=_row_spec(tm, D_MODEL),
        out_shape=jax.ShapeDtypeStruct((n, D_MODEL), F32),
        compiler_params=pltpu.CompilerParams(dimension_semantics=("arbitrary",),
                                             vmem_limit_bytes=VMEM_LIMIT),
        name="dense_out",
    )(h, oa, ob, *consts)


RWKV_PREC = dict(cum="xr", seg="b", pair="b", invm="b", sread="x3", akv="b", solve="x3", inter="b",
                 state="x3")


def _rwkv_kernel(pa_ref, prev_ref, s0_ref, mu_ref, w0_ref, wdu_ref, a0_ref, wau_ref, wgu_ref,
                 kk_ref, ka_ref, rk_ref, lnw_ref, lnb_ref, o_ref, sout_ref, xbuf, s_scr,
                 *, chunk, nvalid, nseq, shared_init):
    c = pl.program_id(1)
    C = chunk
    P = RWKV_PREC

    @pl.when(c == 0)
    def _():
        for i in range(nseq):
            j = 0 if shared_init else i
            xbuf[i, 0:SUBLANE, :] = prev_ref[j]
            s_scr[i] = s0_ref[j]

    tri = (_iota2((C, C), 0) >= _iota2((C, C), 1)).astype(F32)
    mid = C // 2 - 1

    def prep(i):
        x = pa_ref[i]
        xbuf[i, SUBLANE:SUBLANE + C, :] = x
        prev = xbuf[i, SUBLANE - 1:SUBLANE - 1 + C, :]
        xm = x + (prev - x) * mu_ref[...]
        xbuf[i, 0:SUBLANE, :] = xbuf[i, C:C + SUBLANE, :]

        r = xm[:, 0:W_A]
        k = xm[:, W_A:2 * W_A]
        v = xm[:, 2 * W_A:3 * W_A]
        wd = xm[:, PA_DECAY:PA_DECAY + LANE]
        ad = xm[:, PA_AAA:PA_AAA + LANE]
        gd = xm[:, PA_GATE:PA_GATE + LANE]

        wl = w0_ref[...] + _dot(jnp.tanh(wd).astype(BF16), wdu_ref[...])
        lw = -jnp.exp(-_softplus(-wl) - 0.5)
        a = _sigmoid(a0_ref[...] + _dot(ad.astype(BF16), wau_ref[...]))
        g = _dot(_sigmoid(gd).astype(BF16), wgu_ref[...])
        kkr = k * kk_ref[...]
        k2 = k * (1.0 + (a - 1.0) * ka_ref[...])
        if nvalid < C:
            valid = _iota2((C, 1), 0) < nvalid
            lw = jnp.where(valid, lw, 0.0)
            kkr = jnp.where(valid, kkr, 0.0)
            k2 = jnp.where(valid, k2, 0.0)
        G = _mm(tri, lw, NN, P["cum"])
        return dict(r=r, v=v, a=a, g=g, kkr=kkr, k2=k2, G=G, Gx=G - lw, Gm=G[mid:mid + 1, :],
                    Gc=G[C - 1:C, :])

    seqs = [prep(i) for i in range(nseq)]

    C2 = 2 * C
    trow = _iota2((C2, C2), 0) % C
    tcol = _iota2((C2, C2), 1) % C
    incl = trow >= tcol
    strict = trow > tcol
    lane = _iota2((1, LANE), 1)
    m0 = (lane < N_A).astype(F32)
    m1 = (lane >= N_A).astype(F32)
    seg = ((_iota2((LANE, LANE), 0) // N_A) == (_iota2((LANE, LANE), 1) // N_A)).astype(F32)
    stack = lambda t: jnp.concatenate([t * m0, t * m1], axis=0)
    merged = C2 % LANE == 0

    inst = [(i, p) for i in range(nseq) for p in range(H_A // 2)]
    n = range(len(inst))
    sls = [slice(p * LANE, (p + 1) * LANE) for _, p in inst]
    take = lambda name: [seqs[i][name][:, sls[j]] for j, (i, _) in enumerate(inst)]
    rs, vs_, k2s, kkrs, as_, gs = take("r"), take("v"), take("k2"), take("kkr"), take("a"), take("g")
    Gs, Gxs, Gms, Gcs = take("G"), take("Gx"), take("Gm"), take("Gc")
    Ss = [s_scr[i, p] for i, p in inst]

    ssq = [_mm(t * t, seg, NN, P["seg"]) for t in kkrs]
    kks = [t * lax.rsqrt(q + 1e-12) for t, q in zip(kkrs, ssq)]
    kkas = [kks[j] * as_[j] for j in n]
    inv_rel = [jnp.exp(Gms[j] - Gs[j]) for j in n]
    dec = [jnp.exp(Gcs[j] - Gs[j]) for j in n]
    lhs = [jnp.concatenate([stack(kks[j] * jnp.exp(Gxs[j] - Gms[j])),
                            stack(rs[j] * jnp.exp(Gs[j] - Gms[j]))], axis=0) for j in n]
    ais = [stack(kkas[j] * inv_rel[j]) for j in n]
    kis = [stack(k2s[j] * inv_rel[j]) for j in n]
    if merged:
        pm = [_mm(lhs[j], jnp.concatenate([ais[j], kis[j]], axis=0), NT, P["pair"]) for j in n]
        pas_, pks_ = [t[:, :C2] for t in pm], [t[:, C2:] for t in pm]
    else:
        pas_ = [_mm(lhs[j], ais[j], NT, P["pair"]) for j in n]
        pks_ = [_mm(lhs[j], kis[j], NT, P["pair"]) for j in n]
    a_aa = [jnp.where(strict, t[:C2], 0.0) for t in pas_]
    a_ak = [jnp.where(strict, t[:C2], 0.0) for t in pks_]
    a_ra = [jnp.where(incl, t[C2:], 0.0) for t in pas_]
    a_rk = [jnp.where(incl, t[C2:], 0.0) for t in pks_]
    ainv = _inv_unit_lower(a_aa, C, P["invm"])

    vst = [stack(t) for t in vs_]
    akv = [_mm(a_ak[j], vst[j], NN, P["akv"]) for j in n]
    sx = [_mm(jnp.concatenate([kks[j] * jnp.exp(Gxs[j]), rs[j] * jnp.exp(Gs[j])], axis=0), Ss[j], NT,
              P["sread"]) for j in n]
    us = [-_mm(ainv[j], stack(sx[j][:C]) + akv[j], NN, P["solve"]) for j in n]
    uv = [jnp.concatenate([us[j], vst[j]], axis=0) for j in n]
    s_new = [Ss[j] * jnp.exp(Gcs[j])
             + _mm(uv[j], jnp.concatenate([stack(kkas[j] * dec[j]), stack(k2s[j] * dec[j])], axis=0),
                   TN, P["state"]) for j in n]
    for j, (i, p) in enumerate(inst):
        s_scr[i, p] = s_new[j]
    if merged:
        inter = [_mm(jnp.concatenate([a_ra[j], a_rk[j]], axis=1), uv[j], NN, P["inter"]) for j in n]
    else:
        inter = [_mm(a_ra[j], us[j], NN, P["inter"]) + _mm(a_rk[j], vst[j], NN, P["inter"]) for j in n]
    Os = [sx[j][C:] + inter[j][:C] + inter[j][C:] for j in n]
    means = [_mm(t, seg, NN, P["seg"]) * (1.0 / N_A) for t in Os]
    dlts = [t - m for t, m in zip(Os, means)]
    vars_ = [_mm(t * t, seg, NN, P["seg"]) * (1.0 / N_A) for t in dlts]
    bonus = [_mm(rs[j] * k2s[j] * rk_ref[:, sls[j]], seg, NN, P["seg"]) * vs_[j] for j in n]
    for j, (i, p) in enumerate(inst):
        on = dlts[j] * lax.rsqrt(vars_[j] + LNX_EPS) * lnw_ref[:, sls[j]] + lnb_ref[:, sls[j]]
        o_ref[i, :, sls[j]] = (on + bonus[j]) * gs[j]

    @pl.when(c == pl.num_programs(1) - 1)
    def _():
        sout_ref[...] = s_scr[...]


def _seq_specs(nseq, shared_init, hist_w, state_shape):
    zeros = (0,) * len(state_shape)
    if shared_init:
        return [pl.BlockSpec((1, SUBLANE, hist_w), lambda b, c: (0, 0, 0)),
                pl.BlockSpec((1,) + state_shape, lambda b, c: (0,) + zeros)]
    return [pl.BlockSpec((nseq, SUBLANE, hist_w), lambda b, c: (b, 0, 0)),
            pl.BlockSpec((nseq,) + state_shape, lambda b, c: (b,) + zeros)]


def _rwkv(pa, prev8, s0, consts, chunk, nvalid, nseq):
    B, L, _ = pa.shape
    shared_init = prev8.shape[0] == 1 and B > 1
    state_shape = (H_A // 2, LANE, LANE)
    return pl.pallas_call(
        functools.partial(_rwkv_kernel, chunk=chunk, nvalid=nvalid, nseq=nseq, shared_init=shared_init),
        grid=(B // nseq, L // chunk),
        in_specs=[pl.BlockSpec((nseq, chunk, PA_W), lambda b, c: (b, c, 0))]
        + _seq_specs(nseq, shared_init, PA_W, state_shape)
        + [pl.BlockSpec(t.shape, lambda b, c: (0, 0)) for t in consts],
        out_specs=[pl.BlockSpec((nseq, chunk, W_A), lambda b, c: (b, c, 0)),
                   pl.BlockSpec((nseq,) + state_shape, lambda b, c: (b, 0, 0, 0))],
        out_shape=[jax.ShapeDtypeStruct((B, L, W_A), F32),
                   jax.ShapeDtypeStruct((B,) + state_shape, F32)],
        scratch_shapes=[pltpu.VMEM((nseq, SUBLANE + chunk, PA_W), F32),
                        pltpu.VMEM((nseq,) + state_shape, F32)],
        compiler_params=pltpu.CompilerParams(dimension_semantics=("arbitrary", "arbitrary"),
                                             vmem_limit_bytes=VMEM_LIMIT),
        name="rwkv_chunk%d" % chunk,
    )(pa, prev8, s0, *consts)


GDN_PREC = dict(cum="xr", tr="xr", kk="b", invm="b", wu="b", qk="b", sread="b", qkv="b", state="b")


def _gdn_kernel(pb_ref, hist_ref, s0_ref, cw_ref, alog_ref, dtb_ref, nw_ref, o_ref, sout_ref,
                xbuf, s_scr, *, chunk, nvalid, nseq, shared_init):
    c = pl.program_id(1)
    C = chunk
    P = GDN_PREC

    @pl.when(c == 0)
    def _():
        for i in range(nseq):
            j = 0 if shared_init else i
            xbuf[i, 0:SUBLANE, :] = hist_ref[j]
            s_scr[i] = s0_ref[j]

    row = _iota2((C, C), 0)
    col = _iota2((C, C), 1)
    incl = row >= col
    strict = row > col
    tri = incl.astype(F32)
    sel = (_iota2((SUBLANE, LANE), 0) == _iota2((SUBLANE, LANE), 1)).astype(F32)

    def prep(i):
        x = pb_ref[i]
        xbuf[i, SUBLANE:SUBLANE + C, :] = x[:, 0:3 * W_B]
        conv = xbuf[i, SUBLANE - 3:SUBLANE - 3 + C, :] * cw_ref[0:1, :]
        for j in range(1, CONV_W):
            off = SUBLANE - 3 + j
            conv = conv + xbuf[i, off:off + C, :] * cw_ref[j:j + 1, :]
        xbuf[i, 0:SUBLANE, :] = xbuf[i, C:C + SUBLANE, :]
        cs = conv * _sigmoid(conv)
        gates = x[:, PB_GATES:PB_GATES + LANE]
        glog = -jnp.exp(alog_ref[...]) * _softplus(gates + dtb_ref[...])
        beta = _sigmoid(gates)
        if nvalid < C:
            valid = _iota2((C, 1), 0) < nvalid
            glog = jnp.where(valid, glog, 0.0)
            beta = jnp.where(valid, beta, 0.0)
        G = _mm(tri, glog, NN, P["cum"])
        Gt = _mm(sel, G, NT, P["tr"])
        return dict(cs=cs, z=x[:, PB_Z:PB_Z + W_B], beta=beta, G=G, Gt=Gt)

    seqs = [prep(i) for i in range(nseq)]

    inst = [(i, h) for i in range(nseq) for h in range(H_B)]
    n = range(len(inst))
    l2n = lambda t: t * lax.rsqrt(jnp.sum(t * t, axis=-1, keepdims=True) + 1e-12)
    head = lambda i, h, part: seqs[i]["cs"][:, part * W_B + h * LANE:part * W_B + (h + 1) * LANE]
    qs = [l2n(head(i, h, 0)) * (DK_B ** -0.5) for i, h in inst]
    ks = [l2n(head(i, h, 1)) for i, h in inst]
    vs = [head(i, h, 2) for i, h in inst]
    gcols = [seqs[i]["G"][:, h:h + 1] for i, h in inst]
    grows = [seqs[i]["Gt"][h:h + 1, :] for i, h in inst]
    bcols = [seqs[i]["beta"][:, H_B + h:H_B + h + 1] for i, h in inst]
    Ss = [s_scr[i, h] for i, h in inst]

    dmats = [jnp.where(incl, jnp.exp(jnp.where(incl, gcols[j] - grows[j], 0.0)), 0.0) for j in n]
    kbs = [ks[j] * bcols[j] for j in n]
    lows = [jnp.where(strict, _mm(kbs[j], ks[j], NT, P["kk"]) * dmats[j], 0.0) for j in n]
    ainv = _inv_unit_lower(lows, C, P["invm"])
    egs = [jnp.exp(t) for t in gcols]
    wus = [_mm(ainv[j], jnp.concatenate([kbs[j] * egs[j], vs[j] * bcols[j]], axis=1), NN, P["wu"])
           for j in n]
    qks = [_mm(qs[j], ks[j], NT, P["qk"]) * dmats[j] for j in n]
    srs = [_mm(jnp.concatenate([wus[j][:, :DK_B], qs[j] * egs[j]], axis=0), Ss[j], NN, P["sread"])
           for j in n]
    v_new = [wus[j][:, DK_B:] - srs[j][:C] for j in n]
    glast = [t[C - 1:C, :] for t in gcols]
    s_new = [Ss[j] * jnp.exp(glast[j])
             + _mm(ks[j] * jnp.exp(glast[j] - gcols[j]), v_new[j], TN, P["state"]) for j in n]
    for j, (i, h) in enumerate(inst):
        s_scr[i, h] = s_new[j]
    os_ = [srs[j][C:] + _mm(qks[j], v_new[j], NN, P["qkv"]) for j in n]
    for j, (i, h) in enumerate(inst):
        o = os_[j]
        o = o * lax.rsqrt(jnp.mean(o * o, axis=-1, keepdims=True) + NORM_EPS) * nw_ref[...]
        zh = seqs[i]["z"][:, h * LANE:(h + 1) * LANE]
        o_ref[i, :, h * LANE:(h + 1) * LANE] = o * (zh * _sigmoid(zh))

    @pl.when(c == pl.num_programs(1) - 1)
    def _():
        sout_ref[...] = s_scr[...]


def _gdn(pb, hist8, s0, consts, chunk, nvalid, nseq):
    B, L, _ = pb.shape
    shared_init = hist8.shape[0] == 1 and B > 1
    state_shape = (H_B, DK_B, DV_B)
    return pl.pallas_call(
        functools.partial(_gdn_kernel, chunk=chunk, nvalid=nvalid, nseq=nseq, shared_init=shared_init),
        grid=(B // nseq, L // chunk),
        in_specs=[pl.BlockSpec((nseq, chunk, PB_W), lambda b, c: (b, c, 0))]
        + _seq_specs(nseq, shared_init, 3 * W_B, state_shape)
        + [pl.BlockSpec(t.shape, lambda b, c: (0, 0)) for t in consts],
        out_specs=[pl.BlockSpec((nseq, chunk, W_B), lambda b, c: (b, c, 0)),
                   pl.BlockSpec((nseq,) + state_shape, lambda b, c: (b, 0, 0, 0))],
        out_shape=[jax.ShapeDtypeStruct((B, L, W_B), F32),
                   jax.ShapeDtypeStruct((B,) + state_shape, F32)],
        scratch_shapes=[pltpu.VMEM((nseq, SUBLANE + chunk, 3 * W_B), F32),
                        pltpu.VMEM((nseq,) + state_shape, F32)],
        compiler_params=pltpu.CompilerParams(dimension_semantics=("arbitrary", "arbitrary"),
                                             vmem_limit_bytes=VMEM_LIMIT),
        name="gdn_chunk%d" % chunk,
    )(pb, hist8, s0, *consts)


def _pad_cols(t, width):
    return jnp.pad(t, [(0, 0)] * (t.ndim - 1) + [(0, width - t.shape[-1])])


def _pa_layout(t):
    main = t[..., :3 * W_A]
    dec = t[..., 3 * W_A:3 * W_A + D_DECAY]
    aaa = t[..., 3 * W_A + D_DECAY:3 * W_A + D_DECAY + D_AAA]
    gate = t[..., 3 * W_A + D_DECAY + D_AAA:]
    return jnp.concatenate([main, _pad_cols(dec, LANE), _pad_cols(aaa, LANE), _pad_cols(gate, LANE)],
                           axis=-1)


def _pa_unlayout(t):
    return jnp.concatenate([t[..., :3 * W_A], t[..., PA_DECAY:PA_DECAY + D_DECAY],
                            t[..., PA_AAA:PA_AAA + D_AAA], t[..., PA_GATE:PA_GATE + D_GATE]], axis=-1)


def _pad_rows(t, rows):
    return jnp.pad(t, [(0, rows - t.shape[0]), (0, 0)])


def _pair_blockdiag(s):
    B = s.shape[0]
    s = s.reshape(B, H_A // 2, 2, N_A, N_A)
    zero = jnp.zeros_like(s[:, :, 0])
    top = jnp.concatenate([s[:, :, 0], zero], axis=-1)
    bot = jnp.concatenate([zero, s[:, :, 1]], axis=-1)
    return jnp.concatenate([top, bot], axis=-2)


def _pair_unblock(s):
    B = s.shape[0]
    return jnp.stack([s[:, :, :N_A, :N_A], s[:, :, N_A:, N_A:]], axis=2).reshape(B, H_A, N_A, N_A)


def _history_rows(rows):
    B, n, w = rows.shape
    return jnp.concatenate([jnp.zeros((B, SUBLANE - n, w), F32), rows], axis=1)


def kernel(x_prompt, x_sample, state_rwkv, state_shift, state_gdn, state_conv, meta_tokens,
           g_ffn1, w_gate1, w_up1, w_down1, g_mix, w_in, mu_shift, w0, w_decay_up, a0, w_a_up,
           w_g_up, k_k, k_a, r_k, lnx_w, lnx_b, conv_w, a_log, dt_bias, gdn_norm_w, w_out,
           g_ffn2, w_gate2, w_up2, w_down2, g_final):
    assert g_ffn1.shape[0] == 1, "single trunk layer"
    bp, sp, _ = x_prompt.shape
    bs = x_sample.shape[0]
    assert x_sample.shape[1] == 1 and sp % CHUNK_RWKV == 0 and sp % CHUNK_GDN == 0
    assert (bp * sp) % TM_DENSE == 0 and bp % SEQS_PER_STEP == 0 and bs % DEC_SEQS_PER_STEP == 0
    row = lambda t: t.reshape(1, -1).astype(F32)

    ffn1 = (w_gate1[0].astype(BF16), w_up1[0].astype(BF16), w_down1[0].astype(BF16))
    ffn2 = (w_gate2[0].astype(BF16), w_up2[0].astype(BF16), w_down2[0].astype(BF16))
    win_a = _pa_layout(w_in[0][:, :N_A_IN]).astype(BF16)
    win_b = _pad_cols(w_in[0][:, N_A_IN:], PB_W).astype(BF16)
    wo_a = w_out[0][:W_A].astype(BF16)
    wo_b = w_out[0][W_A:].astype(BF16)
    dense_in_consts = (row(g_ffn1[0]), *ffn1, row(g_mix[0]), win_a, win_b)
    dense_out_consts = (wo_a, wo_b, row(g_ffn2[0]), *ffn2, row(g_final))
    rwkv_consts = (_pa_layout(row(mu_shift[0])), row(w0[0]),
                   _pad_rows(w_decay_up[0], LANE).astype(BF16), row(a0[0]),
                   _pad_rows(w_a_up[0], LANE).astype(BF16), _pad_rows(w_g_up[0], LANE).astype(BF16),
                   row(k_k[0]), row(k_a[0]), row(r_k[0]), row(lnx_w[0]), row(lnx_b[0]))
    gdn_consts = (conv_w[0].astype(F32), _pad_cols(row(a_log[0]), LANE), _pad_cols(row(dt_bias[0]), LANE),
                  row(gdn_norm_w[0]))

    xs = jnp.concatenate([x_sample[:, 0, :].astype(F32), meta_tokens.astype(F32)], axis=0)
    hs, pas, pbs = _dense_in(xs, *dense_in_consts, tm=xs.shape[0])

    pa_meta = pas[bs:][None]
    pb_meta = pbs[bs:][None]
    _, rw_meta = _rwkv(pa_meta, jnp.zeros((1, SUBLANE, PA_W), F32),
                       jnp.zeros((1, H_A // 2, LANE, LANE), F32), rwkv_consts, N_META, N_META, 1)
    _, gd_meta = _gdn(pb_meta, jnp.zeros((1, SUBLANE, 3 * W_B), F32),
                      jnp.zeros((1, H_B, DK_B, DV_B), F32), gdn_consts, N_META, N_META, 1)

    hp, pap, pbp = _dense_in(x_prompt.reshape(bp * sp, D_MODEL).astype(F32), *dense_in_consts,
                             tm=TM_DENSE)
    pap3 = pap.reshape(bp, sp, PA_W)
    pbp3 = pbp.reshape(bp, sp, PB_W)
    oa_p, rw_p = _rwkv(pap3, _history_rows(pa_meta[:, -1:, :]), rw_meta, rwkv_consts,
                       CHUNK_RWKV, CHUNK_RWKV, SEQS_PER_STEP)
    ob_p, gd_p = _gdn(pbp3, _history_rows(pb_meta[:, -(CONV_W - 1):, :3 * W_B]), gd_meta, gdn_consts,
                      CHUNK_GDN, CHUNK_GDN, SEQS_PER_STEP)
    y_p = _dense_out(hp, oa_p.reshape(bp * sp, W_A), ob_p.reshape(bp * sp, W_B), *dense_out_consts,
                     tm=TM_DENSE)

    pa_s = jnp.pad(pas[:bs, None, :], ((0, 0), (0, DEC_CHUNK - 1), (0, 0)))
    pb_s = jnp.pad(pbs[:bs, None, :], ((0, 0), (0, DEC_CHUNK - 1), (0, 0)))
    oa_s, rw_s = _rwkv(pa_s, _history_rows(_pa_layout(state_shift[0].astype(F32))[:, None, :]),
                       _pair_blockdiag(state_rwkv[0].astype(F32)), rwkv_consts, DEC_CHUNK, 1,
                       DEC_SEQS_PER_STEP)
    ob_s, gd_s = _gdn(pb_s, _history_rows(state_conv[0].astype(F32)), state_gdn[0].astype(F32),
                      gdn_consts, DEC_CHUNK, 1, DEC_SEQS_PER_STEP)
    y_s = _dense_out(hs[:bs], oa_s[:, 0, :], ob_s[:, 0, :], *dense_out_consts, tm=bs)

    new_conv_s = jnp.concatenate([state_conv[0].astype(F32)[:, 1:, :], pbs[:bs, None, :3 * W_B]], axis=1)
    return (y_p.reshape(bp, sp, D_MODEL).astype(x_prompt.dtype),
            y_s.reshape(bs, 1, D_MODEL).astype(x_sample.dtype),
            _pair_unblock(rw_p)[None],
            _pa_unlayout(pap3[:, -1, :])[None],
            gd_p[None],
            pbp3[:, -(CONV_W - 1):, :3 * W_B][None],
            _pair_unblock(rw_s)[None],
            _pa_unlayout(pas[:bs])[None],
            gd_s[None],
            new_conv_s[None])
```

```python
import functools

import jax
import jax.numpy as jnp
from jax import lax
from jax.experimental import pallas as pl
from jax.experimental.pallas import tpu as pltpu

F32 = jnp.float32
BF16 = jnp.bfloat16

LANE = 128
SUBLANE = 8
VMEM_LIMIT = 56 * 1024 * 1024

D_MODEL = 1024
D_FF = 2816
N_META = 16
H_A, N_A = 8, 64
W_A = H_A * N_A
D_DECAY, D_AAA, D_GATE = 32, 32, 96
N_A_IN = 3 * W_A + D_DECAY + D_AAA + D_GATE
H_B, DK_B, DV_B = 4, 128, 128
W_B = H_B * DV_B
CONV_W = 4
N_B_IN = 4 * W_B + 2 * H_B
LNX_EPS = 64e-5
NORM_EPS = 1e-6

PA_W = 3 * W_A + 3 * LANE
PA_DECAY = 3 * W_A
PA_AAA = PA_DECAY + LANE
PA_GATE = PA_AAA + LANE
PB_W = 4 * W_B + LANE
PB_Z = 3 * W_B
PB_GATES = 4 * W_B

TM_DENSE = 256
CHUNK_RWKV = 64
CHUNK_GDN = 128
SEQS_PER_STEP = 4
DEC_SEQS_PER_STEP = SUBLANE

NN = (((1,), (0,)), ((), ()))
NT = (((1,), (1,)), ((), ()))
TN = (((0,), (0,)), ((), ()))


def _dot(a, b):
    return jnp.dot(a, b, preferred_element_type=F32)


def _sigmoid(x):
    return 1.0 / (1.0 + jnp.exp(-x))


def _softplus(x):
    return jnp.maximum(x, 0.0) + jnp.log(1.0 + jnp.exp(-jnp.abs(x)))


def _rms(x, g):
    return x * lax.rsqrt(jnp.mean(x * x, axis=-1, keepdims=True) + NORM_EPS) * g


def _iota2(shape, dim):
    return lax.broadcasted_iota(jnp.int32, shape, dim)


def _split(x):
    hi = x.astype(BF16)
    return hi, (x - hi.astype(F32)).astype(BF16)


def _dg(a, b, dims):
    return lax.dot_general(a, b, dims, preferred_element_type=F32)


def _mm(a, b, dims=NN, mode="b"):
    if mode == "b":
        return _dg(a.astype(BF16), b.astype(BF16), dims)
    if mode == "xl":
        ah, al = _split(a)
        bh = b.astype(BF16)
        return _dg(ah, bh, dims) + _dg(al, bh, dims)
    if mode == "xr":
        ah = a.astype(BF16)
        bh, bl = _split(b)
        return _dg(ah, bh, dims) + _dg(ah, bl, dims)
    assert mode == "x3"
    ah, al = _split(a)
    bh, bl = _split(b)
    return _dg(ah, bh, dims) + (_dg(ah, bl, dims) + _dg(al, bh, dims))


def _inv_unit_lower(lows, nil, merge_mode):
    n = lows[0].shape[0]
    row = _iota2((n, n), 0)
    col = _iota2((n, n), 1)
    same_block = lambda s: (row // s) == (col // s)
    eye = (row == col).astype(F32)
    s = 2
    in_base = same_block(s)
    invs = [eye - jnp.where(in_base, low, 0.0) for low in lows]
    while s < nil:
        newly = jnp.logical_and(same_block(2 * s), jnp.logical_not(same_block(s)))
        ts = [_mm(inv, jnp.where(newly, low, 0.0), NN, merge_mode) for inv, low in zip(invs, lows)]
        invs = [inv - _mm(t, inv, NN, merge_mode) for inv, t in zip(invs, ts)]
        s *= 2
    return invs


def _swiglu(n, wg_ref, wu_ref, wd_ref):
    gate = _dot(n, wg_ref[...])
    up = _dot(n, wu_ref[...])
    act = (gate * _sigmoid(gate) * up).astype(BF16)
    return _dot(act, wd_ref[...])


def _dense_in_kernel(x_ref, g1_ref, wg_ref, wu_ref, wd_ref, gm_ref, wa_ref, wb_ref,
                     h_ref, pa_ref, pb_ref):
    x = x_ref[...]
    h = x + 0.5 * _swiglu(_rms(x, g1_ref[...]).astype(BF16), wg_ref, wu_ref, wd_ref)
    h_ref[...] = h
    n = _rms(h, gm_ref[...]).astype(BF16)
    pa_ref[...] = _dot(n, wa_ref[...])
    pb_ref[...] = _dot(n, wb_ref[...])


def _dense_out_kernel(h_ref, oa_ref, ob_ref, woa_ref, wob_ref, g2_ref, wg_ref, wu_ref, wd_ref,
                      gf_ref, y_ref):
    h = (h_ref[...] + _dot(oa_ref[...].astype(BF16), woa_ref[...])
         + _dot(ob_ref[...].astype(BF16), wob_ref[...]))
    h = h + 0.5 * _swiglu(_rms(h, g2_ref[...]).astype(BF16), wg_ref, wu_ref, wd_ref)
    y_ref[...] = _rms(h, gf_ref[...])


def _const_spec(shape):
    return pl.BlockSpec(shape, lambda *_: (0,) * len(shape), pipeline_mode=pl.Buffered(1))


def _row_spec(tm, width):
    return pl.BlockSpec((tm, width), lambda i: (i, 0))


def _dense_in(x, g1, wg, wu, wd, gm, wa, wb, tm):
    n = x.shape[0]
    consts = (g1, wg, wu, wd, gm, wa, wb)
    return pl.pallas_call(
        _dense_in_kernel,
        grid=(n // tm,),
        in_specs=[_row_spec(tm, D_MODEL)] + [_const_spec(c.shape) for c in consts],
        out_specs=[_row_spec(tm, D_MODEL), _row_spec(tm, PA_W), _row_spec(tm, PB_W)],
        out_shape=[jax.ShapeDtypeStruct((n, D_MODEL), F32),
                   jax.ShapeDtypeStruct((n, PA_W), F32),
                   jax.ShapeDtypeStruct((n, PB_W), F32)],
        compiler_params=pltpu.CompilerParams(dimension_semantics=("arbitrary",),
                                             vmem_limit_bytes=VMEM_LIMIT),
        name="dense_in",
    )(x, *consts)


def _dense_out(h, oa, ob, woa, wob, g2, wg, wu, wd, gf, tm):
    n = h.shape[0]
    consts = (woa, wob, g2, wg, wu, wd, gf)
    return pl.pallas_call(
        _dense_out_kernel,
        grid=(n // tm,),
        in_specs=[_row_spec(tm, D_MODEL), _row_spec(tm, W_A), _row_spec(tm, W_B)]
        + [_const_spec(c.shape) for c in consts],
        out_specs=_row_spec(tm, D_MODEL),
        out_shape=jax.ShapeDtypeStruct((n, D_MODEL), F32),
        compiler_params=pltpu.CompilerParams(dimension_semantics=("arbitrary",),
                                             vmem_limit_bytes=VMEM_LIMIT),
        name="dense_out",
    )(h, oa, ob, *consts)


RWKV_PREC = dict(cum="xr", seg="b", pair="b", invm="b", sread="x3", akv="b", solve="x3", inter="b",
                 state="x3")


def _rwkv_kernel(pa_ref, prev_ref, s0_ref, mu_ref, w0_ref, wdu_ref, a0_ref, wau_ref, wgu_ref,
                 kk_ref, ka_ref, rk_ref, lnw_ref, lnb_ref, o_ref, sout_ref, xbuf, s_scr,
                 *, chunk, nseq, shared_init):
    c = pl.program_id(1)
    C = chunk
    P = RWKV_PREC

    @pl.when(c == 0)
    def _():
        for i in range(nseq):
            j = 0 if shared_init else i
            xbuf[i, 0:SUBLANE, :] = prev_ref[j]
            s_scr[i] = s0_ref[j]

    tri = (_iota2((C, C), 0) >= _iota2((C, C), 1)).astype(F32)
    mid = C // 2 - 1

    def prep(i):
        x = pa_ref[i]
        xbuf[i, SUBLANE:SUBLANE + C, :] = x
        prev = xbuf[i, SUBLANE - 1:SUBLANE - 1 + C, :]
        xm = x + (prev - x) * mu_ref[...]
        xbuf[i, 0:SUBLANE, :] = xbuf[i, C:C + SUBLANE, :]

        r = xm[:, 0:W_A]
        k = xm[:, W_A:2 * W_A]
        v = xm[:, 2 * W_A:3 * W_A]
        wd = xm[:, PA_DECAY:PA_DECAY + LANE]
        ad = xm[:, PA_AAA:PA_AAA + LANE]
        gd = xm[:, PA_GATE:PA_GATE + LANE]

        wl = w0_ref[...] + _dot(jnp.tanh(wd).astype(BF16), wdu_ref[...])
        lw = -jnp.exp(-_softplus(-wl) - 0.5)
        a = _sigmoid(a0_ref[...] + _dot(ad.astype(BF16), wau_ref[...]))
        g = _dot(_sigmoid(gd).astype(BF16), wgu_ref[...])
        kkr = k * kk_ref[...]
        k2 = k * (1.0 + (a - 1.0) * ka_ref[...])
        G = _mm(tri, lw, NN, P["cum"])
        return dict(r=r, v=v, a=a, g=g, kkr=kkr, k2=k2, G=G, Gx=G - lw, Gm=G[mid:mid + 1, :],
                    Gc=G[C - 1:C, :])

    seqs = [prep(i) for i in range(nseq)]

    C2 = 2 * C
    trow = _iota2((C2, C2), 0) % C
    tcol = _iota2((C2, C2), 1) % C
    incl = trow >= tcol
    strict = trow > tcol
    lane = _iota2((1, LANE), 1)
    m0 = (lane < N_A).astype(F32)
    m1 = (lane >= N_A).astype(F32)
    seg = ((_iota2((LANE, LANE), 0) // N_A) == (_iota2((LANE, LANE), 1) // N_A)).astype(F32)
    stack = lambda t: jnp.concatenate([t * m0, t * m1], axis=0)
    merged = C2 % LANE == 0

    inst = [(i, p) for i in range(nseq) for p in range(H_A // 2)]
    n = range(len(inst))
    sls = [slice(p * LANE, (p + 1) * LANE) for _, p in inst]
    take = lambda name: [seqs[i][name][:, sls[j]] for j, (i, _) in enumerate(inst)]
    rs, vs_, k2s, kkrs, as_, gs = take("r"), take("v"), take("k2"), take("kkr"), take("a"), take("g")
    Gs, Gxs, Gms, Gcs = take("G"), take("Gx"), take("Gm"), take("Gc")
    Ss = [s_scr[i, p] for i, p in inst]

    ssq = [_mm(t * t, seg, NN, P["seg"]) for t in kkrs]
    kks = [t * lax.rsqrt(q + 1e-12) for t, q in zip(kkrs, ssq)]
    kkas = [kks[j] * as_[j] for j in n]
    inv_rel = [jnp.exp(Gms[j] - Gs[j]) for j in n]
    dec = [jnp.exp(Gcs[j] - Gs[j]) for j in n]
    lhs = [jnp.concatenate([stack(kks[j] * jnp.exp(Gxs[j] - Gms[j])),
                            stack(rs[j] * jnp.exp(Gs[j] - Gms[j]))], axis=0) for j in n]
    ais = [stack(kkas[j] * inv_rel[j]) for j in n]
    kis = [stack(k2s[j] * inv_rel[j]) for j in n]
    if merged:
        pm = [_mm(lhs[j], jnp.concatenate([ais[j], kis[j]], axis=0), NT, P["pair"]) for j in n]
        pas_, pks_ = [t[:, :C2] for t in pm], [t[:, C2:] for t in pm]
    else:
        pas_ = [_mm(lhs[j], ais[j], NT, P["pair"]) for j in n]
        pks_ = [_mm(lhs[j], kis[j], NT, P["pair"]) for j in n]
    a_aa = [jnp.where(strict, t[:C2], 0.0) for t in pas_]
    a_ak = [jnp.where(strict, t[:C2], 0.0) for t in pks_]
    a_ra = [jnp.where(incl, t[C2:], 0.0) for t in pas_]
    a_rk = [jnp.where(incl, t[C2:], 0.0) for t in pks_]
    ainv = _inv_unit_lower(a_aa, C, P["invm"])

    vst = [stack(t) for t in vs_]
    akv = [_mm(a_ak[j], vst[j], NN, P["akv"]) for j in n]
    sx = [_mm(jnp.concatenate([kks[j] * jnp.exp(Gxs[j]), rs[j] * jnp.exp(Gs[j])], axis=0), Ss[j], NT,
              P["sread"]) for j in n]
    us = [-_mm(ainv[j], stack(sx[j][:C]) + akv[j], NN, P["solve"]) for j in n]
    uv = [jnp.concatenate([us[j], vst[j]], axis=0) for j in n]
    s_new = [Ss[j] * jnp.exp(Gcs[j])
             + _mm(uv[j], jnp.concatenate([stack(kkas[j] * dec[j]), stack(k2s[j] * dec[j])], axis=0),
                   TN, P["state"]) for j in n]
    for j, (i, p) in enumerate(inst):
        s_scr[i, p] = s_new[j]
    if merged:
        inter = [_mm(jnp.concatenate([a_ra[j], a_rk[j]], axis=1), uv[j], NN, P["inter"]) for j in n]
    else:
        inter = [_mm(a_ra[j], us[j], NN, P["inter"]) + _mm(a_rk[j], vst[j], NN, P["inter"]) for j in n]
    Os = [sx[j][C:] + inter[j][:C] + inter[j][C:] for j in n]
    means = [_mm(t, seg, NN, P["seg"]) * (1.0 / N_A) for t in Os]
    dlts = [t - m for t, m in zip(Os, means)]
    vars_ = [_mm(t * t, seg, NN, P["seg"]) * (1.0 / N_A) for t in dlts]
    bonus = [_mm(rs[j] * k2s[j] * rk_ref[:, sls[j]], seg, NN, P["seg"]) * vs_[j] for j in n]
    for j, (i, p) in enumerate(inst):
        on = dlts[j] * lax.rsqrt(vars_[j] + LNX_EPS) * lnw_ref[:, sls[j]] + lnb_ref[:, sls[j]]
        o_ref[i, :, sls[j]] = (on + bonus[j]) * gs[j]

    @pl.when(c == pl.num_programs(1) - 1)
    def _():
        sout_ref[...] = s_scr[...]


def _seq_specs(nseq, shared_init, hist_w, state_shape):
    zeros = (0,) * len(state_shape)
    if shared_init:
        return [pl.BlockSpec((1, SUBLANE, hist_w), lambda b, c: (0, 0, 0)),
                pl.BlockSpec((1,) + state_shape, lambda b, c: (0,) + zeros)]
    return [pl.BlockSpec((nseq, SUBLANE, hist_w), lambda b, c: (b, 0, 0)),
            pl.BlockSpec((nseq,) + state_shape, lambda b, c: (b,) + zeros)]


def _rwkv(pa, prev8, s0, consts, chunk, nseq):
    B, L, _ = pa.shape
    shared_init = prev8.shape[0] == 1 and B > 1
    state_shape = (H_A // 2, LANE, LANE)
    return pl.pallas_call(
        functools.partial(_rwkv_kernel, chunk=chunk, nseq=nseq, shared_init=shared_init),
        grid=(B // nseq, L // chunk),
        in_specs=[pl.BlockSpec((nseq, chunk, PA_W), lambda b, c: (b, c, 0))]
        + _seq_specs(nseq, shared_init, PA_W, state_shape)
        + [pl.BlockSpec(t.shape, lambda b, c: (0, 0)) for t in consts],
        out_specs=[pl.BlockSpec((nseq, chunk, W_A), lambda b, c: (b, c, 0)),
                   pl.BlockSpec((nseq,) + state_shape, lambda b, c: (b, 0, 0, 0))],
        out_shape=[jax.ShapeDtypeStruct((B, L, W_A), F32),
                   jax.ShapeDtypeStruct((B,) + state_shape, F32)],
        scratch_shapes=[pltpu.VMEM((nseq, SUBLANE + chunk, PA_W), F32),
                        pltpu.VMEM((nseq,) + state_shape, F32)],
        compiler_params=pltpu.CompilerParams(dimension_semantics=("arbitrary", "arbitrary"),
                                             vmem_limit_bytes=VMEM_LIMIT),
        name="rwkv_chunk%d" % chunk,
    )(pa, prev8, s0, *consts)


GDN_PREC = dict(cum="xr", tr="xr", kk="b", invm="b", wu="b", qk="b", sread="b", qkv="b", state="b")


def _gdn_kernel(pb_ref, hist_ref, s0_ref, cw_ref, alog_ref, dtb_ref, nw_ref, o_ref, sout_ref,
                xbuf, s_scr, *, chunk, nseq, shared_init):
    c = pl.program_id(1)
    C = chunk
    P = GDN_PREC

    @pl.when(c == 0)
    def _():
        for i in range(nseq):
            j = 0 if shared_init else i
            xbuf[i, 0:SUBLANE, :] = hist_ref[j]
            s_scr[i] = s0_ref[j]

    row = _iota2((C, C), 0)
    col = _iota2((C, C), 1)
    incl = row >= col
    strict = row > col
    tri = incl.astype(F32)
    sel = (_iota2((SUBLANE, LANE), 0) == _iota2((SUBLANE, LANE), 1)).astype(F32)

    def prep(i):
        x = pb_ref[i]
        xbuf[i, SUBLANE:SUBLANE + C, :] = x[:, 0:3 * W_B]
        conv = xbuf[i, SUBLANE - 3:SUBLANE - 3 + C, :] * cw_ref[0:1, :]
        for j in range(1, CONV_W):
            off = SUBLANE - 3 + j
            conv = conv + xbuf[i, off:off + C, :] * cw_ref[j:j + 1, :]
        xbuf[i, 0:SUBLANE, :] = xbuf[i, C:C + SUBLANE, :]
        cs = conv * _sigmoid(conv)
        gates = x[:, PB_GATES:PB_GATES + LANE]
        glog = -jnp.exp(alog_ref[...]) * _softplus(gates + dtb_ref[...])
        beta = _sigmoid(gates)
        G = _mm(tri, glog, NN, P["cum"])
        Gt = _mm(sel, G, NT, P["tr"])
        return dict(cs=cs, z=x[:, PB_Z:PB_Z + W_B], beta=beta, G=G, Gt=Gt)

    seqs = [prep(i) for i in range(nseq)]

    inst = [(i, h) for i in range(nseq) for h in range(H_B)]
    n = range(len(inst))
    l2n = lambda t: t * lax.rsqrt(jnp.sum(t * t, axis=-1, keepdims=True) + 1e-12)
    head = lambda i, h, part: seqs[i]["cs"][:, part * W_B + h * LANE:part * W_B + (h + 1) * LANE]
    qs = [l2n(head(i, h, 0)) * (DK_B ** -0.5) for i, h in inst]
    ks = [l2n(head(i, h, 1)) for i, h in inst]
    vs = [head(i, h, 2) for i, h in inst]
    gcols = [seqs[i]["G"][:, h:h + 1] for i, h in inst]
    grows = [seqs[i]["Gt"][h:h + 1, :] for i, h in inst]
    bcols = [seqs[i]["beta"][:, H_B + h:H_B + h + 1] for i, h in inst]
    Ss = [s_scr[i, h] for i, h in inst]

    dmats = [jnp.where(incl, jnp.exp(jnp.where(incl, gcols[j] - grows[j], 0.0)), 0.0) for j in n]
    kbs = [ks[j] * bcols[j] for j in n]
    lows = [jnp.where(strict, _mm(kbs[j], ks[j], NT, P["kk"]) * dmats[j], 0.0) for j in n]
    ainv = _inv_unit_lower(lows, C, P["invm"])
    egs = [jnp.exp(t) for t in gcols]
    wus = [_mm(ainv[j], jnp.concatenate([kbs[j] * egs[j], vs[j] * bcols[j]], axis=1), NN, P["wu"])
           for j in n]
    qks = [_mm(qs[j], ks[j], NT, P["qk"]) * dmats[j] for j in n]
    srs = [_mm(jnp.concatenate([wus[j][:, :DK_B], qs[j] * egs[j]], axis=0), Ss[j], NN, P["sread"])
           for j in n]
    v_new = [wus[j][:, DK_B:] - srs[j][:C] for j in n]
    glast = [t[C - 1:C, :] for t in gcols]
    s_new = [Ss[j] * jnp.exp(glast[j])
             + _mm(ks[j] * jnp.exp(glast[j] - gcols[j]), v_new[j], TN, P["state"]) for j in n]
    for j, (i, h) in enumerate(inst):
        s_scr[i, h] = s_new[j]
    os_ = [srs[j][C:] + _mm(qks[j], v_new[j], NN, P["qkv"]) for j in n]
    for j, (i, h) in enumerate(inst):
        o = os_[j]
        o = o * lax.rsqrt(jnp.mean(o * o, axis=-1, keepdims=True) + NORM_EPS) * nw_ref[...]
        zh = seqs[i]["z"][:, h * LANE:(h + 1) * LANE]
        o_ref[i, :, h * LANE:(h + 1) * LANE] = o * (zh * _sigmoid(zh))

    @pl.when(c == pl.num_programs(1) - 1)
    def _():
        sout_ref[...] = s_scr[...]


def _gdn(pb, hist8, s0, consts, chunk, nseq):
    B, L, _ = pb.shape
    shared_init = hist8.shape[0] == 1 and B > 1
    state_shape = (H_B, DK_B, DV_B)
    return pl.pallas_call(
        functools.partial(_gdn_kernel, chunk=chunk, nseq=nseq, shared_init=shared_init),
        grid=(B // nseq, L // chunk),
        in_specs=[pl.BlockSpec((nseq, chunk, PB_W), lambda b, c: (b, c, 0))]
        + _seq_specs(nseq, shared_init, 3 * W_B, state_shape)
        + [pl.BlockSpec(t.shape, lambda b, c: (0, 0)) for t in consts],
        out_specs=[pl.BlockSpec((nseq, chunk, W_B), lambda b, c: (b, c, 0)),
                   pl.BlockSpec((nseq,) + state_shape, lambda b, c: (b, 0, 0, 0))],
        out_shape=[jax.ShapeDtypeStruct((B, L, W_B), F32),
                   jax.ShapeDtypeStruct((B,) + state_shape, F32)],
        scratch_shapes=[pltpu.VMEM((nseq, SUBLANE + chunk, 3 * W_B), F32),
                        pltpu.VMEM((nseq,) + state_shape, F32)],
        compiler_params=pltpu.CompilerParams(dimension_semantics=("arbitrary", "arbitrary"),
                                             vmem_limit_bytes=VMEM_LIMIT),
        name="gdn_chunk%d" % chunk,
    )(pb, hist8, s0, *consts)


def _row_mask(nrow, i):
    return _iota2((nrow, 1), 0) == i


def _to_col(row, eye):
    return jnp.sum(eye * row, axis=-1, keepdims=True)


def _rwkv_step_kernel(pa_ref, prev_ref, s_ref, mu_ref, w0_ref, wdu_ref, a0_ref, wau_ref, wgu_ref,
                      kk_ref, ka_ref, rk_ref, lnw_ref, lnb_ref, o_ref, sout_ref):
    ns = pa_ref.shape[0]
    x = pa_ref[...]
    xm = x + (prev_ref[...] - x) * mu_ref[...]
    r = xm[:, 0:W_A]
    k = xm[:, W_A:2 * W_A]
    v = xm[:, 2 * W_A:3 * W_A]
    wd = xm[:, PA_DECAY:PA_DECAY + LANE]
    ad = xm[:, PA_AAA:PA_AAA + LANE]
    gd = xm[:, PA_GATE:PA_GATE + LANE]
    wl = w0_ref[...] + _dot(jnp.tanh(wd).astype(BF16), wdu_ref[...])
    w = jnp.exp(-jnp.exp(-_softplus(-wl) - 0.5))
    a = _sigmoid(a0_ref[...] + _dot(ad.astype(BF16), wau_ref[...]))
    g = _dot(_sigmoid(gd).astype(BF16), wgu_ref[...])
    kkr = k * kk_ref[...]
    k2 = k * (1.0 + (a - 1.0) * ka_ref[...])
    bon = r * k2 * rk_ref[...]

    outs = []
    seq = range(ns)
    for h in range(H_A):
        hs = slice(h * N_A, (h + 1) * N_A)
        kkr_h = kkr[:, hs]
        kk_h = kkr_h * lax.rsqrt(jnp.sum(kkr_h * kkr_h, axis=-1, keepdims=True) + 1e-12)
        kka_h = kk_h * a[:, hs]
        r_h, v_h, k2_h, w_h = r[:, hs], v[:, hs], k2[:, hs], w[:, hs]
        Ss = [s_ref[i, h] for i in seq]
        sas = [_mm(Ss[i], jnp.broadcast_to(kk_h[i:i + 1, :], (N_A, N_A)), NT, "xl") for i in seq]
        vks = [_mm(jnp.where(_row_mask(ns, i), v_h, 0.0), k2_h, TN, "b") for i in seq]
        s_new = [Ss[i] * w_h[i:i + 1, :] - sas[i] * kka_h[i:i + 1, :] + vks[i] for i in seq]
        for i in seq:
            sout_ref[i, h] = s_new[i]
        o_all = [_mm(r_h, s_new[i], NT, "b") for i in seq]
        o_h = jnp.zeros((ns, N_A), F32)
        for i in seq:
            o_h = jnp.where(_row_mask(ns, i), o_all[i], o_h)
        mean = jnp.mean(o_h, axis=-1, keepdims=True)
        dlt = o_h - mean
        var = jnp.mean(dlt * dlt, axis=-1, keepdims=True)
        on = dlt * lax.rsqrt(var + LNX_EPS) * lnw_ref[:, hs] + lnb_ref[:, hs]
        bonus = jnp.sum(bon[:, hs], axis=-1, keepdims=True) * v_h
        outs.append((on + bonus) * g[:, hs])
    o_ref[...] = jnp.concatenate(outs, axis=-1)


def _gdn_step_kernel(pb_ref, hist_ref, s_ref, cw_ref, alog_ref, dtb_ref, nw_ref, o_ref, sout_ref):
    ns = pb_ref.shape[0]
    x = pb_ref[...]
    conv = x[:, 0:3 * W_B] * cw_ref[CONV_W - 1:CONV_W, :]
    for j in range(CONV_W - 1):
        conv = conv + hist_ref[j] * cw_ref[j:j + 1, :]
    cs = conv * _sigmoid(conv)
    z = x[:, PB_Z:PB_Z + W_B]
    gates = x[:, PB_GATES:PB_GATES + LANE]
    eg = jnp.exp(-jnp.exp(alog_ref[...]) * _softplus(gates + dtb_ref[...]))
    beta = _sigmoid(gates)

    eye = (_iota2((DK_B, DK_B), 0) == _iota2((DK_B, DK_B), 1)).astype(F32)
    l2n = lambda t: t * lax.rsqrt(jnp.sum(t * t, axis=-1, keepdims=True) + 1e-12)
    for h in range(H_B):
        hs = slice(h * LANE, (h + 1) * LANE)
        q_h = l2n(cs[:, hs]) * (DK_B ** -0.5)
        k_h = l2n(cs[:, W_B + h * LANE:W_B + (h + 1) * LANE])
        v_h = cs[:, 2 * W_B + h * LANE:2 * W_B + (h + 1) * LANE]
        eg_h = eg[:, h:h + 1]
        b_h = beta[:, H_B + h:H_B + h + 1]
        seq = range(ns)
        Ss = [s_ref[i, h] for i in seq]
        k_cols = [_to_col(k_h[i:i + 1, :], eye) for i in seq]
        q_cols = [_to_col(q_h[i:i + 1, :], eye) for i in seq]
        kss = [jnp.sum(k_cols[i] * Ss[i], axis=0, keepdims=True) for i in seq]
        v_new = [b_h[i:i + 1, :] * (v_h[i:i + 1, :] - eg_h[i:i + 1, :] * kss[i]) for i in seq]
        s_new = [Ss[i] * eg_h[i:i + 1, :] + k_cols[i] * v_new[i] for i in seq]
        for i in seq:
            sout_ref[i, h] = s_new[i]
        o_rows = [jnp.sum(q_cols[i] * s_new[i], axis=0, keepdims=True) for i in seq]
        o_h = jnp.zeros((ns, DV_B), F32)
        for i in seq:
            o_h = jnp.where(_row_mask(ns, i), o_rows[i], o_h)
        o_h = o_h * lax.rsqrt(jnp.mean(o_h * o_h, axis=-1, keepdims=True) + NORM_EPS) * nw_ref[...]
        zh = z[:, hs]
        o_ref[:, hs] = o_h * (zh * _sigmoid(zh))


def _step_const_specs(ts):
    return [pl.BlockSpec(t.shape, lambda b: (0,) * t.ndim) for t in ts]


def _rwkv_step(pa, prev, s, consts, ns):
    B = s.shape[0]
    return pl.pallas_call(
        _rwkv_step_kernel,
        grid=(B // ns,),
        in_specs=[pl.BlockSpec((ns, PA_W), lambda b: (b, 0)),
                  pl.BlockSpec((ns, PA_W), lambda b: (b, 0)),
                  pl.BlockSpec((ns, H_A, N_A, N_A), lambda b: (b, 0, 0, 0))] + _step_const_specs(consts),
        out_specs=[pl.BlockSpec((ns, W_A), lambda b: (b, 0)),
                   pl.BlockSpec((ns, H_A, N_A, N_A), lambda b: (b, 0, 0, 0))],
        out_shape=[jax.ShapeDtypeStruct((B, W_A), F32), jax.ShapeDtypeStruct(s.shape, F32)],
        compiler_params=pltpu.CompilerParams(dimension_semantics=("arbitrary",),
                                             vmem_limit_bytes=VMEM_LIMIT),
        name="rwkv_step",
    )(pa, prev, s, *consts)


def _gdn_step(pb, hist, s, consts, ns):
    B = s.shape[0]
    return pl.pallas_call(
        _gdn_step_kernel,
        grid=(B // ns,),
        in_specs=[pl.BlockSpec((ns, PB_W), lambda b: (b, 0)),
                  pl.BlockSpec((CONV_W - 1, ns, 3 * W_B), lambda b: (0, b, 0)),
                  pl.BlockSpec((ns, H_B, DK_B, DV_B), lambda b: (b, 0, 0, 0))] + _step_const_specs(consts),
        out_specs=[pl.BlockSpec((ns, W_B), lambda b: (b, 0)),
                   pl.BlockSpec((ns, H_B, DK_B, DV_B), lambda b: (b, 0, 0, 0))],
        out_shape=[jax.ShapeDtypeStruct((B, W_B), F32), jax.ShapeDtypeStruct(s.shape, F32)],
        compiler_params=pltpu.CompilerParams(dimension_semantics=("arbitrary",),
                                             vmem_limit_bytes=VMEM_LIMIT),
        name="gdn_step",
    )(pb, hist, s, *consts)


def _pad_cols(t, width):
    return jnp.pad(t, [(0, 0)] * (t.ndim - 1) + [(0, width - t.shape[-1])])


def _pa_layout(t):
    main = t[..., :3 * W_A]
    dec = t[..., 3 * W_A:3 * W_A + D_DECAY]
    aaa = t[..., 3 * W_A + D_DECAY:3 * W_A + D_DECAY + D_AAA]
    gate = t[..., 3 * W_A + D_DECAY + D_AAA:]
    return jnp.concatenate([main, _pad_cols(dec, LANE), _pad_cols(aaa, LANE), _pad_cols(gate, LANE)],
                           axis=-1)


def _pa_unlayout(t):
    return jnp.concatenate([t[..., :3 * W_A], t[..., PA_DECAY:PA_DECAY + D_DECAY],
                            t[..., PA_AAA:PA_AAA + D_AAA], t[..., PA_GATE:PA_GATE + D_GATE]], axis=-1)


def _pad_rows(t, rows):
    return jnp.pad(t, [(0, rows - t.shape[0]), (0, 0)])


def _pair_unblock(s):
    B = s.shape[0]
    return jnp.stack([s[:, :, :N_A, :N_A], s[:, :, N_A:, N_A:]], axis=2).reshape(B, H_A, N_A, N_A)


def _history_rows(rows):
    B, n, w = rows.shape
    return jnp.concatenate([jnp.zeros((B, SUBLANE - n, w), F32), rows], axis=1)


def kernel(x_prompt, x_sample, state_rwkv, state_shift, state_gdn, state_conv, meta_tokens,
           g_ffn1, w_gate1, w_up1, w_down1, g_mix, w_in, mu_shift, w0, w_decay_up, a0, w_a_up,
           w_g_up, k_k, k_a, r_k, lnx_w, lnx_b, conv_w, a_log, dt_bias, gdn_norm_w, w_out,
           g_ffn2, w_gate2, w_up2, w_down2, g_final):
    assert g_ffn1.shape[0] == 1, "single trunk layer"
    bp, sp, _ = x_prompt.shape
    bs = x_sample.shape[0]
    assert x_sample.shape[1] == 1 and sp % CHUNK_RWKV == 0 and sp % CHUNK_GDN == 0
    assert (bp * sp) % TM_DENSE == 0 and bp % SEQS_PER_STEP == 0 and bs % DEC_SEQS_PER_STEP == 0
    row = lambda t: t.reshape(1, -1).astype(F32)

    ffn1 = (w_gate1[0].astype(BF16), w_up1[0].astype(BF16), w_down1[0].astype(BF16))
    ffn2 = (w_gate2[0].astype(BF16), w_up2[0].astype(BF16), w_down2[0].astype(BF16))
    win_a = _pa_layout(w_in[0][:, :N_A_IN]).astype(BF16)
    win_b = _pad_cols(w_in[0][:, N_A_IN:], PB_W).astype(BF16)
    wo_a = w_out[0][:W_A].astype(BF16)
    wo_b = w_out[0][W_A:].astype(BF16)
    dense_in_consts = (row(g_ffn1[0]), *ffn1, row(g_mix[0]), win_a, win_b)
    dense_out_consts = (wo_a, wo_b, row(g_ffn2[0]), *ffn2, row(g_final))
    rwkv_consts = (_pa_layout(row(mu_shift[0])), row(w0[0]),
                   _pad_rows(w_decay_up[0], LANE).astype(BF16), row(a0[0]),
                   _pad_rows(w_a_up[0], LANE).astype(BF16), _pad_rows(w_g_up[0], LANE).astype(BF16),
                   row(k_k[0]), row(k_a[0]), row(r_k[0]), row(lnx_w[0]), row(lnx_b[0]))
    gdn_consts = (conv_w[0].astype(F32), _pad_cols(row(a_log[0]), LANE), _pad_cols(row(dt_bias[0]), LANE),
                  row(gdn_norm_w[0]))

    xs = jnp.concatenate([x_sample[:, 0, :].astype(F32), meta_tokens.astype(F32)], axis=0)
    hs, pas, pbs = _dense_in(xs, *dense_in_consts, tm=xs.shape[0])

    pa_meta = pas[bs:][None]
    pb_meta = pbs[bs:][None]
    _, rw_meta = _rwkv(pa_meta, jnp.zeros((1, SUBLANE, PA_W), F32),
                       jnp.zeros((1, H_A // 2, LANE, LANE), F32), rwkv_consts, N_META, 1)
    _, gd_meta = _gdn(pb_meta, jnp.zeros((1, SUBLANE, 3 * W_B), F32),
                      jnp.zeros((1, H_B, DK_B, DV_B), F32), gdn_consts, N_META, 1)

    hp, pap, pbp = _dense_in(x_prompt.reshape(bp * sp, D_MODEL).astype(F32), *dense_in_consts,
                             tm=TM_DENSE)
    pap3 = pap.reshape(bp, sp, PA_W)
    pbp3 = pbp.reshape(bp, sp, PB_W)
    oa_p, rw_p = _rwkv(pap3, _history_rows(pa_meta[:, -1:, :]), rw_meta, rwkv_consts,
                       CHUNK_RWKV, SEQS_PER_STEP)
    ob_p, gd_p = _gdn(pbp3, _history_rows(pb_meta[:, -(CONV_W - 1):, :3 * W_B]), gd_meta, gdn_consts,
                      CHUNK_GDN, SEQS_PER_STEP)
    y_p = _dense_out(hp, oa_p.reshape(bp * sp, W_A), ob_p.reshape(bp * sp, W_B), *dense_out_consts,
                     tm=TM_DENSE)

    oa_s, rw_s = _rwkv_step(pas, _pa_layout(state_shift[0].astype(F32)), state_rwkv[0].astype(F32),
                            rwkv_consts, DEC_SEQS_PER_STEP)
    ob_s, gd_s = _gdn_step(pbs, jnp.swapaxes(state_conv[0].astype(F32), 0, 1), state_gdn[0].astype(F32),
                           gdn_consts, DEC_SEQS_PER_STEP)
    y_s = _dense_out(hs[:bs], oa_s, ob_s, *dense_out_consts, tm=bs)

    new_conv_s = jnp.concatenate([state_conv[0].astype(F32)[:, 1:, :], pbs[:bs, None, :3 * W_B]], axis=1)
    return (y_p.reshape(bp, sp, D_MODEL).astype(x_prompt.dtype),
            y_s.reshape(bs, 1, D_MODEL).astype(x_sample.dtype),
            _pair_unblock(rw_p)[None],
            _pa_unlayout(pap3[:, -1, :])[None],
            gd_p[None],
            pbp3[:, -(CONV_W - 1):, :3 * W_B][None],
            rw_s[None],
            _pa_unlayout(pas[:bs])[None],
            gd_s[None],
            new_conv_s[None])
```

```python
import functools

import jax
import jax.numpy as jnp
from jax import lax
from jax.experimental import pallas as pl
from jax.experimental.pallas import tpu as pltpu

F32 = jnp.float32
BF16 = jnp.bfloat16

LANE = 128
SUBLANE = 8
VMEM_LIMIT = 56 * 1024 * 1024

D_MODEL = 1024
D_FF = 2816
N_META = 16
H_A, N_A = 8, 64
W_A = H_A * N_A
D_DECAY, D_AAA, D_GATE = 32, 32, 96
N_A_IN = 3 * W_A + D_DECAY + D_AAA + D_GATE
H_B, DK_B, DV_B = 4, 128, 128
W_B = H_B * DV_B
CONV_W = 4
N_B_IN = 4 * W_B + 2 * H_B
LNX_EPS = 64e-5
NORM_EPS = 1e-6

PA_W = 3 * W_A + 3 * LANE
PA_DECAY = 3 * W_A
PA_AAA = PA_DECAY + LANE
PA_GATE = PA_AAA + LANE
PB_W = 4 * W_B + LANE
PB_Z = 3 * W_B
PB_GATES = 4 * W_B

TM_DENSE = 256
CHUNK_RWKV = 64
CHUNK_GDN = 128
SEQS_PER_STEP = 4
DEC_SEQS_PER_STEP = SUBLANE

NN = (((1,), (0,)), ((), ()))
NT = (((1,), (1,)), ((), ()))
TN = (((0,), (0,)), ((), ()))


def _dot(a, b):
    return jnp.dot(a, b, preferred_element_type=F32)


def _sigmoid(x):
    return 1.0 / (1.0 + jnp.exp(-x))


def _softplus(x):
    return jnp.maximum(x, 0.0) + jnp.log(1.0 + jnp.exp(-jnp.abs(x)))


def _rms(x, g):
    return x * lax.rsqrt(jnp.mean(x * x, axis=-1, keepdims=True) + NORM_EPS) * g


def _iota2(shape, dim):
    return lax.broadcasted_iota(jnp.int32, shape, dim)


def _split(x):
    hi = x.astype(BF16)
    return hi, (x - hi.astype(F32)).astype(BF16)


def _dg(a, b, dims):
    return lax.dot_general(a, b, dims, preferred_element_type=F32)


def _mm(a, b, dims=NN, mode="b"):
    if mode == "b":
        return _dg(a.astype(BF16), b.astype(BF16), dims)
    if mode == "xl":
        ah, al = _split(a)
        bh = b.astype(BF16)
        return _dg(ah, bh, dims) + _dg(al, bh, dims)
    if mode == "xr":
        ah = a.astype(BF16)
        bh, bl = _split(b)
        return _dg(ah, bh, dims) + _dg(ah, bl, dims)
    assert mode == "x3"
    ah, al = _split(a)
    bh, bl = _split(b)
    return _dg(ah, bh, dims) + (_dg(ah, bl, dims) + _dg(al, bh, dims))


def _inv_unit_lower(lows, nil, merge_mode):
    n = lows[0].shape[0]
    row = _iota2((n, n), 0)
    col = _iota2((n, n), 1)
    same_block = lambda s: (row // s) == (col // s)
    eye = (row == col).astype(F32)
    s = 2
    in_base = same_block(s)
    invs = [eye - jnp.where(in_base, low, 0.0) for low in lows]
    while s < nil:
        newly = jnp.logical_and(same_block(2 * s), jnp.logical_not(same_block(s)))
        ts = [_mm(inv, jnp.where(newly, low, 0.0), NN, merge_mode) for inv, low in zip(invs, lows)]
        invs = [inv - _mm(t, inv, NN, merge_mode) for inv, t in zip(invs, ts)]
        s *= 2
    return invs


def _swiglu(n, wg_ref, wu_ref, wd_ref):
    gate = _dot(n, wg_ref[...])
    up = _dot(n, wu_ref[...])
    act = (gate * _sigmoid(gate) * up).astype(BF16)
    return _dot(act, wd_ref[...])


def _dense_in_kernel(x_ref, g1_ref, wg_ref, wu_ref, wd_ref, gm_ref, wa_ref, wb_ref,
                     h_ref, pa_ref, pb_ref):
    x = x_ref[...]
    h = x + 0.5 * _swiglu(_rms(x, g1_ref[...]).astype(BF16), wg_ref, wu_ref, wd_ref)
    h_ref[...] = h
    n = _rms(h, gm_ref[...]).astype(BF16)
    pa_ref[...] = _dot(n, wa_ref[...])
    pb_ref[...] = _dot(n, wb_ref[...])


def _dense_out_kernel(h_ref, oa_ref, ob_ref, woa_ref, wob_ref, g2_ref, wg_ref, wu_ref, wd_ref,
                      gf_ref, y_ref):
    h = (h_ref[...] + _dot(oa_ref[...].astype(BF16), woa_ref[...])
         + _dot(ob_ref[...].astype(BF16), wob_ref[...]))
    h = h + 0.5 * _swiglu(_rms(h, g2_ref[...]).astype(BF16), wg_ref, wu_ref, wd_ref)
    y_ref[...] = _rms(h, gf_ref[...])


def _const_spec(shape):
    return pl.BlockSpec(shape, lambda *_: (0,) * len(shape), pipeline_mode=pl.Buffered(1))


def _row_spec(tm, width):
    return pl.BlockSpec((tm, width), lambda i: (i, 0))


def _dense_in(x, g1, wg, wu, wd, gm, wa, wb, tm):
    n = x.shape[0]
    consts = (g1, wg, wu, wd, gm, wa, wb)
    return pl.pallas_call(
        _dense_in_kernel,
        grid=(n // tm,),
        in_specs=[_row_spec(tm, D_MODEL)] + [_const_spec(c.shape) for c in consts],
        out_specs=[_row_spec(tm, D_MODEL), _row_spec(tm, PA_W), _row_spec(tm, PB_W)],
        out_shape=[jax.ShapeDtypeStruct((n, D_MODEL), F32),
                   jax.ShapeDtypeStruct((n, PA_W), F32),
                   jax.ShapeDtypeStruct((n, PB_W), F32)],
        compiler_params=pltpu.CompilerParams(dimension_semantics=("arbitrary",),
                                             vmem_limit_bytes=VMEM_LIMIT),
        name="dense_in",
    )(x, *consts)


def _dense_out(h, oa, ob, woa, wob, g2, wg, wu, wd, gf, tm):
    n = h.shape[0]
    consts = (woa, wob, g2, wg, wu, wd, gf)
    return pl.pallas_call(
        _dense_out_kernel,
        grid=(n // tm,),
        in_specs=[_row_spec(tm, D_MODEL), _row_spec(tm, W_A), _row_spec(tm, W_B)]
        + [_const_spec(c.shape) for c in consts],
        out_specs=_row_spec(tm, D_MODEL),
        out_shape=jax.ShapeDtypeStruct((n, D_MODEL), F32),
        compiler_params=pltpu.CompilerParams(dimension_semantics=("arbitrary",),
                                             vmem_limit_bytes=VMEM_LIMIT),
        name="dense_out",
    )(h, oa, ob, *consts)


RWKV_PREC = dict(cum="xr", seg="b", pair="b", invm="b", sread="b", akv="b", solve="b", inter="b",
                 state="x3")


def _rwkv_kernel(pa_ref, prev_ref, s0_ref, mu_ref, w0_ref, wdu_ref, a0_ref, wau_ref, wgu_ref,
                 kk_ref, ka_ref, rk_ref, lnw_ref, lnb_ref, o_ref, sout_ref, xbuf, s_scr,
                 *, chunk, nseq, shared_init):
    c = pl.program_id(1)
    C = chunk
    P = RWKV_PREC

    @pl.when(c == 0)
    def _():
        for i in range(nseq):
            j = 0 if shared_init else i
            xbuf[i, 0:SUBLANE, :] = prev_ref[j]
            s_scr[i] = s0_ref[j]

    tri = (_iota2((C, C), 0) >= _iota2((C, C), 1)).astype(F32)
    mid = C // 2 - 1

    def prep(i):
        x = pa_ref[i]
        xbuf[i, SUBLANE:SUBLANE + C, :] = x
        prev = xbuf[i, SUBLANE - 1:SUBLANE - 1 + C, :]
        xm = x + (prev - x) * mu_ref[...]
        xbuf[i, 0:SUBLANE, :] = xbuf[i, C:C + SUBLANE, :]

        r = xm[:, 0:W_A]
        k = xm[:, W_A:2 * W_A]
        v = xm[:, 2 * W_A:3 * W_A]
        wd = xm[:, PA_DECAY:PA_DECAY + LANE]
        ad = xm[:, PA_AAA:PA_AAA + LANE]
        gd = xm[:, PA_GATE:PA_GATE + LANE]

        wl = w0_ref[...] + _dot(jnp.tanh(wd).astype(BF16), wdu_ref[...])
        lw = -jnp.exp(-_softplus(-wl) - 0.5)
        a = _sigmoid(a0_ref[...] + _dot(ad.astype(BF16), wau_ref[...]))
        g = _dot(_sigmoid(gd).astype(BF16), wgu_ref[...])
        kkr = k * kk_ref[...]
        k2 = k * (1.0 + (a - 1.0) * ka_ref[...])
        G = _mm(tri, lw, NN, P["cum"])
        return dict(r=r, v=v, a=a, g=g, kkr=kkr, k2=k2, G=G, Gx=G - lw, Gm=G[mid:mid + 1, :],
                    Gc=G[C - 1:C, :])

    seqs = [prep(i) for i in range(nseq)]

    C2 = 2 * C
    trow = _iota2((C2, C2), 0) % C
    tcol = _iota2((C2, C2), 1) % C
    incl = trow >= tcol
    strict = trow > tcol
    lane = _iota2((1, LANE), 1)
    m0 = (lane < N_A).astype(F32)
    m1 = (lane >= N_A).astype(F32)
    seg = ((_iota2((LANE, LANE), 0) // N_A) == (_iota2((LANE, LANE), 1) // N_A)).astype(F32)
    stack = lambda t: jnp.concatenate([t * m0, t * m1], axis=0)
    merged = C2 % LANE == 0

    inst = [(i, p) for i in range(nseq) for p in range(H_A // 2)]
    n = range(len(inst))
    sls = [slice(p * LANE, (p + 1) * LANE) for _, p in inst]
    take = lambda name: [seqs[i][name][:, sls[j]] for j, (i, _) in enumerate(inst)]
    rs, vs_, k2s, kkrs, as_, gs = take("r"), take("v"), take("k2"), take("kkr"), take("a"), take("g")
    Gs, Gxs, Gms, Gcs = take("G"), take("Gx"), take("Gm"), take("Gc")
    Ss = [s_scr[i, p] for i, p in inst]

    ssq = [_mm(t * t, seg, NN, P["seg"]) for t in kkrs]
    kks = [t * lax.rsqrt(q + 1e-12) for t, q in zip(kkrs, ssq)]
    kkas = [kks[j] * as_[j] for j in n]
    inv_rel = [jnp.exp(Gms[j] - Gs[j]) for j in n]
    dec = [jnp.exp(Gcs[j] - Gs[j]) for j in n]
    lhs = [jnp.concatenate([stack(kks[j] * jnp.exp(Gxs[j] - Gms[j])),
                            stack(rs[j] * jnp.exp(Gs[j] - Gms[j]))], axis=0) for j in n]
    ais = [stack(kkas[j] * inv_rel[j]) for j in n]
    kis = [stack(k2s[j] * inv_rel[j]) for j in n]
    if merged:
        pm = [_mm(lhs[j], jnp.concatenate([ais[j], kis[j]], axis=0), NT, P["pair"]) for j in n]
        pas_, pks_ = [t[:, :C2] for t in pm], [t[:, C2:] for t in pm]
    else:
        pas_ = [_mm(lhs[j], ais[j], NT, P["pair"]) for j in n]
        pks_ = [_mm(lhs[j], kis[j], NT, P["pair"]) for j in n]
    a_aa = [jnp.where(strict, t[:C2], 0.0) for t in pas_]
    a_ak = [jnp.where(strict, t[:C2], 0.0) for t in pks_]
    a_ra = [jnp.where(incl, t[C2:], 0.0) for t in pas_]
    a_rk = [jnp.where(incl, t[C2:], 0.0) for t in pks_]
    ainv = _inv_unit_lower(a_aa, C, P["invm"])

    vst = [stack(t) for t in vs_]
    akv = [_mm(a_ak[j], vst[j], NN, P["akv"]) for j in n]
    sx = [_mm(jnp.concatenate([kks[j] * jnp.exp(Gxs[j]), rs[j] * jnp.exp(Gs[j])], axis=0), Ss[j], NT,
              P["sread"]) for j in n]
    us = [-_mm(ainv[j], stack(sx[j][:C]) + akv[j], NN, P["solve"]) for j in n]
    uv = [jnp.concatenate([us[j], vst[j]], axis=0) for j in n]
    s_new = [Ss[j] * jnp.exp(Gcs[j])
             + _mm(uv[j], jnp.concatenate([stack(kkas[j] * dec[j]), stack(k2s[j] * dec[j])], axis=0),
                   TN, P["state"]) for j in n]
    for j, (i, p) in enumerate(inst):
        s_scr[i, p] = s_new[j]
    if merged:
        inter = [_mm(jnp.concatenate([a_ra[j], a_rk[j]], axis=1), uv[j], NN, P["inter"]) for j in n]
    else:
        inter = [_mm(a_ra[j], us[j], NN, P["inter"]) + _mm(a_rk[j], vst[j], NN, P["inter"]) for j in n]
    Os = [sx[j][C:] + inter[j][:C] + inter[j][C:] for j in n]
    means = [_mm(t, seg, NN, P["seg"]) * (1.0 / N_A) for t in Os]
    dlts = [t - m for t, m in zip(Os, means)]
    vars_ = [_mm(t * t, seg, NN, P["seg"]) * (1.0 / N_A) for t in dlts]
    bonus = [_mm(rs[j] * k2s[j] * rk_ref[:, sls[j]], seg, NN, P["seg"]) * vs_[j] for j in n]
    for j, (i, p) in enumerate(inst):
        on = dlts[j] * lax.rsqrt(vars_[j] + LNX_EPS) * lnw_ref[:, sls[j]] + lnb_ref[:, sls[j]]
        o_ref[i, :, sls[j]] = (on + bonus[j]) * gs[j]

    @pl.when(c == pl.num_programs(1) - 1)
    def _():
        sout_ref[...] = s_scr[...]


def _seq_specs(nseq, shared_init, hist_w, state_shape):
    zeros = (0,) * len(state_shape)
    if shared_init:
        return [pl.BlockSpec((1, SUBLANE, hist_w), lambda b, c: (0, 0, 0)),
                pl.BlockSpec((1,) + state_shape, lambda b, c: (0,) + zeros)]
    return [pl.BlockSpec((nseq, SUBLANE, hist_w), lambda b, c: (b, 0, 0)),
            pl.BlockSpec((nseq,) + state_shape, lambda b, c: (b,) + zeros)]


def _rwkv(pa, prev8, s0, consts, chunk, nseq):
    B, L, _ = pa.shape
    shared_init = prev8.shape[0] == 1 and B > 1
    state_shape = (H_A // 2, LANE, LANE)
    return pl.pallas_call(
        functools.partial(_rwkv_kernel, chunk=chunk, nseq=nseq, shared_init=shared_init),
        grid=(B // nseq, L // chunk),
        in_specs=[pl.BlockSpec((nseq, chunk, PA_W), lambda b, c: (b, c, 0))]
        + _seq_specs(nseq, shared_init, PA_W, state_shape)
        + [pl.BlockSpec(t.shape, lambda b, c: (0, 0)) for t in consts],
        out_specs=[pl.BlockSpec((nseq, chunk, W_A), lambda b, c: (b, c, 0)),
                   pl.BlockSpec((nseq,) + state_shape, lambda b, c: (b, 0, 0, 0))],
        out_shape=[jax.ShapeDtypeStruct((B, L, W_A), F32),
                   jax.ShapeDtypeStruct((B,) + state_shape, F32)],
        scratch_shapes=[pltpu.VMEM((nseq, SUBLANE + chunk, PA_W), F32),
                        pltpu.VMEM((nseq,) + state_shape, F32)],
        compiler_params=pltpu.CompilerParams(dimension_semantics=("arbitrary", "arbitrary"),
                                             vmem_limit_bytes=VMEM_LIMIT),
        name="rwkv_chunk%d" % chunk,
    )(pa, prev8, s0, *consts)


GDN_PREC = dict(cum="xr", tr="xr", kk="b", invm="b", wu="b", qk="b", sread="b", qkv="b", state="b")


def _gdn_kernel(pb_ref, hist_ref, s0_ref, cw_ref, alog_ref, dtb_ref, nw_ref, o_ref, sout_ref,
                xbuf, s_scr, *, chunk, nseq, shared_init):
    c = pl.program_id(1)
    C = chunk
    P = GDN_PREC

    @pl.when(c == 0)
    def _():
        for i in range(nseq):
            j = 0 if shared_init else i
            xbuf[i, 0:SUBLANE, :] = hist_ref[j]
            s_scr[i] = s0_ref[j]

    row = _iota2((C, C), 0)
    col = _iota2((C, C), 1)
    incl = row >= col
    strict = row > col
    tri = incl.astype(F32)
    sel = (_iota2((SUBLANE, LANE), 0) == _iota2((SUBLANE, LANE), 1)).astype(F32)

    def prep(i):
        x = pb_ref[i]
        xbuf[i, SUBLANE:SUBLANE + C, :] = x[:, 0:3 * W_B]
        conv = xbuf[i, SUBLANE - 3:SUBLANE - 3 + C, :] * cw_ref[0:1, :]
        for j in range(1, CONV_W):
            off = SUBLANE - 3 + j
            conv = conv + xbuf[i, off:off + C, :] * cw_ref[j:j + 1, :]
        xbuf[i, 0:SUBLANE, :] = xbuf[i, C:C + SUBLANE, :]
        cs = conv * _sigmoid(conv)
        gates = x[:, PB_GATES:PB_GATES + LANE]
        glog = -jnp.exp(alog_ref[...]) * _softplus(gates + dtb_ref[...])
        beta = _sigmoid(gates)
        G = _mm(tri, glog, NN, P["cum"])
        Gt = _mm(sel, G, NT, P["tr"])
        return dict(cs=cs, z=x[:, PB_Z:PB_Z + W_B], beta=beta, G=G, Gt=Gt)

    seqs = [prep(i) for i in range(nseq)]

    inst = [(i, h) for i in range(nseq) for h in range(H_B)]
    n = range(len(inst))
    l2n = lambda t: t * lax.rsqrt(jnp.sum(t * t, axis=-1, keepdims=True) + 1e-12)
    head = lambda i, h, part: seqs[i]["cs"][:, part * W_B + h * LANE:part * W_B + (h + 1) * LANE]
    qs = [l2n(head(i, h, 0)) * (DK_B ** -0.5) for i, h in inst]
    ks = [l2n(head(i, h, 1)) for i, h in inst]
    vs = [head(i, h, 2) for i, h in inst]
    gcols = [seqs[i]["G"][:, h:h + 1] for i, h in inst]
    grows = [seqs[i]["Gt"][h:h + 1, :] for i, h in inst]
    bcols = [seqs[i]["beta"][:, H_B + h:H_B + h + 1] for i, h in inst]
    Ss = [s_scr[i, h] for i, h in inst]

    dmats = [jnp.where(incl, jnp.exp(jnp.where(incl, gcols[j] - grows[j], 0.0)), 0.0) for j in n]
    kbs = [ks[j] * bcols[j] for j in n]
    lows = [jnp.where(strict, _mm(kbs[j], ks[j], NT, P["kk"]) * dmats[j], 0.0) for j in n]
    ainv = _inv_unit_lower(lows, C, P["invm"])
    egs = [jnp.exp(t) for t in gcols]
    wus = [_mm(ainv[j], jnp.concatenate([kbs[j] * egs[j], vs[j] * bcols[j]], axis=1), NN, P["wu"])
           for j in n]
    qks = [_mm(qs[j], ks[j], NT, P["qk"]) * dmats[j] for j in n]
    srs = [_mm(jnp.concatenate([wus[j][:, :DK_B], qs[j] * egs[j]], axis=0), Ss[j], NN, P["sread"])
           for j in n]
    v_new = [wus[j][:, DK_B:] - srs[j][:C] for j in n]
    glast = [t[C - 1:C, :] for t in gcols]
    s_new = [Ss[j] * jnp.exp(glast[j])
             + _mm(ks[j] * jnp.exp(glast[j] - gcols[j]), v_new[j], TN, P["state"]) for j in n]
    for j, (i, h) in enumerate(inst):
        s_scr[i, h] = s_new[j]
    os_ = [srs[j][C:] + _mm(qks[j], v_new[j], NN, P["qkv"]) for j in n]
    for j, (i, h) in enumerate(inst):
        o = os_[j]
        o = o * lax.rsqrt(jnp.mean(o * o, axis=-1, keepdims=True) + NORM_EPS) * nw_ref[...]
        zh = seqs[i]["z"][:, h * LANE:(h + 1) * LANE]
        o_ref[i, :, h * LANE:(h + 1) * LANE] = o * (zh * _sigmoid(zh))

    @pl.when(c == pl.num_programs(1) - 1)
    def _():
        sout_ref[...] = s_scr[...]


def _gdn(pb, hist8, s0, consts, chunk, nseq):
    B, L, _ = pb.shape
    shared_init = hist8.shape[0] == 1 and B > 1
    state_shape = (H_B, DK_B, DV_B)
    return pl.pallas_call(
        functools.partial(_gdn_kernel, chunk=chunk, nseq=nseq, shared_init=shared_init),
        grid=(B // nseq, L // chunk),
        in_specs=[pl.BlockSpec((nseq, chunk, PB_W), lambda b, c: (b, c, 0))]
        + _seq_specs(nseq, shared_init, 3 * W_B, state_shape)
        + [pl.BlockSpec(t.shape, lambda b, c: (0, 0)) for t in consts],
        out_specs=[pl.BlockSpec((nseq, chunk, W_B), lambda b, c: (b, c, 0)),
                   pl.BlockSpec((nseq,) + state_shape, lambda b, c: (b, 0, 0, 0))],
        out_shape=[jax.ShapeDtypeStruct((B, L, W_B), F32),
                   jax.ShapeDtypeStruct((B,) + state_shape, F32)],
        scratch_shapes=[pltpu.VMEM((nseq, SUBLANE + chunk, 3 * W_B), F32),
                        pltpu.VMEM((nseq,) + state_shape, F32)],
        compiler_params=pltpu.CompilerParams(dimension_semantics=("arbitrary", "arbitrary"),
                                             vmem_limit_bytes=VMEM_LIMIT),
        name="gdn_chunk%d" % chunk,
    )(pb, hist8, s0, *consts)


def _row_mask(nrow, i):
    return _iota2((nrow, 1), 0) == i


def _to_col(row, eye):
    return jnp.sum(eye * row, axis=-1, keepdims=True)


STEP_VECS = ("w", "kk", "kka", "k2", "v", "r", "g", "bon")


def _rwkv_step_prep_kernel(pa_ref, prev_ref, mu_ref, w0_ref, wdu_ref, a0_ref, wau_ref, wgu_ref,
                           kk_ref, ka_ref, rk_ref, vec_ref):
    x = pa_ref[...]
    xm = x + (prev_ref[...] - x) * mu_ref[...]
    r = xm[:, 0:W_A]
    k = xm[:, W_A:2 * W_A]
    v = xm[:, 2 * W_A:3 * W_A]
    wd = xm[:, PA_DECAY:PA_DECAY + LANE]
    ad = xm[:, PA_AAA:PA_AAA + LANE]
    gd = xm[:, PA_GATE:PA_GATE + LANE]
    wl = w0_ref[...] + _dot(jnp.tanh(wd).astype(BF16), wdu_ref[...])
    w = jnp.exp(-jnp.exp(-_softplus(-wl) - 0.5))
    a = _sigmoid(a0_ref[...] + _dot(ad.astype(BF16), wau_ref[...]))
    g = _dot(_sigmoid(gd).astype(BF16), wgu_ref[...])
    kkr = k * kk_ref[...]
    k2 = k * (1.0 + (a - 1.0) * ka_ref[...])
    bon = r * k2 * rk_ref[...]
    seg = ((_iota2((LANE, LANE), 0) // N_A) == (_iota2((LANE, LANE), 1) // N_A)).astype(F32)
    ssq = jnp.concatenate([_mm(kkr[:, p * LANE:(p + 1) * LANE] ** 2, seg, NN, RWKV_PREC["seg"])
                           for p in range(H_A // 2)], axis=-1)
    kk = kkr * lax.rsqrt(ssq + 1e-12)
    vecs = dict(w=w, kk=kk, kka=kk * a, k2=k2, v=v, r=r, g=g, bon=bon)
    for j, name in enumerate(STEP_VECS):
        vec_ref[j] = vecs[name].T


def _rwkv_step_state_kernel(vec_ref, s_ref, lnw_ref, lnb_ref, o_ref, sout_ref, o_scr):
    w, kk, kka, k2, v, r, g, bon = [vec_ref[j] for j in range(len(STEP_VECS))]
    for i in range(N_A):
        S = s_ref[0, i]
        sa = jnp.sum(S * kk, axis=0, keepdims=True)
        s_new = S * w - sa * kka + v[i:i + 1, :] * k2
        sout_ref[0, i] = s_new
        o_scr[i:i + 1, :] = jnp.sum(s_new * r, axis=0, keepdims=True)
    o = o_scr[...]
    mean = jnp.mean(o, axis=0, keepdims=True)
    dlt = o - mean
    var = jnp.mean(dlt * dlt, axis=0, keepdims=True)
    on = dlt * lax.rsqrt(var + LNX_EPS) * lnw_ref[...] + lnb_ref[...]
    o_ref[...] = (on + jnp.sum(bon, axis=0, keepdims=True) * v) * g


def _gdn_step_kernel(pb_ref, hist_ref, s_ref, cw_ref, alog_ref, dtb_ref, nw_ref, o_ref, sout_ref):
    ns = pb_ref.shape[0]
    x = pb_ref[...]
    conv = x[:, 0:3 * W_B] * cw_ref[CONV_W - 1:CONV_W, :]
    for j in range(CONV_W - 1):
        conv = conv + hist_ref[j] * cw_ref[j:j + 1, :]
    cs = conv * _sigmoid(conv)
    z = x[:, PB_Z:PB_Z + W_B]
    gates = x[:, PB_GATES:PB_GATES + LANE]
    eg = jnp.exp(-jnp.exp(alog_ref[...]) * _softplus(gates + dtb_ref[...]))
    beta = _sigmoid(gates)

    eye = (_iota2((DK_B, DK_B), 0) == _iota2((DK_B, DK_B), 1)).astype(F32)
    l2n = lambda t: t * lax.rsqrt(jnp.sum(t * t, axis=-1, keepdims=True) + 1e-12)
    for h in range(H_B):
        hs = slice(h * LANE, (h + 1) * LANE)
        q_h = l2n(cs[:, hs]) * (DK_B ** -0.5)
        k_h = l2n(cs[:, W_B + h * LANE:W_B + (h + 1) * LANE])
        v_h = cs[:, 2 * W_B + h * LANE:2 * W_B + (h + 1) * LANE]
        eg_h = eg[:, h:h + 1]
        b_h = beta[:, H_B + h:H_B + h + 1]
        seq = range(ns)
        Ss = [s_ref[i, h] for i in seq]
        k_cols = [_to_col(k_h[i:i + 1, :], eye) for i in seq]
        q_cols = [_to_col(q_h[i:i + 1, :], eye) for i in seq]
        kss = [jnp.sum(k_cols[i] * Ss[i], axis=0, keepdims=True) for i in seq]
        v_new = [b_h[i:i + 1, :] * (v_h[i:i + 1, :] - eg_h[i:i + 1, :] * kss[i]) for i in seq]
        s_new = [Ss[i] * eg_h[i:i + 1, :] + k_cols[i] * v_new[i] for i in seq]
        for i in seq:
            sout_ref[i, h] = s_new[i]
        o_rows = [jnp.sum(q_cols[i] * s_new[i], axis=0, keepdims=True) for i in seq]
        o_h = jnp.zeros((ns, DV_B), F32)
        for i in seq:
            o_h = jnp.where(_row_mask(ns, i), o_rows[i], o_h)
        o_h = o_h * lax.rsqrt(jnp.mean(o_h * o_h, axis=-1, keepdims=True) + NORM_EPS) * nw_ref[...]
        zh = z[:, hs]
        o_ref[:, hs] = o_h * (zh * _sigmoid(zh))


def _step_const_specs(ts):
    return [pl.BlockSpec(t.shape, lambda b: (0,) * t.ndim) for t in ts]


def _rwkv_step(pa, prev, s_t, consts):
    B = s_t.shape[-1]
    prep_consts = consts[:-2]
    lnw_col, lnb_col = [c.reshape(W_A, 1) for c in consts[-2:]]
    vecs = pl.pallas_call(
        _rwkv_step_prep_kernel,
        grid=(1,),
        in_specs=[pl.BlockSpec((B, PA_W), lambda b: (0, 0)),
                  pl.BlockSpec((B, PA_W), lambda b: (0, 0))] + _step_const_specs(prep_consts),
        out_specs=pl.BlockSpec((len(STEP_VECS), W_A, B), lambda b: (0, 0, 0)),
        out_shape=jax.ShapeDtypeStruct((len(STEP_VECS), W_A, B), F32),
        compiler_params=pltpu.CompilerParams(dimension_semantics=("arbitrary",),
                                             vmem_limit_bytes=VMEM_LIMIT),
        name="rwkv_step_prep",
    )(pa, prev, *prep_consts)
    state_spec = pl.BlockSpec((1, N_A, N_A, B), lambda h: (h, 0, 0, 0))
    return pl.pallas_call(
        _rwkv_step_state_kernel,
        grid=(H_A,),
        in_specs=[pl.BlockSpec((len(STEP_VECS), N_A, B), lambda h: (0, h, 0)), state_spec,
                  pl.BlockSpec((N_A, 1), lambda h: (h, 0)), pl.BlockSpec((N_A, 1), lambda h: (h, 0))],
        out_specs=[pl.BlockSpec((N_A, B), lambda h: (h, 0)), state_spec],
        out_shape=[jax.ShapeDtypeStruct((W_A, B), F32), jax.ShapeDtypeStruct(s_t.shape, F32)],
        scratch_shapes=[pltpu.VMEM((N_A, B), F32)],
        compiler_params=pltpu.CompilerParams(dimension_semantics=("arbitrary",),
                                             vmem_limit_bytes=VMEM_LIMIT),
        name="rwkv_step_state",
    )(vecs, s_t, lnw_col, lnb_col)


def _gdn_step(pb, hist, s, consts, ns):
    B = s.shape[0]
    return pl.pallas_call(
        _gdn_step_kernel,
        grid=(B // ns,),
        in_specs=[pl.BlockSpec((ns, PB_W), lambda b: (b, 0)),
                  pl.BlockSpec((CONV_W - 1, ns, 3 * W_B), lambda b: (0, b, 0)),
                  pl.BlockSpec((ns, H_B, DK_B, DV_B), lambda b: (b, 0, 0, 0))] + _step_const_specs(consts),
        out_specs=[pl.BlockSpec((ns, W_B), lambda b: (b, 0)),
                   pl.BlockSpec((ns, H_B, DK_B, DV_B), lambda b: (b, 0, 0, 0))],
        out_shape=[jax.ShapeDtypeStruct((B, W_B), F32), jax.ShapeDtypeStruct(s.shape, F32)],
        compiler_params=pltpu.CompilerParams(dimension_semantics=("arbitrary",),
                                             vmem_limit_bytes=VMEM_LIMIT),
        name="gdn_step",
    )(pb, hist, s, *consts)


def _pad_cols(t, width):
    return jnp.pad(t, [(0, 0)] * (t.ndim - 1) + [(0, width - t.shape[-1])])


def _pa_layout(t):
    main = t[..., :3 * W_A]
    dec = t[..., 3 * W_A:3 * W_A + D_DECAY]
    aaa = t[..., 3 * W_A + D_DECAY:3 * W_A + D_DECAY + D_AAA]
    gate = t[..., 3 * W_A + D_DECAY + D_AAA:]
    return jnp.concatenate([main, _pad_cols(dec, LANE), _pad_cols(aaa, LANE), _pad_cols(gate, LANE)],
                           axis=-1)


def _pa_unlayout(t):
    return jnp.concatenate([t[..., :3 * W_A], t[..., PA_DECAY:PA_DECAY + D_DECAY],
                            t[..., PA_AAA:PA_AAA + D_AAA], t[..., PA_GATE:PA_GATE + D_GATE]], axis=-1)


def _pad_rows(t, rows):
    return jnp.pad(t, [(0, rows - t.shape[0]), (0, 0)])


def _pair_unblock(s):
    B = s.shape[0]
    return jnp.stack([s[:, :, :N_A, :N_A], s[:, :, N_A:, N_A:]], axis=2).reshape(B, H_A, N_A, N_A)


def _history_rows(rows):
    B, n, w = rows.shape
    return jnp.concatenate([jnp.zeros((B, SUBLANE - n, w), F32), rows], axis=1)


def kernel(x_prompt, x_sample, state_rwkv, state_shift, state_gdn, state_conv, meta_tokens,
           g_ffn1, w_gate1, w_up1, w_down1, g_mix, w_in, mu_shift, w0, w_decay_up, a0, w_a_up,
           w_g_up, k_k, k_a, r_k, lnx_w, lnx_b, conv_w, a_log, dt_bias, gdn_norm_w, w_out,
           g_ffn2, w_gate2, w_up2, w_down2, g_final):
    assert g_ffn1.shape[0] == 1, "single trunk layer"
    bp, sp, _ = x_prompt.shape
    bs = x_sample.shape[0]
    assert x_sample.shape[1] == 1 and sp % CHUNK_RWKV == 0 and sp % CHUNK_GDN == 0
    assert (bp * sp) % TM_DENSE == 0 and bp % SEQS_PER_STEP == 0 and bs % DEC_SEQS_PER_STEP == 0
    row = lambda t: t.reshape(1, -1).astype(F32)

    ffn1 = (w_gate1[0].astype(BF16), w_up1[0].astype(BF16), w_down1[0].astype(BF16))
    ffn2 = (w_gate2[0].astype(BF16), w_up2[0].astype(BF16), w_down2[0].astype(BF16))
    win_a = _pa_layout(w_in[0][:, :N_A_IN]).astype(BF16)
    win_b = _pad_cols(w_in[0][:, N_A_IN:], PB_W).astype(BF16)
    wo_a = w_out[0][:W_A].astype(BF16)
    wo_b = w_out[0][W_A:].astype(BF16)
    dense_in_consts = (row(g_ffn1[0]), *ffn1, row(g_mix[0]), win_a, win_b)
    dense_out_consts = (wo_a, wo_b, row(g_ffn2[0]), *ffn2, row(g_final))
    rwkv_consts = (_pa_layout(row(mu_shift[0])), row(w0[0]),
                   _pad_rows(w_decay_up[0], LANE).astype(BF16), row(a0[0]),
                   _pad_rows(w_a_up[0], LANE).astype(BF16), _pad_rows(w_g_up[0], LANE).astype(BF16),
                   row(k_k[0]), row(k_a[0]), row(r_k[0]), row(lnx_w[0]), row(lnx_b[0]))
    gdn_consts = (conv_w[0].astype(F32), _pad_cols(row(a_log[0]), LANE), _pad_cols(row(dt_bias[0]), LANE),
                  row(gdn_norm_w[0]))

    xs = jnp.concatenate([x_sample[:, 0, :].astype(F32), meta_tokens.astype(F32)], axis=0)
    hs, pas, pbs = _dense_in(xs, *dense_in_consts, tm=xs.shape[0])

    pa_meta = pas[bs:][None]
    pb_meta = pbs[bs:][None]
    _, rw_meta = _rwkv(pa_meta, jnp.zeros((1, SUBLANE, PA_W), F32),
                       jnp.zeros((1, H_A // 2, LANE, LANE), F32), rwkv_consts, N_META, 1)
    _, gd_meta = _gdn(pb_meta, jnp.zeros((1, SUBLANE, 3 * W_B), F32),
                      jnp.zeros((1, H_B, DK_B, DV_B), F32), gdn_consts, N_META, 1)

    hp, pap, pbp = _dense_in(x_prompt.reshape(bp * sp, D_MODEL).astype(F32), *dense_in_consts,
                             tm=TM_DENSE)
    pap3 = pap.reshape(bp, sp, PA_W)
    pbp3 = pbp.reshape(bp, sp, PB_W)
    oa_p, rw_p = _rwkv(pap3, _history_rows(pa_meta[:, -1:, :]), rw_meta, rwkv_consts,
                       CHUNK_RWKV, SEQS_PER_STEP)
    ob_p, gd_p = _gdn(pbp3, _history_rows(pb_meta[:, -(CONV_W - 1):, :3 * W_B]), gd_meta, gdn_consts,
                      CHUNK_GDN, SEQS_PER_STEP)
    y_p = _dense_out(hp, oa_p.reshape(bp * sp, W_A), ob_p.reshape(bp * sp, W_B), *dense_out_consts,
                     tm=TM_DENSE)

    oa_t, rw_t = _rwkv_step(pas, _pa_layout(state_shift[0].astype(F32)),
                            jnp.transpose(state_rwkv[0].astype(F32), (1, 2, 3, 0)), rwkv_consts)
    oa_s = oa_t.T
    rw_s = jnp.transpose(rw_t, (3, 0, 1, 2))
    ob_s, gd_s = _gdn_step(pbs, jnp.swapaxes(state_conv[0].astype(F32), 0, 1), state_gdn[0].astype(F32),
                           gdn_consts, DEC_SEQS_PER_STEP)
    y_s = _dense_out(hs[:bs], oa_s, ob_s, *dense_out_consts, tm=bs)

    new_conv_s = jnp.concatenate([state_conv[0].astype(F32)[:, 1:, :], pbs[:bs, None, :3 * W_B]], axis=1)
    return (y_p.reshape(bp, sp, D_MODEL).astype(x_prompt.dtype),
            y_s.reshape(bs, 1, D_MODEL).astype(x_sample.dtype),
            _pair_unblock(rw_p)[None],
            _pa_unlayout(pap3[:, -1, :])[None],
            gd_p[None],
            pbp3[:, -(CONV_W - 1):, :3 * W_B][None],
            rw_s[None],
            _pa_unlayout(pas[:bs])[None],
            gd_s[None],
            new_conv_s[None])
```

```python
import functools

import jax
import jax.numpy as jnp
from jax import lax
from jax.experimental import pallas as pl
from jax.experimental.pallas import tpu as pltpu

F32 = jnp.float32
BF16 = jnp.bfloat16

LANE = 128
SUBLANE = 8
VMEM_LIMIT = 56 * 1024 * 1024

D_MODEL = 1024
D_FF = 2816
N_META = 16
H_A, N_A = 8, 64
W_A = H_A * N_A
D_DECAY, D_AAA, D_GATE = 32, 32, 96
N_A_IN = 3 * W_A + D_DECAY + D_AAA + D_GATE
H_B, DK_B, DV_B = 4, 128, 128
W_B = H_B * DV_B
CONV_W = 4
N_B_IN = 4 * W_B + 2 * H_B
LNX_EPS = 64e-5
NORM_EPS = 1e-6

PA_W = 3 * W_A + 3 * LANE
PA_DECAY = 3 * W_A
PA_AAA = PA_DECAY + LANE
PA_GATE = PA_AAA + LANE
PB_W = 4 * W_B + LANE
PB_Z = 3 * W_B
PB_GATES = 4 * W_B

TM_DENSE = 256
CHUNK_RWKV = 64
CHUNK_GDN = 128
SEQS_PER_STEP = 4
DEC_SEQS_PER_STEP = SUBLANE

NN = (((1,), (0,)), ((), ()))
NT = (((1,), (1,)), ((), ()))
TN = (((0,), (0,)), ((), ()))


def _dot(a, b):
    return jnp.dot(a, b, preferred_element_type=F32)


def _sigmoid(x):
    return 1.0 / (1.0 + jnp.exp(-x))


def _softplus(x):
    return jnp.maximum(x, 0.0) + jnp.log(1.0 + jnp.exp(-jnp.abs(x)))


def _rms(x, g):
    return x * lax.rsqrt(jnp.mean(x * x, axis=-1, keepdims=True) + NORM_EPS) * g


def _iota2(shape, dim):
    return lax.broadcasted_iota(jnp.int32, shape, dim)


def _split(x):
    hi = x.astype(BF16)
    return hi, (x - hi.astype(F32)).astype(BF16)


def _dg(a, b, dims):
    return lax.dot_general(a, b, dims, preferred_element_type=F32)


def _mm(a, b, dims=NN, mode="b"):
    if mode == "b":
        return _dg(a.astype(BF16), b.astype(BF16), dims)
    if mode == "xl":
        ah, al = _split(a)
        bh = b.astype(BF16)
        return _dg(ah, bh, dims) + _dg(al, bh, dims)
    if mode == "xr":
        ah = a.astype(BF16)
        bh, bl = _split(b)
        return _dg(ah, bh, dims) + _dg(ah, bl, dims)
    assert mode == "x3"
    ah, al = _split(a)
    bh, bl = _split(b)
    return _dg(ah, bh, dims) + (_dg(ah, bl, dims) + _dg(al, bh, dims))


def _inv_unit_lower(lows, nil, merge_mode):
    n = lows[0].shape[0]
    row = _iota2((n, n), 0)
    col = _iota2((n, n), 1)
    same_block = lambda s: (row // s) == (col // s)
    eye = (row == col).astype(F32)
    s = 2
    in_base = same_block(s)
    invs = [eye - jnp.where(in_base, low, 0.0) for low in lows]
    while s < nil:
        newly = jnp.logical_and(same_block(2 * s), jnp.logical_not(same_block(s)))
        ts = [_mm(inv, jnp.where(newly, low, 0.0), NN, merge_mode) for inv, low in zip(invs, lows)]
        invs = [inv - _mm(t, inv, NN, merge_mode) for inv, t in zip(invs, ts)]
        s *= 2
    return invs


def _swiglu(n, wg_ref, wu_ref, wd_ref):
    gate = _dot(n, wg_ref[...])
    up = _dot(n, wu_ref[...])
    act = (gate * _sigmoid(gate) * up).astype(BF16)
    return _dot(act, wd_ref[...])


def _dense_in_kernel(x_ref, g1_ref, wg_ref, wu_ref, wd_ref, gm_ref, wa_ref, wb_ref,
                     h_ref, pa_ref, pb_ref):
    x = x_ref[...]
    h = x + 0.5 * _swiglu(_rms(x, g1_ref[...]).astype(BF16), wg_ref, wu_ref, wd_ref)
    h_ref[...] = h
    n = _rms(h, gm_ref[...]).astype(BF16)
    pa_ref[...] = _dot(n, wa_ref[...])
    pb_ref[...] = _dot(n, wb_ref[...])


def _dense_out_kernel(h_ref, oa_ref, ob_ref, woa_ref, wob_ref, g2_ref, wg_ref, wu_ref, wd_ref,
                      gf_ref, y_ref):
    h = (h_ref[...] + _dot(oa_ref[...].astype(BF16), woa_ref[...])
         + _dot(ob_ref[...].astype(BF16), wob_ref[...]))
    h = h + 0.5 * _swiglu(_rms(h, g2_ref[...]).astype(BF16), wg_ref, wu_ref, wd_ref)
    y_ref[...] = _rms(h, gf_ref[...])


def _const_spec(shape):
    return pl.BlockSpec(shape, lambda *_: (0,) * len(shape), pipeline_mode=pl.Buffered(1))


def _row_spec(tm, width):
    return pl.BlockSpec((tm, width), lambda i: (i, 0))


def _dense_in(x, g1, wg, wu, wd, gm, wa, wb, tm):
    n = x.shape[0]
    consts = (g1, wg, wu, wd, gm, wa, wb)
    return pl.pallas_call(
        _dense_in_kernel,
        grid=(n // tm,),
        in_specs=[_row_spec(tm, D_MODEL)] + [_const_spec(c.shape) for c in consts],
        out_specs=[_row_spec(tm, D_MODEL), _row_spec(tm, PA_W), _row_spec(tm, PB_W)],
        out_shape=[jax.ShapeDtypeStruct((n, D_MODEL), F32),
                   jax.ShapeDtypeStruct((n, PA_W), F32),
                   jax.ShapeDtypeStruct((n, PB_W), F32)],
        compiler_params=pltpu.CompilerParams(dimension_semantics=("arbitrary",),
                                             vmem_limit_bytes=VMEM_LIMIT),
        name="dense_in",
    )(x, *consts)


def _dense_out(h, oa, ob, woa, wob, g2, wg, wu, wd, gf, tm):
    n = h.shape[0]
    consts = (woa, wob, g2, wg, wu, wd, gf)
    return pl.pallas_call(
        _dense_out_kernel,
        grid=(n // tm,),
        in_specs=[_row_spec(tm, D_MODEL), _row_spec(tm, W_A), _row_spec(tm, W_B)]
        + [_const_spec(c.shape) for c in consts],
        out_specs=_row_spec(tm, D_MODEL),
        out_shape=jax.ShapeDtypeStruct((n, D_MODEL), F32),
        compiler_params=pltpu.CompilerParams(dimension_semantics=("arbitrary",),
                                             vmem_limit_bytes=VMEM_LIMIT),
        name="dense_out",
    )(h, oa, ob, *consts)


RWKV_PREC = dict(cum="xr", seg="b", pair="b", invm="b", sread="b", akv="b", solve="b", inter="b",
                 state="x3")


def _rwkv_kernel(pa_ref, prev_ref, s0_ref, mu_ref, w0_ref, wdu_ref, a0_ref, wau_ref, wgu_ref,
                 kk_ref, ka_ref, rk_ref, lnw_ref, lnb_ref, o_ref, sout_ref, xbuf, s_scr,
                 *, chunk, nseq, shared_init):
    c = pl.program_id(1)
    C = chunk
    P = RWKV_PREC

    @pl.when(c == 0)
    def _():
        for i in range(nseq):
            j = 0 if shared_init else i
            xbuf[i, 0:SUBLANE, :] = prev_ref[j]
            s_scr[i] = s0_ref[j]

    tri = (_iota2((C, C), 0) >= _iota2((C, C), 1)).astype(F32)
    mid = C // 2 - 1

    def prep(i):
        x = pa_ref[i]
        full = jnp.concatenate([xbuf[i], x], axis=0)
        prev = pltpu.roll(full, 1, axis=0)[SUBLANE:, :]
        xm = x + (prev - x) * mu_ref[...]
        xbuf[i] = x[C - SUBLANE:, :]

        r = xm[:, 0:W_A]
        k = xm[:, W_A:2 * W_A]
        v = xm[:, 2 * W_A:3 * W_A]
        wd = xm[:, PA_DECAY:PA_DECAY + LANE]
        ad = xm[:, PA_AAA:PA_AAA + LANE]
        gd = xm[:, PA_GATE:PA_GATE + LANE]

        wl = w0_ref[...] + _dot(jnp.tanh(wd).astype(BF16), wdu_ref[...])
        lw = -jnp.exp(-_softplus(-wl) - 0.5)
        a = _sigmoid(a0_ref[...] + _dot(ad.astype(BF16), wau_ref[...]))
        g = _dot(_sigmoid(gd).astype(BF16), wgu_ref[...])
        kkr = k * kk_ref[...]
        k2 = k * (1.0 + (a - 1.0) * ka_ref[...])
        G = _mm(tri, lw, NN, P["cum"])
        return dict(r=r, v=v, a=a, g=g, kkr=kkr, k2=k2, G=G, Gx=G - lw, Gm=G[mid:mid + 1, :],
                    Gc=G[C - 1:C, :])

    seqs = [prep(i) for i in range(nseq)]

    C2 = 2 * C
    trow = _iota2((C2, C2), 0) % C
    tcol = _iota2((C2, C2), 1) % C
    incl = trow >= tcol
    strict = trow > tcol
    lane = _iota2((1, LANE), 1)
    m0 = (lane < N_A).astype(F32)
    m1 = (lane >= N_A).astype(F32)
    seg = ((_iota2((LANE, LANE), 0) // N_A) == (_iota2((LANE, LANE), 1) // N_A)).astype(F32)
    stack = lambda t: jnp.concatenate([t * m0, t * m1], axis=0)
    merged = C2 % LANE == 0

    inst = [(i, p) for i in range(nseq) for p in range(H_A // 2)]
    n = range(len(inst))
    sls = [slice(p * LANE, (p + 1) * LANE) for _, p in inst]
    take = lambda name: [seqs[i][name][:, sls[j]] for j, (i, _) in enumerate(inst)]
    rs, vs_, k2s, kkrs, as_, gs = take("r"), take("v"), take("k2"), take("kkr"), take("a"), take("g")
    Gs, Gxs, Gms, Gcs = take("G"), take("Gx"), take("Gm"), take("Gc")
    Ss = [s_scr[i, p] for i, p in inst]

    ssq = [_mm(t * t, seg, NN, P["seg"]) for t in kkrs]
    kks = [t * lax.rsqrt(q + 1e-12) for t, q in zip(kkrs, ssq)]
    kkas = [kks[j] * as_[j] for j in n]
    inv_rel = [jnp.exp(Gms[j] - Gs[j]) for j in n]
    dec = [jnp.exp(Gcs[j] - Gs[j]) for j in n]
    lhs = [jnp.concatenate([stack(kks[j] * jnp.exp(Gxs[j] - Gms[j])),
                            stack(rs[j] * jnp.exp(Gs[j] - Gms[j]))], axis=0) for j in n]
    ais = [stack(kkas[j] * inv_rel[j]) for j in n]
    kis = [stack(k2s[j] * inv_rel[j]) for j in n]
    if merged:
        pm = [_mm(lhs[j], jnp.concatenate([ais[j], kis[j]], axis=0), NT, P["pair"]) for j in n]
        pas_, pks_ = [t[:, :C2] for t in pm], [t[:, C2:] for t in pm]
    else:
        pas_ = [_mm(lhs[j], ais[j], NT, P["pair"]) for j in n]
        pks_ = [_mm(lhs[j], kis[j], NT, P["pair"]) for j in n]
    a_aa = [jnp.where(strict, t[:C2], 0.0) for t in pas_]
    a_ak = [jnp.where(strict, t[:C2], 0.0) for t in pks_]
    a_ra = [jnp.where(incl, t[C2:], 0.0) for t in pas_]
    a_rk = [jnp.where(incl, t[C2:], 0.0) for t in pks_]
    ainv = _inv_unit_lower(a_aa, C, P["invm"])

    vst = [stack(t) for t in vs_]
    akv = [_mm(a_ak[j], vst[j], NN, P["akv"]) for j in n]
    sx = [_mm(jnp.concatenate([kks[j] * jnp.exp(Gxs[j]), rs[j] * jnp.exp(Gs[j])], axis=0), Ss[j], NT,
              P["sread"]) for j in n]
    us = [-_mm(ainv[j], stack(sx[j][:C]) + akv[j], NN, P["solve"]) for j in n]
    uv = [jnp.concatenate([us[j], vst[j]], axis=0) for j in n]
    s_new = [Ss[j] * jnp.exp(Gcs[j])
             + _mm(uv[j], jnp.concatenate([stack(kkas[j] * dec[j]), stack(k2s[j] * dec[j])], axis=0),
                   TN, P["state"]) for j in n]
    for j, (i, p) in enumerate(inst):
        s_scr[i, p] = s_new[j]
    if merged:
        inter = [_mm(jnp.concatenate([a_ra[j], a_rk[j]], axis=1), uv[j], NN, P["inter"]) for j in n]
    else:
        inter = [_mm(a_ra[j], us[j], NN, P["inter"]) + _mm(a_rk[j], vst[j], NN, P["inter"]) for j in n]
    Os = [sx[j][C:] + inter[j][:C] + inter[j][C:] for j in n]
    means = [_mm(t, seg, NN, P["seg"]) * (1.0 / N_A) for t in Os]
    dlts = [t - m for t, m in zip(Os, means)]
    vars_ = [_mm(t * t, seg, NN, P["seg"]) * (1.0 / N_A) for t in dlts]
    bonus = [_mm(rs[j] * k2s[j] * rk_ref[:, sls[j]], seg, NN, P["seg"]) * vs_[j] for j in n]
    for j, (i, p) in enumerate(inst):
        on = dlts[j] * lax.rsqrt(vars_[j] + LNX_EPS) * lnw_ref[:, sls[j]] + lnb_ref[:, sls[j]]
        o_ref[i, :, sls[j]] = (on + bonus[j]) * gs[j]

    @pl.when(c == pl.num_programs(1) - 1)
    def _():
        sout_ref[...] = s_scr[...]


def _seq_specs(nseq, shared_init, hist_w, state_shape):
    zeros = (0,) * len(state_shape)
    if shared_init:
        return [pl.BlockSpec((1, SUBLANE, hist_w), lambda b, c: (0, 0, 0)),
                pl.BlockSpec((1,) + state_shape, lambda b, c: (0,) + zeros)]
    return [pl.BlockSpec((nseq, SUBLANE, hist_w), lambda b, c: (b, 0, 0)),
            pl.BlockSpec((nseq,) + state_shape, lambda b, c: (b,) + zeros)]


def _rwkv(pa, prev8, s0, consts, chunk, nseq):
    B, L, _ = pa.shape
    shared_init = prev8.shape[0] == 1 and B > 1
    state_shape = (H_A // 2, LANE, LANE)
    return pl.pallas_call(
        functools.partial(_rwkv_kernel, chunk=chunk, nseq=nseq, shared_init=shared_init),
        grid=(B // nseq, L // chunk),
        in_specs=[pl.BlockSpec((nseq, chunk, PA_W), lambda b, c: (b, c, 0))]
        + _seq_specs(nseq, shared_init, PA_W, state_shape)
        + [pl.BlockSpec(t.shape, lambda b, c: (0, 0)) for t in consts],
        out_specs=[pl.BlockSpec((nseq, chunk, W_A), lambda b, c: (b, c, 0)),
                   pl.BlockSpec((nseq,) + state_shape, lambda b, c: (b, 0, 0, 0))],
        out_shape=[jax.ShapeDtypeStruct((B, L, W_A), F32),
                   jax.ShapeDtypeStruct((B,) + state_shape, F32)],
        scratch_shapes=[pltpu.VMEM((nseq, SUBLANE, PA_W), F32),
                        pltpu.VMEM((nseq,) + state_shape, F32)],
        compiler_params=pltpu.CompilerParams(dimension_semantics=("arbitrary", "arbitrary"),
                                             vmem_limit_bytes=VMEM_LIMIT),
        name="rwkv_chunk%d" % chunk,
    )(pa, prev8, s0, *consts)


GDN_PREC = dict(cum="xr", tr="xr", kk="b", invm="b", wu="b", qk="b", sread="b", qkv="b", state="b")


def _gdn_kernel(pb_ref, hist_ref, s0_ref, cw_ref, alog_ref, dtb_ref, nw_ref, o_ref, sout_ref,
                xbuf, s_scr, *, chunk, nseq, shared_init):
    c = pl.program_id(1)
    C = chunk
    P = GDN_PREC

    @pl.when(c == 0)
    def _():
        for i in range(nseq):
            j = 0 if shared_init else i
            xbuf[i, 0:SUBLANE, :] = hist_ref[j]
            s_scr[i] = s0_ref[j]

    row = _iota2((C, C), 0)
    col = _iota2((C, C), 1)
    incl = row >= col
    strict = row > col
    tri = incl.astype(F32)
    sel = (_iota2((SUBLANE, LANE), 0) == _iota2((SUBLANE, LANE), 1)).astype(F32)

    def prep(i):
        x = pb_ref[i]
        qkv = x[:, 0:3 * W_B]
        full = jnp.concatenate([xbuf[i], qkv], axis=0)
        conv = qkv * cw_ref[CONV_W - 1:CONV_W, :]
        for j in range(1, CONV_W):
            conv = conv + pltpu.roll(full, j, axis=0)[SUBLANE:, :] * cw_ref[CONV_W - 1 - j:CONV_W - j, :]
        xbuf[i] = qkv[C - SUBLANE:, :]
        cs = conv * _sigmoid(conv)
        gates = x[:, PB_GATES:PB_GATES + LANE]
        glog = -jnp.exp(alog_ref[...]) * _softplus(gates + dtb_ref[...])
        beta = _sigmoid(gates)
        G = _mm(tri, glog, NN, P["cum"])
        Gt = _mm(sel, G, NT, P["tr"])
        return dict(cs=cs, z=x[:, PB_Z:PB_Z + W_B], beta=beta, G=G, Gt=Gt)

    seqs = [prep(i) for i in range(nseq)]

    inst = [(i, h) for i in range(nseq) for h in range(H_B)]
    n = range(len(inst))
    l2n = lambda t, c: t * (lax.rsqrt(jnp.sum(t * t, axis=-1, keepdims=True) + 1e-12) * c)
    head = lambda i, h, part: seqs[i]["cs"][:, part * W_B + h * LANE:part * W_B + (h + 1) * LANE]
    qs = [l2n(head(i, h, 0), DK_B ** -0.5) for i, h in inst]
    ks = [l2n(head(i, h, 1), 1.0) for i, h in inst]
    vs = [head(i, h, 2) for i, h in inst]
    gcols = [seqs[i]["G"][:, h:h + 1] for i, h in inst]
    grows = [seqs[i]["Gt"][h:h + 1, :] for i, h in inst]
    bcols = [seqs[i]["beta"][:, H_B + h:H_B + h + 1] for i, h in inst]
    Ss = [s_scr[i, h] for i, h in inst]

    dmats = [jnp.where(incl, jnp.exp(jnp.where(incl, gcols[j] - grows[j], 0.0)), 0.0) for j in n]
    kbs = [ks[j] * bcols[j] for j in n]
    lows = [jnp.where(strict, _mm(kbs[j], ks[j], NT, P["kk"]) * dmats[j], 0.0) for j in n]
    ainv = _inv_unit_lower(lows, C, P["invm"])
    egs = [jnp.exp(t) for t in gcols]
    wus = [_mm(ainv[j], jnp.concatenate([kbs[j] * egs[j], vs[j] * bcols[j]], axis=1), NN, P["wu"])
           for j in n]
    qks = [_mm(qs[j], ks[j], NT, P["qk"]) * dmats[j] for j in n]
    srs = [_mm(jnp.concatenate([wus[j][:, :DK_B], qs[j] * egs[j]], axis=0), Ss[j], NN, P["sread"])
           for j in n]
    v_new = [wus[j][:, DK_B:] - srs[j][:C] for j in n]
    glast = [t[C - 1:C, :] for t in gcols]
    s_new = [Ss[j] * jnp.exp(glast[j])
             + _mm(ks[j] * jnp.exp(glast[j] - gcols[j]), v_new[j], TN, P["state"]) for j in n]
    for j, (i, h) in enumerate(inst):
        s_scr[i, h] = s_new[j]
    os_ = [srs[j][C:] + _mm(qks[j], v_new[j], NN, P["qkv"]) for j in n]
    for j, (i, h) in enumerate(inst):
        o = os_[j]
        o = o * lax.rsqrt(jnp.mean(o * o, axis=-1, keepdims=True) + NORM_EPS) * nw_ref[...]
        zh = seqs[i]["z"][:, h * LANE:(h + 1) * LANE]
        o_ref[i, :, h * LANE:(h + 1) * LANE] = o * (zh * _sigmoid(zh))

    @pl.when(c == pl.num_programs(1) - 1)
    def _():
        sout_ref[...] = s_scr[...]


def _gdn(pb, hist8, s0, consts, chunk, nseq):
    B, L, _ = pb.shape
    shared_init = hist8.shape[0] == 1 and B > 1
    state_shape = (H_B, DK_B, DV_B)
    return pl.pallas_call(
        functools.partial(_gdn_kernel, chunk=chunk, nseq=nseq, shared_init=shared_init),
        grid=(B // nseq, L // chunk),
        in_specs=[pl.BlockSpec((nseq, chunk, PB_W), lambda b, c: (b, c, 0))]
        + _seq_specs(nseq, shared_init, 3 * W_B, state_shape)
        + [pl.BlockSpec(t.shape, lambda b, c: (0, 0)) for t in consts],
        out_specs=[pl.BlockSpec((nseq, chunk, W_B), lambda b, c: (b, c, 0)),
                   pl.BlockSpec((nseq,) + state_shape, lambda b, c: (b, 0, 0, 0))],
        out_shape=[jax.ShapeDtypeStruct((B, L, W_B), F32),
                   jax.ShapeDtypeStruct((B,) + state_shape, F32)],
        scratch_shapes=[pltpu.VMEM((nseq, SUBLANE, 3 * W_B), F32),
                        pltpu.VMEM((nseq,) + state_shape, F32)],
        compiler_params=pltpu.CompilerParams(dimension_semantics=("arbitrary", "arbitrary"),
                                             vmem_limit_bytes=VMEM_LIMIT),
        name="gdn_chunk%d" % chunk,
    )(pb, hist8, s0, *consts)


def _row_mask(nrow, i):
    return _iota2((nrow, 1), 0) == i


def _to_col(row, eye):
    return jnp.sum(eye * row, axis=-1, keepdims=True)


STEP_VECS = ("w", "kk", "kka", "k2", "v", "r", "g", "bon")


def _rwkv_step_prep_kernel(pa_ref, prev_ref, mu_ref, w0_ref, wdu_ref, a0_ref, wau_ref, wgu_ref,
                           kk_ref, ka_ref, rk_ref, vec_ref):
    x = pa_ref[...]
    xm = x + (prev_ref[...] - x) * mu_ref[...]
    r = xm[:, 0:W_A]
    k = xm[:, W_A:2 * W_A]
    v = xm[:, 2 * W_A:3 * W_A]
    wd = xm[:, PA_DECAY:PA_DECAY + LANE]
    ad = xm[:, PA_AAA:PA_AAA + LANE]
    gd = xm[:, PA_GATE:PA_GATE + LANE]
    wl = w0_ref[...] + _dot(jnp.tanh(wd).astype(BF16), wdu_ref[...])
    w = jnp.exp(-jnp.exp(-_softplus(-wl) - 0.5))
    a = _sigmoid(a0_ref[...] + _dot(ad.astype(BF16), wau_ref[...]))
    g = _dot(_sigmoid(gd).astype(BF16), wgu_ref[...])
    kkr = k * kk_ref[...]
    k2 = k * (1.0 + (a - 1.0) * ka_ref[...])
    bon = r * k2 * rk_ref[...]
    seg = ((_iota2((LANE, LANE), 0) // N_A) == (_iota2((LANE, LANE), 1) // N_A)).astype(F32)
    ssq = jnp.concatenate([_mm(kkr[:, p * LANE:(p + 1) * LANE] ** 2, seg, NN, RWKV_PREC["seg"])
                           for p in range(H_A // 2)], axis=-1)
    kk = kkr * lax.rsqrt(ssq + 1e-12)
    vecs = dict(w=w, kk=kk, kka=kk * a, k2=k2, v=v, r=r, g=g, bon=bon)
    for j, name in enumerate(STEP_VECS):
        vec_ref[j] = vecs[name].T


def _rwkv_step_state_kernel(vec_ref, s_ref, lnw_ref, lnb_ref, o_ref, sout_ref, o_scr):
    w, kk, kka, k2, v, r, g, bon = [vec_ref[j] for j in range(len(STEP_VECS))]
    for i in range(N_A):
        S = s_ref[0, i]
        sa = jnp.sum(S * kk, axis=0, keepdims=True)
        s_new = S * w - sa * kka + v[i:i + 1, :] * k2
        sout_ref[0, i] = s_new
        o_scr[i:i + 1, :] = jnp.sum(s_new * r, axis=0, keepdims=True)
    o = o_scr[...]
    mean = jnp.mean(o, axis=0, keepdims=True)
    dlt = o - mean
    var = jnp.mean(dlt * dlt, axis=0, keepdims=True)
    on = dlt * lax.rsqrt(var + LNX_EPS) * lnw_ref[...] + lnb_ref[...]
    o_ref[...] = (on + jnp.sum(bon, axis=0, keepdims=True) * v) * g


def _gdn_step_kernel(pb_ref, hist_ref, s_ref, cw_ref, alog_ref, dtb_ref, nw_ref, o_ref, sout_ref):
    ns = pb_ref.shape[0]
    x = pb_ref[...]
    conv = x[:, 0:3 * W_B] * cw_ref[CONV_W - 1:CONV_W, :]
    for j in range(CONV_W - 1):
        conv = conv + hist_ref[j] * cw_ref[j:j + 1, :]
    cs = conv * _sigmoid(conv)
    z = x[:, PB_Z:PB_Z + W_B]
    gates = x[:, PB_GATES:PB_GATES + LANE]
    eg = jnp.exp(-jnp.exp(alog_ref[...]) * _softplus(gates + dtb_ref[...]))
    beta = _sigmoid(gates)

    eye = (_iota2((DK_B, DK_B), 0) == _iota2((DK_B, DK_B), 1)).astype(F32)
    l2n = lambda t: t * lax.rsqrt(jnp.sum(t * t, axis=-1, keepdims=True) + 1e-12)
    for h in range(H_B):
        hs = slice(h * LANE, (h + 1) * LANE)
        q_h = l2n(cs[:, hs]) * (DK_B ** -0.5)
        k_h = l2n(cs[:, W_B + h * LANE:W_B + (h + 1) * LANE])
        v_h = cs[:, 2 * W_B + h * LANE:2 * W_B + (h + 1) * LANE]
        eg_h = eg[:, h:h + 1]
        b_h = beta[:, H_B + h:H_B + h + 1]
        seq = range(ns)
        Ss = [s_ref[i, h] for i in seq]
        k_cols = [_to_col(k_h[i:i + 1, :], eye) for i in seq]
        q_cols = [_to_col(q_h[i:i + 1, :], eye) for i in seq]
        kss = [jnp.sum(k_cols[i] * Ss[i], axis=0, keepdims=True) for i in seq]
        v_new = [b_h[i:i + 1, :] * (v_h[i:i + 1, :] - eg_h[i:i + 1, :] * kss[i]) for i in seq]
        s_new = [Ss[i] * eg_h[i:i + 1, :] + k_cols[i] * v_new[i] for i in seq]
        for i in seq:
            sout_ref[i, h] = s_new[i]
        o_rows = [jnp.sum(q_cols[i] * s_new[i], axis=0, keepdims=True) for i in seq]
        o_h = jnp.zeros((ns, DV_B), F32)
        for i in seq:
            o_h = jnp.where(_row_mask(ns, i), o_rows[i], o_h)
        o_h = o_h * lax.rsqrt(jnp.mean(o_h * o_h, axis=-1, keepdims=True) + NORM_EPS) * nw_ref[...]
        zh = z[:, hs]
        o_ref[:, hs] = o_h * (zh * _sigmoid(zh))


def _step_const_specs(ts):
    return [pl.BlockSpec(t.shape, lambda b: (0,) * t.ndim) for t in ts]


def _rwkv_step(pa, prev, s_t, consts):
    B = s_t.shape[-1]
    prep_consts = consts[:-2]
    lnw_col, lnb_col = [c.reshape(W_A, 1) for c in consts[-2:]]
    vecs = pl.pallas_call(
        _rwkv_step_prep_kernel,
        grid=(1,),
        in_specs=[pl.BlockSpec((B, PA_W), lambda b: (0, 0)),
                  pl.BlockSpec((B, PA_W), lambda b: (0, 0))] + _step_const_specs(prep_consts),
        out_specs=pl.BlockSpec((len(STEP_VECS), W_A, B), lambda b: (0, 0, 0)),
        out_shape=jax.ShapeDtypeStruct((len(STEP_VECS), W_A, B), F32),
        compiler_params=pltpu.CompilerParams(dimension_semantics=("arbitrary",),
                                             vmem_limit_bytes=VMEM_LIMIT),
        name="rwkv_step_prep",
    )(pa, prev, *prep_consts)
    state_spec = pl.BlockSpec((1, N_A, N_A, B), lambda h: (h, 0, 0, 0))
    return pl.pallas_call(
        _rwkv_step_state_kernel,
        grid=(H_A,),
        in_specs=[pl.BlockSpec((len(STEP_VECS), N_A, B), lambda h: (0, h, 0)), state_spec,
                  pl.BlockSpec((N_A, 1), lambda h: (h, 0)), pl.BlockSpec((N_A, 1), lambda h: (h, 0))],
        out_specs=[pl.BlockSpec((N_A, B), lambda h: (h, 0)), state_spec],
        out_shape=[jax.ShapeDtypeStruct((W_A, B), F32), jax.ShapeDtypeStruct(s_t.shape, F32)],
        scratch_shapes=[pltpu.VMEM((N_A, B), F32)],
        compiler_params=pltpu.CompilerParams(dimension_semantics=("arbitrary",),
                                             vmem_limit_bytes=VMEM_LIMIT),
        name="rwkv_step_state",
    )(vecs, s_t, lnw_col, lnb_col)


def _gdn_step(pb, hist, s, consts, ns):
    B = s.shape[0]
    return pl.pallas_call(
        _gdn_step_kernel,
        grid=(B // ns,),
        in_specs=[pl.BlockSpec((ns, PB_W), lambda b: (b, 0)),
                  pl.BlockSpec((CONV_W - 1, ns, 3 * W_B), lambda b: (0, b, 0)),
                  pl.BlockSpec((ns, H_B, DK_B, DV_B), lambda b: (b, 0, 0, 0))] + _step_const_specs(consts),
        out_specs=[pl.BlockSpec((ns, W_B), lambda b: (b, 0)),
                   pl.BlockSpec((ns, H_B, DK_B, DV_B), lambda b: (b, 0, 0, 0))],
        out_shape=[jax.ShapeDtypeStruct((B, W_B), F32), jax.ShapeDtypeStruct(s.shape, F32)],
        compiler_params=pltpu.CompilerParams(dimension_semantics=("arbitrary",),
                                             vmem_limit_bytes=VMEM_LIMIT),
        name="gdn_step",
    )(pb, hist, s, *consts)


def _pad_cols(t, width):
    return jnp.pad(t, [(0, 0)] * (t.ndim - 1) + [(0, width - t.shape[-1])])


def _pa_layout(t):
    main = t[..., :3 * W_A]
    dec = t[..., 3 * W_A:3 * W_A + D_DECAY]
    aaa = t[..., 3 * W_A + D_DECAY:3 * W_A + D_DECAY + D_AAA]
    gate = t[..., 3 * W_A + D_DECAY + D_AAA:]
    return jnp.concatenate([main, _pad_cols(dec, LANE), _pad_cols(aaa, LANE), _pad_cols(gate, LANE)],
                           axis=-1)


def _pa_unlayout(t):
    return jnp.concatenate([t[..., :3 * W_A], t[..., PA_DECAY:PA_DECAY + D_DECAY],
                            t[..., PA_AAA:PA_AAA + D_AAA], t[..., PA_GATE:PA_GATE + D_GATE]], axis=-1)


def _pad_rows(t, rows):
    return jnp.pad(t, [(0, rows - t.shape[0]), (0, 0)])


def _pair_unblock(s):
    B = s.shape[0]
    return jnp.stack([s[:, :, :N_A, :N_A], s[:, :, N_A:, N_A:]], axis=2).reshape(B, H_A, N_A, N_A)


def _history_rows(rows):
    B, n, w = rows.shape
    return jnp.concatenate([jnp.zeros((B, SUBLANE - n, w), F32), rows], axis=1)


def kernel(x_prompt, x_sample, state_rwkv, state_shift, state_gdn, state_conv, meta_tokens,
           g_ffn1, w_gate1, w_up1, w_down1, g_mix, w_in, mu_shift, w0, w_decay_up, a0, w_a_up,
           w_g_up, k_k, k_a, r_k, lnx_w, lnx_b, conv_w, a_log, dt_bias, gdn_norm_w, w_out,
           g_ffn2, w_gate2, w_up2, w_down2, g_final):
    assert g_ffn1.shape[0] == 1, "single trunk layer"
    bp, sp, _ = x_prompt.shape
    bs = x_sample.shape[0]
    assert x_sample.shape[1] == 1 and sp % CHUNK_RWKV == 0 and sp % CHUNK_GDN == 0
    assert (bp * sp) % TM_DENSE == 0 and bp % SEQS_PER_STEP == 0 and bs % DEC_SEQS_PER_STEP == 0
    row = lambda t: t.reshape(1, -1).astype(F32)

    ffn1 = (w_gate1[0].astype(BF16), w_up1[0].astype(BF16), w_down1[0].astype(BF16))
    ffn2 = (w_gate2[0].astype(BF16), w_up2[0].astype(BF16), w_down2[0].astype(BF16))
    win_a = _pa_layout(w_in[0][:, :N_A_IN]).astype(BF16)
    win_b = _pad_cols(w_in[0][:, N_A_IN:], PB_W).astype(BF16)
    wo_a = w_out[0][:W_A].astype(BF16)
    wo_b = w_out[0][W_A:].astype(BF16)
    dense_in_consts = (row(g_ffn1[0]), *ffn1, row(g_mix[0]), win_a, win_b)
    dense_out_consts = (wo_a, wo_b, row(g_ffn2[0]), *ffn2, row(g_final))
    rwkv_consts = (_pa_layout(row(mu_shift[0])), row(w0[0]),
                   _pad_rows(w_decay_up[0], LANE).astype(BF16), row(a0[0]),
                   _pad_rows(w_a_up[0], LANE).astype(BF16), _pad_rows(w_g_up[0], LANE).astype(BF16),
                   row(k_k[0]), row(k_a[0]), row(r_k[0]), row(lnx_w[0]), row(lnx_b[0]))
    gdn_consts = (conv_w[0].astype(F32), _pad_cols(row(a_log[0]), LANE), _pad_cols(row(dt_bias[0]), LANE),
                  row(gdn_norm_w[0]))

    xs = jnp.concatenate([x_sample[:, 0, :].astype(F32), meta_tokens.astype(F32)], axis=0)
    hs, pas, pbs = _dense_in(xs, *dense_in_consts, tm=xs.shape[0])

    pa_meta = pas[bs:][None]
    pb_meta = pbs[bs:][None]
    _, rw_meta = _rwkv(pa_meta, jnp.zeros((1, SUBLANE, PA_W), F32),
                       jnp.zeros((1, H_A // 2, LANE, LANE), F32), rwkv_consts, N_META, 1)
    _, gd_meta = _gdn(pb_meta, jnp.zeros((1, SUBLANE, 3 * W_B), F32),
                      jnp.zeros((1, H_B, DK_B, DV_B), F32), gdn_consts, N_META, 1)

    hp, pap, pbp = _dense_in(x_prompt.reshape(bp * sp, D_MODEL).astype(F32), *dense_in_consts,
                             tm=TM_DENSE)
    pap3 = pap.reshape(bp, sp, PA_W)
    pbp3 = pbp.reshape(bp, sp, PB_W)
    oa_p, rw_p = _rwkv(pap3, _history_rows(pa_meta[:, -1:, :]), rw_meta, rwkv_consts,
                       CHUNK_RWKV, SEQS_PER_STEP)
    ob_p, gd_p = _gdn(pbp3, _history_rows(pb_meta[:, -(CONV_W - 1):, :3 * W_B]), gd_meta, gdn_consts,
                      CHUNK_GDN, SEQS_PER_STEP)
    y_p = _dense_out(hp, oa_p.reshape(bp * sp, W_A), ob_p.reshape(bp * sp, W_B), *dense_out_consts,
                     tm=TM_DENSE)

    oa_t, rw_t = _rwkv_step(pas, _pa_layout(state_shift[0].astype(F32)),
                            jnp.transpose(state_rwkv[0].astype(F32), (1, 2, 3, 0)), rwkv_consts)
    oa_s = oa_t.T
    rw_s = jnp.transpose(rw_t, (3, 0, 1, 2))
    ob_s, gd_s = _gdn_step(pbs, jnp.swapaxes(state_conv[0].astype(F32), 0, 1), state_gdn[0].astype(F32),
                           gdn_consts, DEC_SEQS_PER_STEP)
    y_s = _dense_out(hs[:bs], oa_s, ob_s, *dense_out_consts, tm=bs)

    new_conv_s = jnp.concatenate([state_conv[0].astype(F32)[:, 1:, :], pbs[:bs, None, :3 * W_B]], axis=1)
    return (y_p.reshape(bp, sp, D_MODEL).astype(x_prompt.dtype),
            y_s.reshape(bs, 1, D_MODEL).astype(x_sample.dtype),
            _pair_unblock(rw_p)[None],
            _pa_unlayout(pap3[:, -1, :])[None],
            gd_p[None],
            pbp3[:, -(CONV_W - 1):, :3 * W_B][None],
            rw_s[None],
            _pa_unlayout(pas[:bs])[None],
            gd_s[None],
            new_conv_s[None])
```

```python
import functools

import jax
import jax.numpy as jnp
from jax import lax
from jax.experimental import pallas as pl
from jax.experimental.pallas import tpu as pltpu

F32 = jnp.float32
BF16 = jnp.bfloat16

LANE = 128
SUBLANE = 8
VMEM_LIMIT = 56 * 1024 * 1024

D_MODEL = 1024
D_FF = 2816
N_META = 16
H_A, N_A = 8, 64
W_A = H_A * N_A
D_DECAY, D_AAA, D_GATE = 32, 32, 96
N_A_IN = 3 * W_A + D_DECAY + D_AAA + D_GATE
H_B, DK_B, DV_B = 4, 128, 128
W_B = H_B * DV_B
CONV_W = 4
N_B_IN = 4 * W_B + 2 * H_B
LNX_EPS = 64e-5
NORM_EPS = 1e-6

PA_W = 3 * W_A + 2 * LANE
PA_SMALL = 3 * W_A
PA_GATE = PA_SMALL + LANE
SMALL_AAA = D_DECAY
SMALL_GDN_A = D_DECAY + D_AAA
SMALL_GDN_B = SMALL_GDN_A + H_B
PB_W = 4 * W_B
PB_Z = 3 * W_B

TM_DENSE = 256
CHUNK_RWKV = 64
CHUNK_GDN = 128
SEQS_PER_STEP = 4
DEC_SEQS_PER_STEP = SUBLANE

NN = (((1,), (0,)), ((), ()))
NT = (((1,), (1,)), ((), ()))
TN = (((0,), (0,)), ((), ()))


def _dot(a, b):
    return jnp.dot(a, b, preferred_element_type=F32)


def _sigmoid(x):
    return 1.0 / (1.0 + jnp.exp(-x))


def _softplus(x):
    return jnp.maximum(x, 0.0) + jnp.log(1.0 + jnp.exp(-jnp.abs(x)))


def _rms(x, g):
    return x * lax.rsqrt(jnp.mean(x * x, axis=-1, keepdims=True) + NORM_EPS) * g


def _iota2(shape, dim):
    return lax.broadcasted_iota(jnp.int32, shape, dim)


def _split(x):
    hi = x.astype(BF16)
    return hi, (x - hi.astype(F32)).astype(BF16)


def _dg(a, b, dims):
    return lax.dot_general(a, b, dims, preferred_element_type=F32)


def _mm(a, b, dims=NN, mode="b"):
    if mode == "b":
        return _dg(a.astype(BF16), b.astype(BF16), dims)
    if mode == "xl":
        ah, al = _split(a)
        bh = b.astype(BF16)
        return _dg(ah, bh, dims) + _dg(al, bh, dims)
    if mode == "xr":
        ah = a.astype(BF16)
        bh, bl = _split(b)
        return _dg(ah, bh, dims) + _dg(ah, bl, dims)
    assert mode == "x3"
    ah, al = _split(a)
    bh, bl = _split(b)
    return _dg(ah, bh, dims) + (_dg(ah, bl, dims) + _dg(al, bh, dims))


def _inv_unit_lower(lows, nil, merge_mode):
    n = lows[0].shape[0]
    row = _iota2((n, n), 0)
    col = _iota2((n, n), 1)
    same_block = lambda s: (row // s) == (col // s)
    eye = (row == col).astype(F32)
    s = 2
    in_base = same_block(s)
    invs = [eye - jnp.where(in_base, low, 0.0) for low in lows]
    while s < nil:
        newly = jnp.logical_and(same_block(2 * s), jnp.logical_not(same_block(s)))
        ts = [_mm(inv, jnp.where(newly, low, 0.0), NN, merge_mode) for inv, low in zip(invs, lows)]
        invs = [inv - _mm(t, inv, NN, merge_mode) for inv, t in zip(invs, ts)]
        s *= 2
    return invs


def _swiglu(n, wg_ref, wu_ref, wd_ref):
    gate = _dot(n, wg_ref[...])
    up = _dot(n, wu_ref[...])
    act = (gate * _sigmoid(gate) * up).astype(BF16)
    return _dot(act, wd_ref[...])


def _dense_in_kernel(x_ref, g1_ref, wg_ref, wu_ref, wd_ref, gm_ref, wa_ref, wb_ref,
                     h_ref, pa_ref, pb_ref):
    x = x_ref[...]
    h = x + 0.5 * _swiglu(_rms(x, g1_ref[...]).astype(BF16), wg_ref, wu_ref, wd_ref)
    h_ref[...] = h
    n = _rms(h, gm_ref[...]).astype(BF16)
    pa_ref[...] = _dot(n, wa_ref[...])
    pb_ref[...] = _dot(n, wb_ref[...])


def _dense_out_kernel(h_ref, oa_ref, ob_ref, woa_ref, wob_ref, g2_ref, wg_ref, wu_ref, wd_ref,
                      gf_ref, y_ref):
    h = (h_ref[...] + _dot(oa_ref[...].astype(BF16), woa_ref[...])
         + _dot(ob_ref[...].astype(BF16), wob_ref[...]))
    h = h + 0.5 * _swiglu(_rms(h, g2_ref[...]).astype(BF16), wg_ref, wu_ref, wd_ref)
    y_ref[...] = _rms(h, gf_ref[...])


def _const_spec(shape):
    return pl.BlockSpec(shape, lambda *_: (0,) * len(shape), pipeline_mode=pl.Buffered(1))


def _row_spec(tm, width):
    return pl.BlockSpec((tm, width), lambda i: (i, 0))


def _dense_in(x, g1, wg, wu, wd, gm, wa, wb, tm):
    n = x.shape[0]
    consts = (g1, wg, wu, wd, gm, wa, wb)
    return pl.pallas_call(
        _dense_in_kernel,
        grid=(n // tm,),
        in_specs=[_row_spec(tm, D_MODEL)] + [_const_spec(c.shape) for c in consts],
        out_specs=[_row_spec(tm, D_MODEL), _row_spec(tm, PA_W), _row_spec(tm, PB_W)],
        out_shape=[jax.ShapeDtypeStruct((n, D_MODEL), F32),
                   jax.ShapeDtypeStruct((n, PA_W), F32),
                   jax.ShapeDtypeStruct((n, PB_W), F32)],
        compiler_params=pltpu.CompilerParams(dimension_semantics=("arbitrary",),
                                             vmem_limit_bytes=VMEM_LIMIT),
        name="dense_in",
    )(x, *consts)


def _dense_out(h, oa, ob, woa, wob, g2, wg, wu, wd, gf, tm):
    n = h.shape[0]
    consts = (woa, wob, g2, wg, wu, wd, gf)
    return pl.pallas_call(
        _dense_out_kernel,
        grid=(n // tm,),
        in_specs=[_row_spec(tm, D_MODEL), _row_spec(tm, W_A), _row_spec(tm, W_B)]
        + [_const_spec(c.shape) for c in consts],
        out_specs=_row_spec(tm, D_MODEL),
        out_shape=jax.ShapeDtypeStruct((n, D_MODEL), F32),
        compiler_params=pltpu.CompilerParams(dimension_semantics=("arbitrary",),
                                             vmem_limit_bytes=VMEM_LIMIT),
        name="dense_out",
    )(h, oa, ob, *consts)


RWKV_PREC = dict(cum="xr", seg="b", pair="b", invm="b", sread="b", akv="b", solve="b", inter="b",
                 state="x3")


def _rwkv_kernel(pa_ref, prev_ref, s0_ref, mu_ref, w0_ref, wdu_ref, a0_ref, wau_ref, wgu_ref,
                 kk_ref, ka_ref, rk_ref, lnw_ref, lnb_ref, o_ref, sout_ref, xbuf, s_scr,
                 *, chunk, nseq, shared_init):
    c = pl.program_id(1)
    C = chunk
    P = RWKV_PREC

    @pl.when(c == 0)
    def _():
        for i in range(nseq):
            j = 0 if shared_init else i
            xbuf[i, 0:SUBLANE, :] = prev_ref[j]
            s_scr[i] = s0_ref[j]

    tri = (_iota2((C, C), 0) >= _iota2((C, C), 1)).astype(F32)
    mid = C // 2 - 1

    def prep(i):
        x = pa_ref[i]
        full = jnp.concatenate([xbuf[i], x], axis=0)
        prev = pltpu.roll(full, 1, axis=0)[SUBLANE:, :]
        xm = x + (prev - x) * mu_ref[...]
        xbuf[i] = x[C - SUBLANE:, :]

        r = xm[:, 0:W_A]
        k = xm[:, W_A:2 * W_A]
        v = xm[:, 2 * W_A:3 * W_A]
        sm = xm[:, PA_SMALL:PA_SMALL + LANE]
        gd = xm[:, PA_GATE:PA_GATE + LANE]

        wl = w0_ref[...] + _dot(jnp.tanh(sm).astype(BF16), wdu_ref[...])
        lw = -jnp.exp(-_softplus(-wl) - 0.5)
        a = _sigmoid(a0_ref[...] + _dot(sm.astype(BF16), wau_ref[...]))
        g = _dot(_sigmoid(gd).astype(BF16), wgu_ref[...])
        kkr = k * kk_ref[...]
        k2 = k * (1.0 + (a - 1.0) * ka_ref[...])
        G = _mm(tri, lw, NN, P["cum"])
        return dict(r=r, v=v, a=a, g=g, kkr=kkr, k2=k2, G=G, Gx=G - lw, Gm=G[mid:mid + 1, :],
                    Gc=G[C - 1:C, :])

    seqs = [prep(i) for i in range(nseq)]

    C2 = 2 * C
    trow = _iota2((C2, C2), 0) % C
    tcol = _iota2((C2, C2), 1) % C
    incl = trow >= tcol
    strict = trow > tcol
    lane = _iota2((1, LANE), 1)
    m0 = (lane < N_A).astype(F32)
    m1 = (lane >= N_A).astype(F32)
    seg = ((_iota2((LANE, LANE), 0) // N_A) == (_iota2((LANE, LANE), 1) // N_A)).astype(F32)
    stack = lambda t: jnp.concatenate([t * m0, t * m1], axis=0)
    merged = C2 % LANE == 0

    inst = [(i, p) for i in range(nseq) for p in range(H_A // 2)]
    n = range(len(inst))
    sls = [slice(p * LANE, (p + 1) * LANE) for _, p in inst]
    take = lambda name: [seqs[i][name][:, sls[j]] for j, (i, _) in enumerate(inst)]
    rs, vs_, k2s, kkrs, as_, gs = take("r"), take("v"), take("k2"), take("kkr"), take("a"), take("g")
    Gs, Gxs, Gms, Gcs = take("G"), take("Gx"), take("Gm"), take("Gc")
    Ss = [s_scr[i, p] for i, p in inst]

    ssq = [_mm(t * t, seg, NN, P["seg"]) for t in kkrs]
    kks = [t * lax.rsqrt(q + 1e-12) for t, q in zip(kkrs, ssq)]
    kkas = [kks[j] * as_[j] for j in n]
    inv_rel = [jnp.exp(Gms[j] - Gs[j]) for j in n]
    dec = [jnp.exp(Gcs[j] - Gs[j]) for j in n]
    lhs = [jnp.concatenate([stack(kks[j] * jnp.exp(Gxs[j] - Gms[j])),
                            stack(rs[j] * jnp.exp(Gs[j] - Gms[j]))], axis=0) for j in n]
    ais = [stack(kkas[j] * inv_rel[j]) for j in n]
    kis = [stack(k2s[j] * inv_rel[j]) for j in n]
    if merged:
        pm = [_mm(lhs[j], jnp.concatenate([ais[j], kis[j]], axis=0), NT, P["pair"]) for j in n]
        pas_, pks_ = [t[:, :C2] for t in pm], [t[:, C2:] for t in pm]
    else:
        pas_ = [_mm(lhs[j], ais[j], NT, P["pair"]) for j in n]
        pks_ = [_mm(lhs[j], kis[j], NT, P["pair"]) for j in n]
    a_aa = [jnp.where(strict, t[:C2], 0.0) for t in pas_]
    a_ak = [jnp.where(strict, t[:C2], 0.0) for t in pks_]
    a_ra = [jnp.where(incl, t[C2:], 0.0) for t in pas_]
    a_rk = [jnp.where(incl, t[C2:], 0.0) for t in pks_]
    ainv = _inv_unit_lower(a_aa, C, P["invm"])

    vst = [stack(t) for t in vs_]
    akv = [_mm(a_ak[j], vst[j], NN, P["akv"]) for j in n]
    sx = [_mm(jnp.concatenate([kks[j] * jnp.exp(Gxs[j]), rs[j] * jnp.exp(Gs[j])], axis=0), Ss[j], NT,
              P["sread"]) for j in n]
    us = [-_mm(ainv[j], stack(sx[j][:C]) + akv[j], NN, P["solve"]) for j in n]
    uv = [jnp.concatenate([us[j], vst[j]], axis=0) for j in n]
    s_new = [Ss[j] * jnp.exp(Gcs[j])
             + _mm(uv[j], jnp.concatenate([stack(kkas[j] * dec[j]), stack(k2s[j] * dec[j])], axis=0),
                   TN, P["state"]) for j in n]
    for j, (i, p) in enumerate(inst):
        s_scr[i, p] = s_new[j]
    if merged:
        inter = [_mm(jnp.concatenate([a_ra[j], a_rk[j]], axis=1), uv[j], NN, P["inter"]) for j in n]
    else:
        inter = [_mm(a_ra[j], us[j], NN, P["inter"]) + _mm(a_rk[j], vst[j], NN, P["inter"]) for j in n]
    Os = [sx[j][C:] + inter[j][:C] + inter[j][C:] for j in n]
    means = [_mm(t, seg, NN, P["seg"]) * (1.0 / N_A) for t in Os]
    dlts = [t - m for t, m in zip(Os, means)]
    vars_ = [_mm(t * t, seg, NN, P["seg"]) * (1.0 / N_A) for t in dlts]
    bonus = [_mm(rs[j] * k2s[j] * rk_ref[:, sls[j]], seg, NN, P["seg"]) * vs_[j] for j in n]
    for j, (i, p) in enumerate(inst):
        on = dlts[j] * lax.rsqrt(vars_[j] + LNX_EPS) * lnw_ref[:, sls[j]] + lnb_ref[:, sls[j]]
        o_ref[i, :, sls[j]] = (on + bonus[j]) * gs[j]

    @pl.when(c == pl.num_programs(1) - 1)
    def _():
        sout_ref[...] = s_scr[...]


def _seq_specs(nseq, shared_init, hist_w, state_shape):
    zeros = (0,) * len(state_shape)
    if shared_init:
        return [pl.BlockSpec((1, SUBLANE, hist_w), lambda b, c: (0, 0, 0)),
                pl.BlockSpec((1,) + state_shape, lambda b, c: (0,) + zeros)]
    return [pl.BlockSpec((nseq, SUBLANE, hist_w), lambda b, c: (b, 0, 0)),
            pl.BlockSpec((nseq,) + state_shape, lambda b, c: (b,) + zeros)]


def _rwkv(pa, prev8, s0, consts, chunk, nseq):
    B, L, _ = pa.shape
    shared_init = prev8.shape[0] == 1 and B > 1
    state_shape = (H_A // 2, LANE, LANE)
    return pl.pallas_call(
        functools.partial(_rwkv_kernel, chunk=chunk, nseq=nseq, shared_init=shared_init),
        grid=(B // nseq, L // chunk),
        in_specs=[pl.BlockSpec((nseq, chunk, PA_W), lambda b, c: (b, c, 0))]
        + _seq_specs(nseq, shared_init, PA_W, state_shape)
        + [pl.BlockSpec(t.shape, lambda b, c: (0, 0)) for t in consts],
        out_specs=[pl.BlockSpec((nseq, chunk, W_A), lambda b, c: (b, c, 0)),
                   pl.BlockSpec((nseq,) + state_shape, lambda b, c: (b, 0, 0, 0))],
        out_shape=[jax.ShapeDtypeStruct((B, L, W_A), F32),
                   jax.ShapeDtypeStruct((B,) + state_shape, F32)],
        scratch_shapes=[pltpu.VMEM((nseq, SUBLANE, PA_W), F32),
                        pltpu.VMEM((nseq,) + state_shape, F32)],
        compiler_params=pltpu.CompilerParams(dimension_semantics=("arbitrary", "arbitrary"),
                                             vmem_limit_bytes=VMEM_LIMIT),
        name="rwkv_chunk%d" % chunk,
    )(pa, prev8, s0, *consts)


GDN_PREC = dict(cum="xr", tr="xr", kk="b", invm="b", wu="b", qk="b", sread="b", qkv="b", state="b")


def _gdn_kernel(pb_ref, ga_ref, hist_ref, s0_ref, cw_ref, alog_ref, dtb_ref, nw_ref, o_ref, sout_ref,
                xbuf, s_scr, *, chunk, nseq, shared_init):
    c = pl.program_id(1)
    C = chunk
    P = GDN_PREC

    @pl.when(c == 0)
    def _():
        for i in range(nseq):
            j = 0 if shared_init else i
            xbuf[i, 0:SUBLANE, :] = hist_ref[j]
            s_scr[i] = s0_ref[j]

    row = _iota2((C, C), 0)
    col = _iota2((C, C), 1)
    incl = row >= col
    strict = row > col
    tri = incl.astype(F32)
    sel = (_iota2((SUBLANE, LANE), 0) + SMALL_GDN_A == _iota2((SUBLANE, LANE), 1)).astype(F32)

    def prep(i):
        x = pb_ref[i]
        qkv = x[:, 0:3 * W_B]
        full = jnp.concatenate([xbuf[i], qkv], axis=0)
        conv = qkv * cw_ref[CONV_W - 1:CONV_W, :]
        for j in range(1, CONV_W):
            conv = conv + pltpu.roll(full, j, axis=0)[SUBLANE:, :] * cw_ref[CONV_W - 1 - j:CONV_W - j, :]
        xbuf[i] = qkv[C - SUBLANE:, :]
        cs = conv * _sigmoid(conv)
        gates = ga_ref[i]
        glog = -jnp.exp(alog_ref[...]) * _softplus(gates + dtb_ref[...])
        beta = _sigmoid(gates)
        G = _mm(tri, glog, NN, P["cum"])
        Gt = _mm(sel, G, NT, P["tr"])
        return dict(cs=cs, z=x[:, PB_Z:PB_Z + W_B], beta=beta, G=G, Gt=Gt)

    seqs = [prep(i) for i in range(nseq)]

    inst = [(i, h) for i in range(nseq) for h in range(H_B)]
    n = range(len(inst))
    l2n = lambda t, c: t * (lax.rsqrt(jnp.sum(t * t, axis=-1, keepdims=True) + 1e-12) * c)
    head = lambda i, h, part: seqs[i]["cs"][:, part * W_B + h * LANE:part * W_B + (h + 1) * LANE]
    qs = [l2n(head(i, h, 0), DK_B ** -0.5) for i, h in inst]
    ks = [l2n(head(i, h, 1), 1.0) for i, h in inst]
    vs = [head(i, h, 2) for i, h in inst]
    gcols = [seqs[i]["G"][:, SMALL_GDN_A + h:SMALL_GDN_A + h + 1] for i, h in inst]
    grows = [seqs[i]["Gt"][h:h + 1, :] for i, h in inst]
    bcols = [seqs[i]["beta"][:, SMALL_GDN_B + h:SMALL_GDN_B + h + 1] for i, h in inst]
    Ss = [s_scr[i, h] for i, h in inst]

    dmats = [jnp.where(incl, jnp.exp(jnp.where(incl, gcols[j] - grows[j], 0.0)), 0.0) for j in n]
    kbs = [ks[j] * bcols[j] for j in n]
    lows = [jnp.where(strict, _mm(kbs[j], ks[j], NT, P["kk"]) * dmats[j], 0.0) for j in n]
    ainv = _inv_unit_lower(lows, C, P["invm"])
    egs = [jnp.exp(t) for t in gcols]
    wus = [_mm(ainv[j], jnp.concatenate([kbs[j] * egs[j], vs[j] * bcols[j]], axis=1), NN, P["wu"])
           for j in n]
    qks = [_mm(qs[j], ks[j], NT, P["qk"]) * dmats[j] for j in n]
    srs = [_mm(jnp.concatenate([wus[j][:, :DK_B], qs[j] * egs[j]], axis=0), Ss[j], NN, P["sread"])
           for j in n]
    v_new = [wus[j][:, DK_B:] - srs[j][:C] for j in n]
    glast = [t[C - 1:C, :] for t in gcols]
    s_new = [Ss[j] * jnp.exp(glast[j])
             + _mm(ks[j] * jnp.exp(glast[j] - gcols[j]), v_new[j], TN, P["state"]) for j in n]
    for j, (i, h) in enumerate(inst):
        s_scr[i, h] = s_new[j]
    os_ = [srs[j][C:] + _mm(qks[j], v_new[j], NN, P["qkv"]) for j in n]
    for j, (i, h) in enumerate(inst):
        o = os_[j]
        o = o * lax.rsqrt(jnp.mean(o * o, axis=-1, keepdims=True) + NORM_EPS) * nw_ref[...]
        zh = seqs[i]["z"][:, h * LANE:(h + 1) * LANE]
        o_ref[i, :, h * LANE:(h + 1) * LANE] = o * (zh * _sigmoid(zh))

    @pl.when(c == pl.num_programs(1) - 1)
    def _():
        sout_ref[...] = s_scr[...]


def _gdn(pb, pa, hist8, s0, consts, chunk, nseq):
    B, L, _ = pb.shape
    shared_init = hist8.shape[0] == 1 and B > 1
    state_shape = (H_B, DK_B, DV_B)
    return pl.pallas_call(
        functools.partial(_gdn_kernel, chunk=chunk, nseq=nseq, shared_init=shared_init),
        grid=(B // nseq, L // chunk),
        in_specs=[pl.BlockSpec((nseq, chunk, PB_W), lambda b, c: (b, c, 0)),
                  pl.BlockSpec((nseq, chunk, LANE), lambda b, c: (b, c, PA_SMALL // LANE))]
        + _seq_specs(nseq, shared_init, 3 * W_B, state_shape)
        + [pl.BlockSpec(t.shape, lambda b, c: (0, 0)) for t in consts],
        out_specs=[pl.BlockSpec((nseq, chunk, W_B), lambda b, c: (b, c, 0)),
                   pl.BlockSpec((nseq,) + state_shape, lambda b, c: (b, 0, 0, 0))],
        out_shape=[jax.ShapeDtypeStruct((B, L, W_B), F32),
                   jax.ShapeDtypeStruct((B,) + state_shape, F32)],
        scratch_shapes=[pltpu.VMEM((nseq, SUBLANE, 3 * W_B), F32),
                        pltpu.VMEM((nseq,) + state_shape, F32)],
        compiler_params=pltpu.CompilerParams(dimension_semantics=("arbitrary", "arbitrary"),
                                             vmem_limit_bytes=VMEM_LIMIT),
        name="gdn_chunk%d" % chunk,
    )(pb, pa, hist8, s0, *consts)


def _row_mask(nrow, i):
    return _iota2((nrow, 1), 0) == i


STEP_VECS = ("w", "kk", "kka", "k2", "v", "r", "g", "bon")


def _rwkv_step_prep_kernel(pa_ref, prev_ref, mu_ref, w0_ref, wdu_ref, a0_ref, wau_ref, wgu_ref,
                           kk_ref, ka_ref, rk_ref, vec_ref):
    x = pa_ref[...]
    xm = x + (prev_ref[...] - x) * mu_ref[...]
    r = xm[:, 0:W_A]
    k = xm[:, W_A:2 * W_A]
    v = xm[:, 2 * W_A:3 * W_A]
    sm = xm[:, PA_SMALL:PA_SMALL + LANE]
    gd = xm[:, PA_GATE:PA_GATE + LANE]
    wl = w0_ref[...] + _dot(jnp.tanh(sm).astype(BF16), wdu_ref[...])
    w = jnp.exp(-jnp.exp(-_softplus(-wl) - 0.5))
    a = _sigmoid(a0_ref[...] + _dot(sm.astype(BF16), wau_ref[...]))
    g = _dot(_sigmoid(gd).astype(BF16), wgu_ref[...])
    kkr = k * kk_ref[...]
    k2 = k * (1.0 + (a - 1.0) * ka_ref[...])
    bon = r * k2 * rk_ref[...]
    seg = ((_iota2((LANE, LANE), 0) // N_A) == (_iota2((LANE, LANE), 1) // N_A)).astype(F32)
    ssq = jnp.concatenate([_mm(kkr[:, p * LANE:(p + 1) * LANE] ** 2, seg, NN, RWKV_PREC["seg"])
                           for p in range(H_A // 2)], axis=-1)
    kk = kkr * lax.rsqrt(ssq + 1e-12)
    vecs = dict(w=w, kk=kk, kka=kk * a, k2=k2, v=v, r=r, g=g, bon=bon)
    for j, name in enumerate(STEP_VECS):
        vec_ref[j] = vecs[name].T


def _rwkv_step_state_kernel(vec_ref, s_ref, lnw_ref, lnb_ref, o_ref, sout_ref, o_scr):
    w, kk, kka, k2, v, r, g, bon = [vec_ref[j] for j in range(len(STEP_VECS))]
    for i in range(N_A):
        S = s_ref[0, i]
        sa = jnp.sum(S * kk, axis=0, keepdims=True)
        s_new = S * w - sa * kka + v[i:i + 1, :] * k2
        sout_ref[0, i] = s_new
        o_scr[i:i + 1, :] = jnp.sum(s_new * r, axis=0, keepdims=True)
    o = o_scr[...]
    mean = jnp.mean(o, axis=0, keepdims=True)
    dlt = o - mean
    var = jnp.mean(dlt * dlt, axis=0, keepdims=True)
    on = dlt * lax.rsqrt(var + LNX_EPS) * lnw_ref[...] + lnb_ref[...]
    o_ref[...] = (on + jnp.sum(bon, axis=0, keepdims=True) * v) * g


def _gdn_step_kernel(pb_ref, ga_ref, hist_ref, s_ref, cw_ref, alog_ref, dtb_ref, nw_ref, o_ref, sout_ref):
    ns = pb_ref.shape[0]
    x = pb_ref[...]
    conv = x[:, 0:3 * W_B] * cw_ref[CONV_W - 1:CONV_W, :]
    for j in range(CONV_W - 1):
        conv = conv + hist_ref[j] * cw_ref[j:j + 1, :]
    cs = conv * _sigmoid(conv)
    z = x[:, PB_Z:PB_Z + W_B]
    gates = ga_ref[...]
    eg = jnp.exp(-jnp.exp(alog_ref[...]) * _softplus(gates + dtb_ref[...]))
    beta = _sigmoid(gates)

    l2n = lambda t: t * lax.rsqrt(jnp.sum(t * t, axis=-1, keepdims=True) + 1e-12)
    for h in range(H_B):
        hs = slice(h * LANE, (h + 1) * LANE)
        q_h = l2n(cs[:, hs]) * (DK_B ** -0.5)
        k_h = l2n(cs[:, W_B + h * LANE:W_B + (h + 1) * LANE])
        v_h = cs[:, 2 * W_B + h * LANE:2 * W_B + (h + 1) * LANE]
        eg_h = eg[:, SMALL_GDN_A + h:SMALL_GDN_A + h + 1]
        b_h = beta[:, SMALL_GDN_B + h:SMALL_GDN_B + h + 1]
        seq = range(ns)
        Ss = [s_ref[i, h] for i in seq]
        kss = [_mm(k_h, Ss[i], NN, "xr") for i in seq]
        v_new = [b_h * (v_h - eg_h * kss[i]) for i in seq]
        s_new = [Ss[i] * eg_h[i:i + 1, :] + _mm(jnp.where(_row_mask(ns, i), k_h, 0.0), v_new[i], TN, "b")
                 for i in seq]
        for i in seq:
            sout_ref[i, h] = s_new[i]
        o_all = [_mm(q_h, s_new[i], NN, "b") for i in seq]
        o_h = jnp.zeros((ns, DV_B), F32)
        for i in seq:
            o_h = jnp.where(_row_mask(ns, i), o_all[i], o_h)
        o_h = o_h * lax.rsqrt(jnp.mean(o_h * o_h, axis=-1, keepdims=True) + NORM_EPS) * nw_ref[...]
        zh = z[:, hs]
        o_ref[:, hs] = o_h * (zh * _sigmoid(zh))


def _step_const_specs(ts):
    return [pl.BlockSpec(t.shape, lambda b: (0,) * t.ndim) for t in ts]


def _rwkv_step(pa, prev, s_t, consts):
    B = s_t.shape[-1]
    prep_consts = consts[:-2]
    lnw_col, lnb_col = [c.reshape(W_A, 1) for c in consts[-2:]]
    vecs = pl.pallas_call(
        _rwkv_step_prep_kernel,
        grid=(1,),
        in_specs=[pl.BlockSpec((B, PA_W), lambda b: (0, 0)),
                  pl.BlockSpec((B, PA_W), lambda b: (0, 0))] + _step_const_specs(prep_consts),
        out_specs=pl.BlockSpec((len(STEP_VECS), W_A, B), lambda b: (0, 0, 0)),
        out_shape=jax.ShapeDtypeStruct((len(STEP_VECS), W_A, B), F32),
        compiler_params=pltpu.CompilerParams(dimension_semantics=("arbitrary",),
                                             vmem_limit_bytes=VMEM_LIMIT),
        name="rwkv_step_prep",
    )(pa, prev, *prep_consts)
    state_spec = pl.BlockSpec((1, N_A, N_A, B), lambda h: (h, 0, 0, 0))
    return pl.pallas_call(
        _rwkv_step_state_kernel,
        grid=(H_A,),
        in_specs=[pl.BlockSpec((len(STEP_VECS), N_A, B), lambda h: (0, h, 0)), state_spec,
                  pl.BlockSpec((N_A, 1), lambda h: (h, 0)), pl.BlockSpec((N_A, 1), lambda h: (h, 0))],
        out_specs=[pl.BlockSpec((N_A, B), lambda h: (h, 0)), state_spec],
        out_shape=[jax.ShapeDtypeStruct((W_A, B), F32), jax.ShapeDtypeStruct(s_t.shape, F32)],
        scratch_shapes=[pltpu.VMEM((N_A, B), F32)],
        compiler_params=pltpu.CompilerParams(dimension_semantics=("arbitrary",),
                                             vmem_limit_bytes=VMEM_LIMIT),
        name="rwkv_step_state",
    )(vecs, s_t, lnw_col, lnb_col)


def _gdn_step(pb, pa, hist, s, consts, ns):
    B = s.shape[0]
    return pl.pallas_call(
        _gdn_step_kernel,
        grid=(B // ns,),
        in_specs=[pl.BlockSpec((ns, PB_W), lambda b: (b, 0)),
                  pl.BlockSpec((ns, LANE), lambda b: (b, PA_SMALL // LANE)),
                  pl.BlockSpec((CONV_W - 1, ns, 3 * W_B), lambda b: (0, b, 0)),
                  pl.BlockSpec((ns, H_B, DK_B, DV_B), lambda b: (b, 0, 0, 0))] + _step_const_specs(consts),
        out_specs=[pl.BlockSpec((ns, W_B), lambda b: (b, 0)),
                   pl.BlockSpec((ns, H_B, DK_B, DV_B), lambda b: (b, 0, 0, 0))],
        out_shape=[jax.ShapeDtypeStruct((B, W_B), F32), jax.ShapeDtypeStruct(s.shape, F32)],
        compiler_params=pltpu.CompilerParams(dimension_semantics=("arbitrary",),
                                             vmem_limit_bytes=VMEM_LIMIT),
        name="gdn_step",
    )(pb, pa, hist, s, *consts)


def _pad_cols(t, width):
    return jnp.pad(t, [(0, 0)] * (t.ndim - 1) + [(0, width - t.shape[-1])])


def _pa_layout(t, gdn_gates=None):
    main = t[..., :3 * W_A]
    small = t[..., 3 * W_A:3 * W_A + D_DECAY + D_AAA]
    gate = t[..., 3 * W_A + D_DECAY + D_AAA:]
    if gdn_gates is not None:
        small = jnp.concatenate([small, gdn_gates], axis=-1)
    return jnp.concatenate([main, _pad_cols(small, LANE), _pad_cols(gate, LANE)], axis=-1)


def _pa_unlayout(t):
    return jnp.concatenate([t[..., :3 * W_A], t[..., PA_SMALL:PA_SMALL + D_DECAY + D_AAA],
                            t[..., PA_GATE:PA_GATE + D_GATE]], axis=-1)


def _small_rows(t, first_row):
    return jnp.pad(t, [(first_row, LANE - first_row - t.shape[0]), (0, 0)])


def _small_lanes(t, first_lane):
    return jnp.pad(t, [(0, 0), (first_lane, LANE - first_lane - t.shape[1])])


def _pair_unblock(s):
    B = s.shape[0]
    return jnp.stack([s[:, :, :N_A, :N_A], s[:, :, N_A:, N_A:]], axis=2).reshape(B, H_A, N_A, N_A)


def _history_rows(rows):
    B, n, w = rows.shape
    return jnp.concatenate([jnp.zeros((B, SUBLANE - n, w), F32), rows], axis=1)


def kernel(x_prompt, x_sample, state_rwkv, state_shift, state_gdn, state_conv, meta_tokens,
           g_ffn1, w_gate1, w_up1, w_down1, g_mix, w_in, mu_shift, w0, w_decay_up, a0, w_a_up,
           w_g_up, k_k, k_a, r_k, lnx_w, lnx_b, conv_w, a_log, dt_bias, gdn_norm_w, w_out,
           g_ffn2, w_gate2, w_up2, w_down2, g_final):
    assert g_ffn1.shape[0] == 1, "single trunk layer"
    bp, sp, _ = x_prompt.shape
    bs = x_sample.shape[0]
    assert x_sample.shape[1] == 1 and sp % CHUNK_RWKV == 0 and sp % CHUNK_GDN == 0
    assert (bp * sp) % TM_DENSE == 0 and bp % SEQS_PER_STEP == 0 and bs % DEC_SEQS_PER_STEP == 0
    row = lambda t: t.reshape(1, -1).astype(F32)

    ffn1 = (w_gate1[0].astype(BF16), w_up1[0].astype(BF16), w_down1[0].astype(BF16))
    ffn2 = (w_gate2[0].astype(BF16), w_up2[0].astype(BF16), w_down2[0].astype(BF16))
    win_a = _pa_layout(w_in[0][:, :N_A_IN], w_in[0][:, N_A_IN + PB_W:]).astype(BF16)
    win_b = w_in[0][:, N_A_IN:N_A_IN + PB_W].astype(BF16)
    wo_a = w_out[0][:W_A].astype(BF16)
    wo_b = w_out[0][W_A:].astype(BF16)
    dense_in_consts = (row(g_ffn1[0]), *ffn1, row(g_mix[0]), win_a, win_b)
    dense_out_consts = (wo_a, wo_b, row(g_ffn2[0]), *ffn2, row(g_final))
    rwkv_consts = (_pa_layout(row(mu_shift[0])), row(w0[0]),
                   _small_rows(w_decay_up[0], 0).astype(BF16), row(a0[0]),
                   _small_rows(w_a_up[0], SMALL_AAA).astype(BF16), _small_rows(w_g_up[0], 0).astype(BF16),
                   row(k_k[0]), row(k_a[0]), row(r_k[0]), row(lnx_w[0]), row(lnx_b[0]))
    gdn_consts = (conv_w[0].astype(F32), _small_lanes(row(a_log[0]), SMALL_GDN_A),
                  _small_lanes(row(dt_bias[0]), SMALL_GDN_A), row(gdn_norm_w[0]))

    xs = jnp.concatenate([x_sample[:, 0, :].astype(F32), meta_tokens.astype(F32)], axis=0)
    hs, pas, pbs = _dense_in(xs, *dense_in_consts, tm=xs.shape[0])

    pa_meta = pas[bs:][None]
    pb_meta = pbs[bs:][None]
    _, rw_meta = _rwkv(pa_meta, jnp.zeros((1, SUBLANE, PA_W), F32),
                       jnp.zeros((1, H_A // 2, LANE, LANE), F32), rwkv_consts, N_META, 1)
    _, gd_meta = _gdn(pb_meta, pa_meta, jnp.zeros((1, SUBLANE, 3 * W_B), F32),
                      jnp.zeros((1, H_B, DK_B, DV_B), F32), gdn_consts, N_META, 1)

    hp, pap, pbp = _dense_in(x_prompt.reshape(bp * sp, D_MODEL).astype(F32), *dense_in_consts,
                             tm=TM_DENSE)
    pap3 = pap.reshape(bp, sp, PA_W)
    pbp3 = pbp.reshape(bp, sp, PB_W)
    oa_p, rw_p = _rwkv(pap3, _history_rows(pa_meta[:, -1:, :]), rw_meta, rwkv_consts,
                       CHUNK_RWKV, SEQS_PER_STEP)
    ob_p, gd_p = _gdn(pbp3, pap3, _history_rows(pb_meta[:, -(CONV_W - 1):, :3 * W_B]), gd_meta, gdn_consts,
                      CHUNK_GDN, SEQS_PER_STEP)
    y_p = _dense_out(hp, oa_p.reshape(bp * sp, W_A), ob_p.reshape(bp * sp, W_B), *dense_out_consts,
                     tm=TM_DENSE)

    oa_t, rw_t = _rwkv_step(pas, _pa_layout(state_shift[0].astype(F32)),
                            jnp.transpose(state_rwkv[0].astype(F32), (1, 2, 3, 0)), rwkv_consts)
    oa_s = oa_t.T
    rw_s = jnp.transpose(rw_t, (3, 0, 1, 2))
    ob_s, gd_s = _gdn_step(pbs, pas, jnp.swapaxes(state_conv[0].astype(F32), 0, 1), state_gdn[0].astype(F32),
                           gdn_consts, DEC_SEQS_PER_STEP)
    y_s = _dense_out(hs[:bs], oa_s, ob_s, *dense_out_consts, tm=bs)

    new_conv_s = jnp.concatenate([state_conv[0].astype(F32)[:, 1:, :], pbs[:bs, None, :3 * W_B]], axis=1)
    return (y_p.reshape(bp, sp, D_MODEL).astype(x_prompt.dtype),
            y_s.reshape(bs, 1, D_MODEL).astype(x_sample.dtype),
            _pair_unblock(rw_p)[None],
            _pa_unlayout(pap3[:, -1, :])[None],
            gd_p[None],
            pbp3[:, -(CONV_W - 1):, :3 * W_B][None],
            rw_s[None],
            _pa_unlayout(pas[:bs])[None],
            gd_s[None],
            new_conv_s[None])
```

```python
import functools

import jax
import jax.numpy as jnp
from jax import lax
from jax.experimental import pallas as pl
from jax.experimental.pallas import tpu as pltpu

F32 = jnp.float32
BF16 = jnp.bfloat16

LANE = 128
SUBLANE = 8
VMEM_LIMIT = 56 * 1024 * 1024

D_MODEL = 1024
D_FF = 2816
N_META = 16
H_A, N_A = 8, 64
W_A = H_A * N_A
D_DECAY, D_AAA, D_GATE = 32, 32, 96
N_A_IN = 3 * W_A + D_DECAY + D_AAA + D_GATE
H_B, DK_B, DV_B = 4, 128, 128
W_B = H_B * DV_B
CONV_W = 4
N_B_IN = 4 * W_B + 2 * H_B
LNX_EPS = 64e-5
NORM_EPS = 1e-6

PA_W = 3 * W_A + 2 * LANE
PA_SMALL = 3 * W_A
PA_GATE = PA_SMALL + LANE
SMALL_AAA = D_DECAY
SMALL_GDN_A = D_DECAY + D_AAA
SMALL_GDN_B = SMALL_GDN_A + H_B
PB_W = 4 * W_B
PB_Z = 3 * W_B

TM_DENSE = 256
TM_DENSE_OUT = 512
CHUNK_RWKV = 64
CHUNK_GDN = 128
SEQS_PER_STEP = 4
DEC_SEQS_PER_STEP = SUBLANE

NN = (((1,), (0,)), ((), ()))
NT = (((1,), (1,)), ((), ()))
TN = (((0,), (0,)), ((), ()))


def _dot(a, b):
    return jnp.dot(a, b, preferred_element_type=F32)


def _sigmoid(x):
    return 1.0 / (1.0 + jnp.exp(-x))


def _softplus(x):
    return jnp.maximum(x, 0.0) + jnp.log(1.0 + jnp.exp(-jnp.abs(x)))


def _rms(x, g):
    return x * lax.rsqrt(jnp.mean(x * x, axis=-1, keepdims=True) + NORM_EPS) * g


def _iota2(shape, dim):
    return lax.broadcasted_iota(jnp.int32, shape, dim)


def _split(x):
    hi = x.astype(BF16)
    return hi, (x - hi.astype(F32)).astype(BF16)


def _dg(a, b, dims):
    return lax.dot_general(a, b, dims, preferred_element_type=F32)


def _mm(a, b, dims=NN, mode="b"):
    if mode == "b":
        return _dg(a.astype(BF16), b.astype(BF16), dims)
    if mode == "xl":
        ah, al = _split(a)
        bh = b.astype(BF16)
        return _dg(ah, bh, dims) + _dg(al, bh, dims)
    if mode == "xr":
        ah = a.astype(BF16)
        bh, bl = _split(b)
        return _dg(ah, bh, dims) + _dg(ah, bl, dims)
    assert mode == "x3"
    ah, al = _split(a)
    bh, bl = _split(b)
    return _dg(ah, bh, dims) + (_dg(ah, bl, dims) + _dg(al, bh, dims))


def _inv_unit_lower(lows, nil, merge_mode):
    n = lows[0].shape[0]
    row = _iota2((n, n), 0)
    col = _iota2((n, n), 1)
    same_block = lambda s: (row // s) == (col // s)
    eye = (row == col).astype(F32)
    s = 2
    in_base = same_block(s)
    invs = [eye - jnp.where(in_base, low, 0.0) for low in lows]
    while s < nil:
        newly = jnp.logical_and(same_block(2 * s), jnp.logical_not(same_block(s)))
        ts = [_mm(inv, jnp.where(newly, low, 0.0), NN, merge_mode) for inv, low in zip(invs, lows)]
        invs = [inv - _mm(t, inv, NN, merge_mode) for inv, t in zip(invs, ts)]
        s *= 2
    return invs


def _swiglu(n, wg_ref, wu_ref, wd_ref):
    gate = _dot(n, wg_ref[...])
    up = _dot(n, wu_ref[...])
    act = (gate * _sigmoid(gate) * up).astype(BF16)
    return _dot(act, wd_ref[...])


def _dense_in_kernel(x_ref, g1_ref, wg_ref, wu_ref, wd_ref, gm_ref, wa_ref, wb_ref,
                     h_ref, pa_ref, pb_ref):
    x = x_ref[...]
    h = x + 0.5 * _swiglu(_rms(x, g1_ref[...]).astype(BF16), wg_ref, wu_ref, wd_ref)
    h_ref[...] = h
    n = _rms(h, gm_ref[...]).astype(BF16)
    pa_ref[...] = _dot(n, wa_ref[...])
    pb_ref[...] = _dot(n, wb_ref[...])


def _dense_out_kernel(h_ref, oa_ref, ob_ref, woa_ref, wob_ref, g2_ref, wg_ref, wu_ref, wd_ref,
                      gf_ref, y_ref):
    h = (h_ref[...] + _dot(oa_ref[...].astype(BF16), woa_ref[...])
         + _dot(ob_ref[...].astype(BF16), wob_ref[...]))
    h = h + 0.5 * _swiglu(_rms(h, g2_ref[...]).astype(BF16), wg_ref, wu_ref, wd_ref)
    y_ref[...] = _rms(h, gf_ref[...])


def _const_spec(shape):
    return pl.BlockSpec(shape, lambda *_: (0,) * len(shape), pipeline_mode=pl.Buffered(1))


def _row_spec(tm, width):
    return pl.BlockSpec((tm, width), lambda i: (i, 0))


def _dense_in(x, g1, wg, wu, wd, gm, wa, wb, tm):
    n = x.shape[0]
    consts = (g1, wg, wu, wd, gm, wa, wb)
    return pl.pallas_call(
        _dense_in_kernel,
        grid=(n // tm,),
        in_specs=[_row_spec(tm, D_MODEL)] + [_const_spec(c.shape) for c in consts],
        out_specs=[_row_spec(tm, D_MODEL), _row_spec(tm, PA_W), _row_spec(tm, PB_W)],
        out_shape=[jax.ShapeDtypeStruct((n, D_MODEL), F32),
                   jax.ShapeDtypeStruct((n, PA_W), F32),
                   jax.ShapeDtypeStruct((n, PB_W), F32)],
        compiler_params=pltpu.CompilerParams(dimension_semantics=("arbitrary",),
                                             vmem_limit_bytes=VMEM_LIMIT),
        name="dense_in",
    )(x, *consts)


def _dense_out(h, oa, ob, woa, wob, g2, wg, wu, wd, gf, tm):
    n = h.shape[0]
    consts = (woa, wob, g2, wg, wu, wd, gf)
    return pl.pallas_call(
        _dense_out_kernel,
        grid=(n // tm,),
        in_specs=[_row_spec(tm, D_MODEL), _row_spec(tm, W_A), _row_spec(tm, W_B)]
        + [_const_spec(c.shape) for c in consts],
        out_specs=_row_spec(tm, D_MODEL),
        out_shape=jax.ShapeDtypeStruct((n, D_MODEL), F32),
        compiler_params=pltpu.CompilerParams(dimension_semantics=("arbitrary",),
                                             vmem_limit_bytes=VMEM_LIMIT),
        name="dense_out",
    )(h, oa, ob, *consts)


RWKV_PREC = dict(cum="xr", seg="b", pair="b", invm="b", sread="b", akv="b", solve="b", inter="b",
                 state="x3")


def _rwkv_kernel(pa_ref, prev_ref, s0_ref, mu_ref, w0_ref, wdu_ref, a0_ref, wau_ref, wgu_ref,
                 kk_ref, ka_ref, rk_ref, lnw_ref, lnb_ref, o_ref, sout_ref, xbuf, s_scr,
                 *, chunk, nseq, shared_init):
    c = pl.program_id(1)
    C = chunk
    P = RWKV_PREC

    @pl.when(c == 0)
    def _():
        for i in range(nseq):
            j = 0 if shared_init else i
            xbuf[i, 0:SUBLANE, :] = prev_ref[j]
            s_scr[i] = s0_ref[j]

    tri = (_iota2((C, C), 0) >= _iota2((C, C), 1)).astype(F32)
    mid = C // 2 - 1

    def prep(i):
        x = pa_ref[i]
        full = jnp.concatenate([xbuf[i], x], axis=0)
        prev = pltpu.roll(full, 1, axis=0)[SUBLANE:, :]
        xm = x + (prev - x) * mu_ref[...]
        xbuf[i] = x[C - SUBLANE:, :]

        r = xm[:, 0:W_A]
        k = xm[:, W_A:2 * W_A]
        v = xm[:, 2 * W_A:3 * W_A]
        sm = xm[:, PA_SMALL:PA_SMALL + LANE]
        gd = xm[:, PA_GATE:PA_GATE + LANE]

        wl = w0_ref[...] + _dot(jnp.tanh(sm).astype(BF16), wdu_ref[...])
        lw = -jnp.exp(-_softplus(-wl) - 0.5)
        a = _sigmoid(a0_ref[...] + _dot(sm.astype(BF16), wau_ref[...]))
        g = _dot(_sigmoid(gd).astype(BF16), wgu_ref[...])
        kkr = k * kk_ref[...]
        k2 = k * (1.0 + (a - 1.0) * ka_ref[...])
        G = _mm(tri, lw, NN, P["cum"])
        return dict(r=r, v=v, a=a, g=g, kkr=kkr, k2=k2, G=G, Gx=G - lw, Gm=G[mid:mid + 1, :],
                    Gc=G[C - 1:C, :])

    seqs = [prep(i) for i in range(nseq)]

    C2 = 2 * C
    trow = _iota2((C2, C2), 0) % C
    tcol = _iota2((C2, C2), 1) % C
    incl = trow >= tcol
    strict = trow > tcol
    lane = _iota2((1, LANE), 1)
    m0 = (lane < N_A).astype(F32)
    m1 = (lane >= N_A).astype(F32)
    seg = ((_iota2((LANE, LANE), 0) // N_A) == (_iota2((LANE, LANE), 1) // N_A)).astype(F32)
    stack = lambda t: jnp.concatenate([t * m0, t * m1], axis=0)
    merged = C2 % LANE == 0

    inst = [(i, p) for i in range(nseq) for p in range(H_A // 2)]
    n = range(len(inst))
    sls = [slice(p * LANE, (p + 1) * LANE) for _, p in inst]
    take = lambda name: [seqs[i][name][:, sls[j]] for j, (i, _) in enumerate(inst)]
    rs, vs_, k2s, kkrs, as_, gs = take("r"), take("v"), take("k2"), take("kkr"), take("a"), take("g")
    Gs, Gxs, Gms, Gcs = take("G"), take("Gx"), take("Gm"), take("Gc")
    Ss = [s_scr[i, p] for i, p in inst]

    ssq = [_mm(t * t, seg, NN, P["seg"]) for t in kkrs]
    kks = [t * lax.rsqrt(q + 1e-12) for t, q in zip(kkrs, ssq)]
    kkas = [kks[j] * as_[j] for j in n]
    inv_rel = [jnp.exp(Gms[j] - Gs[j]) for j in n]
    dec = [jnp.exp(Gcs[j] - Gs[j]) for j in n]
    lhs = [jnp.concatenate([stack(kks[j] * jnp.exp(Gxs[j] - Gms[j])),
                            stack(rs[j] * jnp.exp(Gs[j] - Gms[j]))], axis=0) for j in n]
    ais = [stack(kkas[j] * inv_rel[j]) for j in n]
    kis = [stack(k2s[j] * inv_rel[j]) for j in n]
    if merged:
        pm = [_mm(lhs[j], jnp.concatenate([ais[j], kis[j]], axis=0), NT, P["pair"]) for j in n]
        pas_, pks_ = [t[:, :C2] for t in pm], [t[:, C2:] for t in pm]
    else:
        pas_ = [_mm(lhs[j], ais[j], NT, P["pair"]) for j in n]
        pks_ = [_mm(lhs[j], kis[j], NT, P["pair"]) for j in n]
    a_aa = [jnp.where(strict, t[:C2], 0.0) for t in pas_]
    a_ak = [jnp.where(strict, t[:C2], 0.0) for t in pks_]
    a_ra = [jnp.where(incl, t[C2:], 0.0) for t in pas_]
    a_rk = [jnp.where(incl, t[C2:], 0.0) for t in pks_]
    ainv = _inv_unit_lower(a_aa, C, P["invm"])

    vst = [stack(t) for t in vs_]
    akv = [_mm(a_ak[j], vst[j], NN, P["akv"]) for j in n]
    sx = [_mm(jnp.concatenate([kks[j] * jnp.exp(Gxs[j]), rs[j] * jnp.exp(Gs[j])], axis=0), Ss[j], NT,
              P["sread"]) for j in n]
    us = [-_mm(ainv[j], stack(sx[j][:C]) + akv[j], NN, P["solve"]) for j in n]
    uv = [jnp.concatenate([us[j], vst[j]], axis=0) for j in n]
    s_new = [Ss[j] * jnp.exp(Gcs[j])
             + _mm(uv[j], jnp.concatenate([stack(kkas[j] * dec[j]), stack(k2s[j] * dec[j])], axis=0),
                   TN, P["state"]) for j in n]
    for j, (i, p) in enumerate(inst):
        s_scr[i, p] = s_new[j]
    if merged:
        inter = [_mm(jnp.concatenate([a_ra[j], a_rk[j]], axis=1), uv[j], NN, P["inter"]) for j in n]
    else:
        inter = [_mm(a_ra[j], us[j], NN, P["inter"]) + _mm(a_rk[j], vst[j], NN, P["inter"]) for j in n]
    Os = [sx[j][C:] + inter[j][:C] + inter[j][C:] for j in n]
    means = [_mm(t, seg, NN, P["seg"]) * (1.0 / N_A) for t in Os]
    dlts = [t - m for t, m in zip(Os, means)]
    vars_ = [_mm(t * t, seg, NN, P["seg"]) * (1.0 / N_A) for t in dlts]
    bonus = [_mm(rs[j] * k2s[j] * rk_ref[:, sls[j]], seg, NN, P["seg"]) * vs_[j] for j in n]
    for j, (i, p) in enumerate(inst):
        on = dlts[j] * lax.rsqrt(vars_[j] + LNX_EPS) * lnw_ref[:, sls[j]] + lnb_ref[:, sls[j]]
        o_ref[i, :, sls[j]] = (on + bonus[j]) * gs[j]

    @pl.when(c == pl.num_programs(1) - 1)
    def _():
        sout_ref[...] = s_scr[...]


def _seq_specs(nseq, shared_init, hist_w, state_shape):
    zeros = (0,) * len(state_shape)
    if shared_init:
        return [pl.BlockSpec((1, SUBLANE, hist_w), lambda b, c: (0, 0, 0)),
                pl.BlockSpec((1,) + state_shape, lambda b, c: (0,) + zeros)]
    return [pl.BlockSpec((nseq, SUBLANE, hist_w), lambda b, c: (b, 0, 0)),
            pl.BlockSpec((nseq,) + state_shape, lambda b, c: (b,) + zeros)]


def _rwkv(pa, prev8, s0, consts, chunk, nseq):
    B, L, _ = pa.shape
    shared_init = prev8.shape[0] == 1 and B > 1
    state_shape = (H_A // 2, LANE, LANE)
    return pl.pallas_call(
        functools.partial(_rwkv_kernel, chunk=chunk, nseq=nseq, shared_init=shared_init),
        grid=(B // nseq, L // chunk),
        in_specs=[pl.BlockSpec((nseq, chunk, PA_W), lambda b, c: (b, c, 0))]
        + _seq_specs(nseq, shared_init, PA_W, state_shape)
        + [pl.BlockSpec(t.shape, lambda b, c: (0, 0)) for t in consts],
        out_specs=[pl.BlockSpec((nseq, chunk, W_A), lambda b, c: (b, c, 0)),
                   pl.BlockSpec((nseq,) + state_shape, lambda b, c: (b, 0, 0, 0))],
        out_shape=[jax.ShapeDtypeStruct((B, L, W_A), F32),
                   jax.ShapeDtypeStruct((B,) + state_shape, F32)],
        scratch_shapes=[pltpu.VMEM((nseq, SUBLANE, PA_W), F32),
                        pltpu.VMEM((nseq,) + state_shape, F32)],
        compiler_params=pltpu.CompilerParams(dimension_semantics=("arbitrary", "arbitrary"),
                                             vmem_limit_bytes=VMEM_LIMIT),
        name="rwkv_chunk%d" % chunk,
    )(pa, prev8, s0, *consts)


GDN_PREC = dict(cum="xr", tr="xr", kk="b", invm="b", wu="b", qk="b", sread="b", qkv="b", state="b")


def _gdn_kernel(pb_ref, ga_ref, hist_ref, s0_ref, cw_ref, alog_ref, dtb_ref, nw_ref, o_ref, sout_ref,
                xbuf, s_scr, *, chunk, nseq, shared_init):
    c = pl.program_id(1)
    C = chunk
    P = GDN_PREC

    @pl.when(c == 0)
    def _():
        for i in range(nseq):
            j = 0 if shared_init else i
            xbuf[i, 0:SUBLANE, :] = hist_ref[j]
            s_scr[i] = s0_ref[j]

    row = _iota2((C, C), 0)
    col = _iota2((C, C), 1)
    incl = row >= col
    strict = row > col
    tri = incl.astype(F32)
    sel = (_iota2((SUBLANE, LANE), 0) + SMALL_GDN_A == _iota2((SUBLANE, LANE), 1)).astype(F32)

    def prep(i):
        x = pb_ref[i]
        qkv = x[:, 0:3 * W_B]
        full = jnp.concatenate([xbuf[i], qkv], axis=0)
        conv = qkv * cw_ref[CONV_W - 1:CONV_W, :]
        for j in range(1, CONV_W):
            conv = conv + pltpu.roll(full, j, axis=0)[SUBLANE:, :] * cw_ref[CONV_W - 1 - j:CONV_W - j, :]
        xbuf[i] = qkv[C - SUBLANE:, :]
        cs = conv * _sigmoid(conv)
        gates = ga_ref[i]
        glog = -jnp.exp(alog_ref[...]) * _softplus(gates + dtb_ref[...])
        beta = _sigmoid(gates)
        G = _mm(tri, glog, NN, P["cum"])
        Gt = _mm(sel, G, NT, P["tr"])
        return dict(cs=cs, z=x[:, PB_Z:PB_Z + W_B], beta=beta, G=G, Gt=Gt)

    seqs = [prep(i) for i in range(nseq)]

    inst = [(i, h) for i in range(nseq) for h in range(H_B)]
    n = range(len(inst))
    l2n = lambda t, c: t * (lax.rsqrt(jnp.sum(t * t, axis=-1, keepdims=True) + 1e-12) * c)
    head = lambda i, h, part: seqs[i]["cs"][:, part * W_B + h * LANE:part * W_B + (h + 1) * LANE]
    qs = [l2n(head(i, h, 0), DK_B ** -0.5) for i, h in inst]
    ks = [l2n(head(i, h, 1), 1.0) for i, h in inst]
    vs = [head(i, h, 2) for i, h in inst]
    gcols = [seqs[i]["G"][:, SMALL_GDN_A + h:SMALL_GDN_A + h + 1] for i, h in inst]
    grows = [seqs[i]["Gt"][h:h + 1, :] for i, h in inst]
    bcols = [seqs[i]["beta"][:, SMALL_GDN_B + h:SMALL_GDN_B + h + 1] for i, h in inst]
    Ss = [s_scr[i, h] for i, h in inst]

    dmats = [jnp.where(incl, jnp.exp(jnp.where(incl, gcols[j] - grows[j], 0.0)), 0.0) for j in n]
    kbs = [ks[j] * bcols[j] for j in n]
    lows = [jnp.where(strict, _mm(kbs[j], ks[j], NT, P["kk"]) * dmats[j], 0.0) for j in n]
    ainv = _inv_unit_lower(lows, C, P["invm"])
    egs = [jnp.exp(t) for t in gcols]
    wus = [_mm(ainv[j], jnp.concatenate([kbs[j] * egs[j], vs[j] * bcols[j]], axis=1), NN, P["wu"])
           for j in n]
    qks = [_mm(qs[j], ks[j], NT, P["qk"]) * dmats[j] for j in n]
    srs = [_mm(jnp.concatenate([wus[j][:, :DK_B], qs[j] * egs[j]], axis=0), Ss[j], NN, P["sread"])
           for j in n]
    v_new = [wus[j][:, DK_B:] - srs[j][:C] for j in n]
    glast = [t[C - 1:C, :] for t in gcols]
    s_new = [Ss[j] * jnp.exp(glast[j])
             + _mm(ks[j] * jnp.exp(glast[j] - gcols[j]), v_new[j], TN, P["state"]) for j in n]
    for j, (i, h) in enumerate(inst):
        s_scr[i, h] = s_new[j]
    os_ = [srs[j][C:] + _mm(qks[j], v_new[j], NN, P["qkv"]) for j in n]
    for j, (i, h) in enumerate(inst):
        o = os_[j]
        o = o * lax.rsqrt(jnp.mean(o * o, axis=-1, keepdims=True) + NORM_EPS) * nw_ref[...]
        zh = seqs[i]["z"][:, h * LANE:(h + 1) * LANE]
        o_ref[i, :, h * LANE:(h + 1) * LANE] = o * (zh * _sigmoid(zh))

    @pl.when(c == pl.num_programs(1) - 1)
    def _():
        sout_ref[...] = s_scr[...]


def _gdn(pb, pa, hist8, s0, consts, chunk, nseq):
    B, L, _ = pb.shape
    shared_init = hist8.shape[0] == 1 and B > 1
    state_shape = (H_B, DK_B, DV_B)
    return pl.pallas_call(
        functools.partial(_gdn_kernel, chunk=chunk, nseq=nseq, shared_init=shared_init),
        grid=(B // nseq, L // chunk),
        in_specs=[pl.BlockSpec((nseq, chunk, PB_W), lambda b, c: (b, c, 0)),
                  pl.BlockSpec((nseq, chunk, LANE), lambda b, c: (b, c, PA_SMALL // LANE))]
        + _seq_specs(nseq, shared_init, 3 * W_B, state_shape)
        + [pl.BlockSpec(t.shape, lambda b, c: (0, 0)) for t in consts],
        out_specs=[pl.BlockSpec((nseq, chunk, W_B), lambda b, c: (b, c, 0)),
                   pl.BlockSpec((nseq,) + state_shape, lambda b, c: (b, 0, 0, 0))],
        out_shape=[jax.ShapeDtypeStruct((B, L, W_B), F32),
                   jax.ShapeDtypeStruct((B,) + state_shape, F32)],
        scratch_shapes=[pltpu.VMEM((nseq, SUBLANE, 3 * W_B), F32),
                        pltpu.VMEM((nseq,) + state_shape, F32)],
        compiler_params=pltpu.CompilerParams(dimension_semantics=("arbitrary", "arbitrary"),
                                             vmem_limit_bytes=VMEM_LIMIT),
        name="gdn_chunk%d" % chunk,
    )(pb, pa, hist8, s0, *consts)


def _row_mask(nrow, i):
    return _iota2((nrow, 1), 0) == i


STEP_VECS = ("w", "kk", "kka", "k2", "v", "r", "g", "bon")


def _rwkv_step_prep_kernel(pa_ref, prev_ref, mu_ref, w0_ref, wdu_ref, a0_ref, wau_ref, wgu_ref,
                           kk_ref, ka_ref, rk_ref, vec_ref):
    x = pa_ref[...]
    xm = x + (prev_ref[...] - x) * mu_ref[...]
    r = xm[:, 0:W_A]
    k = xm[:, W_A:2 * W_A]
    v = xm[:, 2 * W_A:3 * W_A]
    sm = xm[:, PA_SMALL:PA_SMALL + LANE]
    gd = xm[:, PA_GATE:PA_GATE + LANE]
    wl = w0_ref[...] + _dot(jnp.tanh(sm).astype(BF16), wdu_ref[...])
    w = jnp.exp(-jnp.exp(-_softplus(-wl) - 0.5))
    a = _sigmoid(a0_ref[...] + _dot(sm.astype(BF16), wau_ref[...]))
    g = _dot(_sigmoid(gd).astype(BF16), wgu_ref[...])
    kkr = k * kk_ref[...]
    k2 = k * (1.0 + (a - 1.0) * ka_ref[...])
    bon = r * k2 * rk_ref[...]
    seg = ((_iota2((LANE, LANE), 0) // N_A) == (_iota2((LANE, LANE), 1) // N_A)).astype(F32)
    ssq = jnp.concatenate([_mm(kkr[:, p * LANE:(p + 1) * LANE] ** 2, seg, NN, RWKV_PREC["seg"])
                           for p in range(H_A // 2)], axis=-1)
    kk = kkr * lax.rsqrt(ssq + 1e-12)
    vecs = dict(w=w, kk=kk, kka=kk * a, k2=k2, v=v, r=r, g=g, bon=bon)
    for j, name in enumerate(STEP_VECS):
        vec_ref[j] = vecs[name].T


def _rwkv_step_state_kernel(vec_ref, s_ref, lnw_ref, lnb_ref, o_ref, sout_ref, o_scr):
    w, kk, kka, k2, v, r, g, bon = [vec_ref[j] for j in range(len(STEP_VECS))]
    for i in range(N_A):
        S = s_ref[0, i]
        sa = jnp.sum(S * kk, axis=0, keepdims=True)
        s_new = S * w - sa * kka + v[i:i + 1, :] * k2
        sout_ref[0, i] = s_new
        o_scr[i:i + 1, :] = jnp.sum(s_new * r, axis=0, keepdims=True)
    o = o_scr[...]
    mean = jnp.mean(o, axis=0, keepdims=True)
    dlt = o - mean
    var = jnp.mean(dlt * dlt, axis=0, keepdims=True)
    on = dlt * lax.rsqrt(var + LNX_EPS) * lnw_ref[...] + lnb_ref[...]
    o_ref[...] = (on + jnp.sum(bon, axis=0, keepdims=True) * v) * g


def _gdn_step_kernel(pb_ref, ga_ref, hist_ref, s_ref, cw_ref, alog_ref, dtb_ref, nw_ref, o_ref, sout_ref):
    ns = pb_ref.shape[0]
    x = pb_ref[...]
    conv = x[:, 0:3 * W_B] * cw_ref[CONV_W - 1:CONV_W, :]
    for j in range(CONV_W - 1):
        conv = conv + hist_ref[j] * cw_ref[j:j + 1, :]
    cs = conv * _sigmoid(conv)
    z = x[:, PB_Z:PB_Z + W_B]
    gates = ga_ref[...]
    eg = jnp.exp(-jnp.exp(alog_ref[...]) * _softplus(gates + dtb_ref[...]))
    beta = _sigmoid(gates)

    l2n = lambda t: t * lax.rsqrt(jnp.sum(t * t, axis=-1, keepdims=True) + 1e-12)
    for h in range(H_B):
        hs = slice(h * LANE, (h + 1) * LANE)
        q_h = l2n(cs[:, hs]) * (DK_B ** -0.5)
        k_h = l2n(cs[:, W_B + h * LANE:W_B + (h + 1) * LANE])
        v_h = cs[:, 2 * W_B + h * LANE:2 * W_B + (h + 1) * LANE]
        eg_h = eg[:, SMALL_GDN_A + h:SMALL_GDN_A + h + 1]
        b_h = beta[:, SMALL_GDN_B + h:SMALL_GDN_B + h + 1]
        seq = range(ns)
        Ss = [s_ref[i, h] for i in seq]
        kss = [_mm(k_h, Ss[i], NN, "xr") for i in seq]
        v_new = [b_h * (v_h - eg_h * kss[i]) for i in seq]
        s_new = [Ss[i] * eg_h[i:i + 1, :] + _mm(jnp.where(_row_mask(ns, i), k_h, 0.0), v_new[i], TN, "b")
                 for i in seq]
        for i in seq:
            sout_ref[i, h] = s_new[i]
        o_all = [_mm(q_h, s_new[i], NN, "b") for i in seq]
        o_h = jnp.zeros((ns, DV_B), F32)
        for i in seq:
            o_h = jnp.where(_row_mask(ns, i), o_all[i], o_h)
        o_h = o_h * lax.rsqrt(jnp.mean(o_h * o_h, axis=-1, keepdims=True) + NORM_EPS) * nw_ref[...]
        zh = z[:, hs]
        o_ref[:, hs] = o_h * (zh * _sigmoid(zh))


def _step_const_specs(ts):
    return [pl.BlockSpec(t.shape, lambda b: (0,) * t.ndim) for t in ts]


def _rwkv_step(pa, prev, s_t, consts):
    B = s_t.shape[-1]
    prep_consts = consts[:-2]
    lnw_col, lnb_col = [c.reshape(W_A, 1) for c in consts[-2:]]
    vecs = pl.pallas_call(
        _rwkv_step_prep_kernel,
        grid=(1,),
        in_specs=[pl.BlockSpec((B, PA_W), lambda b: (0, 0)),
                  pl.BlockSpec((B, PA_W), lambda b: (0, 0))] + _step_const_specs(prep_consts),
        out_specs=pl.BlockSpec((len(STEP_VECS), W_A, B), lambda b: (0, 0, 0)),
        out_shape=jax.ShapeDtypeStruct((len(STEP_VECS), W_A, B), F32),
        compiler_params=pltpu.CompilerParams(dimension_semantics=("arbitrary",),
                                             vmem_limit_bytes=VMEM_LIMIT),
        name="rwkv_step_prep",
    )(pa, prev, *prep_consts)
    state_spec = pl.BlockSpec((1, N_A, N_A, B), lambda h: (h, 0, 0, 0))
    return pl.pallas_call(
        _rwkv_step_state_kernel,
        grid=(H_A,),
        in_specs=[pl.BlockSpec((len(STEP_VECS), N_A, B), lambda h: (0, h, 0)), state_spec,
                  pl.BlockSpec((N_A, 1), lambda h: (h, 0)), pl.BlockSpec((N_A, 1), lambda h: (h, 0))],
        out_specs=[pl.BlockSpec((N_A, B), lambda h: (h, 0)), state_spec],
        out_shape=[jax.ShapeDtypeStruct((W_A, B), F32), jax.ShapeDtypeStruct(s_t.shape, F32)],
        scratch_shapes=[pltpu.VMEM((N_A, B), F32)],
        compiler_params=pltpu.CompilerParams(dimension_semantics=("arbitrary",),
                                             vmem_limit_bytes=VMEM_LIMIT),
        name="rwkv_step_state",
    )(vecs, s_t, lnw_col, lnb_col)


def _gdn_step(pb, pa, hist, s, consts, ns):
    B = s.shape[0]
    return pl.pallas_call(
        _gdn_step_kernel,
        grid=(B // ns,),
        in_specs=[pl.BlockSpec((ns, PB_W), lambda b: (b, 0)),
                  pl.BlockSpec((ns, LANE), lambda b: (b, PA_SMALL // LANE)),
                  pl.BlockSpec((CONV_W - 1, ns, 3 * W_B), lambda b: (0, b, 0)),
                  pl.BlockSpec((ns, H_B, DK_B, DV_B), lambda b: (b, 0, 0, 0))] + _step_const_specs(consts),
        out_specs=[pl.BlockSpec((ns, W_B), lambda b: (b, 0)),
                   pl.BlockSpec((ns, H_B, DK_B, DV_B), lambda b: (b, 0, 0, 0))],
        out_shape=[jax.ShapeDtypeStruct((B, W_B), F32), jax.ShapeDtypeStruct(s.shape, F32)],
        compiler_params=pltpu.CompilerParams(dimension_semantics=("arbitrary",),
                                             vmem_limit_bytes=VMEM_LIMIT),
        name="gdn_step",
    )(pb, pa, hist, s, *consts)


def _pad_cols(t, width):
    return jnp.pad(t, [(0, 0)] * (t.ndim - 1) + [(0, width - t.shape[-1])])


def _pa_layout(t, gdn_gates=None):
    main = t[..., :3 * W_A]
    small = t[..., 3 * W_A:3 * W_A + D_DECAY + D_AAA]
    gate = t[..., 3 * W_A + D_DECAY + D_AAA:]
    if gdn_gates is not None:
        small = jnp.concatenate([small, gdn_gates], axis=-1)
    return jnp.concatenate([main, _pad_cols(small, LANE), _pad_cols(gate, LANE)], axis=-1)


def _pa_unlayout(t):
    return jnp.concatenate([t[..., :3 * W_A], t[..., PA_SMALL:PA_SMALL + D_DECAY + D_AAA],
                            t[..., PA_GATE:PA_GATE + D_GATE]], axis=-1)


def _small_rows(t, first_row):
    return jnp.pad(t, [(first_row, LANE - first_row - t.shape[0]), (0, 0)])


def _small_lanes(t, first_lane):
    return jnp.pad(t, [(0, 0), (first_lane, LANE - first_lane - t.shape[1])])


def _pair_unblock(s):
    B = s.shape[0]
    return jnp.stack([s[:, :, :N_A, :N_A], s[:, :, N_A:, N_A:]], axis=2).reshape(B, H_A, N_A, N_A)


def _history_rows(rows):
    B, n, w = rows.shape
    return jnp.concatenate([jnp.zeros((B, SUBLANE - n, w), F32), rows], axis=1)


def kernel(x_prompt, x_sample, state_rwkv, state_shift, state_gdn, state_conv, meta_tokens,
           g_ffn1, w_gate1, w_up1, w_down1, g_mix, w_in, mu_shift, w0, w_decay_up, a0, w_a_up,
           w_g_up, k_k, k_a, r_k, lnx_w, lnx_b, conv_w, a_log, dt_bias, gdn_norm_w, w_out,
           g_ffn2, w_gate2, w_up2, w_down2, g_final):
    assert g_ffn1.shape[0] == 1, "single trunk layer"
    bp, sp, _ = x_prompt.shape
    bs = x_sample.shape[0]
    assert x_sample.shape[1] == 1 and sp % CHUNK_RWKV == 0 and sp % CHUNK_GDN == 0
    assert (bp * sp) % TM_DENSE == 0 and (bp * sp) % TM_DENSE_OUT == 0
    assert bp % SEQS_PER_STEP == 0 and bs % DEC_SEQS_PER_STEP == 0
    row = lambda t: t.reshape(1, -1).astype(F32)

    ffn1 = (w_gate1[0].astype(BF16), w_up1[0].astype(BF16), w_down1[0].astype(BF16))
    ffn2 = (w_gate2[0].astype(BF16), w_up2[0].astype(BF16), w_down2[0].astype(BF16))
    win_a = _pa_layout(w_in[0][:, :N_A_IN], w_in[0][:, N_A_IN + PB_W:]).astype(BF16)
    win_b = w_in[0][:, N_A_IN:N_A_IN + PB_W].astype(BF16)
    wo_a = w_out[0][:W_A].astype(BF16)
    wo_b = w_out[0][W_A:].astype(BF16)
    dense_in_consts = (row(g_ffn1[0]), *ffn1, row(g_mix[0]), win_a, win_b)
    dense_out_consts = (wo_a, wo_b, row(g_ffn2[0]), *ffn2, row(g_final))
    rwkv_consts = (_pa_layout(row(mu_shift[0])), row(w0[0]),
                   _small_rows(w_decay_up[0], 0).astype(BF16), row(a0[0]),
                   _small_rows(w_a_up[0], SMALL_AAA).astype(BF16), _small_rows(w_g_up[0], 0).astype(BF16),
                   row(k_k[0]), row(k_a[0]), row(r_k[0]), row(lnx_w[0]), row(lnx_b[0]))
    gdn_consts = (conv_w[0].astype(F32), _small_lanes(row(a_log[0]), SMALL_GDN_A),
                  _small_lanes(row(dt_bias[0]), SMALL_GDN_A), row(gdn_norm_w[0]))

    xs = jnp.concatenate([x_sample[:, 0, :].astype(F32), meta_tokens.astype(F32)], axis=0)
    hs, pas, pbs = _dense_in(xs, *dense_in_consts, tm=xs.shape[0])

    pa_meta = pas[bs:][None]
    pb_meta = pbs[bs:][None]
    _, rw_meta = _rwkv(pa_meta, jnp.zeros((1, SUBLANE, PA_W), F32),
                       jnp.zeros((1, H_A // 2, LANE, LANE), F32), rwkv_consts, N_META, 1)
    _, gd_meta = _gdn(pb_meta, pa_meta, jnp.zeros((1, SUBLANE, 3 * W_B), F32),
                      jnp.zeros((1, H_B, DK_B, DV_B), F32), gdn_consts, N_META, 1)

    hp, pap, pbp = _dense_in(x_prompt.reshape(bp * sp, D_MODEL).astype(F32), *dense_in_consts,
                             tm=TM_DENSE)
    pap3 = pap.reshape(bp, sp, PA_W)
    pbp3 = pbp.reshape(bp, sp, PB_W)
    oa_p, rw_p = _rwkv(pap3, _history_rows(pa_meta[:, -1:, :]), rw_meta, rwkv_consts,
                       CHUNK_RWKV, SEQS_PER_STEP)
    ob_p, gd_p = _gdn(pbp3, pap3, _history_rows(pb_meta[:, -(CONV_W - 1):, :3 * W_B]), gd_meta, gdn_consts,
                      CHUNK_GDN, SEQS_PER_STEP)
    y_p = _dense_out(hp, oa_p.reshape(bp * sp, W_A), ob_p.reshape(bp * sp, W_B), *dense_out_consts,
                     tm=TM_DENSE_OUT)

    oa_t, rw_t = _rwkv_step(pas, _pa_layout(state_shift[0].astype(F32)),
                            jnp.transpose(state_rwkv[0].astype(F32), (1, 2, 3, 0)), rwkv_consts)
    oa_s = oa_t.T
    rw_s = jnp.transpose(rw_t, (3, 0, 1, 2))
    ob_s, gd_s = _gdn_step(pbs, pas, jnp.swapaxes(state_conv[0].astype(F32), 0, 1), state_gdn[0].astype(F32),
                           gdn_consts, DEC_SEQS_PER_STEP)
    y_s = _dense_out(hs[:bs], oa_s, ob_s, *dense_out_consts, tm=bs)

    new_conv_s = jnp.concatenate([state_conv[0].astype(F32)[:, 1:, :], pbs[:bs, None, :3 * W_B]], axis=1)
    return (y_p.reshape(bp, sp, D_MODEL).astype(x_prompt.dtype),
            y_s.reshape(bs, 1, D_MODEL).astype(x_sample.dtype),
            _pair_unblock(rw_p)[None],
            _pa_unlayout(pap3[:, -1, :])[None],
            gd_p[None],
            pbp3[:, -(CONV_W - 1):, :3 * W_B][None],
            rw_s[None],
            _pa_unlayout(pas[:bs])[None],
            gd_s[None],
            new_conv_s[None])
```

```python
import functools

import jax
import jax.numpy as jnp
from jax import lax
from jax.experimental import pallas as pl
from jax.experimental.pallas import tpu as pltpu

F32 = jnp.float32
BF16 = jnp.bfloat16

LANE = 128
SUBLANE = 8
VMEM_LIMIT = 56 * 1024 * 1024

D_MODEL = 1024
D_FF = 2816
N_META = 16
H_A, N_A = 8, 64
W_A = H_A * N_A
D_DECAY, D_AAA, D_GATE = 32, 32, 96
N_A_IN = 3 * W_A + D_DECAY + D_AAA + D_GATE
H_B, DK_B, DV_B = 4, 128, 128
W_B = H_B * DV_B
CONV_W = 4
N_B_IN = 4 * W_B + 2 * H_B
LNX_EPS = 64e-5
NORM_EPS = 1e-6

PA_W = 3 * W_A + 2 * LANE
PA_SMALL = 3 * W_A
PA_GATE = PA_SMALL + LANE
SMALL_AAA = D_DECAY
SMALL_GDN_A = D_DECAY + D_AAA
SMALL_GDN_B = SMALL_GDN_A + H_B
PB_W = 4 * W_B
PB_Z = 3 * W_B

TM_DENSE = 512
TM_DENSE_OUT = 512
CHUNK_RWKV = 64
CHUNK_GDN = 128
SEQS_PER_STEP = 4
DEC_SEQS_PER_STEP = SUBLANE

NN = (((1,), (0,)), ((), ()))
NT = (((1,), (1,)), ((), ()))
TN = (((0,), (0,)), ((), ()))


def _dot(a, b):
    return jnp.dot(a, b, preferred_element_type=F32)


def _sigmoid(x):
    return 1.0 / (1.0 + jnp.exp(-x))


def _softplus(x):
    return jnp.maximum(x, 0.0) + jnp.log(1.0 + jnp.exp(-jnp.abs(x)))


def _rms(x, g):
    return x * lax.rsqrt(jnp.mean(x * x, axis=-1, keepdims=True) + NORM_EPS) * g


def _iota2(shape, dim):
    return lax.broadcasted_iota(jnp.int32, shape, dim)


def _split(x):
    hi = x.astype(BF16)
    return hi, (x - hi.astype(F32)).astype(BF16)


def _dg(a, b, dims):
    return lax.dot_general(a, b, dims, preferred_element_type=F32)


def _mm(a, b, dims=NN, mode="b"):
    if mode == "b":
        return _dg(a.astype(BF16), b.astype(BF16), dims)
    if mode == "xl":
        ah, al = _split(a)
        bh = b.astype(BF16)
        return _dg(ah, bh, dims) + _dg(al, bh, dims)
    if mode == "xr":
        ah = a.astype(BF16)
        bh, bl = _split(b)
        return _dg(ah, bh, dims) + _dg(ah, bl, dims)
    assert mode == "x3"
    ah, al = _split(a)
    bh, bl = _split(b)
    return _dg(ah, bh, dims) + (_dg(ah, bl, dims) + _dg(al, bh, dims))


def _inv_unit_lower(lows, nil, merge_mode):
    n = lows[0].shape[0]
    row = _iota2((n, n), 0)
    col = _iota2((n, n), 1)
    same_block = lambda s: (row // s) == (col // s)
    eye = (row == col).astype(F32)
    s = 2
    in_base = same_block(s)
    invs = [eye - jnp.where(in_base, low, 0.0) for low in lows]
    while s < nil:
        newly = jnp.logical_and(same_block(2 * s), jnp.logical_not(same_block(s)))
        ts = [_mm(inv, jnp.where(newly, low, 0.0), NN, merge_mode) for inv, low in zip(invs, lows)]
        invs = [inv - _mm(t, inv, NN, merge_mode) for inv, t in zip(invs, ts)]
        s *= 2
    return invs


FF_SPLIT = (0, 1536, D_FF)


def _swiglu(n, wg_ref, wu_ref, wd_ref):
    out = None
    for lo, hi in zip(FF_SPLIT[:-1], FF_SPLIT[1:]):
        gate = _dot(n, wg_ref[:, lo:hi])
        up = _dot(n, wu_ref[:, lo:hi])
        act = (gate * _sigmoid(gate) * up).astype(BF16)
        part = _dot(act, wd_ref[lo:hi, :])
        out = part if out is None else out + part
    return out


def _dense_in_kernel(x_ref, g1_ref, wg_ref, wu_ref, wd_ref, gm_ref, wa_ref, wb_ref,
                     h_ref, pa_ref, pb_ref):
    x = x_ref[...]
    h = x + 0.5 * _swiglu(_rms(x, g1_ref[...]).astype(BF16), wg_ref, wu_ref, wd_ref)
    h_ref[...] = h
    n = _rms(h, gm_ref[...]).astype(BF16)
    pa_ref[...] = _dot(n, wa_ref[...])
    pb_ref[...] = _dot(n, wb_ref[...])


def _dense_out_kernel(h_ref, oa_ref, ob_ref, woa_ref, wob_ref, g2_ref, wg_ref, wu_ref, wd_ref,
                      gf_ref, y_ref):
    h = (h_ref[...] + _dot(oa_ref[...].astype(BF16), woa_ref[...])
         + _dot(ob_ref[...].astype(BF16), wob_ref[...]))
    h = h + 0.5 * _swiglu(_rms(h, g2_ref[...]).astype(BF16), wg_ref, wu_ref, wd_ref)
    y_ref[...] = _rms(h, gf_ref[...])


def _const_spec(shape):
    return pl.BlockSpec(shape, lambda *_: (0,) * len(shape), pipeline_mode=pl.Buffered(1))


def _row_spec(tm, width):
    return pl.BlockSpec((tm, width), lambda i: (i, 0))


def _dense_in(x, g1, wg, wu, wd, gm, wa, wb, tm):
    n = x.shape[0]
    consts = (g1, wg, wu, wd, gm, wa, wb)
    return pl.pallas_call(
        _dense_in_kernel,
        grid=(n // tm,),
        in_specs=[_row_spec(tm, D_MODEL)] + [_const_spec(c.shape) for c in consts],
        out_specs=[_row_spec(tm, D_MODEL), _row_spec(tm, PA_W), _row_spec(tm, PB_W)],
        out_shape=[jax.ShapeDtypeStruct((n, D_MODEL), F32),
                   jax.ShapeDtypeStruct((n, PA_W), F32),
                   jax.ShapeDtypeStruct((n, PB_W), F32)],
        compiler_params=pltpu.CompilerParams(dimension_semantics=("arbitrary",),
                                             vmem_limit_bytes=VMEM_LIMIT),
        name="dense_in",
    )(x, *consts)


def _dense_out(h, oa, ob, woa, wob, g2, wg, wu, wd, gf, tm):
    n = h.shape[0]
    consts = (woa, wob, g2, wg, wu, wd, gf)
    return pl.pallas_call(
        _dense_out_kernel,
        grid=(n // tm,),
        in_specs=[_row_spec(tm, D_MODEL), _row_spec(tm, W_A), _row_spec(tm, W_B)]
        + [_const_spec(c.shape) for c in consts],
        out_specs=_row_spec(tm, D_MODEL),
        out_shape=jax.ShapeDtypeStruct((n, D_MODEL), F32),
        compiler_params=pltpu.CompilerParams(dimension_semantics=("arbitrary",),
                                             vmem_limit_bytes=VMEM_LIMIT),
        name="dense_out",
    )(h, oa, ob, *consts)


RWKV_PREC = dict(cum="xr", seg="b", pair="b", invm="b", sread="b", akv="b", solve="b", inter="b",
                 state="x3")


def _rwkv_kernel(pa_ref, prev_ref, s0_ref, mu_ref, w0_ref, wdu_ref, a0_ref, wau_ref, wgu_ref,
                 kk_ref, ka_ref, rk_ref, lnw_ref, lnb_ref, o_ref, sout_ref, xbuf, s_scr,
                 *, chunk, nseq, shared_init):
    c = pl.program_id(1)
    C = chunk
    P = RWKV_PREC

    @pl.when(c == 0)
    def _():
        for i in range(nseq):
            j = 0 if shared_init else i
            xbuf[i, 0:SUBLANE, :] = prev_ref[j]
            s_scr[i] = s0_ref[j]

    tri = (_iota2((C, C), 0) >= _iota2((C, C), 1)).astype(F32)
    mid = C // 2 - 1

    def prep(i):
        x = pa_ref[i]
        full = jnp.concatenate([xbuf[i], x], axis=0)
        prev = pltpu.roll(full, 1, axis=0)[SUBLANE:, :]
        xm = x + (prev - x) * mu_ref[...]
        xbuf[i] = x[C - SUBLANE:, :]

        r = xm[:, 0:W_A]
        k = xm[:, W_A:2 * W_A]
        v = xm[:, 2 * W_A:3 * W_A]
        sm = xm[:, PA_SMALL:PA_SMALL + LANE]
        gd = xm[:, PA_GATE:PA_GATE + LANE]

        wl = w0_ref[...] + _dot(jnp.tanh(sm).astype(BF16), wdu_ref[...])
        lw = -jnp.exp(-_softplus(-wl) - 0.5)
        a = _sigmoid(a0_ref[...] + _dot(sm.astype(BF16), wau_ref[...]))
        g = _dot(_sigmoid(gd).astype(BF16), wgu_ref[...])
        kkr = k * kk_ref[...]
        k2 = k * (1.0 + (a - 1.0) * ka_ref[...])
        G = _mm(tri, lw, NN, P["cum"])
        return dict(r=r, v=v, a=a, g=g, kkr=kkr, k2=k2, G=G, Gx=G - lw, Gm=G[mid:mid + 1, :],
                    Gc=G[C - 1:C, :])

    seqs = [prep(i) for i in range(nseq)]

    C2 = 2 * C
    trow = _iota2((C2, C2), 0) % C
    tcol = _iota2((C2, C2), 1) % C
    incl = trow >= tcol
    strict = trow > tcol
    lane = _iota2((1, LANE), 1)
    m0 = (lane < N_A).astype(F32)
    m1 = (lane >= N_A).astype(F32)
    seg = ((_iota2((LANE, LANE), 0) // N_A) == (_iota2((LANE, LANE), 1) // N_A)).astype(F32)
    stack = lambda t: jnp.concatenate([t * m0, t * m1], axis=0)
    merged = C2 % LANE == 0

    inst = [(i, p) for i in range(nseq) for p in range(H_A // 2)]
    n = range(len(inst))
    sls = [slice(p * LANE, (p + 1) * LANE) for _, p in inst]
    take = lambda name: [seqs[i][name][:, sls[j]] for j, (i, _) in enumerate(inst)]
    rs, vs_, k2s, kkrs, as_, gs = take("r"), take("v"), take("k2"), take("kkr"), take("a"), take("g")
    Gs, Gxs, Gms, Gcs = take("G"), take("Gx"), take("Gm"), take("Gc")
    Ss = [s_scr[i, p] for i, p in inst]

    ssq = [_mm(t * t, seg, NN, P["seg"]) for t in kkrs]
    kks = [t * lax.rsqrt(q + 1e-12) for t, q in zip(kkrs, ssq)]
    kkas = [kks[j] * as_[j] for j in n]
    inv_rel = [jnp.exp(Gms[j] - Gs[j]) for j in n]
    dec = [jnp.exp(Gcs[j] - Gs[j]) for j in n]
    lhs = [jnp.concatenate([stack(kks[j] * jnp.exp(Gxs[j] - Gms[j])),
                            stack(rs[j] * jnp.exp(Gs[j] - Gms[j]))], axis=0) for j in n]
    ais = [stack(kkas[j] * inv_rel[j]) for j in n]
    kis = [stack(k2s[j] * inv_rel[j]) for j in n]
    if merged:
        pm = [_mm(lhs[j], jnp.concatenate([ais[j], kis[j]], axis=0), NT, P["pair"]) for j in n]
        pas_, pks_ = [t[:, :C2] for t in pm], [t[:, C2:] for t in pm]
    else:
        pas_ = [_mm(lhs[j], ais[j], NT, P["pair"]) for j in n]
        pks_ = [_mm(lhs[j], kis[j], NT, P["pair"]) for j in n]
    a_aa = [jnp.where(strict, t[:C2], 0.0) for t in pas_]
    a_ak = [jnp.where(strict, t[:C2], 0.0) for t in pks_]
    a_ra = [jnp.where(incl, t[C2:], 0.0) for t in pas_]
    a_rk = [jnp.where(incl, t[C2:], 0.0) for t in pks_]
    ainv = _inv_unit_lower(a_aa, C, P["invm"])

    vst = [stack(t) for t in vs_]
    akv = [_mm(a_ak[j], vst[j], NN, P["akv"]) for j in n]
    sx = [_mm(jnp.concatenate([kks[j] * jnp.exp(Gxs[j]), rs[j] * jnp.exp(Gs[j])], axis=0), Ss[j], NT,
              P["sread"]) for j in n]
    us = [-_mm(ainv[j], stack(sx[j][:C]) + akv[j], NN, P["solve"]) for j in n]
    uv = [jnp.concatenate([us[j], vst[j]], axis=0) for j in n]
    s_new = [Ss[j] * jnp.exp(Gcs[j])
             + _mm(uv[j], jnp.concatenate([stack(kkas[j] * dec[j]), stack(k2s[j] * dec[j])], axis=0),
                   TN, P["state"]) for j in n]
    for j, (i, p) in enumerate(inst):
        s_scr[i, p] = s_new[j]
    if merged:
        inter = [_mm(jnp.concatenate([a_ra[j], a_rk[j]], axis=1), uv[j], NN, P["inter"]) for j in n]
    else:
        inter = [_mm(a_ra[j], us[j], NN, P["inter"]) + _mm(a_rk[j], vst[j], NN, P["inter"]) for j in n]
    Os = [sx[j][C:] + inter[j][:C] + inter[j][C:] for j in n]
    means = [_mm(t, seg, NN, P["seg"]) * (1.0 / N_A) for t in Os]
    dlts = [t - m for t, m in zip(Os, means)]
    vars_ = [_mm(t * t, seg, NN, P["seg"]) * (1.0 / N_A) for t in dlts]
    bonus = [_mm(rs[j] * k2s[j] * rk_ref[:, sls[j]], seg, NN, P["seg"]) * vs_[j] for j in n]
    for j, (i, p) in enumerate(inst):
        on = dlts[j] * lax.rsqrt(vars_[j] + LNX_EPS) * lnw_ref[:, sls[j]] + lnb_ref[:, sls[j]]
        o_ref[i, :, sls[j]] = (on + bonus[j]) * gs[j]

    @pl.when(c == pl.num_programs(1) - 1)
    def _():
        sout_ref[...] = s_scr[...]


def _seq_specs(nseq, shared_init, hist_w, state_shape):
    zeros = (0,) * len(state_shape)
    if shared_init:
        return [pl.BlockSpec((1, SUBLANE, hist_w), lambda b, c: (0, 0, 0)),
                pl.BlockSpec((1,) + state_shape, lambda b, c: (0,) + zeros)]
    return [pl.BlockSpec((nseq, SUBLANE, hist_w), lambda b, c: (b, 0, 0)),
            pl.BlockSpec((nseq,) + state_shape, lambda b, c: (b,) + zeros)]


def _rwkv(pa, prev8, s0, consts, chunk, nseq):
    B, L, _ = pa.shape
    shared_init = prev8.shape[0] == 1 and B > 1
    state_shape = (H_A // 2, LANE, LANE)
    return pl.pallas_call(
        functools.partial(_rwkv_kernel, chunk=chunk, nseq=nseq, shared_init=shared_init),
        grid=(B // nseq, L // chunk),
        in_specs=[pl.BlockSpec((nseq, chunk, PA_W), lambda b, c: (b, c, 0))]
        + _seq_specs(nseq, shared_init, PA_W, state_shape)
        + [pl.BlockSpec(t.shape, lambda b, c: (0, 0)) for t in consts],
        out_specs=[pl.BlockSpec((nseq, chunk, W_A), lambda b, c: (b, c, 0)),
                   pl.BlockSpec((nseq,) + state_shape, lambda b, c: (b, 0, 0, 0))],
        out_shape=[jax.ShapeDtypeStruct((B, L, W_A), F32),
                   jax.ShapeDtypeStruct((B,) + state_shape, F32)],
        scratch_shapes=[pltpu.VMEM((nseq, SUBLANE, PA_W), F32),
                        pltpu.VMEM((nseq,) + state_shape, F32)],
        compiler_params=pltpu.CompilerParams(dimension_semantics=("arbitrary", "arbitrary"),
                                             vmem_limit_bytes=VMEM_LIMIT),
        name="rwkv_chunk%d" % chunk,
    )(pa, prev8, s0, *consts)


GDN_PREC = dict(cum="xr", tr="xr", kk="b", invm="b", wu="b", qk="b", sread="b", qkv="b", state="b")


def _gdn_kernel(pb_ref, ga_ref, hist_ref, s0_ref, cw_ref, alog_ref, dtb_ref, nw_ref, o_ref, sout_ref,
                xbuf, s_scr, *, chunk, nseq, shared_init):
    c = pl.program_id(1)
    C = chunk
    P = GDN_PREC

    @pl.when(c == 0)
    def _():
        for i in range(nseq):
            j = 0 if shared_init else i
            xbuf[i, 0:SUBLANE, :] = hist_ref[j]
            s_scr[i] = s0_ref[j]

    row = _iota2((C, C), 0)
    col = _iota2((C, C), 1)
    incl = row >= col
    strict = row > col
    tri = incl.astype(F32)
    sel = (_iota2((SUBLANE, LANE), 0) + SMALL_GDN_A == _iota2((SUBLANE, LANE), 1)).astype(F32)

    def prep(i):
        x = pb_ref[i]
        qkv = x[:, 0:3 * W_B]
        full = jnp.concatenate([xbuf[i], qkv], axis=0)
        conv = qkv * cw_ref[CONV_W - 1:CONV_W, :]
        for j in range(1, CONV_W):
            conv = conv + pltpu.roll(full, j, axis=0)[SUBLANE:, :] * cw_ref[CONV_W - 1 - j:CONV_W - j, :]
        xbuf[i] = qkv[C - SUBLANE:, :]
        cs = conv * _sigmoid(conv)
        gates = ga_ref[i]
        glog = -jnp.exp(alog_ref[...]) * _softplus(gates + dtb_ref[...])
        beta = _sigmoid(gates)
        G = _mm(tri, glog, NN, P["cum"])
        Gt = _mm(sel, G, NT, P["tr"])
        return dict(cs=cs, z=x[:, PB_Z:PB_Z + W_B], beta=beta, G=G, Gt=Gt)

    seqs = [prep(i) for i in range(nseq)]

    inst = [(i, h) for i in range(nseq) for h in range(H_B)]
    n = range(len(inst))
    l2n = lambda t, c: t * (lax.rsqrt(jnp.sum(t * t, axis=-1, keepdims=True) + 1e-12) * c)
    head = lambda i, h, part: seqs[i]["cs"][:, part * W_B + h * LANE:part * W_B + (h + 1) * LANE]
    qs = [l2n(head(i, h, 0), DK_B ** -0.5) for i, h in inst]
    ks = [l2n(head(i, h, 1), 1.0) for i, h in inst]
    vs = [head(i, h, 2) for i, h in inst]
    gcols = [seqs[i]["G"][:, SMALL_GDN_A + h:SMALL_GDN_A + h + 1] for i, h in inst]
    grows = [seqs[i]["Gt"][h:h + 1, :] for i, h in inst]
    bcols = [seqs[i]["beta"][:, SMALL_GDN_B + h:SMALL_GDN_B + h + 1] for i, h in inst]
    Ss = [s_scr[i, h] for i, h in inst]

    dmats = [jnp.where(incl, jnp.exp(jnp.where(incl, gcols[j] - grows[j], 0.0)), 0.0) for j in n]
    kbs = [ks[j] * bcols[j] for j in n]
    lows = [jnp.where(strict, _mm(kbs[j], ks[j], NT, P["kk"]) * dmats[j], 0.0) for j in n]
    ainv = _inv_unit_lower(lows, C, P["invm"])
    egs = [jnp.exp(t) for t in gcols]
    wus = [_mm(ainv[j], jnp.concatenate([kbs[j] * egs[j], vs[j] * bcols[j]], axis=1), NN, P["wu"])
           for j in n]
    qks = [_mm(qs[j], ks[j], NT, P["qk"]) * dmats[j] for j in n]
    srs = [_mm(jnp.concatenate([wus[j][:, :DK_B], qs[j] * egs[j]], axis=0), Ss[j], NN, P["sread"])
           for j in n]
    v_new = [wus[j][:, DK_B:] - srs[j][:C] for j in n]
    glast = [t[C - 1:C, :] for t in gcols]
    s_new = [Ss[j] * jnp.exp(glast[j])
             + _mm(ks[j] * jnp.exp(glast[j] - gcols[j]), v_new[j], TN, P["state"]) for j in n]
    for j, (i, h) in enumerate(inst):
        s_scr[i, h] = s_new[j]
    os_ = [srs[j][C:] + _mm(qks[j], v_new[j], NN, P["qkv"]) for j in n]
    for j, (i, h) in enumerate(inst):
        o = os_[j]
        o = o * lax.rsqrt(jnp.mean(o * o, axis=-1, keepdims=True) + NORM_EPS) * nw_ref[...]
        zh = seqs[i]["z"][:, h * LANE:(h + 1) * LANE]
        o_ref[i, :, h * LANE:(h + 1) * LANE] = o * (zh * _sigmoid(zh))

    @pl.when(c == pl.num_programs(1) - 1)
    def _():
        sout_ref[...] = s_scr[...]


def _gdn(pb, pa, hist8, s0, consts, chunk, nseq):
    B, L, _ = pb.shape
    shared_init = hist8.shape[0] == 1 and B > 1
    state_shape = (H_B, DK_B, DV_B)
    return pl.pallas_call(
        functools.partial(_gdn_kernel, chunk=chunk, nseq=nseq, shared_init=shared_init),
        grid=(B // nseq, L // chunk),
        in_specs=[pl.BlockSpec((nseq, chunk, PB_W), lambda b, c: (b, c, 0)),
                  pl.BlockSpec((nseq, chunk, LANE), lambda b, c: (b, c, PA_SMALL // LANE))]
        + _seq_specs(nseq, shared_init, 3 * W_B, state_shape)
        + [pl.BlockSpec(t.shape, lambda b, c: (0, 0)) for t in consts],
        out_specs=[pl.BlockSpec((nseq, chunk, W_B), lambda b, c: (b, c, 0)),
                   pl.BlockSpec((nseq,) + state_shape, lambda b, c: (b, 0, 0, 0))],
        out_shape=[jax.ShapeDtypeStruct((B, L, W_B), F32),
                   jax.ShapeDtypeStruct((B,) + state_shape, F32)],
        scratch_shapes=[pltpu.VMEM((nseq, SUBLANE, 3 * W_B), F32),
                        pltpu.VMEM((nseq,) + state_shape, F32)],
        compiler_params=pltpu.CompilerParams(dimension_semantics=("arbitrary", "arbitrary"),
                                             vmem_limit_bytes=VMEM_LIMIT),
        name="gdn_chunk%d" % chunk,
    )(pb, pa, hist8, s0, *consts)


def _row_mask(nrow, i):
    return _iota2((nrow, 1), 0) == i


STEP_VECS = ("w", "kk", "kka", "k2", "v", "r", "g", "bon")


def _rwkv_step_prep_kernel(pa_ref, prev_ref, mu_ref, w0_ref, wdu_ref, a0_ref, wau_ref, wgu_ref,
                           kk_ref, ka_ref, rk_ref, vec_ref):
    x = pa_ref[...]
    xm = x + (prev_ref[...] - x) * mu_ref[...]
    r = xm[:, 0:W_A]
    k = xm[:, W_A:2 * W_A]
    v = xm[:, 2 * W_A:3 * W_A]
    sm = xm[:, PA_SMALL:PA_SMALL + LANE]
    gd = xm[:, PA_GATE:PA_GATE + LANE]
    wl = w0_ref[...] + _dot(jnp.tanh(sm).astype(BF16), wdu_ref[...])
    w = jnp.exp(-jnp.exp(-_softplus(-wl) - 0.5))
    a = _sigmoid(a0_ref[...] + _dot(sm.astype(BF16), wau_ref[...]))
    g = _dot(_sigmoid(gd).astype(BF16), wgu_ref[...])
    kkr = k * kk_ref[...]
    k2 = k * (1.0 + (a - 1.0) * ka_ref[...])
    bon = r * k2 * rk_ref[...]
    seg = ((_iota2((LANE, LANE), 0) // N_A) == (_iota2((LANE, LANE), 1) // N_A)).astype(F32)
    ssq = jnp.concatenate([_mm(kkr[:, p * LANE:(p + 1) * LANE] ** 2, seg, NN, RWKV_PREC["seg"])
                           for p in range(H_A // 2)], axis=-1)
    kk = kkr * lax.rsqrt(ssq + 1e-12)
    vecs = dict(w=w, kk=kk, kka=kk * a, k2=k2, v=v, r=r, g=g, bon=bon)
    for j, name in enumerate(STEP_VECS):
        vec_ref[j] = vecs[name].T


def _rwkv_step_state_kernel(vec_ref, s_ref, lnw_ref, lnb_ref, o_ref, sout_ref, o_scr):
    w, kk, kka, k2, v, r, g, bon = [vec_ref[j] for j in range(len(STEP_VECS))]
    for i in range(N_A):
        S = s_ref[0, i]
        sa = jnp.sum(S * kk, axis=0, keepdims=True)
        s_new = S * w - sa * kka + v[i:i + 1, :] * k2
        sout_ref[0, i] = s_new
        o_scr[i:i + 1, :] = jnp.sum(s_new * r, axis=0, keepdims=True)
    o = o_scr[...]
    mean = jnp.mean(o, axis=0, keepdims=True)
    dlt = o - mean
    var = jnp.mean(dlt * dlt, axis=0, keepdims=True)
    on = dlt * lax.rsqrt(var + LNX_EPS) * lnw_ref[...] + lnb_ref[...]
    o_ref[...] = (on + jnp.sum(bon, axis=0, keepdims=True) * v) * g


def _gdn_step_kernel(pb_ref, ga_ref, hist_ref, s_ref, cw_ref, alog_ref, dtb_ref, nw_ref, o_ref, sout_ref):
    ns = pb_ref.shape[0]
    x = pb_ref[...]
    conv = x[:, 0:3 * W_B] * cw_ref[CONV_W - 1:CONV_W, :]
    for j in range(CONV_W - 1):
        conv = conv + hist_ref[j] * cw_ref[j:j + 1, :]
    cs = conv * _sigmoid(conv)
    z = x[:, PB_Z:PB_Z + W_B]
    gates = ga_ref[...]
    eg = jnp.exp(-jnp.exp(alog_ref[...]) * _softplus(gates + dtb_ref[...]))
    beta = _sigmoid(gates)

    l2n = lambda t: t * lax.rsqrt(jnp.sum(t * t, axis=-1, keepdims=True) + 1e-12)
    for h in range(H_B):
        hs = slice(h * LANE, (h + 1) * LANE)
        q_h = l2n(cs[:, hs]) * (DK_B ** -0.5)
        k_h = l2n(cs[:, W_B + h * LANE:W_B + (h + 1) * LANE])
        v_h = cs[:, 2 * W_B + h * LANE:2 * W_B + (h + 1) * LANE]
        eg_h = eg[:, SMALL_GDN_A + h:SMALL_GDN_A + h + 1]
        b_h = beta[:, SMALL_GDN_B + h:SMALL_GDN_B + h + 1]
        seq = range(ns)
        Ss = [s_ref[i, h] for i in seq]
        kss = [_mm(k_h, Ss[i], NN, "xr") for i in seq]
        v_new = [b_h * (v_h - eg_h * kss[i]) for i in seq]
        s_new = [Ss[i] * eg_h[i:i + 1, :] + _mm(jnp.where(_row_mask(ns, i), k_h, 0.0), v_new[i], TN, "b")
                 for i in seq]
        for i in seq:
            sout_ref[i, h] = s_new[i]
        o_all = [_mm(q_h, s_new[i], NN, "b") for i in seq]
        o_h = jnp.zeros((ns, DV_B), F32)
        for i in seq:
            o_h = jnp.where(_row_mask(ns, i), o_all[i], o_h)
        o_h = o_h * lax.rsqrt(jnp.mean(o_h * o_h, axis=-1, keepdims=True) + NORM_EPS) * nw_ref[...]
        zh = z[:, hs]
        o_ref[:, hs] = o_h * (zh * _sigmoid(zh))


def _step_const_specs(ts):
    return [pl.BlockSpec(t.shape, lambda b: (0,) * t.ndim) for t in ts]


def _rwkv_step(pa, prev, s_t, consts):
    B = s_t.shape[-1]
    prep_consts = consts[:-2]
    lnw_col, lnb_col = [c.reshape(W_A, 1) for c in consts[-2:]]
    vecs = pl.pallas_call(
        _rwkv_step_prep_kernel,
        grid=(1,),
        in_specs=[pl.BlockSpec((B, PA_W), lambda b: (0, 0)),
                  pl.BlockSpec((B, PA_W), lambda b: (0, 0))] + _step_const_specs(prep_consts),
        out_specs=pl.BlockSpec((len(STEP_VECS), W_A, B), lambda b: (0, 0, 0)),
        out_shape=jax.ShapeDtypeStruct((len(STEP_VECS), W_A, B), F32),
        compiler_params=pltpu.CompilerParams(dimension_semantics=("arbitrary",),
                                             vmem_limit_bytes=VMEM_LIMIT),
        name="rwkv_step_prep",
    )(pa, prev, *prep_consts)
    state_spec = pl.BlockSpec((1, N_A, N_A, B), lambda h: (h, 0, 0, 0))
    return pl.pallas_call(
        _rwkv_step_state_kernel,
        grid=(H_A,),
        in_specs=[pl.BlockSpec((len(STEP_VECS), N_A, B), lambda h: (0, h, 0)), state_spec,
                  pl.BlockSpec((N_A, 1), lambda h: (h, 0)), pl.BlockSpec((N_A, 1), lambda h: (h, 0))],
        out_specs=[pl.BlockSpec((N_A, B), lambda h: (h, 0)), state_spec],
        out_shape=[jax.ShapeDtypeStruct((W_A, B), F32), jax.ShapeDtypeStruct(s_t.shape, F32)],
        scratch_shapes=[pltpu.VMEM((N_A, B), F32)],
        compiler_params=pltpu.CompilerParams(dimension_semantics=("arbitrary",),
                                             vmem_limit_bytes=VMEM_LIMIT),
        name="rwkv_step_state",
    )(vecs, s_t, lnw_col, lnb_col)


def _gdn_step(pb, pa, hist, s, consts, ns):
    B = s.shape[0]
    return pl.pallas_call(
        _gdn_step_kernel,
        grid=(B // ns,),
        in_specs=[pl.BlockSpec((ns, PB_W), lambda b: (b, 0)),
                  pl.BlockSpec((ns, LANE), lambda b: (b, PA_SMALL // LANE)),
                  pl.BlockSpec((CONV_W - 1, ns, 3 * W_B), lambda b: (0, b, 0)),
                  pl.BlockSpec((ns, H_B, DK_B, DV_B), lambda b: (b, 0, 0, 0))] + _step_const_specs(consts),
        out_specs=[pl.BlockSpec((ns, W_B), lambda b: (b, 0)),
                   pl.BlockSpec((ns, H_B, DK_B, DV_B), lambda b: (b, 0, 0, 0))],
        out_shape=[jax.ShapeDtypeStruct((B, W_B), F32), jax.ShapeDtypeStruct(s.shape, F32)],
        compiler_params=pltpu.CompilerParams(dimension_semantics=("arbitrary",),
                                             vmem_limit_bytes=VMEM_LIMIT),
        name="gdn_step",
    )(pb, pa, hist, s, *consts)


def _pad_cols(t, width):
    return jnp.pad(t, [(0, 0)] * (t.ndim - 1) + [(0, width - t.shape[-1])])


def _pa_layout(t, gdn_gates=None):
    main = t[..., :3 * W_A]
    small = t[..., 3 * W_A:3 * W_A + D_DECAY + D_AAA]
    gate = t[..., 3 * W_A + D_DECAY + D_AAA:]
    if gdn_gates is not None:
        small = jnp.concatenate([small, gdn_gates], axis=-1)
    return jnp.concatenate([main, _pad_cols(small, LANE), _pad_cols(gate, LANE)], axis=-1)


def _pa_unlayout(t):
    return jnp.concatenate([t[..., :3 * W_A], t[..., PA_SMALL:PA_SMALL + D_DECAY + D_AAA],
                            t[..., PA_GATE:PA_GATE + D_GATE]], axis=-1)


def _small_rows(t, first_row):
    return jnp.pad(t, [(first_row, LANE - first_row - t.shape[0]), (0, 0)])


def _small_lanes(t, first_lane):
    return jnp.pad(t, [(0, 0), (first_lane, LANE - first_lane - t.shape[1])])


def _pair_unblock(s):
    B = s.shape[0]
    return jnp.stack([s[:, :, :N_A, :N_A], s[:, :, N_A:, N_A:]], axis=2).reshape(B, H_A, N_A, N_A)


def _history_rows(rows):
    B, n, w = rows.shape
    return jnp.concatenate([jnp.zeros((B, SUBLANE - n, w), F32), rows], axis=1)


def kernel(x_prompt, x_sample, state_rwkv, state_shift, state_gdn, state_conv, meta_tokens,
           g_ffn1, w_gate1, w_up1, w_down1, g_mix, w_in, mu_shift, w0, w_decay_up, a0, w_a_up,
           w_g_up, k_k, k_a, r_k, lnx_w, lnx_b, conv_w, a_log, dt_bias, gdn_norm_w, w_out,
           g_ffn2, w_gate2, w_up2, w_down2, g_final):
    assert g_ffn1.shape[0] == 1, "single trunk layer"
    bp, sp, _ = x_prompt.shape
    bs = x_sample.shape[0]
    assert x_sample.shape[1] == 1 and sp % CHUNK_RWKV == 0 and sp % CHUNK_GDN == 0
    assert (bp * sp) % TM_DENSE == 0 and (bp * sp) % TM_DENSE_OUT == 0
    assert bp % SEQS_PER_STEP == 0 and bs % DEC_SEQS_PER_STEP == 0
    row = lambda t: t.reshape(1, -1).astype(F32)

    ffn1 = (w_gate1[0].astype(BF16), w_up1[0].astype(BF16), w_down1[0].astype(BF16))
    ffn2 = (w_gate2[0].astype(BF16), w_up2[0].astype(BF16), w_down2[0].astype(BF16))
    win_a = _pa_layout(w_in[0][:, :N_A_IN], w_in[0][:, N_A_IN + PB_W:]).astype(BF16)
    win_b = w_in[0][:, N_A_IN:N_A_IN + PB_W].astype(BF16)
    wo_a = w_out[0][:W_A].astype(BF16)
    wo_b = w_out[0][W_A:].astype(BF16)
    dense_in_consts = (row(g_ffn1[0]), *ffn1, row(g_mix[0]), win_a, win_b)
    dense_out_consts = (wo_a, wo_b, row(g_ffn2[0]), *ffn2, row(g_final))
    rwkv_consts = (_pa_layout(row(mu_shift[0])), row(w0[0]),
                   _small_rows(w_decay_up[0], 0).astype(BF16), row(a0[0]),
                   _small_rows(w_a_up[0], SMALL_AAA).astype(BF16), _small_rows(w_g_up[0], 0).astype(BF16),
                   row(k_k[0]), row(k_a[0]), row(r_k[0]), row(lnx_w[0]), row(lnx_b[0]))
    gdn_consts = (conv_w[0].astype(F32), _small_lanes(row(a_log[0]), SMALL_GDN_A),
                  _small_lanes(row(dt_bias[0]), SMALL_GDN_A), row(gdn_norm_w[0]))

    xs = jnp.concatenate([x_sample[:, 0, :].astype(F32), meta_tokens.astype(F32)], axis=0)
    hs, pas, pbs = _dense_in(xs, *dense_in_consts, tm=xs.shape[0])

    pa_meta = pas[bs:][None]
    pb_meta = pbs[bs:][None]
    _, rw_meta = _rwkv(pa_meta, jnp.zeros((1, SUBLANE, PA_W), F32),
                       jnp.zeros((1, H_A // 2, LANE, LANE), F32), rwkv_consts, N_META, 1)
    _, gd_meta = _gdn(pb_meta, pa_meta, jnp.zeros((1, SUBLANE, 3 * W_B), F32),
                      jnp.zeros((1, H_B, DK_B, DV_B), F32), gdn_consts, N_META, 1)

    hp, pap, pbp = _dense_in(x_prompt.reshape(bp * sp, D_MODEL).astype(F32), *dense_in_consts,
                             tm=TM_DENSE)
    pap3 = pap.reshape(bp, sp, PA_W)
    pbp3 = pbp.reshape(bp, sp, PB_W)
    oa_p, rw_p = _rwkv(pap3, _history_rows(pa_meta[:, -1:, :]), rw_meta, rwkv_consts,
                       CHUNK_RWKV, SEQS_PER_STEP)
    ob_p, gd_p = _gdn(pbp3, pap3, _history_rows(pb_meta[:, -(CONV_W - 1):, :3 * W_B]), gd_meta, gdn_consts,
                      CHUNK_GDN, SEQS_PER_STEP)
    y_p = _dense_out(hp, oa_p.reshape(bp * sp, W_A), ob_p.reshape(bp * sp, W_B), *dense_out_consts,
                     tm=TM_DENSE_OUT)

    oa_t, rw_t = _rwkv_step(pas, _pa_layout(state_shift[0].astype(F32)),
                            jnp.transpose(state_rwkv[0].astype(F32), (1, 2, 3, 0)), rwkv_consts)
    oa_s = oa_t.T
    rw_s = jnp.transpose(rw_t, (3, 0, 1, 2))
    ob_s, gd_s = _gdn_step(pbs, pas, jnp.swapaxes(state_conv[0].astype(F32), 0, 1), state_gdn[0].astype(F32),
                           gdn_consts, DEC_SEQS_PER_STEP)
    y_s = _dense_out(hs[:bs], oa_s, ob_s, *dense_out_consts, tm=bs)

    new_conv_s = jnp.concatenate([state_conv[0].astype(F32)[:, 1:, :], pbs[:bs, None, :3 * W_B]], axis=1)
    return (y_p.reshape(bp, sp, D_MODEL).astype(x_prompt.dtype),
            y_s.reshape(bs, 1, D_MODEL).astype(x_sample.dtype),
            _pair_unblock(rw_p)[None],
            _pa_unlayout(pap3[:, -1, :])[None],
            gd_p[None],
            pbp3[:, -(CONV_W - 1):, :3 * W_B][None],
            rw_s[None],
            _pa_unlayout(pas[:bs])[None],
            gd_s[None],
            new_conv_s[None])
```

```python
import functools

import jax
import jax.numpy as jnp
from jax import lax
from jax.experimental import pallas as pl
from jax.experimental.pallas import tpu as pltpu

F32 = jnp.float32
BF16 = jnp.bfloat16

LANE = 128
SUBLANE = 8
VMEM_LIMIT = 56 * 1024 * 1024

D_MODEL = 1024
D_FF = 2816
N_META = 16
H_A, N_A = 8, 64
W_A = H_A * N_A
D_DECAY, D_AAA, D_GATE = 32, 32, 96
N_A_IN = 3 * W_A + D_DECAY + D_AAA + D_GATE
H_B, DK_B, DV_B = 4, 128, 128
W_B = H_B * DV_B
CONV_W = 4
N_B_IN = 4 * W_B + 2 * H_B
LNX_EPS = 64e-5
NORM_EPS = 1e-6

PA_W = 3 * W_A + 2 * LANE
PA_SMALL = 3 * W_A
PA_GATE = PA_SMALL + LANE
SMALL_AAA = D_DECAY
SMALL_GDN_A = D_DECAY + D_AAA
SMALL_GDN_B = SMALL_GDN_A + H_B
PB_W = 4 * W_B
PB_Z = 3 * W_B

TM_DENSE = 512
TM_DENSE_OUT = 512
CHUNK_RWKV = 64
CHUNK_GDN = 128
SEQS_PER_STEP = 4
DEC_SEQS_PER_STEP = SUBLANE

NN = (((1,), (0,)), ((), ()))
NT = (((1,), (1,)), ((), ()))
TN = (((0,), (0,)), ((), ()))


def _dot(a, b):
    return jnp.dot(a, b, preferred_element_type=F32)


def _sigmoid(x):
    return 1.0 / (1.0 + jnp.exp(-x))


def _softplus(x):
    return jnp.maximum(x, 0.0) + jnp.log(1.0 + jnp.exp(-jnp.abs(x)))


def _rms(x, g):
    return x * lax.rsqrt(jnp.mean(x * x, axis=-1, keepdims=True) + NORM_EPS) * g


def _iota2(shape, dim):
    return lax.broadcasted_iota(jnp.int32, shape, dim)


def _split(x):
    hi = x.astype(BF16)
    return hi, (x - hi.astype(F32)).astype(BF16)


def _dg(a, b, dims):
    return lax.dot_general(a, b, dims, preferred_element_type=F32)


def _mm(a, b, dims=NN, mode="b"):
    if mode == "b":
        return _dg(a.astype(BF16), b.astype(BF16), dims)
    if mode == "xl":
        ah, al = _split(a)
        bh = b.astype(BF16)
        return _dg(ah, bh, dims) + _dg(al, bh, dims)
    if mode == "xr":
        ah = a.astype(BF16)
        bh, bl = _split(b)
        return _dg(ah, bh, dims) + _dg(ah, bl, dims)
    assert mode == "x3"
    ah, al = _split(a)
    bh, bl = _split(b)
    return _dg(ah, bh, dims) + (_dg(ah, bl, dims) + _dg(al, bh, dims))


def _inv_unit_lower(lows, nil, merge_mode):
    n = lows[0].shape[0]
    row = _iota2((n, n), 0)
    col = _iota2((n, n), 1)
    same_block = lambda s: (row // s) == (col // s)
    eye = (row == col).astype(F32)
    s = 2
    in_base = same_block(s)
    invs = [eye - jnp.where(in_base, low, 0.0) for low in lows]
    while s < nil:
        newly = jnp.logical_and(same_block(2 * s), jnp.logical_not(same_block(s)))
        ts = [_mm(inv, jnp.where(newly, low, 0.0), NN, merge_mode) for inv, low in zip(invs, lows)]
        invs = [inv - _mm(t, inv, NN, merge_mode) for inv, t in zip(invs, ts)]
        s *= 2
    return invs


FF_SPLIT = (0, 1536, D_FF)


def _swiglu(n, wg_ref, wu_ref, wd_ref):
    out = None
    for lo, hi in zip(FF_SPLIT[:-1], FF_SPLIT[1:]):
        gate = _dot(n, wg_ref[:, lo:hi])
        up = _dot(n, wu_ref[:, lo:hi])
        act = (gate * _sigmoid(gate) * up).astype(BF16)
        part = _dot(act, wd_ref[lo:hi, :])
        out = part if out is None else out + part
    return out


def _l2n(t, scale):
    return t * (lax.rsqrt(jnp.sum(t * t, axis=-1, keepdims=True) + 1e-12) * scale)


def _dense_in_kernel(x_ref, g1_ref, wg_ref, wu_ref, wd_ref, gm_ref, wa_ref, wb_ref, *rest, tiles_per_seq):
    if tiles_per_seq:
        cw_ref, hist_ref, h_ref, pa_ref, pb_ref, tail_ref, carry = rest
    else:
        h_ref, pa_ref, pb_ref = rest
    x = x_ref[...]
    h = x + 0.5 * _swiglu(_rms(x, g1_ref[...]).astype(BF16), wg_ref, wu_ref, wd_ref)
    h_ref[...] = h
    n = _rms(h, gm_ref[...]).astype(BF16)
    if not tiles_per_seq:
        pa_ref[...] = _dot(n, wa_ref[...])
        pb_ref[...] = _dot(n, wb_ref[...])
        return

    tm = x.shape[0]
    first = (pl.program_id(0) % tiles_per_seq) == 0
    group = lambda part: slice(part * W_B, (part + 1) * W_B)

    def finish(part, raw, scale):
        cols = group(part)
        hist = jnp.where(first, hist_ref[:, cols], carry[:, cols])
        full = jnp.concatenate([hist, raw], axis=0)
        conv = raw * cw_ref[CONV_W - 1:CONV_W, cols]
        for j in range(1, CONV_W):
            conv = conv + pltpu.roll(full, j, axis=0)[SUBLANE:, :] * cw_ref[CONV_W - 1 - j:CONV_W - j, cols]
        carry[:, cols] = raw[tm - SUBLANE:, :]
        tail_ref[0, :, cols] = raw[tm - SUBLANE:, :]
        cs = conv * _sigmoid(conv)
        for hd in range(H_B):
            lanes = slice(hd * LANE, (hd + 1) * LANE)
            t = cs[:, lanes]
            pb_ref[:, part * W_B + hd * LANE:part * W_B + (hd + 1) * LANE] = (
                t if scale is None else _l2n(t, scale))

    raw_q = _dot(n, wb_ref[:, group(0)])
    raw_k = _dot(n, wb_ref[:, group(1)])
    finish(0, raw_q, DK_B ** -0.5)
    raw_v = _dot(n, wb_ref[:, group(2)])
    finish(1, raw_k, 1.0)
    pa_ref[...] = _dot(n, wa_ref[...])
    finish(2, raw_v, None)
    pb_ref[:, PB_Z:] = _dot(n, wb_ref[:, PB_Z:])


def _dense_out_kernel(h_ref, oa_ref, ob_ref, woa_ref, wob_ref, g2_ref, wg_ref, wu_ref, wd_ref,
                      gf_ref, y_ref):
    h = (h_ref[...] + _dot(oa_ref[...].astype(BF16), woa_ref[...])
         + _dot(ob_ref[...].astype(BF16), wob_ref[...]))
    h = h + 0.5 * _swiglu(_rms(h, g2_ref[...]).astype(BF16), wg_ref, wu_ref, wd_ref)
    y_ref[...] = _rms(h, gf_ref[...])


def _const_spec(shape):
    return pl.BlockSpec(shape, lambda *_: (0,) * len(shape), pipeline_mode=pl.Buffered(1))


def _row_spec(tm, width):
    return pl.BlockSpec((tm, width), lambda i: (i, 0))


def _dense_in(x, g1, wg, wu, wd, gm, wa, wb, tm, conv=None):
    n = x.shape[0]
    consts = (g1, wg, wu, wd, gm, wa, wb)
    out_specs = [_row_spec(tm, D_MODEL), _row_spec(tm, PA_W), _row_spec(tm, PB_W)]
    out_shape = [jax.ShapeDtypeStruct((n, D_MODEL), F32), jax.ShapeDtypeStruct((n, PA_W), F32),
                 jax.ShapeDtypeStruct((n, PB_W), F32)]
    scratch, tiles_per_seq = [], 0
    if conv is not None:
        cw, hist, seq_rows = conv
        assert seq_rows % tm == 0
        consts += (cw, hist)
        tiles_per_seq = seq_rows // tm
        out_specs.append(pl.BlockSpec((1, SUBLANE, 3 * W_B), lambda i: (i, 0, 0)))
        out_shape.append(jax.ShapeDtypeStruct((n // tm, SUBLANE, 3 * W_B), F32))
        scratch = [pltpu.VMEM((SUBLANE, 3 * W_B), F32)]
    return pl.pallas_call(
        functools.partial(_dense_in_kernel, tiles_per_seq=tiles_per_seq),
        grid=(n // tm,),
        in_specs=[_row_spec(tm, D_MODEL)] + [_const_spec(c.shape) for c in consts],
        out_specs=out_specs,
        out_shape=out_shape,
        scratch_shapes=scratch,
        compiler_params=pltpu.CompilerParams(dimension_semantics=("arbitrary",),
                                             vmem_limit_bytes=VMEM_LIMIT),
        name="dense_in",
    )(x, *consts)


def _dense_out(h, oa, ob, woa, wob, g2, wg, wu, wd, gf, tm):
    n = h.shape[0]
    consts = (woa, wob, g2, wg, wu, wd, gf)
    return pl.pallas_call(
        _dense_out_kernel,
        grid=(n // tm,),
        in_specs=[_row_spec(tm, D_MODEL), _row_spec(tm, W_A), _row_spec(tm, W_B)]
        + [_const_spec(c.shape) for c in consts],
        out_specs=_row_spec(tm, D_MODEL),
        out_shape=jax.ShapeDtypeStruct((n, D_MODEL), F32),
        compiler_params=pltpu.CompilerParams(dimension_semantics=("arbitrary",),
                                             vmem_limit_bytes=VMEM_LIMIT),
        name="dense_out",
    )(h, oa, ob, *consts)


RWKV_PREC = dict(cum="xr", seg="b", pair="b", invm="b", sread="b", akv="b", solve="b", inter="b",
                 state="x3")


def _rwkv_kernel(pa_ref, prev_ref, s0_ref, mu_ref, w0_ref, wdu_ref, a0_ref, wau_ref, wgu_ref,
                 kk_ref, ka_ref, rk_ref, lnw_ref, lnb_ref, o_ref, sout_ref, xbuf, s_scr,
                 *, chunk, nseq, shared_init):
    c = pl.program_id(1)
    C = chunk
    P = RWKV_PREC

    @pl.when(c == 0)
    def _():
        for i in range(nseq):
            j = 0 if shared_init else i
            xbuf[i, 0:SUBLANE, :] = prev_ref[j]
            s_scr[i] = s0_ref[j]

    tri = (_iota2((C, C), 0) >= _iota2((C, C), 1)).astype(F32)
    mid = C // 2 - 1

    def prep(i):
        x = pa_ref[i]
        full = jnp.concatenate([xbuf[i], x], axis=0)
        prev = pltpu.roll(full, 1, axis=0)[SUBLANE:, :]
        xm = x + (prev - x) * mu_ref[...]
        xbuf[i] = x[C - SUBLANE:, :]

        r = xm[:, 0:W_A]
        k = xm[:, W_A:2 * W_A]
        v = xm[:, 2 * W_A:3 * W_A]
        sm = xm[:, PA_SMALL:PA_SMALL + LANE]
        gd = xm[:, PA_GATE:PA_GATE + LANE]

        wl = w0_ref[...] + _dot(jnp.tanh(sm).astype(BF16), wdu_ref[...])
        lw = -jnp.exp(-_softplus(-wl) - 0.5)
        a = _sigmoid(a0_ref[...] + _dot(sm.astype(BF16), wau_ref[...]))
        g = _dot(_sigmoid(gd).astype(BF16), wgu_ref[...])
        kkr = k * kk_ref[...]
        k2 = k * (1.0 + (a - 1.0) * ka_ref[...])
        G = _mm(tri, lw, NN, P["cum"])
        return dict(r=r, v=v, a=a, g=g, kkr=kkr, k2=k2, G=G, Gx=G - lw, Gm=G[mid:mid + 1, :],
                    Gc=G[C - 1:C, :])

    seqs = [prep(i) for i in range(nseq)]

    C2 = 2 * C
    trow = _iota2((C2, C2), 0) % C
    tcol = _iota2((C2, C2), 1) % C
    incl = trow >= tcol
    strict = trow > tcol
    lane = _iota2((1, LANE), 1)
    m0 = (lane < N_A).astype(F32)
    m1 = (lane >= N_A).astype(F32)
    seg = ((_iota2((LANE, LANE), 0) // N_A) == (_iota2((LANE, LANE), 1) // N_A)).astype(F32)
    stack = lambda t: jnp.concatenate([t * m0, t * m1], axis=0)
    merged = C2 % LANE == 0

    inst = [(i, p) for i in range(nseq) for p in range(H_A // 2)]
    n = range(len(inst))
    sls = [slice(p * LANE, (p + 1) * LANE) for _, p in inst]
    take = lambda name: [seqs[i][name][:, sls[j]] for j, (i, _) in enumerate(inst)]
    rs, vs_, k2s, kkrs, as_, gs = take("r"), take("v"), take("k2"), take("kkr"), take("a"), take("g")
    Gs, Gxs, Gms, Gcs = take("G"), take("Gx"), take("Gm"), take("Gc")
    Ss = [s_scr[i, p] for i, p in inst]

    ssq = [_mm(t * t, seg, NN, P["seg"]) for t in kkrs]
    kks = [t * lax.rsqrt(q + 1e-12) for t, q in zip(kkrs, ssq)]
    kkas = [kks[j] * as_[j] for j in n]
    inv_rel = [jnp.exp(Gms[j] - Gs[j]) for j in n]
    dec = [jnp.exp(Gcs[j] - Gs[j]) for j in n]
    lhs = [jnp.concatenate([stack(kks[j] * jnp.exp(Gxs[j] - Gms[j])),
                            stack(rs[j] * jnp.exp(Gs[j] - Gms[j]))], axis=0) for j in n]
    ais = [stack(kkas[j] * inv_rel[j]) for j in n]
    kis = [stack(k2s[j] * inv_rel[j]) for j in n]
    if merged:
        pm = [_mm(lhs[j], jnp.concatenate([ais[j], kis[j]], axis=0), NT, P["pair"]) for j in n]
        pas_, pks_ = [t[:, :C2] for t in pm], [t[:, C2:] for t in pm]
    else:
        pas_ = [_mm(lhs[j], ais[j], NT, P["pair"]) for j in n]
        pks_ = [_mm(lhs[j], kis[j], NT, P["pair"]) for j in n]
    a_aa = [jnp.where(strict, t[:C2], 0.0) for t in pas_]
    a_ak = [jnp.where(strict, t[:C2], 0.0) for t in pks_]
    a_ra = [jnp.where(incl, t[C2:], 0.0) for t in pas_]
    a_rk = [jnp.where(incl, t[C2:], 0.0) for t in pks_]
    ainv = _inv_unit_lower(a_aa, C, P["invm"])

    vst = [stack(t) for t in vs_]
    akv = [_mm(a_ak[j], vst[j], NN, P["akv"]) for j in n]
    sx = [_mm(jnp.concatenate([kks[j] * jnp.exp(Gxs[j]), rs[j] * jnp.exp(Gs[j])], axis=0), Ss[j], NT,
              P["sread"]) for j in n]
    us = [-_mm(ainv[j], stack(sx[j][:C]) + akv[j], NN, P["solve"]) for j in n]
    uv = [jnp.concatenate([us[j], vst[j]], axis=0) for j in n]
    s_new = [Ss[j] * jnp.exp(Gcs[j])
             + _mm(uv[j], jnp.concatenate([stack(kkas[j] * dec[j]), stack(k2s[j] * dec[j])], axis=0),
                   TN, P["state"]) for j in n]
    for j, (i, p) in enumerate(inst):
        s_scr[i, p] = s_new[j]
    if merged:
        inter = [_mm(jnp.concatenate([a_ra[j], a_rk[j]], axis=1), uv[j], NN, P["inter"]) for j in n]
    else:
        inter = [_mm(a_ra[j], us[j], NN, P["inter"]) + _mm(a_rk[j], vst[j], NN, P["inter"]) for j in n]
    Os = [sx[j][C:] + inter[j][:C] + inter[j][C:] for j in n]
    means = [_mm(t, seg, NN, P["seg"]) * (1.0 / N_A) for t in Os]
    dlts = [t - m for t, m in zip(Os, means)]
    vars_ = [_mm(t * t, seg, NN, P["seg"]) * (1.0 / N_A) for t in dlts]
    bonus = [_mm(rs[j] * k2s[j] * rk_ref[:, sls[j]], seg, NN, P["seg"]) * vs_[j] for j in n]
    for j, (i, p) in enumerate(inst):
        on = dlts[j] * lax.rsqrt(vars_[j] + LNX_EPS) * lnw_ref[:, sls[j]] + lnb_ref[:, sls[j]]
        o_ref[i, :, sls[j]] = (on + bonus[j]) * gs[j]

    @pl.when(c == pl.num_programs(1) - 1)
    def _():
        sout_ref[...] = s_scr[...]


def _seq_specs(nseq, shared_init, hist_w, state_shape):
    zeros = (0,) * len(state_shape)
    if shared_init:
        return [pl.BlockSpec((1, SUBLANE, hist_w), lambda b, c: (0, 0, 0)),
                pl.BlockSpec((1,) + state_shape, lambda b, c: (0,) + zeros)]
    return [pl.BlockSpec((nseq, SUBLANE, hist_w), lambda b, c: (b, 0, 0)),
            pl.BlockSpec((nseq,) + state_shape, lambda b, c: (b,) + zeros)]


def _rwkv(pa, prev8, s0, consts, chunk, nseq):
    B, L, _ = pa.shape
    shared_init = prev8.shape[0] == 1 and B > 1
    state_shape = (H_A // 2, LANE, LANE)
    return pl.pallas_call(
        functools.partial(_rwkv_kernel, chunk=chunk, nseq=nseq, shared_init=shared_init),
        grid=(B // nseq, L // chunk),
        in_specs=[pl.BlockSpec((nseq, chunk, PA_W), lambda b, c: (b, c, 0))]
        + _seq_specs(nseq, shared_init, PA_W, state_shape)
        + [pl.BlockSpec(t.shape, lambda b, c: (0, 0)) for t in consts],
        out_specs=[pl.BlockSpec((nseq, chunk, W_A), lambda b, c: (b, c, 0)),
                   pl.BlockSpec((nseq,) + state_shape, lambda b, c: (b, 0, 0, 0))],
        out_shape=[jax.ShapeDtypeStruct((B, L, W_A), F32),
                   jax.ShapeDtypeStruct((B,) + state_shape, F32)],
        scratch_shapes=[pltpu.VMEM((nseq, SUBLANE, PA_W), F32),
                        pltpu.VMEM((nseq,) + state_shape, F32)],
        compiler_params=pltpu.CompilerParams(dimension_semantics=("arbitrary", "arbitrary"),
                                             vmem_limit_bytes=VMEM_LIMIT),
        name="rwkv_chunk%d" % chunk,
    )(pa, prev8, s0, *consts)


GDN_PREC = dict(cum="xr", tr="xr", kk="b", invm="b", wu="b", qk="b", sread="b", qkv="b", state="b")


def _gdn_kernel(pb_ref, ga_ref, hist_ref, s0_ref, cw_ref, alog_ref, dtb_ref, nw_ref, o_ref, sout_ref,
                xbuf, s_scr, *, chunk, nseq, shared_init, prepared):
    c = pl.program_id(1)
    C = chunk
    P = GDN_PREC

    @pl.when(c == 0)
    def _():
        for i in range(nseq):
            j = 0 if shared_init else i
            xbuf[i, 0:SUBLANE, :] = hist_ref[j]
            s_scr[i] = s0_ref[j]

    row = _iota2((C, C), 0)
    col = _iota2((C, C), 1)
    incl = row >= col
    strict = row > col
    tri = incl.astype(F32)
    sel = (_iota2((SUBLANE, LANE), 0) + SMALL_GDN_A == _iota2((SUBLANE, LANE), 1)).astype(F32)

    def prep(i):
        x = pb_ref[i]
        qkv = x[:, 0:3 * W_B]
        if prepared:
            cs = qkv
        else:
            full = jnp.concatenate([xbuf[i], qkv], axis=0)
            conv = qkv * cw_ref[CONV_W - 1:CONV_W, :]
            for j in range(1, CONV_W):
                conv = conv + pltpu.roll(full, j, axis=0)[SUBLANE:, :] * cw_ref[CONV_W - 1 - j:CONV_W - j, :]
            xbuf[i] = qkv[C - SUBLANE:, :]
            cs = conv * _sigmoid(conv)
        gates = ga_ref[i]
        glog = -jnp.exp(alog_ref[...]) * _softplus(gates + dtb_ref[...])
        beta = _sigmoid(gates)
        G = _mm(tri, glog, NN, P["cum"])
        Gt = _mm(sel, G, NT, P["tr"])
        return dict(cs=cs, z=x[:, PB_Z:PB_Z + W_B], beta=beta, G=G, Gt=Gt)

    seqs = [prep(i) for i in range(nseq)]

    inst = [(i, h) for i in range(nseq) for h in range(H_B)]
    n = range(len(inst))
    l2n = (lambda t, c: t) if prepared else _l2n
    head = lambda i, h, part: seqs[i]["cs"][:, part * W_B + h * LANE:part * W_B + (h + 1) * LANE]
    qs = [l2n(head(i, h, 0), DK_B ** -0.5) for i, h in inst]
    ks = [l2n(head(i, h, 1), 1.0) for i, h in inst]
    vs = [head(i, h, 2) for i, h in inst]
    gcols = [seqs[i]["G"][:, SMALL_GDN_A + h:SMALL_GDN_A + h + 1] for i, h in inst]
    grows = [seqs[i]["Gt"][h:h + 1, :] for i, h in inst]
    bcols = [seqs[i]["beta"][:, SMALL_GDN_B + h:SMALL_GDN_B + h + 1] for i, h in inst]
    Ss = [s_scr[i, h] for i, h in inst]

    dmats = [jnp.where(incl, jnp.exp(jnp.where(incl, gcols[j] - grows[j], 0.0)), 0.0) for j in n]
    kbs = [ks[j] * bcols[j] for j in n]
    lows = [jnp.where(strict, _mm(kbs[j], ks[j], NT, P["kk"]) * dmats[j], 0.0) for j in n]
    ainv = _inv_unit_lower(lows, C, P["invm"])
    egs = [jnp.exp(t) for t in gcols]
    wus = [_mm(ainv[j], jnp.concatenate([kbs[j] * egs[j], vs[j] * bcols[j]], axis=1), NN, P["wu"])
           for j in n]
    qks = [_mm(qs[j], ks[j], NT, P["qk"]) * dmats[j] for j in n]
    srs = [_mm(jnp.concatenate([wus[j][:, :DK_B], qs[j] * egs[j]], axis=0), Ss[j], NN, P["sread"])
           for j in n]
    v_new = [wus[j][:, DK_B:] - srs[j][:C] for j in n]
    glast = [t[C - 1:C, :] for t in gcols]
    s_new = [Ss[j] * jnp.exp(glast[j])
             + _mm(ks[j] * jnp.exp(glast[j] - gcols[j]), v_new[j], TN, P["state"]) for j in n]
    for j, (i, h) in enumerate(inst):
        s_scr[i, h] = s_new[j]
    os_ = [srs[j][C:] + _mm(qks[j], v_new[j], NN, P["qkv"]) for j in n]
    for j, (i, h) in enumerate(inst):
        o = os_[j]
        o = o * lax.rsqrt(jnp.mean(o * o, axis=-1, keepdims=True) + NORM_EPS) * nw_ref[...]
        zh = seqs[i]["z"][:, h * LANE:(h + 1) * LANE]
        o_ref[i, :, h * LANE:(h + 1) * LANE] = o * (zh * _sigmoid(zh))

    @pl.when(c == pl.num_programs(1) - 1)
    def _():
        sout_ref[...] = s_scr[...]


def _gdn(pb, pa, hist8, s0, consts, chunk, nseq, prepared=False):
    B, L, _ = pb.shape
    shared_init = hist8.shape[0] == 1 and B > 1
    state_shape = (H_B, DK_B, DV_B)
    return pl.pallas_call(
        functools.partial(_gdn_kernel, chunk=chunk, nseq=nseq, shared_init=shared_init,
                          prepared=prepared),
        grid=(B // nseq, L // chunk),
        in_specs=[pl.BlockSpec((nseq, chunk, PB_W), lambda b, c: (b, c, 0)),
                  pl.BlockSpec((nseq, chunk, LANE), lambda b, c: (b, c, PA_SMALL // LANE))]
        + _seq_specs(nseq, shared_init, 3 * W_B, state_shape)
        + [pl.BlockSpec(t.shape, lambda b, c: (0, 0)) for t in consts],
        out_specs=[pl.BlockSpec((nseq, chunk, W_B), lambda b, c: (b, c, 0)),
                   pl.BlockSpec((nseq,) + state_shape, lambda b, c: (b, 0, 0, 0))],
        out_shape=[jax.ShapeDtypeStruct((B, L, W_B), F32),
                   jax.ShapeDtypeStruct((B,) + state_shape, F32)],
        scratch_shapes=[pltpu.VMEM((nseq, SUBLANE, 3 * W_B), F32),
                        pltpu.VMEM((nseq,) + state_shape, F32)],
        compiler_params=pltpu.CompilerParams(dimension_semantics=("arbitrary", "arbitrary"),
                                             vmem_limit_bytes=VMEM_LIMIT),
        name="gdn_chunk%d" % chunk,
    )(pb, pa, hist8, s0, *consts)


def _row_mask(nrow, i):
    return _iota2((nrow, 1), 0) == i


STEP_VECS = ("w", "kk", "kka", "k2", "v", "r", "g", "bon")


def _rwkv_step_prep_kernel(pa_ref, prev_ref, mu_ref, w0_ref, wdu_ref, a0_ref, wau_ref, wgu_ref,
                           kk_ref, ka_ref, rk_ref, vec_ref):
    x = pa_ref[...]
    xm = x + (prev_ref[...] - x) * mu_ref[...]
    r = xm[:, 0:W_A]
    k = xm[:, W_A:2 * W_A]
    v = xm[:, 2 * W_A:3 * W_A]
    sm = xm[:, PA_SMALL:PA_SMALL + LANE]
    gd = xm[:, PA_GATE:PA_GATE + LANE]
    wl = w0_ref[...] + _dot(jnp.tanh(sm).astype(BF16), wdu_ref[...])
    w = jnp.exp(-jnp.exp(-_softplus(-wl) - 0.5))
    a = _sigmoid(a0_ref[...] + _dot(sm.astype(BF16), wau_ref[...]))
    g = _dot(_sigmoid(gd).astype(BF16), wgu_ref[...])
    kkr = k * kk_ref[...]
    k2 = k * (1.0 + (a - 1.0) * ka_ref[...])
    bon = r * k2 * rk_ref[...]
    seg = ((_iota2((LANE, LANE), 0) // N_A) == (_iota2((LANE, LANE), 1) // N_A)).astype(F32)
    ssq = jnp.concatenate([_mm(kkr[:, p * LANE:(p + 1) * LANE] ** 2, seg, NN, RWKV_PREC["seg"])
                           for p in range(H_A // 2)], axis=-1)
    kk = kkr * lax.rsqrt(ssq + 1e-12)
    vecs = dict(w=w, kk=kk, kka=kk * a, k2=k2, v=v, r=r, g=g, bon=bon)
    for j, name in enumerate(STEP_VECS):
        vec_ref[j] = vecs[name].T


def _rwkv_step_state_kernel(vec_ref, s_ref, lnw_ref, lnb_ref, o_ref, sout_ref, o_scr):
    w, kk, kka, k2, v, r, g, bon = [vec_ref[j] for j in range(len(STEP_VECS))]
    for i in range(N_A):
        S = s_ref[0, i]
        sa = jnp.sum(S * kk, axis=0, keepdims=True)
        s_new = S * w - sa * kka + v[i:i + 1, :] * k2
        sout_ref[0, i] = s_new
        o_scr[i:i + 1, :] = jnp.sum(s_new * r, axis=0, keepdims=True)
    o = o_scr[...]
    mean = jnp.mean(o, axis=0, keepdims=True)
    dlt = o - mean
    var = jnp.mean(dlt * dlt, axis=0, keepdims=True)
    on = dlt * lax.rsqrt(var + LNX_EPS) * lnw_ref[...] + lnb_ref[...]
    o_ref[...] = (on + jnp.sum(bon, axis=0, keepdims=True) * v) * g


def _gdn_step_kernel(pb_ref, ga_ref, hist_ref, s_ref, cw_ref, alog_ref, dtb_ref, nw_ref, o_ref, sout_ref):
    ns = pb_ref.shape[0]
    x = pb_ref[...]
    conv = x[:, 0:3 * W_B] * cw_ref[CONV_W - 1:CONV_W, :]
    for j in range(CONV_W - 1):
        conv = conv + hist_ref[j] * cw_ref[j:j + 1, :]
    cs = conv * _sigmoid(conv)
    z = x[:, PB_Z:PB_Z + W_B]
    gates = ga_ref[...]
    eg = jnp.exp(-jnp.exp(alog_ref[...]) * _softplus(gates + dtb_ref[...]))
    beta = _sigmoid(gates)

    l2n = lambda t: t * lax.rsqrt(jnp.sum(t * t, axis=-1, keepdims=True) + 1e-12)
    for h in range(H_B):
        hs = slice(h * LANE, (h + 1) * LANE)
        q_h = l2n(cs[:, hs]) * (DK_B ** -0.5)
        k_h = l2n(cs[:, W_B + h * LANE:W_B + (h + 1) * LANE])
        v_h = cs[:, 2 * W_B + h * LANE:2 * W_B + (h + 1) * LANE]
        eg_h = eg[:, SMALL_GDN_A + h:SMALL_GDN_A + h + 1]
        b_h = beta[:, SMALL_GDN_B + h:SMALL_GDN_B + h + 1]
        seq = range(ns)
        Ss = [s_ref[i, h] for i in seq]
        kss = [_mm(k_h, Ss[i], NN, "xr") for i in seq]
        v_new = [b_h * (v_h - eg_h * kss[i]) for i in seq]
        s_new = [Ss[i] * eg_h[i:i + 1, :] + _mm(jnp.where(_row_mask(ns, i), k_h, 0.0), v_new[i], TN, "b")
                 for i in seq]
        for i in seq:
            sout_ref[i, h] = s_new[i]
        o_all = [_mm(q_h, s_new[i], NN, "b") for i in seq]
        o_h = jnp.zeros((ns, DV_B), F32)
        for i in seq:
            o_h = jnp.where(_row_mask(ns, i), o_all[i], o_h)
        o_h = o_h * lax.rsqrt(jnp.mean(o_h * o_h, axis=-1, keepdims=True) + NORM_EPS) * nw_ref[...]
        zh = z[:, hs]
        o_ref[:, hs] = o_h * (zh * _sigmoid(zh))


def _step_const_specs(ts):
    return [pl.BlockSpec(t.shape, lambda b: (0,) * t.ndim) for t in ts]


def _rwkv_step(pa, prev, s_t, consts):
    B = s_t.shape[-1]
    prep_consts = consts[:-2]
    lnw_col, lnb_col = [c.reshape(W_A, 1) for c in consts[-2:]]
    vecs = pl.pallas_call(
        _rwkv_step_prep_kernel,
        grid=(1,),
        in_specs=[pl.BlockSpec((B, PA_W), lambda b: (0, 0)),
                  pl.BlockSpec((B, PA_W), lambda b: (0, 0))] + _step_const_specs(prep_consts),
        out_specs=pl.BlockSpec((len(STEP_VECS), W_A, B), lambda b: (0, 0, 0)),
        out_shape=jax.ShapeDtypeStruct((len(STEP_VECS), W_A, B), F32),
        compiler_params=pltpu.CompilerParams(dimension_semantics=("arbitrary",),
                                             vmem_limit_bytes=VMEM_LIMIT),
        name="rwkv_step_prep",
    )(pa, prev, *prep_consts)
    state_spec = pl.BlockSpec((1, N_A, N_A, B), lambda h: (h, 0, 0, 0))
    return pl.pallas_call(
        _rwkv_step_state_kernel,
        grid=(H_A,),
        in_specs=[pl.BlockSpec((len(STEP_VECS), N_A, B), lambda h: (0, h, 0)), state_spec,
                  pl.BlockSpec((N_A, 1), lambda h: (h, 0)), pl.BlockSpec((N_A, 1), lambda h: (h, 0))],
        out_specs=[pl.BlockSpec((N_A, B), lambda h: (h, 0)), state_spec],
        out_shape=[jax.ShapeDtypeStruct((W_A, B), F32), jax.ShapeDtypeStruct(s_t.shape, F32)],
        scratch_shapes=[pltpu.VMEM((N_A, B), F32)],
        compiler_params=pltpu.CompilerParams(dimension_semantics=("arbitrary",),
                                             vmem_limit_bytes=VMEM_LIMIT),
        name="rwkv_step_state",
    )(vecs, s_t, lnw_col, lnb_col)


def _gdn_step(pb, pa, hist, s, consts, ns):
    B = s.shape[0]
    return pl.pallas_call(
        _gdn_step_kernel,
        grid=(B // ns,),
        in_specs=[pl.BlockSpec((ns, PB_W), lambda b: (b, 0)),
                  pl.BlockSpec((ns, LANE), lambda b: (b, PA_SMALL // LANE)),
                  pl.BlockSpec((CONV_W - 1, ns, 3 * W_B), lambda b: (0, b, 0)),
                  pl.BlockSpec((ns, H_B, DK_B, DV_B), lambda b: (b, 0, 0, 0))] + _step_const_specs(consts),
        out_specs=[pl.BlockSpec((ns, W_B), lambda b: (b, 0)),
                   pl.BlockSpec((ns, H_B, DK_B, DV_B), lambda b: (b, 0, 0, 0))],
        out_shape=[jax.ShapeDtypeStruct((B, W_B), F32), jax.ShapeDtypeStruct(s.shape, F32)],
        compiler_params=pltpu.CompilerParams(dimension_semantics=("arbitrary",),
                                             vmem_limit_bytes=VMEM_LIMIT),
        name="gdn_step",
    )(pb, pa, hist, s, *consts)


def _pad_cols(t, width):
    return jnp.pad(t, [(0, 0)] * (t.ndim - 1) + [(0, width - t.shape[-1])])


def _pa_layout(t, gdn_gates=None):
    main = t[..., :3 * W_A]
    small = t[..., 3 * W_A:3 * W_A + D_DECAY + D_AAA]
    gate = t[..., 3 * W_A + D_DECAY + D_AAA:]
    if gdn_gates is not None:
        small = jnp.concatenate([small, gdn_gates], axis=-1)
    return jnp.concatenate([main, _pad_cols(small, LANE), _pad_cols(gate, LANE)], axis=-1)


def _pa_unlayout(t):
    return jnp.concatenate([t[..., :3 * W_A], t[..., PA_SMALL:PA_SMALL + D_DECAY + D_AAA],
                            t[..., PA_GATE:PA_GATE + D_GATE]], axis=-1)


def _small_rows(t, first_row):
    return jnp.pad(t, [(first_row, LANE - first_row - t.shape[0]), (0, 0)])


def _small_lanes(t, first_lane):
    return jnp.pad(t, [(0, 0), (first_lane, LANE - first_lane - t.shape[1])])


def _pair_unblock(s):
    B = s.shape[0]
    return jnp.stack([s[:, :, :N_A, :N_A], s[:, :, N_A:, N_A:]], axis=2).reshape(B, H_A, N_A, N_A)


def _history_rows(rows):
    B, n, w = rows.shape
    return jnp.concatenate([jnp.zeros((B, SUBLANE - n, w), F32), rows], axis=1)


def kernel(x_prompt, x_sample, state_rwkv, state_shift, state_gdn, state_conv, meta_tokens,
           g_ffn1, w_gate1, w_up1, w_down1, g_mix, w_in, mu_shift, w0, w_decay_up, a0, w_a_up,
           w_g_up, k_k, k_a, r_k, lnx_w, lnx_b, conv_w, a_log, dt_bias, gdn_norm_w, w_out,
           g_ffn2, w_gate2, w_up2, w_down2, g_final):
    assert g_ffn1.shape[0] == 1, "single trunk layer"
    bp, sp, _ = x_prompt.shape
    bs = x_sample.shape[0]
    assert x_sample.shape[1] == 1 and sp % CHUNK_RWKV == 0 and sp % CHUNK_GDN == 0
    assert (bp * sp) % TM_DENSE == 0 and (bp * sp) % TM_DENSE_OUT == 0
    assert bp % SEQS_PER_STEP == 0 and bs % DEC_SEQS_PER_STEP == 0
    row = lambda t: t.reshape(1, -1).astype(F32)

    ffn1 = (w_gate1[0].astype(BF16), w_up1[0].astype(BF16), w_down1[0].astype(BF16))
    ffn2 = (w_gate2[0].astype(BF16), w_up2[0].astype(BF16), w_down2[0].astype(BF16))
    win_a = _pa_layout(w_in[0][:, :N_A_IN], w_in[0][:, N_A_IN + PB_W:]).astype(BF16)
    win_b = w_in[0][:, N_A_IN:N_A_IN + PB_W].astype(BF16)
    wo_a = w_out[0][:W_A].astype(BF16)
    wo_b = w_out[0][W_A:].astype(BF16)
    dense_in_consts = (row(g_ffn1[0]), *ffn1, row(g_mix[0]), win_a, win_b)
    dense_out_consts = (wo_a, wo_b, row(g_ffn2[0]), *ffn2, row(g_final))
    rwkv_consts = (_pa_layout(row(mu_shift[0])), row(w0[0]),
                   _small_rows(w_decay_up[0], 0).astype(BF16), row(a0[0]),
                   _small_rows(w_a_up[0], SMALL_AAA).astype(BF16), _small_rows(w_g_up[0], 0).astype(BF16),
                   row(k_k[0]), row(k_a[0]), row(r_k[0]), row(lnx_w[0]), row(lnx_b[0]))
    gdn_consts = (conv_w[0].astype(F32), _small_lanes(row(a_log[0]), SMALL_GDN_A),
                  _small_lanes(row(dt_bias[0]), SMALL_GDN_A), row(gdn_norm_w[0]))

    xs = jnp.concatenate([x_sample[:, 0, :].astype(F32), meta_tokens.astype(F32)], axis=0)
    hs, pas, pbs = _dense_in(xs, *dense_in_consts, tm=xs.shape[0])

    pa_meta = pas[bs:][None]
    pb_meta = pbs[bs:][None]
    _, rw_meta = _rwkv(pa_meta, jnp.zeros((1, SUBLANE, PA_W), F32),
                       jnp.zeros((1, H_A // 2, LANE, LANE), F32), rwkv_consts, N_META, 1)
    _, gd_meta = _gdn(pb_meta, pa_meta, jnp.zeros((1, SUBLANE, 3 * W_B), F32),
                      jnp.zeros((1, H_B, DK_B, DV_B), F32), gdn_consts, N_META, 1)

    conv_hist = _history_rows(pb_meta[:, -(CONV_W - 1):, :3 * W_B])
    hp, pap, pbp, qkv_tail = _dense_in(x_prompt.reshape(bp * sp, D_MODEL).astype(F32), *dense_in_consts,
                                       tm=TM_DENSE, conv=(gdn_consts[0], conv_hist[0], sp))
    pap3 = pap.reshape(bp, sp, PA_W)
    pbp3 = pbp.reshape(bp, sp, PB_W)
    oa_p, rw_p = _rwkv(pap3, _history_rows(pa_meta[:, -1:, :]), rw_meta, rwkv_consts,
                       CHUNK_RWKV, SEQS_PER_STEP)
    ob_p, gd_p = _gdn(pbp3, pap3, conv_hist, gd_meta, gdn_consts, CHUNK_GDN, SEQS_PER_STEP, prepared=True)
    conv_p = qkv_tail.reshape(bp, sp // TM_DENSE, SUBLANE, 3 * W_B)[:, -1, SUBLANE - (CONV_W - 1):, :]
    y_p = _dense_out(hp, oa_p.reshape(bp * sp, W_A), ob_p.reshape(bp * sp, W_B), *dense_out_consts,
                     tm=TM_DENSE_OUT)

    oa_t, rw_t = _rwkv_step(pas, _pa_layout(state_shift[0].astype(F32)),
                            jnp.transpose(state_rwkv[0].astype(F32), (1, 2, 3, 0)), rwkv_consts)
    oa_s = oa_t.T
    rw_s = jnp.transpose(rw_t, (3, 0, 1, 2))
    ob_s, gd_s = _gdn_step(pbs, pas, jnp.swapaxes(state_conv[0].astype(F32), 0, 1), state_gdn[0].astype(F32),
                           gdn_consts, DEC_SEQS_PER_STEP)
    y_s = _dense_out(hs[:bs], oa_s, ob_s, *dense_out_consts, tm=bs)

    new_conv_s = jnp.concatenate([state_conv[0].astype(F32)[:, 1:, :], pbs[:bs, None, :3 * W_B]], axis=1)
    return (y_p.reshape(bp, sp, D_MODEL).astype(x_prompt.dtype),
            y_s.reshape(bs, 1, D_MODEL).astype(x_sample.dtype),
            _pair_unblock(rw_p)[None],
            _pa_unlayout(pap3[:, -1, :])[None],
            gd_p[None],
            conv_p[None],
            rw_s[None],
            _pa_unlayout(pas[:bs])[None],
            gd_s[None],
            new_conv_s[None])
```

```python
import functools

import jax
import jax.numpy as jnp
from jax import lax
from jax.experimental import pallas as pl
from jax.experimental.pallas import tpu as pltpu

F32 = jnp.float32
BF16 = jnp.bfloat16

LANE = 128
SUBLANE = 8
VMEM_LIMIT = 56 * 1024 * 1024

D_MODEL = 1024
D_FF = 2816
N_META = 16
H_A, N_A = 8, 64
W_A = H_A * N_A
D_DECAY, D_AAA, D_GATE = 32, 32, 96
N_A_IN = 3 * W_A + D_DECAY + D_AAA + D_GATE
H_B, DK_B, DV_B = 4, 128, 128
W_B = H_B * DV_B
CONV_W = 4
N_B_IN = 4 * W_B + 2 * H_B
LNX_EPS = 64e-5
NORM_EPS = 1e-6

PA_W = 3 * W_A + 2 * LANE
PA_SMALL = 3 * W_A
PA_GATE = PA_SMALL + LANE
SMALL_AAA = D_DECAY
SMALL_GDN_A = D_DECAY + D_AAA
SMALL_GDN_B = SMALL_GDN_A + H_B
PB_W = 4 * W_B
PB_Z = 3 * W_B

TM_DENSE = 512
TM_DENSE_OUT = 512
CHUNK_RWKV = 64
CHUNK_GDN = 128
SEQS_PER_STEP = 4
DEC_SEQS_PER_STEP = SUBLANE

NN = (((1,), (0,)), ((), ()))
NT = (((1,), (1,)), ((), ()))
TN = (((0,), (0,)), ((), ()))


def _dot(a, b):
    return jnp.dot(a, b, preferred_element_type=F32)


def _sigmoid(x):
    return 1.0 / (1.0 + jnp.exp(-x))


def _softplus(x):
    return jnp.maximum(x, 0.0) + jnp.log(1.0 + jnp.exp(-jnp.abs(x)))


def _rms(x, g):
    return x * lax.rsqrt(jnp.mean(x * x, axis=-1, keepdims=True) + NORM_EPS) * g


def _iota2(shape, dim):
    return lax.broadcasted_iota(jnp.int32, shape, dim)


def _split(x):
    hi = x.astype(BF16)
    return hi, (x - hi.astype(F32)).astype(BF16)


def _dg(a, b, dims):
    return lax.dot_general(a, b, dims, preferred_element_type=F32)


def _mm(a, b, dims=NN, mode="b"):
    if mode == "b":
        return _dg(a.astype(BF16), b.astype(BF16), dims)
    if mode == "xl":
        ah, al = _split(a)
        bh = b.astype(BF16)
        return _dg(ah, bh, dims) + _dg(al, bh, dims)
    if mode == "xr":
        ah = a.astype(BF16)
        bh, bl = _split(b)
        return _dg(ah, bh, dims) + _dg(ah, bl, dims)
    assert mode == "x3"
    ah, al = _split(a)
    bh, bl = _split(b)
    return _dg(ah, bh, dims) + (_dg(ah, bl, dims) + _dg(al, bh, dims))


def _inv_unit_lower(lows, nil, merge_mode, expand=None):
    shape = lows[0].shape
    row = _iota2(shape, 0)
    col = _iota2(shape, 1) % nil
    same_block = lambda s: (row // s) == (col // s)
    eye = (row == col).astype(F32)
    if expand is None:
        expand = lambda t: t
    narrow = (lambda t: t.astype(BF16)) if merge_mode == "b" else (lambda t: t)
    s = 2
    in_base = same_block(s)
    invs = [eye - jnp.where(in_base, low, 0.0) for low in lows]
    while s < nil:
        newly = jnp.logical_and(same_block(2 * s), jnp.logical_not(same_block(s)))
        ts = [_mm(inv, expand(narrow(jnp.where(newly, low, 0.0))), NN, merge_mode)
              for inv, low in zip(invs, lows)]
        invs = [inv - _mm(t, expand(narrow(inv)), NN, merge_mode) for inv, t in zip(invs, ts)]
        s *= 2
    return invs


FF_SPLIT = (0, 1536, D_FF)


def _swiglu(n, wg_ref, wu_ref, wd_ref):
    out = None
    for lo, hi in zip(FF_SPLIT[:-1], FF_SPLIT[1:]):
        gate = _dot(n, wg_ref[:, lo:hi])
        up = _dot(n, wu_ref[:, lo:hi])
        act = (gate * _sigmoid(gate) * up).astype(BF16)
        part = _dot(act, wd_ref[lo:hi, :])
        out = part if out is None else out + part
    return out


def _dense_in_kernel(x_ref, g1_ref, wg_ref, wu_ref, wd_ref, gm_ref, wa_ref, wb_ref,
                     h_ref, pa_ref, pb_ref):
    x = x_ref[...]
    h = x + 0.5 * _swiglu(_rms(x, g1_ref[...]).astype(BF16), wg_ref, wu_ref, wd_ref)
    h_ref[...] = h
    n = _rms(h, gm_ref[...]).astype(BF16)
    pa_ref[...] = _dot(n, wa_ref[...])
    pb_ref[...] = _dot(n, wb_ref[...])


def _dense_out_kernel(h_ref, oa_ref, ob_ref, woa_ref, wob_ref, g2_ref, wg_ref, wu_ref, wd_ref,
                      gf_ref, y_ref):
    h = (h_ref[...] + _dot(oa_ref[...].astype(BF16), woa_ref[...])
         + _dot(ob_ref[...].astype(BF16), wob_ref[...]))
    h = h + 0.5 * _swiglu(_rms(h, g2_ref[...]).astype(BF16), wg_ref, wu_ref, wd_ref)
    y_ref[...] = _rms(h, gf_ref[...])


def _const_spec(shape):
    return pl.BlockSpec(shape, lambda *_: (0,) * len(shape), pipeline_mode=pl.Buffered(1))


def _row_spec(tm, width):
    return pl.BlockSpec((tm, width), lambda i: (i, 0))


def _dense_in(x, g1, wg, wu, wd, gm, wa, wb, tm):
    n = x.shape[0]
    consts = (g1, wg, wu, wd, gm, wa, wb)
    return pl.pallas_call(
        _dense_in_kernel,
        grid=(n // tm,),
        in_specs=[_row_spec(tm, D_MODEL)] + [_const_spec(c.shape) for c in consts],
        out_specs=[_row_spec(tm, D_MODEL), _row_spec(tm, PA_W), _row_spec(tm, PB_W)],
        out_shape=[jax.ShapeDtypeStruct((n, D_MODEL), F32),
                   jax.ShapeDtypeStruct((n, PA_W), F32),
                   jax.ShapeDtypeStruct((n, PB_W), F32)],
        compiler_params=pltpu.CompilerParams(dimension_semantics=("arbitrary",),
                                             vmem_limit_bytes=VMEM_LIMIT),
        name="dense_in",
    )(x, *consts)


def _dense_out(h, oa, ob, woa, wob, g2, wg, wu, wd, gf, tm):
    n = h.shape[0]
    consts = (woa, wob, g2, wg, wu, wd, gf)
    return pl.pallas_call(
        _dense_out_kernel,
        grid=(n // tm,),
        in_specs=[_row_spec(tm, D_MODEL), _row_spec(tm, W_A), _row_spec(tm, W_B)]
        + [_const_spec(c.shape) for c in consts],
        out_specs=_row_spec(tm, D_MODEL),
        out_shape=jax.ShapeDtypeStruct((n, D_MODEL), F32),
        compiler_params=pltpu.CompilerParams(dimension_semantics=("arbitrary",),
                                             vmem_limit_bytes=VMEM_LIMIT),
        name="dense_out",
    )(h, oa, ob, *consts)


RWKV_PREC = dict(cum="xr", seg="b", pair="b", invm="b", sread="b", akv="b", solve="b", inter="b",
                 state="x3")


def _rwkv_kernel(pa_ref, prev_ref, s0_ref, mu_ref, w0_ref, wdu_ref, a0_ref, wau_ref, wgu_ref,
                 kk_ref, ka_ref, rk_ref, lnw_ref, lnb_ref, o_ref, sout_ref, xbuf, s_scr,
                 *, chunk, nseq, shared_init):
    c = pl.program_id(1)
    C = chunk
    P = RWKV_PREC

    @pl.when(c == 0)
    def _():
        for i in range(nseq):
            j = 0 if shared_init else i
            xbuf[i, 0:SUBLANE, :] = prev_ref[j]
            s_scr[i] = s0_ref[j]

    tri = (_iota2((C, C), 0) >= _iota2((C, C), 1)).astype(F32)
    mid = C // 2 - 1

    def prep(i):
        x = pa_ref[i]
        full = jnp.concatenate([xbuf[i], x], axis=0)
        prev = pltpu.roll(full, 1, axis=0)[SUBLANE:, :]
        xm = x + (prev - x) * mu_ref[...]
        xbuf[i] = x[C - SUBLANE:, :]

        r = xm[:, 0:W_A]
        k = xm[:, W_A:2 * W_A]
        v = xm[:, 2 * W_A:3 * W_A]
        sm = xm[:, PA_SMALL:PA_SMALL + LANE]
        gd = xm[:, PA_GATE:PA_GATE + LANE]

        wl = w0_ref[...] + _dot(jnp.tanh(sm).astype(BF16), wdu_ref[...])
        lw = -jnp.exp(-_softplus(-wl) - 0.5)
        a = _sigmoid(a0_ref[...] + _dot(sm.astype(BF16), wau_ref[...]))
        g = _dot(_sigmoid(gd).astype(BF16), wgu_ref[...])
        kkr = k * kk_ref[...]
        k2 = k * (1.0 + (a - 1.0) * ka_ref[...])
        G = _mm(tri, lw, NN, P["cum"])
        return dict(r=r, v=v, a=a, g=g, kkr=kkr, k2=k2, G=G, Gx=G - lw, Gm=G[mid:mid + 1, :],
                    Gc=G[C - 1:C, :])

    seqs = [prep(i) for i in range(nseq)]

    C2 = 2 * C
    trow = _iota2((C, C2), 0)
    tcol = _iota2((C, C2), 1) % C
    incl = trow >= tcol
    strict = trow > tcol
    lane = _iota2((1, LANE), 1)
    m0 = (lane < N_A).astype(F32)
    m1 = (lane >= N_A).astype(F32)
    col2 = _iota2((1, C2), 1)
    c0 = (col2 < C).astype(F32)
    c1 = (col2 >= C).astype(F32)
    seg = ((_iota2((LANE, LANE), 0) // N_A) == (_iota2((LANE, LANE), 1) // N_A)).astype(F32)
    stack = lambda t: jnp.concatenate([t * m0.astype(t.dtype), t * m1.astype(t.dtype)], axis=0)
    expand = lambda t: jnp.concatenate([t * c0.astype(t.dtype), t * c1.astype(t.dtype)], axis=0)
    narrow = lambda t, site: t.astype(BF16) if P[site] == "b" else t
    merged = C2 % LANE == 0

    inst = [(i, p) for i in range(nseq) for p in range(H_A // 2)]
    n = range(len(inst))
    sls = [slice(p * LANE, (p + 1) * LANE) for _, p in inst]
    take = lambda name: [seqs[i][name][:, sls[j]] for j, (i, _) in enumerate(inst)]
    rs, vs_, k2s, kkrs, as_, gs = take("r"), take("v"), take("k2"), take("kkr"), take("a"), take("g")
    Gs, Gxs, Gms, Gcs = take("G"), take("Gx"), take("Gm"), take("Gc")
    Ss = [s_scr[i, p] for i, p in inst]

    ssq = [_mm(t * t, seg, NN, P["seg"]) for t in kkrs]
    kks = [t * lax.rsqrt(q + 1e-12) for t, q in zip(kkrs, ssq)]
    kkas = [kks[j] * as_[j] for j in n]
    inv_rel = [jnp.exp(Gms[j] - Gs[j]) for j in n]
    dec = [jnp.exp(Gcs[j] - Gs[j]) for j in n]
    lhs = [jnp.concatenate([kks[j] * jnp.exp(Gxs[j] - Gms[j]), rs[j] * jnp.exp(Gs[j] - Gms[j])], axis=0)
           for j in n]
    ais = [stack(narrow(kkas[j] * inv_rel[j], "pair")) for j in n]
    kis = [stack(narrow(k2s[j] * inv_rel[j], "pair")) for j in n]
    if merged:
        pm = [_mm(lhs[j], jnp.concatenate([ais[j], kis[j]], axis=0), NT, P["pair"]) for j in n]
        pas_, pks_ = [t[:, :C2] for t in pm], [t[:, C2:] for t in pm]
    else:
        pas_ = [_mm(lhs[j], ais[j], NT, P["pair"]) for j in n]
        pks_ = [_mm(lhs[j], kis[j], NT, P["pair"]) for j in n]
    a_aa = [jnp.where(strict, t[:C], 0.0) for t in pas_]
    a_ak = [jnp.where(strict, t[:C], 0.0) for t in pks_]
    a_ra = [jnp.where(incl, t[C:], 0.0) for t in pas_]
    a_rk = [jnp.where(incl, t[C:], 0.0) for t in pks_]
    ainv = _inv_unit_lower(a_aa, C, P["invm"], expand)

    assert P["akv"] == "b" and P["inter"] == "b" and P["state"] == "x3"
    split_stack = lambda t: tuple(stack(part) for part in _split(t))
    vst = [split_stack(t) for t in vs_]
    akv = [_mm(a_ak[j], vst[j][0], NN, "b") for j in n]
    sx = [_mm(jnp.concatenate([kks[j] * jnp.exp(Gxs[j]), rs[j] * jnp.exp(Gs[j])], axis=0), Ss[j], NT,
              P["sread"]) for j in n]
    us = [split_stack(-_mm(ainv[j], stack(narrow(sx[j][:C] + akv[j], "solve")), NN, P["solve"]))
          for j in n]
    uv = [tuple(jnp.concatenate([us[j][q], vst[j][q]], axis=0) for q in range(2)) for j in n]
    dks = [tuple(jnp.concatenate([a_, k_], axis=0)
                 for a_, k_ in zip(split_stack(kkas[j] * dec[j]), split_stack(k2s[j] * dec[j]))) for j in n]
    s_new = [Ss[j] * jnp.exp(Gcs[j])
             + (_dg(uv[j][0], dks[j][0], TN) + (_dg(uv[j][0], dks[j][1], TN) + _dg(uv[j][1], dks[j][0], TN)))
             for j in n]
    for j, (i, p) in enumerate(inst):
        s_scr[i, p] = s_new[j]
    if merged:
        inter = [_mm(jnp.concatenate([a_ra[j], a_rk[j]], axis=1), uv[j][0], NN, "b") for j in n]
    else:
        inter = [_mm(a_ra[j], us[j][0], NN, "b") + _mm(a_rk[j], vst[j][0], NN, "b") for j in n]
    Os = [sx[j][C:] + inter[j] for j in n]
    means = [_mm(t, seg, NN, P["seg"]) * (1.0 / N_A) for t in Os]
    dlts = [t - m for t, m in zip(Os, means)]
    vars_ = [_mm(t * t, seg, NN, P["seg"]) * (1.0 / N_A) for t in dlts]
    bonus = [_mm(rs[j] * k2s[j] * rk_ref[:, sls[j]], seg, NN, P["seg"]) * vs_[j] for j in n]
    for j, (i, p) in enumerate(inst):
        on = dlts[j] * lax.rsqrt(vars_[j] + LNX_EPS) * lnw_ref[:, sls[j]] + lnb_ref[:, sls[j]]
        o_ref[i, :, sls[j]] = (on + bonus[j]) * gs[j]

    @pl.when(c == pl.num_programs(1) - 1)
    def _():
        sout_ref[...] = s_scr[...]


def _seq_specs(nseq, shared_init, hist_w, state_shape):
    zeros = (0,) * len(state_shape)
    if shared_init:
        return [pl.BlockSpec((1, SUBLANE, hist_w), lambda b, c: (0, 0, 0)),
                pl.BlockSpec((1,) + state_shape, lambda b, c: (0,) + zeros)]
    return [pl.BlockSpec((nseq, SUBLANE, hist_w), lambda b, c: (b, 0, 0)),
            pl.BlockSpec((nseq,) + state_shape, lambda b, c: (b,) + zeros)]


def _rwkv(pa, prev8, s0, consts, chunk, nseq):
    B, L, _ = pa.shape
    shared_init = prev8.shape[0] == 1 and B > 1
    state_shape = (H_A // 2, LANE, LANE)
    return pl.pallas_call(
        functools.partial(_rwkv_kernel, chunk=chunk, nseq=nseq, shared_init=shared_init),
        grid=(B // nseq, L // chunk),
        in_specs=[pl.BlockSpec((nseq, chunk, PA_W), lambda b, c: (b, c, 0))]
        + _seq_specs(nseq, shared_init, PA_W, state_shape)
        + [pl.BlockSpec(t.shape, lambda b, c: (0, 0)) for t in consts],
        out_specs=[pl.BlockSpec((nseq, chunk, W_A), lambda b, c: (b, c, 0)),
                   pl.BlockSpec((nseq,) + state_shape, lambda b, c: (b, 0, 0, 0))],
        out_shape=[jax.ShapeDtypeStruct((B, L, W_A), F32),
                   jax.ShapeDtypeStruct((B,) + state_shape, F32)],
        scratch_shapes=[pltpu.VMEM((nseq, SUBLANE, PA_W), F32),
                        pltpu.VMEM((nseq,) + state_shape, F32)],
        compiler_params=pltpu.CompilerParams(dimension_semantics=("arbitrary", "arbitrary"),
                                             vmem_limit_bytes=VMEM_LIMIT),
        name="rwkv_chunk%d" % chunk,
    )(pa, prev8, s0, *consts)


GDN_PREC = dict(cum="xr", tr="xr", kk="b", invm="b", wu="b", qk="b", sread="b", qkv="b", state="b")


def _gdn_kernel(pb_ref, ga_ref, hist_ref, s0_ref, cw_ref, alog_ref, dtb_ref, nw_ref, o_ref, sout_ref,
                xbuf, s_scr, *, chunk, nseq, shared_init):
    c = pl.program_id(1)
    C = chunk
    P = GDN_PREC

    @pl.when(c == 0)
    def _():
        for i in range(nseq):
            j = 0 if shared_init else i
            xbuf[i, 0:SUBLANE, :] = hist_ref[j]
            s_scr[i] = s0_ref[j]

    row = _iota2((C, C), 0)
    col = _iota2((C, C), 1)
    incl = row >= col
    strict = row > col
    tri = incl.astype(F32)
    sel = (_iota2((SUBLANE, LANE), 0) + SMALL_GDN_A == _iota2((SUBLANE, LANE), 1)).astype(F32)

    def prep(i):
        x = pb_ref[i]
        qkv = x[:, 0:3 * W_B]
        full = jnp.concatenate([xbuf[i], qkv], axis=0)
        conv = qkv * cw_ref[CONV_W - 1:CONV_W, :]
        for j in range(1, CONV_W):
            conv = conv + pltpu.roll(full, j, axis=0)[SUBLANE:, :] * cw_ref[CONV_W - 1 - j:CONV_W - j, :]
        xbuf[i] = qkv[C - SUBLANE:, :]
        cs = conv * _sigmoid(conv)
        gates = ga_ref[i]
        glog = -jnp.exp(alog_ref[...]) * _softplus(gates + dtb_ref[...])
        beta = _sigmoid(gates)
        G = _mm(tri, glog, NN, P["cum"])
        Gt = _mm(sel, G, NT, P["tr"])
        return dict(cs=cs, z=x[:, PB_Z:PB_Z + W_B], beta=beta, G=G, Gt=Gt)

    seqs = [prep(i) for i in range(nseq)]

    inst = [(i, h) for i in range(nseq) for h in range(H_B)]
    n = range(len(inst))
    l2n = lambda t, c: t * (lax.rsqrt(jnp.sum(t * t, axis=-1, keepdims=True) + 1e-12) * c)
    head = lambda i, h, part: seqs[i]["cs"][:, part * W_B + h * LANE:part * W_B + (h + 1) * LANE]
    qs = [l2n(head(i, h, 0), DK_B ** -0.5) for i, h in inst]
    ks = [l2n(head(i, h, 1), 1.0) for i, h in inst]
    vs = [head(i, h, 2) for i, h in inst]
    gcols = [seqs[i]["G"][:, SMALL_GDN_A + h:SMALL_GDN_A + h + 1] for i, h in inst]
    grows = [seqs[i]["Gt"][h:h + 1, :] for i, h in inst]
    bcols = [seqs[i]["beta"][:, SMALL_GDN_B + h:SMALL_GDN_B + h + 1] for i, h in inst]
    Ss = [s_scr[i, h] for i, h in inst]

    dmats = [jnp.where(incl, jnp.exp(jnp.where(incl, gcols[j] - grows[j], 0.0)), 0.0) for j in n]
    kbs = [ks[j] * bcols[j] for j in n]
    lows = [jnp.where(strict, _mm(kbs[j], ks[j], NT, P["kk"]) * dmats[j], 0.0) for j in n]
    ainv = _inv_unit_lower(lows, C, P["invm"])
    egs = [jnp.exp(t) for t in gcols]
    wus = [_mm(ainv[j], jnp.concatenate([kbs[j] * egs[j], vs[j] * bcols[j]], axis=1), NN, P["wu"])
           for j in n]
    qks = [_mm(qs[j], ks[j], NT, P["qk"]) * dmats[j] for j in n]
    srs = [_mm(jnp.concatenate([wus[j][:, :DK_B], qs[j] * egs[j]], axis=0), Ss[j], NN, P["sread"])
           for j in n]
    v_new = [wus[j][:, DK_B:] - srs[j][:C] for j in n]
    glast = [t[C - 1:C, :] for t in gcols]
    s_new = [Ss[j] * jnp.exp(glast[j])
             + _mm(ks[j] * jnp.exp(glast[j] - gcols[j]), v_new[j], TN, P["state"]) for j in n]
    for j, (i, h) in enumerate(inst):
        s_scr[i, h] = s_new[j]
    os_ = [srs[j][C:] + _mm(qks[j], v_new[j], NN, P["qkv"]) for j in n]
    for j, (i, h) in enumerate(inst):
        o = os_[j]
        o = o * lax.rsqrt(jnp.mean(o * o, axis=-1, keepdims=True) + NORM_EPS) * nw_ref[...]
        zh = seqs[i]["z"][:, h * LANE:(h + 1) * LANE]
        o_ref[i, :, h * LANE:(h + 1) * LANE] = o * (zh * _sigmoid(zh))

    @pl.when(c == pl.num_programs(1) - 1)
    def _():
        sout_ref[...] = s_scr[...]


def _gdn(pb, pa, hist8, s0, consts, chunk, nseq):
    B, L, _ = pb.shape
    shared_init = hist8.shape[0] == 1 and B > 1
    state_shape = (H_B, DK_B, DV_B)
    return pl.pallas_call(
        functools.partial(_gdn_kernel, chunk=chunk, nseq=nseq, shared_init=shared_init),
        grid=(B // nseq, L // chunk),
        in_specs=[pl.BlockSpec((nseq, chunk, PB_W), lambda b, c: (b, c, 0)),
                  pl.BlockSpec((nseq, chunk, LANE), lambda b, c: (b, c, PA_SMALL // LANE))]
        + _seq_specs(nseq, shared_init, 3 * W_B, state_shape)
        + [pl.BlockSpec(t.shape, lambda b, c: (0, 0)) for t in consts],
        out_specs=[pl.BlockSpec((nseq, chunk, W_B), lambda b, c: (b, c, 0)),
                   pl.BlockSpec((nseq,) + state_shape, lambda b, c: (b, 0, 0, 0))],
        out_shape=[jax.ShapeDtypeStruct((B, L, W_B), F32),
                   jax.ShapeDtypeStruct((B,) + state_shape, F32)],
        scratch_shapes=[pltpu.VMEM((nseq, SUBLANE, 3 * W_B), F32),
                        pltpu.VMEM((nseq,) + state_shape, F32)],
        compiler_params=pltpu.CompilerParams(dimension_semantics=("arbitrary", "arbitrary"),
                                             vmem_limit_bytes=VMEM_LIMIT),
        name="gdn_chunk%d" % chunk,
    )(pb, pa, hist8, s0, *consts)


def _row_mask(nrow, i):
    return _iota2((nrow, 1), 0) == i


STEP_VECS = ("w", "kk", "kka", "k2", "v", "r", "g", "bon")


def _rwkv_step_prep_kernel(pa_ref, prev_ref, mu_ref, w0_ref, wdu_ref, a0_ref, wau_ref, wgu_ref,
                           kk_ref, ka_ref, rk_ref, vec_ref):
    x = pa_ref[...]
    xm = x + (prev_ref[...] - x) * mu_ref[...]
    r = xm[:, 0:W_A]
    k = xm[:, W_A:2 * W_A]
    v = xm[:, 2 * W_A:3 * W_A]
    sm = xm[:, PA_SMALL:PA_SMALL + LANE]
    gd = xm[:, PA_GATE:PA_GATE + LANE]
    wl = w0_ref[...] + _dot(jnp.tanh(sm).astype(BF16), wdu_ref[...])
    w = jnp.exp(-jnp.exp(-_softplus(-wl) - 0.5))
    a = _sigmoid(a0_ref[...] + _dot(sm.astype(BF16), wau_ref[...]))
    g = _dot(_sigmoid(gd).astype(BF16), wgu_ref[...])
    kkr = k * kk_ref[...]
    k2 = k * (1.0 + (a - 1.0) * ka_ref[...])
    bon = r * k2 * rk_ref[...]
    seg = ((_iota2((LANE, LANE), 0) // N_A) == (_iota2((LANE, LANE), 1) // N_A)).astype(F32)
    ssq = jnp.concatenate([_mm(kkr[:, p * LANE:(p + 1) * LANE] ** 2, seg, NN, RWKV_PREC["seg"])
                           for p in range(H_A // 2)], axis=-1)
    kk = kkr * lax.rsqrt(ssq + 1e-12)
    vecs = dict(w=w, kk=kk, kka=kk * a, k2=k2, v=v, r=r, g=g, bon=bon)
    for j, name in enumerate(STEP_VECS):
        vec_ref[j] = vecs[name].T


def _rwkv_step_state_kernel(vec_ref, s_ref, lnw_ref, lnb_ref, o_ref, sout_ref, o_scr):
    w, kk, kka, k2, v, r, g, bon = [vec_ref[j] for j in range(len(STEP_VECS))]
    for i in range(N_A):
        S = s_ref[0, i]
        sa = jnp.sum(S * kk, axis=0, keepdims=True)
        s_new = S * w - sa * kka + v[i:i + 1, :] * k2
        sout_ref[0, i] = s_new
        o_scr[i:i + 1, :] = jnp.sum(s_new * r, axis=0, keepdims=True)
    o = o_scr[...]
    mean = jnp.mean(o, axis=0, keepdims=True)
    dlt = o - mean
    var = jnp.mean(dlt * dlt, axis=0, keepdims=True)
    on = dlt * lax.rsqrt(var + LNX_EPS) * lnw_ref[...] + lnb_ref[...]
    o_ref[...] = (on + jnp.sum(bon, axis=0, keepdims=True) * v) * g


def _gdn_step_kernel(pb_ref, ga_ref, hist_ref, s_ref, cw_ref, alog_ref, dtb_ref, nw_ref, o_ref, sout_ref):
    ns = pb_ref.shape[0]
    x = pb_ref[...]
    conv = x[:, 0:3 * W_B] * cw_ref[CONV_W - 1:CONV_W, :]
    for j in range(CONV_W - 1):
        conv = conv + hist_ref[j] * cw_ref[j:j + 1, :]
    cs = conv * _sigmoid(conv)
    z = x[:, PB_Z:PB_Z + W_B]
    gates = ga_ref[...]
    eg = jnp.exp(-jnp.exp(alog_ref[...]) * _softplus(gates + dtb_ref[...]))
    beta = _sigmoid(gates)

    l2n = lambda t: t * lax.rsqrt(jnp.sum(t * t, axis=-1, keepdims=True) + 1e-12)
    for h in range(H_B):
        hs = slice(h * LANE, (h + 1) * LANE)
        q_h = l2n(cs[:, hs]) * (DK_B ** -0.5)
        k_h = l2n(cs[:, W_B + h * LANE:W_B + (h + 1) * LANE])
        v_h = cs[:, 2 * W_B + h * LANE:2 * W_B + (h + 1) * LANE]
        eg_h = eg[:, SMALL_GDN_A + h:SMALL_GDN_A + h + 1]
        b_h = beta[:, SMALL_GDN_B + h:SMALL_GDN_B + h + 1]
        seq = range(ns)
        Ss = [s_ref[i, h] for i in seq]
        kss = [_mm(k_h, Ss[i], NN, "xr") for i in seq]
        v_new = [b_h * (v_h - eg_h * kss[i]) for i in seq]
        s_new = [Ss[i] * eg_h[i:i + 1, :] + _mm(jnp.where(_row_mask(ns, i), k_h, 0.0), v_new[i], TN, "b")
                 for i in seq]
        for i in seq:
            sout_ref[i, h] = s_new[i]
        o_all = [_mm(q_h, s_new[i], NN, "b") for i in seq]
        o_h = jnp.zeros((ns, DV_B), F32)
        for i in seq:
            o_h = jnp.where(_row_mask(ns, i), o_all[i], o_h)
        o_h = o_h * lax.rsqrt(jnp.mean(o_h * o_h, axis=-1, keepdims=True) + NORM_EPS) * nw_ref[...]
        zh = z[:, hs]
        o_ref[:, hs] = o_h * (zh * _sigmoid(zh))


def _step_const_specs(ts):
    return [pl.BlockSpec(t.shape, lambda b: (0,) * t.ndim) for t in ts]


def _rwkv_step(pa, prev, s_t, consts):
    B = s_t.shape[-1]
    prep_consts = consts[:-2]
    lnw_col, lnb_col = [c.reshape(W_A, 1) for c in consts[-2:]]
    vecs = pl.pallas_call(
        _rwkv_step_prep_kernel,
        grid=(1,),
        in_specs=[pl.BlockSpec((B, PA_W), lambda b: (0, 0)),
                  pl.BlockSpec((B, PA_W), lambda b: (0, 0))] + _step_const_specs(prep_consts),
        out_specs=pl.BlockSpec((len(STEP_VECS), W_A, B), lambda b: (0, 0, 0)),
        out_shape=jax.ShapeDtypeStruct((len(STEP_VECS), W_A, B), F32),
        compiler_params=pltpu.CompilerParams(dimension_semantics=("arbitrary",),
                                             vmem_limit_bytes=VMEM_LIMIT),
        name="rwkv_step_prep",
    )(pa, prev, *prep_consts)
    state_spec = pl.BlockSpec((1, N_A, N_A, B), lambda h: (h, 0, 0, 0))
    return pl.pallas_call(
        _rwkv_step_state_kernel,
        grid=(H_A,),
        in_specs=[pl.BlockSpec((len(STEP_VECS), N_A, B), lambda h: (0, h, 0)), state_spec,
                  pl.BlockSpec((N_A, 1), lambda h: (h, 0)), pl.BlockSpec((N_A, 1), lambda h: (h, 0))],
        out_specs=[pl.BlockSpec((N_A, B), lambda h: (h, 0)), state_spec],
        out_shape=[jax.ShapeDtypeStruct((W_A, B), F32), jax.ShapeDtypeStruct(s_t.shape, F32)],
        scratch_shapes=[pltpu.VMEM((N_A, B), F32)],
        compiler_params=pltpu.CompilerParams(dimension_semantics=("arbitrary",),
                                             vmem_limit_bytes=VMEM_LIMIT),
        name="rwkv_step_state",
    )(vecs, s_t, lnw_col, lnb_col)


def _gdn_step(pb, pa, hist, s, consts, ns):
    B = s.shape[0]
    return pl.pallas_call(
        _gdn_step_kernel,
        grid=(B // ns,),
        in_specs=[pl.BlockSpec((ns, PB_W), lambda b: (b, 0)),
                  pl.BlockSpec((ns, LANE), lambda b: (b, PA_SMALL // LANE)),
                  pl.BlockSpec((CONV_W - 1, ns, 3 * W_B), lambda b: (0, b, 0)),
                  pl.BlockSpec((ns, H_B, DK_B, DV_B), lambda b: (b, 0, 0, 0))] + _step_const_specs(consts),
        out_specs=[pl.BlockSpec((ns, W_B), lambda b: (b, 0)),
                   pl.BlockSpec((ns, H_B, DK_B, DV_B), lambda b: (b, 0, 0, 0))],
        out_shape=[jax.ShapeDtypeStruct((B, W_B), F32), jax.ShapeDtypeStruct(s.shape, F32)],
        compiler_params=pltpu.CompilerParams(dimension_semantics=("arbitrary",),
                                             vmem_limit_bytes=VMEM_LIMIT),
        name="gdn_step",
    )(pb, pa, hist, s, *consts)


def _pad_cols(t, width):
    return jnp.pad(t, [(0, 0)] * (t.ndim - 1) + [(0, width - t.shape[-1])])


def _pa_layout(t, gdn_gates=None):
    main = t[..., :3 * W_A]
    small = t[..., 3 * W_A:3 * W_A + D_DECAY + D_AAA]
    gate = t[..., 3 * W_A + D_DECAY + D_AAA:]
    if gdn_gates is not None:
        small = jnp.concatenate([small, gdn_gates], axis=-1)
    return jnp.concatenate([main, _pad_cols(small, LANE), _pad_cols(gate, LANE)], axis=-1)


def _pa_unlayout(t):
    return jnp.concatenate([t[..., :3 * W_A], t[..., PA_SMALL:PA_SMALL + D_DECAY + D_AAA],
                            t[..., PA_GATE:PA_GATE + D_GATE]], axis=-1)


def _small_rows(t, first_row):
    return jnp.pad(t, [(first_row, LANE - first_row - t.shape[0]), (0, 0)])


def _small_lanes(t, first_lane):
    return jnp.pad(t, [(0, 0), (first_lane, LANE - first_lane - t.shape[1])])


def _pair_unblock(s):
    B = s.shape[0]
    return jnp.stack([s[:, :, :N_A, :N_A], s[:, :, N_A:, N_A:]], axis=2).reshape(B, H_A, N_A, N_A)


def _history_rows(rows):
    B, n, w = rows.shape
    return jnp.concatenate([jnp.zeros((B, SUBLANE - n, w), F32), rows], axis=1)


def kernel(x_prompt, x_sample, state_rwkv, state_shift, state_gdn, state_conv, meta_tokens,
           g_ffn1, w_gate1, w_up1, w_down1, g_mix, w_in, mu_shift, w0, w_decay_up, a0, w_a_up,
           w_g_up, k_k, k_a, r_k, lnx_w, lnx_b, conv_w, a_log, dt_bias, gdn_norm_w, w_out,
           g_ffn2, w_gate2, w_up2, w_down2, g_final):
    assert g_ffn1.shape[0] == 1, "single trunk layer"
    bp, sp, _ = x_prompt.shape
    bs = x_sample.shape[0]
    assert x_sample.shape[1] == 1 and sp % CHUNK_RWKV == 0 and sp % CHUNK_GDN == 0
    assert (bp * sp) % TM_DENSE == 0 and (bp * sp) % TM_DENSE_OUT == 0
    assert bp % SEQS_PER_STEP == 0 and bs % DEC_SEQS_PER_STEP == 0
    row = lambda t: t.reshape(1, -1).astype(F32)

    ffn1 = (w_gate1[0].astype(BF16), w_up1[0].astype(BF16), w_down1[0].astype(BF16))
    ffn2 = (w_gate2[0].astype(BF16), w_up2[0].astype(BF16), w_down2[0].astype(BF16))
    win_a = _pa_layout(w_in[0][:, :N_A_IN], w_in[0][:, N_A_IN + PB_W:]).astype(BF16)
    win_b = w_in[0][:, N_A_IN:N_A_IN + PB_W].astype(BF16)
    wo_a = w_out[0][:W_A].astype(BF16)
    wo_b = w_out[0][W_A:].astype(BF16)
    dense_in_consts = (row(g_ffn1[0]), *ffn1, row(g_mix[0]), win_a, win_b)
    dense_out_consts = (wo_a, wo_b, row(g_ffn2[0]), *ffn2, row(g_final))
    rwkv_consts = (_pa_layout(row(mu_shift[0])), row(w0[0]),
                   _small_rows(w_decay_up[0], 0).astype(BF16), row(a0[0]),
                   _small_rows(w_a_up[0], SMALL_AAA).astype(BF16), _small_rows(w_g_up[0], 0).astype(BF16),
                   row(k_k[0]), row(k_a[0]), row(r_k[0]), row(lnx_w[0]), row(lnx_b[0]))
    gdn_consts = (conv_w[0].astype(F32), _small_lanes(row(a_log[0]), SMALL_GDN_A),
                  _small_lanes(row(dt_bias[0]), SMALL_GDN_A), row(gdn_norm_w[0]))

    xs = jnp.concatenate([x_sample[:, 0, :].astype(F32), meta_tokens.astype(F32)], axis=0)
    hs, pas, pbs = _dense_in(xs, *dense_in_consts, tm=xs.shape[0])

    pa_meta = pas[bs:][None]
    pb_meta = pbs[bs:][None]
    _, rw_meta = _rwkv(pa_meta, jnp.zeros((1, SUBLANE, PA_W), F32),
                       jnp.zeros((1, H_A // 2, LANE, LANE), F32), rwkv_consts, N_META, 1)
    _, gd_meta = _gdn(pb_meta, pa_meta, jnp.zeros((1, SUBLANE, 3 * W_B), F32),
                      jnp.zeros((1, H_B, DK_B, DV_B), F32), gdn_consts, N_META, 1)

    hp, pap, pbp = _dense_in(x_prompt.reshape(bp * sp, D_MODEL).astype(F32), *dense_in_consts,
                             tm=TM_DENSE)
    pap3 = pap.reshape(bp, sp, PA_W)
    pbp3 = pbp.reshape(bp, sp, PB_W)
    oa_p, rw_p = _rwkv(pap3, _history_rows(pa_meta[:, -1:, :]), rw_meta, rwkv_consts,
                       CHUNK_RWKV, SEQS_PER_STEP)
    ob_p, gd_p = _gdn(pbp3, pap3, _history_rows(pb_meta[:, -(CONV_W - 1):, :3 * W_B]), gd_meta, gdn_consts,
                      CHUNK_GDN, SEQS_PER_STEP)
    y_p = _dense_out(hp, oa_p.reshape(bp * sp, W_A), ob_p.reshape(bp * sp, W_B), *dense_out_consts,
                     tm=TM_DENSE_OUT)

    oa_t, rw_t = _rwkv_step(pas, _pa_layout(state_shift[0].astype(F32)),
                            jnp.transpose(state_rwkv[0].astype(F32), (1, 2, 3, 0)), rwkv_consts)
    oa_s = oa_t.T
    rw_s = jnp.transpose(rw_t, (3, 0, 1, 2))
    ob_s, gd_s = _gdn_step(pbs, pas, jnp.swapaxes(state_conv[0].astype(F32), 0, 1), state_gdn[0].astype(F32),
                           gdn_consts, DEC_SEQS_PER_STEP)
    y_s = _dense_out(hs[:bs], oa_s, ob_s, *dense_out_consts, tm=bs)

    new_conv_s = jnp.concatenate([state_conv[0].astype(F32)[:, 1:, :], pbs[:bs, None, :3 * W_B]], axis=1)
    return (y_p.reshape(bp, sp, D_MODEL).astype(x_prompt.dtype),
            y_s.reshape(bs, 1, D_MODEL).astype(x_sample.dtype),
            _pair_unblock(rw_p)[None],
            _pa_unlayout(pap3[:, -1, :])[None],
            gd_p[None],
            pbp3[:, -(CONV_W - 1):, :3 * W_B][None],
            rw_s[None],
            _pa_unlayout(pas[:bs])[None],
            gd_s[None],
            new_conv_s[None])
```

```python
import functools

import jax
import jax.numpy as jnp
from jax import lax
from jax.experimental import pallas as pl
from jax.experimental.pallas import tpu as pltpu

F32 = jnp.float32
BF16 = jnp.bfloat16

LANE = 128
SUBLANE = 8
VMEM_LIMIT = 56 * 1024 * 1024

D_MODEL = 1024
D_FF = 2816
N_META = 16
H_A, N_A = 8, 64
W_A = H_A * N_A
D_DECAY, D_AAA, D_GATE = 32, 32, 96
N_A_IN = 3 * W_A + D_DECAY + D_AAA + D_GATE
H_B, DK_B, DV_B = 4, 128, 128
W_B = H_B * DV_B
CONV_W = 4
N_B_IN = 4 * W_B + 2 * H_B
LNX_EPS = 64e-5
NORM_EPS = 1e-6

PA_W = 3 * W_A + 2 * LANE
PA_SMALL = 3 * W_A
PA_GATE = PA_SMALL + LANE
SMALL_AAA = D_DECAY
SMALL_GDN_A = D_DECAY + D_AAA
SMALL_GDN_B = SMALL_GDN_A + H_B
PB_W = 4 * W_B
PB_Z = 3 * W_B

TM_DENSE = 512
TM_DENSE_OUT = 512
CHUNK_RWKV = 64
CHUNK_GDN = 128
SEQS_PER_STEP = 4
DEC_SEQS_PER_STEP = SUBLANE

NN = (((1,), (0,)), ((), ()))
NT = (((1,), (1,)), ((), ()))
TN = (((0,), (0,)), ((), ()))


def _dot(a, b):
    return jnp.dot(a, b, preferred_element_type=F32)


def _sigmoid(x):
    return 1.0 / (1.0 + jnp.exp(-x))


def _softplus(x):
    return jnp.maximum(x, 0.0) + jnp.log(1.0 + jnp.exp(-jnp.abs(x)))


def _rms(x, g):
    return x * lax.rsqrt(jnp.mean(x * x, axis=-1, keepdims=True) + NORM_EPS) * g


def _iota2(shape, dim):
    return lax.broadcasted_iota(jnp.int32, shape, dim)


def _split(x):
    hi = x.astype(BF16)
    return hi, (x - hi.astype(F32)).astype(BF16)


def _dg(a, b, dims):
    return lax.dot_general(a, b, dims, preferred_element_type=F32)


def _mm(a, b, dims=NN, mode="b"):
    if mode == "b":
        return _dg(a.astype(BF16), b.astype(BF16), dims)
    if mode == "xl":
        ah, al = _split(a)
        bh = b.astype(BF16)
        return _dg(ah, bh, dims) + _dg(al, bh, dims)
    if mode == "xr":
        ah = a.astype(BF16)
        bh, bl = _split(b)
        return _dg(ah, bh, dims) + _dg(ah, bl, dims)
    assert mode == "x3"
    ah, al = _split(a)
    bh, bl = _split(b)
    return _dg(ah, bh, dims) + (_dg(ah, bl, dims) + _dg(al, bh, dims))


def _inv_unit_lower(lows, nil, merge_mode, expand=None):
    shape = lows[0].shape
    row = _iota2(shape, 0)
    col = _iota2(shape, 1) % nil
    same_block = lambda s: (row // s) == (col // s)
    eye = (row == col).astype(F32)
    if expand is None:
        expand = lambda t: t
    narrow = (lambda t: t.astype(BF16)) if merge_mode == "b" else (lambda t: t)
    s = 2
    in_base = same_block(s)
    invs = [eye - jnp.where(in_base, low, 0.0) for low in lows]
    while s < nil:
        newly = jnp.logical_and(same_block(2 * s), jnp.logical_not(same_block(s)))
        ts = [_mm(inv, expand(narrow(jnp.where(newly, low, 0.0))), NN, merge_mode)
              for inv, low in zip(invs, lows)]
        invs = [inv - _mm(t, expand(narrow(inv)), NN, merge_mode) for inv, t in zip(invs, ts)]
        s *= 2
    return invs


FF_SPLIT = (0, 1536, D_FF)


def _swiglu(n, wg_ref, wu_ref, wd_ref):
    out = None
    for lo, hi in zip(FF_SPLIT[:-1], FF_SPLIT[1:]):
        gate = _dot(n, wg_ref[:, lo:hi])
        up = _dot(n, wu_ref[:, lo:hi])
        act = (gate * _sigmoid(gate) * up).astype(BF16)
        part = _dot(act, wd_ref[lo:hi, :])
        out = part if out is None else out + part
    return out


def _dense_in_kernel(x_ref, g1_ref, wg_ref, wu_ref, wd_ref, gm_ref, wa_ref, wb_ref,
                     h_ref, pa_ref, pb_ref):
    x = x_ref[...]
    h = x + 0.5 * _swiglu(_rms(x, g1_ref[...]).astype(BF16), wg_ref, wu_ref, wd_ref)
    h_ref[...] = h
    n = _rms(h, gm_ref[...]).astype(BF16)
    pa_ref[...] = _dot(n, wa_ref[...])
    pb_ref[...] = _dot(n, wb_ref[...])


def _dense_out_kernel(h_ref, oa_ref, ob_ref, woa_ref, wob_ref, g2_ref, wg_ref, wu_ref, wd_ref,
                      gf_ref, y_ref):
    h = (h_ref[...] + _dot(oa_ref[...].astype(BF16), woa_ref[...])
         + _dot(ob_ref[...].astype(BF16), wob_ref[...]))
    h = h + 0.5 * _swiglu(_rms(h, g2_ref[...]).astype(BF16), wg_ref, wu_ref, wd_ref)
    y_ref[...] = _rms(h, gf_ref[...])


def _const_spec(shape):
    return pl.BlockSpec(shape, lambda *_: (0,) * len(shape), pipeline_mode=pl.Buffered(1))


def _row_spec(tm, width):
    return pl.BlockSpec((tm, width), lambda i: (i, 0))


def _dense_in(x, g1, wg, wu, wd, gm, wa, wb, tm):
    n = x.shape[0]
    consts = (g1, wg, wu, wd, gm, wa, wb)
    return pl.pallas_call(
        _dense_in_kernel,
        grid=(n // tm,),
        in_specs=[_row_spec(tm, D_MODEL)] + [_const_spec(c.shape) for c in consts],
        out_specs=[_row_spec(tm, D_MODEL), _row_spec(tm, PA_W), _row_spec(tm, PB_W)],
        out_shape=[jax.ShapeDtypeStruct((n, D_MODEL), F32),
                   jax.ShapeDtypeStruct((n, PA_W), F32),
                   jax.ShapeDtypeStruct((n, PB_W), F32)],
        compiler_params=pltpu.CompilerParams(dimension_semantics=("arbitrary",),
                                             vmem_limit_bytes=VMEM_LIMIT),
        name="dense_in",
    )(x, *consts)


def _dense_out(h, oa, ob, woa, wob, g2, wg, wu, wd, gf, tm):
    n = h.shape[0]
    consts = (woa, wob, g2, wg, wu, wd, gf)
    return pl.pallas_call(
        _dense_out_kernel,
        grid=(n // tm,),
        in_specs=[_row_spec(tm, D_MODEL), _row_spec(tm, W_A), _row_spec(tm, W_B)]
        + [_const_spec(c.shape) for c in consts],
        out_specs=_row_spec(tm, D_MODEL),
        out_shape=jax.ShapeDtypeStruct((n, D_MODEL), F32),
        compiler_params=pltpu.CompilerParams(dimension_semantics=("arbitrary",),
                                             vmem_limit_bytes=VMEM_LIMIT),
        name="dense_out",
    )(h, oa, ob, *consts)


RWKV_PREC = dict(cum="xr", seg="b", pair="b", invm="b", sread="b", akv="b", solve="b", inter="b",
                 state="x3")


def _rwkv_kernel(pa_ref, prev_ref, s0_ref, mu_ref, w0_ref, wdu_ref, a0_ref, wau_ref, wgu_ref,
                 kk_ref, ka_ref, rk_ref, lnw_ref, lnb_ref, o_ref, sout_ref, xbuf, s_scr,
                 *, chunk, nseq, shared_init):
    c = pl.program_id(1)
    C = chunk
    P = RWKV_PREC

    @pl.when(c == 0)
    def _():
        for i in range(nseq):
            j = 0 if shared_init else i
            xbuf[i, 0:SUBLANE, :] = prev_ref[j]
            s_scr[i] = s0_ref[j]

    tri = (_iota2((C, C), 0) >= _iota2((C, C), 1)).astype(F32)
    mid = C // 2 - 1

    def prep(i):
        x = pa_ref[i]
        full = jnp.concatenate([xbuf[i], x], axis=0)
        prev = pltpu.roll(full, 1, axis=0)[SUBLANE:, :]
        xm = x + (prev - x) * mu_ref[...]
        xbuf[i] = x[C - SUBLANE:, :]

        r = xm[:, 0:W_A]
        k = xm[:, W_A:2 * W_A]
        v = xm[:, 2 * W_A:3 * W_A]
        sm = xm[:, PA_SMALL:PA_SMALL + LANE]
        gd = xm[:, PA_GATE:PA_GATE + LANE]

        wl = w0_ref[...] + _dot(jnp.tanh(sm).astype(BF16), wdu_ref[...])
        lw = -jnp.exp(-_softplus(-wl) - 0.5)
        a = _sigmoid(a0_ref[...] + _dot(sm.astype(BF16), wau_ref[...]))
        g = _dot(_sigmoid(gd).astype(BF16), wgu_ref[...])
        kkr = k * kk_ref[...]
        k2 = k * (1.0 + (a - 1.0) * ka_ref[...])
        G = _mm(tri, lw, NN, P["cum"])
        return dict(r=r, v=v, a=a, g=g, kkr=kkr, k2=k2, G=G, Gx=G - lw, Gm=G[mid:mid + 1, :],
                    Gc=G[C - 1:C, :])

    seqs = [prep(i) for i in range(nseq)]

    C2 = 2 * C
    trow = _iota2((C, C2), 0)
    tcol = _iota2((C, C2), 1) % C
    incl = trow >= tcol
    strict = trow > tcol
    lane = _iota2((1, LANE), 1)
    m0 = (lane < N_A).astype(F32)
    m1 = (lane >= N_A).astype(F32)
    col2 = _iota2((1, C2), 1)
    c0 = (col2 < C).astype(F32)
    c1 = (col2 >= C).astype(F32)
    seg = ((_iota2((LANE, LANE), 0) // N_A) == (_iota2((LANE, LANE), 1) // N_A)).astype(F32)
    stack = lambda t: jnp.concatenate([t * m0.astype(t.dtype), t * m1.astype(t.dtype)], axis=0)
    expand = lambda t: jnp.concatenate([t * c0.astype(t.dtype), t * c1.astype(t.dtype)], axis=0)
    narrow = lambda t, site: t.astype(BF16) if P[site] == "b" else t
    merged = C2 % LANE == 0

    inst = [(i, p) for i in range(nseq) for p in range(H_A // 2)]
    n = range(len(inst))
    sls = [slice(p * LANE, (p + 1) * LANE) for _, p in inst]
    take = lambda name: [seqs[i][name][:, sls[j]] for j, (i, _) in enumerate(inst)]
    rs, vs_, k2s, kkrs, as_, gs = take("r"), take("v"), take("k2"), take("kkr"), take("a"), take("g")
    Gs, Gxs, Gms, Gcs = take("G"), take("Gx"), take("Gm"), take("Gc")
    Ss = [s_scr[i, p] for i, p in inst]

    ssq = [_mm(t * t, seg, NN, P["seg"]) for t in kkrs]
    kks = [t * lax.rsqrt(q + 1e-12) for t, q in zip(kkrs, ssq)]
    kkas = [kks[j] * as_[j] for j in n]
    inv_rel = [jnp.exp(Gms[j] - Gs[j]) for j in n]
    dec = [jnp.exp(Gcs[j] - Gs[j]) for j in n]
    lhs = [jnp.concatenate([kks[j] * jnp.exp(Gxs[j] - Gms[j]), rs[j] * jnp.exp(Gs[j] - Gms[j])], axis=0)
           for j in n]
    ais = [stack(narrow(kkas[j] * inv_rel[j], "pair")) for j in n]
    kis = [stack(narrow(k2s[j] * inv_rel[j], "pair")) for j in n]
    if merged:
        pm = [_mm(lhs[j], jnp.concatenate([ais[j], kis[j]], axis=0), NT, P["pair"]) for j in n]
        pas_, pks_ = [t[:, :C2] for t in pm], [t[:, C2:] for t in pm]
    else:
        pas_ = [_mm(lhs[j], ais[j], NT, P["pair"]) for j in n]
        pks_ = [_mm(lhs[j], kis[j], NT, P["pair"]) for j in n]
    a_aa = [jnp.where(strict, t[:C], 0.0) for t in pas_]
    a_ak = [jnp.where(strict, t[:C], 0.0) for t in pks_]
    a_ra = [jnp.where(incl, t[C:], 0.0) for t in pas_]
    a_rk = [jnp.where(incl, t[C:], 0.0) for t in pks_]
    ainv = _inv_unit_lower(a_aa, C, P["invm"], expand)

    assert P["akv"] == "b" and P["inter"] == "b" and P["state"] == "x3"
    v_hl = [_split(t) for t in vs_]
    vst = [stack(hl[0]) for hl in v_hl]
    akv = [_mm(a_ak[j], vst[j], NN, "b") for j in n]
    sx = [_mm(jnp.concatenate([kks[j] * jnp.exp(Gxs[j]), rs[j] * jnp.exp(Gs[j])], axis=0), Ss[j], NT,
              P["sread"]) for j in n]
    u_hl = [_split(-_mm(ainv[j], stack(narrow(sx[j][:C] + akv[j], "solve")), NN, P["solve"])) for j in n]
    ust = [stack(hl[0]) for hl in u_hl]
    cat2 = lambda a, b, q: jnp.concatenate([a[q], b[q]], axis=0)
    ad_hl = [_split(kkas[j] * dec[j]) for j in n]
    kd_hl = [_split(k2s[j] * dec[j]) for j in n]
    s_new = [Ss[j] * jnp.exp(Gcs[j])
             + seg * (_dg(cat2(u_hl[j], v_hl[j], 0), cat2(ad_hl[j], kd_hl[j], 0), TN)
                      + (_dg(cat2(u_hl[j], v_hl[j], 0), cat2(ad_hl[j], kd_hl[j], 1), TN)
                         + _dg(cat2(u_hl[j], v_hl[j], 1), cat2(ad_hl[j], kd_hl[j], 0), TN)))
             for j in n]
    for j, (i, p) in enumerate(inst):
        s_scr[i, p] = s_new[j]
    if merged:
        inter = [_mm(jnp.concatenate([a_ra[j], a_rk[j]], axis=1),
                     jnp.concatenate([ust[j], vst[j]], axis=0), NN, "b") for j in n]
    else:
        inter = [_mm(a_ra[j], ust[j], NN, "b") + _mm(a_rk[j], vst[j], NN, "b") for j in n]
    Os = [sx[j][C:] + inter[j] for j in n]
    means = [_mm(t, seg, NN, P["seg"]) * (1.0 / N_A) for t in Os]
    dlts = [t - m for t, m in zip(Os, means)]
    vars_ = [_mm(t * t, seg, NN, P["seg"]) * (1.0 / N_A) for t in dlts]
    bonus = [_mm(rs[j] * k2s[j] * rk_ref[:, sls[j]], seg, NN, P["seg"]) * vs_[j] for j in n]
    for j, (i, p) in enumerate(inst):
        on = dlts[j] * lax.rsqrt(vars_[j] + LNX_EPS) * lnw_ref[:, sls[j]] + lnb_ref[:, sls[j]]
        o_ref[i, :, sls[j]] = (on + bonus[j]) * gs[j]

    @pl.when(c == pl.num_programs(1) - 1)
    def _():
        sout_ref[...] = s_scr[...]


def _seq_specs(nseq, shared_init, hist_w, state_shape):
    zeros = (0,) * len(state_shape)
    if shared_init:
        return [pl.BlockSpec((1, SUBLANE, hist_w), lambda b, c: (0, 0, 0)),
                pl.BlockSpec((1,) + state_shape, lambda b, c: (0,) + zeros)]
    return [pl.BlockSpec((nseq, SUBLANE, hist_w), lambda b, c: (b, 0, 0)),
            pl.BlockSpec((nseq,) + state_shape, lambda b, c: (b,) + zeros)]


def _rwkv(pa, prev8, s0, consts, chunk, nseq):
    B, L, _ = pa.shape
    shared_init = prev8.shape[0] == 1 and B > 1
    state_shape = (H_A // 2, LANE, LANE)
    return pl.pallas_call(
        functools.partial(_rwkv_kernel, chunk=chunk, nseq=nseq, shared_init=shared_init),
        grid=(B // nseq, L // chunk),
        in_specs=[pl.BlockSpec((nseq, chunk, PA_W), lambda b, c: (b, c, 0))]
        + _seq_specs(nseq, shared_init, PA_W, state_shape)
        + [pl.BlockSpec(t.shape, lambda b, c: (0, 0)) for t in consts],
        out_specs=[pl.BlockSpec((nseq, chunk, W_A), lambda b, c: (b, c, 0)),
                   pl.BlockSpec((nseq,) + state_shape, lambda b, c: (b, 0, 0, 0))],
        out_shape=[jax.ShapeDtypeStruct((B, L, W_A), F32),
                   jax.ShapeDtypeStruct((B,) + state_shape, F32)],
        scratch_shapes=[pltpu.VMEM((nseq, SUBLANE, PA_W), F32),
                        pltpu.VMEM((nseq,) + state_shape, F32)],
        compiler_params=pltpu.CompilerParams(dimension_semantics=("arbitrary", "arbitrary"),
                                             vmem_limit_bytes=VMEM_LIMIT),
        name="rwkv_chunk%d" % chunk,
    )(pa, prev8, s0, *consts)


GDN_PREC = dict(cum="xr", tr="xr", kk="b", invm="b", wu="b", qk="b", sread="b", qkv="b", state="b")


def _gdn_kernel(pb_ref, ga_ref, hist_ref, s0_ref, cw_ref, alog_ref, dtb_ref, nw_ref, o_ref, sout_ref,
                xbuf, s_scr, *, chunk, nseq, shared_init):
    c = pl.program_id(1)
    C = chunk
    P = GDN_PREC

    @pl.when(c == 0)
    def _():
        for i in range(nseq):
            j = 0 if shared_init else i
            xbuf[i, 0:SUBLANE, :] = hist_ref[j]
            s_scr[i] = s0_ref[j]

    row = _iota2((C, C), 0)
    col = _iota2((C, C), 1)
    incl = row >= col
    strict = row > col
    tri = incl.astype(F32)
    sel = (_iota2((SUBLANE, LANE), 0) + SMALL_GDN_A == _iota2((SUBLANE, LANE), 1)).astype(F32)

    def prep(i):
        x = pb_ref[i]
        qkv = x[:, 0:3 * W_B]
        full = jnp.concatenate([xbuf[i], qkv], axis=0)
        conv = qkv * cw_ref[CONV_W - 1:CONV_W, :]
        for j in range(1, CONV_W):
            conv = conv + pltpu.roll(full, j, axis=0)[SUBLANE:, :] * cw_ref[CONV_W - 1 - j:CONV_W - j, :]
        xbuf[i] = qkv[C - SUBLANE:, :]
        cs = conv * _sigmoid(conv)
        gates = ga_ref[i]
        glog = -jnp.exp(alog_ref[...]) * _softplus(gates + dtb_ref[...])
        beta = _sigmoid(gates)
        G = _mm(tri, glog, NN, P["cum"])
        Gt = _mm(sel, G, NT, P["tr"])
        return dict(cs=cs, z=x[:, PB_Z:PB_Z + W_B], beta=beta, G=G, Gt=Gt)

    seqs = [prep(i) for i in range(nseq)]

    inst = [(i, h) for i in range(nseq) for h in range(H_B)]
    n = range(len(inst))
    l2n = lambda t, c: t * (lax.rsqrt(jnp.sum(t * t, axis=-1, keepdims=True) + 1e-12) * c)
    head = lambda i, h, part: seqs[i]["cs"][:, part * W_B + h * LANE:part * W_B + (h + 1) * LANE]
    qs = [l2n(head(i, h, 0), DK_B ** -0.5) for i, h in inst]
    ks = [l2n(head(i, h, 1), 1.0) for i, h in inst]
    vs = [head(i, h, 2) for i, h in inst]
    gcols = [seqs[i]["G"][:, SMALL_GDN_A + h:SMALL_GDN_A + h + 1] for i, h in inst]
    grows = [seqs[i]["Gt"][h:h + 1, :] for i, h in inst]
    bcols = [seqs[i]["beta"][:, SMALL_GDN_B + h:SMALL_GDN_B + h + 1] for i, h in inst]
    Ss = [s_scr[i, h] for i, h in inst]

    dmats = [jnp.where(incl, jnp.exp(jnp.where(incl, gcols[j] - grows[j], 0.0)), 0.0) for j in n]
    kbs = [ks[j] * bcols[j] for j in n]
    lows = [jnp.where(strict, _mm(kbs[j], ks[j], NT, P["kk"]) * dmats[j], 0.0) for j in n]
    ainv = _inv_unit_lower(lows, C, P["invm"])
    egs = [jnp.exp(t) for t in gcols]
    wus = [_mm(ainv[j], jnp.concatenate([kbs[j] * egs[j], vs[j] * bcols[j]], axis=1), NN, P["wu"])
           for j in n]
    qks = [_mm(qs[j], ks[j], NT, P["qk"]) * dmats[j] for j in n]
    srs = [_mm(jnp.concatenate([wus[j][:, :DK_B], qs[j] * egs[j]], axis=0), Ss[j], NN, P["sread"])
           for j in n]
    v_new = [wus[j][:, DK_B:] - srs[j][:C] for j in n]
    glast = [t[C - 1:C, :] for t in gcols]
    s_new = [Ss[j] * jnp.exp(glast[j])
             + _mm(ks[j] * jnp.exp(glast[j] - gcols[j]), v_new[j], TN, P["state"]) for j in n]
    for j, (i, h) in enumerate(inst):
        s_scr[i, h] = s_new[j]
    os_ = [srs[j][C:] + _mm(qks[j], v_new[j], NN, P["qkv"]) for j in n]
    for j, (i, h) in enumerate(inst):
        o = os_[j]
        o = o * lax.rsqrt(jnp.mean(o * o, axis=-1, keepdims=True) + NORM_EPS) * nw_ref[...]
        zh = seqs[i]["z"][:, h * LANE:(h + 1) * LANE]
        o_ref[i, :, h * LANE:(h + 1) * LANE] = o * (zh * _sigmoid(zh))

    @pl.when(c == pl.num_programs(1) - 1)
    def _():
        sout_ref[...] = s_scr[...]


def _gdn(pb, pa, hist8, s0, consts, chunk, nseq):
    B, L, _ = pb.shape
    shared_init = hist8.shape[0] == 1 and B > 1
    state_shape = (H_B, DK_B, DV_B)
    return pl.pallas_call(
        functools.partial(_gdn_kernel, chunk=chunk, nseq=nseq, shared_init=shared_init),
        grid=(B // nseq, L // chunk),
        in_specs=[pl.BlockSpec((nseq, chunk, PB_W), lambda b, c: (b, c, 0)),
                  pl.BlockSpec((nseq, chunk, LANE), lambda b, c: (b, c, PA_SMALL // LANE))]
        + _seq_specs(nseq, shared_init, 3 * W_B, state_shape)
        + [pl.BlockSpec(t.shape, lambda b, c: (0, 0)) for t in consts],
        out_specs=[pl.BlockSpec((nseq, chunk, W_B), lambda b, c: (b, c, 0)),
                   pl.BlockSpec((nseq,) + state_shape, lambda b, c: (b, 0, 0, 0))],
        out_shape=[jax.ShapeDtypeStruct((B, L, W_B), F32),
                   jax.ShapeDtypeStruct((B,) + state_shape, F32)],
        scratch_shapes=[pltpu.VMEM((nseq, SUBLANE, 3 * W_B), F32),
                        pltpu.VMEM((nseq,) + state_shape, F32)],
        compiler_params=pltpu.CompilerParams(dimension_semantics=("arbitrary", "arbitrary"),
                                             vmem_limit_bytes=VMEM_LIMIT),
        name="gdn_chunk%d" % chunk,
    )(pb, pa, hist8, s0, *consts)


def _row_mask(nrow, i):
    return _iota2((nrow, 1), 0) == i


STEP_VECS = ("w", "kk", "kka", "k2", "v", "r", "g", "bon")


def _rwkv_step_prep_kernel(pa_ref, prev_ref, mu_ref, w0_ref, wdu_ref, a0_ref, wau_ref, wgu_ref,
                           kk_ref, ka_ref, rk_ref, vec_ref):
    x = pa_ref[...]
    xm = x + (prev_ref[...] - x) * mu_ref[...]
    r = xm[:, 0:W_A]
    k = xm[:, W_A:2 * W_A]
    v = xm[:, 2 * W_A:3 * W_A]
    sm = xm[:, PA_SMALL:PA_SMALL + LANE]
    gd = xm[:, PA_GATE:PA_GATE + LANE]
    wl = w0_ref[...] + _dot(jnp.tanh(sm).astype(BF16), wdu_ref[...])
    w = jnp.exp(-jnp.exp(-_softplus(-wl) - 0.5))
    a = _sigmoid(a0_ref[...] + _dot(sm.astype(BF16), wau_ref[...]))
    g = _dot(_sigmoid(gd).astype(BF16), wgu_ref[...])
    kkr = k * kk_ref[...]
    k2 = k * (1.0 + (a - 1.0) * ka_ref[...])
    bon = r * k2 * rk_ref[...]
    seg = ((_iota2((LANE, LANE), 0) // N_A) == (_iota2((LANE, LANE), 1) // N_A)).astype(F32)
    ssq = jnp.concatenate([_mm(kkr[:, p * LANE:(p + 1) * LANE] ** 2, seg, NN, RWKV_PREC["seg"])
                           for p in range(H_A // 2)], axis=-1)
    kk = kkr * lax.rsqrt(ssq + 1e-12)
    vecs = dict(w=w, kk=kk, kka=kk * a, k2=k2, v=v, r=r, g=g, bon=bon)
    for j, name in enumerate(STEP_VECS):
        vec_ref[j] = vecs[name].T


def _rwkv_step_state_kernel(vec_ref, s_ref, lnw_ref, lnb_ref, o_ref, sout_ref, o_scr):
    w, kk, kka, k2, v, r, g, bon = [vec_ref[j] for j in range(len(STEP_VECS))]
    for i in range(N_A):
        S = s_ref[0, i]
        sa = jnp.sum(S * kk, axis=0, keepdims=True)
        s_new = S * w - sa * kka + v[i:i + 1, :] * k2
        sout_ref[0, i] = s_new
        o_scr[i:i + 1, :] = jnp.sum(s_new * r, axis=0, keepdims=True)
    o = o_scr[...]
    mean = jnp.mean(o, axis=0, keepdims=True)
    dlt = o - mean
    var = jnp.mean(dlt * dlt, axis=0, keepdims=True)
    on = dlt * lax.rsqrt(var + LNX_EPS) * lnw_ref[...] + lnb_ref[...]
    o_ref[...] = (on + jnp.sum(bon, axis=0, keepdims=True) * v) * g


def _gdn_step_kernel(pb_ref, ga_ref, hist_ref, s_ref, cw_ref, alog_ref, dtb_ref, nw_ref, o_ref, sout_ref):
    ns = pb_ref.shape[0]
    x = pb_ref[...]
    conv = x[:, 0:3 * W_B] * cw_ref[CONV_W - 1:CONV_W, :]
    for j in range(CONV_W - 1):
        conv = conv + hist_ref[j] * cw_ref[j:j + 1, :]
    cs = conv * _sigmoid(conv)
    z = x[:, PB_Z:PB_Z + W_B]
    gates = ga_ref[...]
    eg = jnp.exp(-jnp.exp(alog_ref[...]) * _softplus(gates + dtb_ref[...]))
    beta = _sigmoid(gates)

    l2n = lambda t: t * lax.rsqrt(jnp.sum(t * t, axis=-1, keepdims=True) + 1e-12)
    for h in range(H_B):
        hs = slice(h * LANE, (h + 1) * LANE)
        q_h = l2n(cs[:, hs]) * (DK_B ** -0.5)
        k_h = l2n(cs[:, W_B + h * LANE:W_B + (h + 1) * LANE])
        v_h = cs[:, 2 * W_B + h * LANE:2 * W_B + (h + 1) * LANE]
        eg_h = eg[:, SMALL_GDN_A + h:SMALL_GDN_A + h + 1]
        b_h = beta[:, SMALL_GDN_B + h:SMALL_GDN_B + h + 1]
        seq = range(ns)
        Ss = [s_ref[i, h] for i in seq]
        kss = [_mm(k_h, Ss[i], NN, "xr") for i in seq]
        v_new = [b_h * (v_h - eg_h * kss[i]) for i in seq]
        s_new = [Ss[i] * eg_h[i:i + 1, :] + _mm(jnp.where(_row_mask(ns, i), k_h, 0.0), v_new[i], TN, "b")
                 for i in seq]
        for i in seq:
            sout_ref[i, h] = s_new[i]
        o_all = [_mm(q_h, s_new[i], NN, "b") for i in seq]
        o_h = jnp.zeros((ns, DV_B), F32)
        for i in seq:
            o_h = jnp.where(_row_mask(ns, i), o_all[i], o_h)
        o_h = o_h * lax.rsqrt(jnp.mean(o_h * o_h, axis=-1, keepdims=True) + NORM_EPS) * nw_ref[...]
        zh = z[:, hs]
        o_ref[:, hs] = o_h * (zh * _sigmoid(zh))


def _step_const_specs(ts):
    return [pl.BlockSpec(t.shape, lambda b: (0,) * t.ndim) for t in ts]


def _rwkv_step(pa, prev, s_t, consts):
    B = s_t.shape[-1]
    prep_consts = consts[:-2]
    lnw_col, lnb_col = [c.reshape(W_A, 1) for c in consts[-2:]]
    vecs = pl.pallas_call(
        _rwkv_step_prep_kernel,
        grid=(1,),
        in_specs=[pl.BlockSpec((B, PA_W), lambda b: (0, 0)),
                  pl.BlockSpec((B, PA_W), lambda b: (0, 0))] + _step_const_specs(prep_consts),
        out_specs=pl.BlockSpec((len(STEP_VECS), W_A, B), lambda b: (0, 0, 0)),
        out_shape=jax.ShapeDtypeStruct((len(STEP_VECS), W_A, B), F32),
        compiler_params=pltpu.CompilerParams(dimension_semantics=("arbitrary",),
                                             vmem_limit_bytes=VMEM_LIMIT),
        name="rwkv_step_prep",
    )(pa, prev, *prep_consts)
    state_spec = pl.BlockSpec((1, N_A, N_A, B), lambda h: (h, 0, 0, 0))
    return pl.pallas_call(
        _rwkv_step_state_kernel,
        grid=(H_A,),
        in_specs=[pl.BlockSpec((len(STEP_VECS), N_A, B), lambda h: (0, h, 0)), state_spec,
                  pl.BlockSpec((N_A, 1), lambda h: (h, 0)), pl.BlockSpec((N_A, 1), lambda h: (h, 0))],
        out_specs=[pl.BlockSpec((N_A, B), lambda h: (h, 0)), state_spec],
        out_shape=[jax.ShapeDtypeStruct((W_A, B), F32), jax.ShapeDtypeStruct(s_t.shape, F32)],
        scratch_shapes=[pltpu.VMEM((N_A, B), F32)],
        compiler_params=pltpu.CompilerParams(dimension_semantics=("arbitrary",),
                                             vmem_limit_bytes=VMEM_LIMIT),
        name="rwkv_step_state",
    )(vecs, s_t, lnw_col, lnb_col)


def _gdn_step(pb, pa, hist, s, consts, ns):
    B = s.shape[0]
    return pl.pallas_call(
        _gdn_step_kernel,
        grid=(B // ns,),
        in_specs=[pl.BlockSpec((ns, PB_W), lambda b: (b, 0)),
                  pl.BlockSpec((ns, LANE), lambda b: (b, PA_SMALL // LANE)),
                  pl.BlockSpec((CONV_W - 1, ns, 3 * W_B), lambda b: (0, b, 0)),
                  pl.BlockSpec((ns, H_B, DK_B, DV_B), lambda b: (b, 0, 0, 0))] + _step_const_specs(consts),
        out_specs=[pl.BlockSpec((ns, W_B), lambda b: (b, 0)),
                   pl.BlockSpec((ns, H_B, DK_B, DV_B), lambda b: (b, 0, 0, 0))],
        out_shape=[jax.ShapeDtypeStruct((B, W_B), F32), jax.ShapeDtypeStruct(s.shape, F32)],
        compiler_params=pltpu.CompilerParams(dimension_semantics=("arbitrary",),
                                             vmem_limit_bytes=VMEM_LIMIT),
        name="gdn_step",
    )(pb, pa, hist, s, *consts)


def _pad_cols(t, width):
    return jnp.pad(t, [(0, 0)] * (t.ndim - 1) + [(0, width - t.shape[-1])])


def _pa_layout(t, gdn_gates=None):
    main = t[..., :3 * W_A]
    small = t[..., 3 * W_A:3 * W_A + D_DECAY + D_AAA]
    gate = t[..., 3 * W_A + D_DECAY + D_AAA:]
    if gdn_gates is not None:
        small = jnp.concatenate([small, gdn_gates], axis=-1)
    return jnp.concatenate([main, _pad_cols(small, LANE), _pad_cols(gate, LANE)], axis=-1)


def _pa_unlayout(t):
    return jnp.concatenate([t[..., :3 * W_A], t[..., PA_SMALL:PA_SMALL + D_DECAY + D_AAA],
                            t[..., PA_GATE:PA_GATE + D_GATE]], axis=-1)


def _small_rows(t, first_row):
    return jnp.pad(t, [(first_row, LANE - first_row - t.shape[0]), (0, 0)])


def _small_lanes(t, first_lane):
    return jnp.pad(t, [(0, 0), (first_lane, LANE - first_lane - t.shape[1])])


def _pair_unblock(s):
    B = s.shape[0]
    return jnp.stack([s[:, :, :N_A, :N_A], s[:, :, N_A:, N_A:]], axis=2).reshape(B, H_A, N_A, N_A)


def _history_rows(rows):
    B, n, w = rows.shape
    return jnp.concatenate([jnp.zeros((B, SUBLANE - n, w), F32), rows], axis=1)


def kernel(x_prompt, x_sample, state_rwkv, state_shift, state_gdn, state_conv, meta_tokens,
           g_ffn1, w_gate1, w_up1, w_down1, g_mix, w_in, mu_shift, w0, w_decay_up, a0, w_a_up,
           w_g_up, k_k, k_a, r_k, lnx_w, lnx_b, conv_w, a_log, dt_bias, gdn_norm_w, w_out,
           g_ffn2, w_gate2, w_up2, w_down2, g_final):
    assert g_ffn1.shape[0] == 1, "single trunk layer"
    bp, sp, _ = x_prompt.shape
    bs = x_sample.shape[0]
    assert x_sample.shape[1] == 1 and sp % CHUNK_RWKV == 0 and sp % CHUNK_GDN == 0
    assert (bp * sp) % TM_DENSE == 0 and (bp * sp) % TM_DENSE_OUT == 0
    assert bp % SEQS_PER_STEP == 0 and bs % DEC_SEQS_PER_STEP == 0
    row = lambda t: t.reshape(1, -1).astype(F32)

    ffn1 = (w_gate1[0].astype(BF16), w_up1[0].astype(BF16), w_down1[0].astype(BF16))
    ffn2 = (w_gate2[0].astype(BF16), w_up2[0].astype(BF16), w_down2[0].astype(BF16))
    win_a = _pa_layout(w_in[0][:, :N_A_IN], w_in[0][:, N_A_IN + PB_W:]).astype(BF16)
    win_b = w_in[0][:, N_A_IN:N_A_IN + PB_W].astype(BF16)
    wo_a = w_out[0][:W_A].astype(BF16)
    wo_b = w_out[0][W_A:].astype(BF16)
    dense_in_consts = (row(g_ffn1[0]), *ffn1, row(g_mix[0]), win_a, win_b)
    dense_out_consts = (wo_a, wo_b, row(g_ffn2[0]), *ffn2, row(g_final))
    rwkv_consts = (_pa_layout(row(mu_shift[0])), row(w0[0]),
                   _small_rows(w_decay_up[0], 0).astype(BF16), row(a0[0]),
                   _small_rows(w_a_up[0], SMALL_AAA).astype(BF16), _small_rows(w_g_up[0], 0).astype(BF16),
                   row(k_k[0]), row(k_a[0]), row(r_k[0]), row(lnx_w[0]), row(lnx_b[0]))
    gdn_consts = (conv_w[0].astype(F32), _small_lanes(row(a_log[0]), SMALL_GDN_A),
                  _small_lanes(row(dt_bias[0]), SMALL_GDN_A), row(gdn_norm_w[0]))

    xs = jnp.concatenate([x_sample[:, 0, :].astype(F32), meta_tokens.astype(F32)], axis=0)
    hs, pas, pbs = _dense_in(xs, *dense_in_consts, tm=xs.shape[0])

    pa_meta = pas[bs:][None]
    pb_meta = pbs[bs:][None]
    _, rw_meta = _rwkv(pa_meta, jnp.zeros((1, SUBLANE, PA_W), F32),
                       jnp.zeros((1, H_A // 2, LANE, LANE), F32), rwkv_consts, N_META, 1)
    _, gd_meta = _gdn(pb_meta, pa_meta, jnp.zeros((1, SUBLANE, 3 * W_B), F32),
                      jnp.zeros((1, H_B, DK_B, DV_B), F32), gdn_consts, N_META, 1)

    hp, pap, pbp = _dense_in(x_prompt.reshape(bp * sp, D_MODEL).astype(F32), *dense_in_consts,
                             tm=TM_DENSE)
    pap3 = pap.reshape(bp, sp, PA_W)
    pbp3 = pbp.reshape(bp, sp, PB_W)
    oa_p, rw_p = _rwkv(pap3, _history_rows(pa_meta[:, -1:, :]), rw_meta, rwkv_consts,
                       CHUNK_RWKV, SEQS_PER_STEP)
    ob_p, gd_p = _gdn(pbp3, pap3, _history_rows(pb_meta[:, -(CONV_W - 1):, :3 * W_B]), gd_meta, gdn_consts,
                      CHUNK_GDN, SEQS_PER_STEP)
    y_p = _dense_out(hp, oa_p.reshape(bp * sp, W_A), ob_p.reshape(bp * sp, W_B), *dense_out_consts,
                     tm=TM_DENSE_OUT)

    oa_t, rw_t = _rwkv_step(pas, _pa_layout(state_shift[0].astype(F32)),
                            jnp.transpose(state_rwkv[0].astype(F32), (1, 2, 3, 0)), rwkv_consts)
    oa_s = oa_t.T
    rw_s = jnp.transpose(rw_t, (3, 0, 1, 2))
    ob_s, gd_s = _gdn_step(pbs, pas, jnp.swapaxes(state_conv[0].astype(F32), 0, 1), state_gdn[0].astype(F32),
                           gdn_consts, DEC_SEQS_PER_STEP)
    y_s = _dense_out(hs[:bs], oa_s, ob_s, *dense_out_consts, tm=bs)

    new_conv_s = jnp.concatenate([state_conv[0].astype(F32)[:, 1:, :], pbs[:bs, None, :3 * W_B]], axis=1)
    return (y_p.reshape(bp, sp, D_MODEL).astype(x_prompt.dtype),
            y_s.reshape(bs, 1, D_MODEL).astype(x_sample.dtype),
            _pair_unblock(rw_p)[None],
            _pa_unlayout(pap3[:, -1, :])[None],
            gd_p[None],
            pbp3[:, -(CONV_W - 1):, :3 * W_B][None],
            rw_s[None],
            _pa_unlayout(pas[:bs])[None],
            gd_s[None],
            new_conv_s[None])
```

```python
import functools

import jax
import jax.numpy as jnp
from jax import lax
from jax.experimental import pallas as pl
from jax.experimental.pallas import tpu as pltpu

F32 = jnp.float32
BF16 = jnp.bfloat16

LANE = 128
SUBLANE = 8
VMEM_LIMIT = 56 * 1024 * 1024

D_MODEL = 1024
D_FF = 2816
N_META = 16
H_A, N_A = 8, 64
W_A = H_A * N_A
D_DECAY, D_AAA, D_GATE = 32, 32, 96
N_A_IN = 3 * W_A + D_DECAY + D_AAA + D_GATE
H_B, DK_B, DV_B = 4, 128, 128
W_B = H_B * DV_B
CONV_W = 4
LNX_EPS = 64e-5
NORM_EPS = 1e-6
NEG_BIG = -1e30

PA_W = 3 * W_A + 2 * LANE
PA_SMALL = 3 * W_A
PA_GATE = PA_SMALL + LANE
SMALL_AAA = D_DECAY
SMALL_GDN_A = D_DECAY + D_AAA
SMALL_GDN_B = SMALL_GDN_A + H_B
PB_W = 4 * W_B
PB_Z = 3 * W_B

TM_DENSE = 512
TM_DENSE_OUT = 512
CHUNK_RWKV = 64
CHUNK_GDN = 128
SEQS_PER_STEP = 4
DEC_SEQS_PER_STEP = SUBLANE

NN = (((1,), (0,)), ((), ()))
NT = (((1,), (1,)), ((), ()))
TN = (((0,), (0,)), ((), ()))


def _dot(a, b):
    return jnp.dot(a, b, preferred_element_type=F32)


def _sigmoid(x):
    return 1.0 / (1.0 + jnp.exp(-x))


def _softplus(x):
    return jnp.maximum(x, 0.0) + jnp.log(1.0 + jnp.exp(-jnp.abs(x)))


def _rms(x, g):
    return x * lax.rsqrt(jnp.mean(x * x, axis=-1, keepdims=True) + NORM_EPS) * g


def _iota2(shape, dim):
    return lax.broadcasted_iota(jnp.int32, shape, dim)


def _split(x):
    hi = x.astype(BF16)
    return hi, (x - hi.astype(F32)).astype(BF16)


def _dg(a, b, dims):
    return lax.dot_general(a, b, dims, preferred_element_type=F32)


def _mm(a, b, dims=NN, mode="b"):
    if mode == "b":
        return _dg(a.astype(BF16), b.astype(BF16), dims)
    if mode == "xl":
        ah, al = _split(a)
        bh = b.astype(BF16)
        return _dg(ah, bh, dims) + _dg(al, bh, dims)
    if mode == "xr":
        ah = a.astype(BF16)
        bh, bl = _split(b)
        return _dg(ah, bh, dims) + _dg(ah, bl, dims)
    assert mode == "x3"
    ah, al = _split(a)
    bh, bl = _split(b)
    return _dg(ah, bh, dims) + (_dg(ah, bl, dims) + _dg(al, bh, dims))


def _inv_unit_lower(lows, nil, merge_mode, expand=None):
    shape = lows[0].shape
    row = _iota2(shape, 0)
    col = _iota2(shape, 1) % nil
    same_block = lambda s: (row // s) == (col // s)
    eye = (row == col).astype(F32)
    if expand is None:
        expand = lambda t: t
    narrow = (lambda t: t.astype(BF16)) if merge_mode == "b" else (lambda t: t)
    s = 2
    in_base = same_block(s)
    invs = [eye - jnp.where(in_base, low, 0.0) for low in lows]
    while s < nil:
        newly = jnp.logical_and(same_block(2 * s), jnp.logical_not(same_block(s)))
        ts = [_mm(inv, expand(narrow(jnp.where(newly, low, 0.0))), NN, merge_mode)
              for inv, low in zip(invs, lows)]
        invs = [inv - _mm(t, expand(narrow(inv)), NN, merge_mode) for inv, t in zip(invs, ts)]
        s *= 2
    return invs


FF_SPLIT = (0, 1536, D_FF)


def _swiglu(n, wg_ref, wu_ref, wd_ref):
    out = None
    for lo, hi in zip(FF_SPLIT[:-1], FF_SPLIT[1:]):
        gate = _dot(n, wg_ref[:, lo:hi])
        up = _dot(n, wu_ref[:, lo:hi])
        act = (gate * _sigmoid(gate) * up).astype(BF16)
        part = _dot(act, wd_ref[lo:hi, :])
        out = part if out is None else out + part
    return out


def _dense_in_kernel(x_ref, g1_ref, wg_ref, wu_ref, wd_ref, gm_ref, wa_ref, wb_ref,
                     h_ref, pa_ref, pb_ref):
    x = x_ref[...]
    h = x + 0.5 * _swiglu(_rms(x, g1_ref[...]).astype(BF16), wg_ref, wu_ref, wd_ref)
    h_ref[...] = h
    n = _rms(h, gm_ref[...]).astype(BF16)
    pa_ref[...] = _dot(n, wa_ref[...])
    pb_ref[...] = _dot(n, wb_ref[...])


def _dense_out_kernel(h_ref, oa_ref, ob_ref, woa_ref, wob_ref, g2_ref, wg_ref, wu_ref, wd_ref,
                      gf_ref, y_ref):
    h = (h_ref[...] + _dot(oa_ref[...].astype(BF16), woa_ref[...])
         + _dot(ob_ref[...].astype(BF16), wob_ref[...]))
    h = h + 0.5 * _swiglu(_rms(h, g2_ref[...]).astype(BF16), wg_ref, wu_ref, wd_ref)
    y_ref[...] = _rms(h, gf_ref[...])


def _const_spec(shape):
    return pl.BlockSpec(shape, lambda *_: (0,) * len(shape), pipeline_mode=pl.Buffered(1))


def _row_spec(tm, width):
    return pl.BlockSpec((tm, width), lambda i: (i, 0))


def _dense_in(x, g1, wg, wu, wd, gm, wa, wb, tm):
    n = x.shape[0]
    consts = (g1, wg, wu, wd, gm, wa, wb)
    return pl.pallas_call(
        _dense_in_kernel,
        grid=(n // tm,),
        in_specs=[_row_spec(tm, D_MODEL)] + [_const_spec(c.shape) for c in consts],
        out_specs=[_row_spec(tm, D_MODEL), _row_spec(tm, PA_W), _row_spec(tm, PB_W)],
        out_shape=[jax.ShapeDtypeStruct((n, D_MODEL), F32),
                   jax.ShapeDtypeStruct((n, PA_W), F32),
                   jax.ShapeDtypeStruct((n, PB_W), F32)],
        compiler_params=pltpu.CompilerParams(dimension_semantics=("arbitrary",),
                                             vmem_limit_bytes=VMEM_LIMIT),
        name="dense_in",
    )(x, *consts)


def _dense_out(h, oa, ob, woa, wob, g2, wg, wu, wd, gf, tm):
    n = h.shape[0]
    consts = (woa, wob, g2, wg, wu, wd, gf)
    return pl.pallas_call(
        _dense_out_kernel,
        grid=(n // tm,),
        in_specs=[_row_spec(tm, D_MODEL), _row_spec(tm, W_A), _row_spec(tm, W_B)]
        + [_const_spec(c.shape) for c in consts],
        out_specs=_row_spec(tm, D_MODEL),
        out_shape=jax.ShapeDtypeStruct((n, D_MODEL), F32),
        compiler_params=pltpu.CompilerParams(dimension_semantics=("arbitrary",),
                                             vmem_limit_bytes=VMEM_LIMIT),
        name="dense_out",
    )(h, oa, ob, *consts)


RWKV_PREC = dict(cum="b", seg="b", pair="b", invm="b", sread="b", akv="b", solve="b", inter="b",
                 state="x3")


def _rwkv_kernel(pa_ref, prev_ref, s0_ref, mu_ref, w0_ref, wdu_ref, a0_ref, wau_ref, wgu_ref,
                 kk_ref, ka_ref, rk_ref, lnw_ref, lnb_ref, o_ref, sout_ref, xbuf, s_scr,
                 *, chunk, nseq, shared_init):
    c = pl.program_id(1)
    C = chunk
    P = RWKV_PREC

    @pl.when(c == 0)
    def _():
        for i in range(nseq):
            j = 0 if shared_init else i
            xbuf[i, 0:SUBLANE, :] = prev_ref[j]
            s_scr[i] = s0_ref[j]

    tri = (_iota2((C, C), 0) >= _iota2((C, C), 1)).astype(F32)
    mid = C // 2 - 1

    def prep(i):
        x = pa_ref[i]
        full = jnp.concatenate([xbuf[i], x], axis=0)
        prev = pltpu.roll(full, 1, axis=0)[SUBLANE:, :]
        xm = x + (prev - x) * mu_ref[...]
        xbuf[i] = x[C - SUBLANE:, :]

        r = xm[:, 0:W_A]
        k = xm[:, W_A:2 * W_A]
        v = xm[:, 2 * W_A:3 * W_A]
        sm = xm[:, PA_SMALL:PA_SMALL + LANE]
        gd = xm[:, PA_GATE:PA_GATE + LANE]

        wl = w0_ref[...] + _dot(jnp.tanh(sm).astype(BF16), wdu_ref[...])
        lw = -jnp.exp(-_softplus(-wl) - 0.5)
        a = _sigmoid(a0_ref[...] + _dot(sm.astype(BF16), wau_ref[...]))
        g = _dot(_sigmoid(gd).astype(BF16), wgu_ref[...])
        kkr = k * kk_ref[...]
        k2 = k * (1.0 + (a - 1.0) * ka_ref[...])
        G = _mm(tri, lw, NN, P["cum"])
        return dict(r=r, v=v, a=a, g=g, kkr=kkr, k2=k2, G=G, Gx=G - lw, Gm=G[mid:mid + 1, :],
                    Gc=G[C - 1:C, :])

    seqs = [prep(i) for i in range(nseq)]

    C2 = 2 * C
    trow = _iota2((C, C2), 0)
    tcol = _iota2((C, C2), 1) % C
    incl = trow >= tcol
    strict = trow > tcol
    lane = _iota2((1, LANE), 1)
    m0 = (lane < N_A).astype(F32)
    m1 = (lane >= N_A).astype(F32)
    col2 = _iota2((1, C2), 1)
    c0 = (col2 < C).astype(F32)
    c1 = (col2 >= C).astype(F32)
    seg = ((_iota2((LANE, LANE), 0) // N_A) == (_iota2((LANE, LANE), 1) // N_A)).astype(F32)
    stack = lambda t: jnp.concatenate([t * m0.astype(t.dtype), t * m1.astype(t.dtype)], axis=0)
    expand = lambda t: jnp.concatenate([t * c0.astype(t.dtype), t * c1.astype(t.dtype)], axis=0)
    narrow = lambda t, site: t.astype(BF16) if P[site] == "b" else t
    merged = C2 % LANE == 0

    inst = [(i, p) for i in range(nseq) for p in range(H_A // 2)]
    n = range(len(inst))
    sls = [slice(p * LANE, (p + 1) * LANE) for _, p in inst]
    take = lambda name: [seqs[i][name][:, sls[j]] for j, (i, _) in enumerate(inst)]
    rs, vs_, k2s, kkrs, as_, gs = take("r"), take("v"), take("k2"), take("kkr"), take("a"), take("g")
    Gs, Gxs, Gms, Gcs = take("G"), take("Gx"), take("Gm"), take("Gc")
    Ss = [s_scr[i, p] for i, p in inst]

    ssq = [_mm(t * t, seg, NN, P["seg"]) for t in kkrs]
    kks = [t * lax.rsqrt(q + 1e-12) for t, q in zip(kkrs, ssq)]
    kkas = [kks[j] * as_[j] for j in n]
    inv_rel = [jnp.exp(Gms[j] - Gs[j]) for j in n]
    dec = [jnp.exp(Gcs[j] - Gs[j]) for j in n]
    lhs = [jnp.concatenate([kks[j] * jnp.exp(Gxs[j] - Gms[j]), rs[j] * jnp.exp(Gs[j] - Gms[j])], axis=0)
           for j in n]
    ais = [stack(narrow(kkas[j] * inv_rel[j], "pair")) for j in n]
    kis = [stack(narrow(k2s[j] * inv_rel[j], "pair")) for j in n]
    if merged:
        pm = [_mm(lhs[j], jnp.concatenate([ais[j], kis[j]], axis=0), NT, P["pair"]) for j in n]
        pas_, pks_ = [t[:, :C2] for t in pm], [t[:, C2:] for t in pm]
    else:
        pas_ = [_mm(lhs[j], ais[j], NT, P["pair"]) for j in n]
        pks_ = [_mm(lhs[j], kis[j], NT, P["pair"]) for j in n]
    a_aa = [jnp.where(strict, t[:C], 0.0) for t in pas_]
    a_ak = [jnp.where(strict, t[:C], 0.0) for t in pks_]
    a_ra = [jnp.where(incl, t[C:], 0.0) for t in pas_]
    a_rk = [jnp.where(incl, t[C:], 0.0) for t in pks_]
    ainv = _inv_unit_lower(a_aa, C, P["invm"], expand)

    assert P["akv"] == "b" and P["inter"] == "b" and P["state"] == "x3"
    v_hl = [_split(t) for t in vs_]
    vst = [stack(hl[0]) for hl in v_hl]
    akv = [_mm(a_ak[j], vst[j], NN, "b") for j in n]
    sx = [_mm(jnp.concatenate([kks[j] * jnp.exp(Gxs[j]), rs[j] * jnp.exp(Gs[j])], axis=0), Ss[j], NT,
              P["sread"]) for j in n]
    u_hl = [_split(-_mm(ainv[j], stack(narrow(sx[j][:C] + akv[j], "solve")), NN, P["solve"])) for j in n]
    ust = [stack(hl[0]) for hl in u_hl]
    cat2 = lambda a, b, q: jnp.concatenate([a[q], b[q]], axis=0)
    ad_hl = [_split(kkas[j] * dec[j]) for j in n]
    kd_hl = [_split(k2s[j] * dec[j]) for j in n]
    s_new = [Ss[j] * jnp.exp(Gcs[j])
             + seg * (_dg(cat2(u_hl[j], v_hl[j], 0), cat2(ad_hl[j], kd_hl[j], 0), TN)
                      + (_dg(cat2(u_hl[j], v_hl[j], 0), cat2(ad_hl[j], kd_hl[j], 1), TN)
                         + _dg(cat2(u_hl[j], v_hl[j], 1), cat2(ad_hl[j], kd_hl[j], 0), TN)))
             for j in n]
    for j, (i, p) in enumerate(inst):
        s_scr[i, p] = s_new[j]
    if merged:
        inter = [_mm(jnp.concatenate([a_ra[j], a_rk[j]], axis=1),
                     jnp.concatenate([ust[j], vst[j]], axis=0), NN, "b") for j in n]
    else:
        inter = [_mm(a_ra[j], ust[j], NN, "b") + _mm(a_rk[j], vst[j], NN, "b") for j in n]
    Os = [sx[j][C:] + inter[j] for j in n]
    means = [_mm(t, seg, NN, P["seg"]) * (1.0 / N_A) for t in Os]
    dlts = [t - m for t, m in zip(Os, means)]
    vars_ = [_mm(t * t, seg, NN, P["seg"]) * (1.0 / N_A) for t in dlts]
    bonus = [_mm(rs[j] * k2s[j] * rk_ref[:, sls[j]], seg, NN, P["seg"]) * vs_[j] for j in n]
    for j, (i, p) in enumerate(inst):
        on = dlts[j] * lax.rsqrt(vars_[j] + LNX_EPS) * lnw_ref[:, sls[j]] + lnb_ref[:, sls[j]]
        o_ref[i, :, sls[j]] = (on + bonus[j]) * gs[j]

    @pl.when(c == pl.num_programs(1) - 1)
    def _():
        sout_ref[...] = s_scr[...]


def _seq_specs(nseq, shared_init, hist_w, state_shape):
    zeros = (0,) * len(state_shape)
    if shared_init:
        return [pl.BlockSpec((1, SUBLANE, hist_w), lambda b, c: (0, 0, 0)),
                pl.BlockSpec((1,) + state_shape, lambda b, c: (0,) + zeros)]
    return [pl.BlockSpec((nseq, SUBLANE, hist_w), lambda b, c: (b, 0, 0)),
            pl.BlockSpec((nseq,) + state_shape, lambda b, c: (b,) + zeros)]


def _rwkv(pa, prev8, s0, consts, chunk, nseq):
    B, L, _ = pa.shape
    shared_init = prev8.shape[0] == 1 and B > 1
    state_shape = (H_A // 2, LANE, LANE)
    return pl.pallas_call(
        functools.partial(_rwkv_kernel, chunk=chunk, nseq=nseq, shared_init=shared_init),
        grid=(B // nseq, L // chunk),
        in_specs=[pl.BlockSpec((nseq, chunk, PA_W), lambda b, c: (b, c, 0))]
        + _seq_specs(nseq, shared_init, PA_W, state_shape)
        + [pl.BlockSpec(t.shape, lambda b, c: (0, 0)) for t in consts],
        out_specs=[pl.BlockSpec((nseq, chunk, W_A), lambda b, c: (b, c, 0)),
                   pl.BlockSpec((nseq,) + state_shape, lambda b, c: (b, 0, 0, 0))],
        out_shape=[jax.ShapeDtypeStruct((B, L, W_A), F32),
                   jax.ShapeDtypeStruct((B,) + state_shape, F32)],
        scratch_shapes=[pltpu.VMEM((nseq, SUBLANE, PA_W), F32),
                        pltpu.VMEM((nseq,) + state_shape, F32)],
        compiler_params=pltpu.CompilerParams(dimension_semantics=("arbitrary", "arbitrary"),
                                             vmem_limit_bytes=VMEM_LIMIT),
        name="rwkv_chunk%d" % chunk,
    )(pa, prev8, s0, *consts)


GDN_PREC = dict(cum="xr", tr="xr", kk="b", invm="b", wu="b", qk="b", sread="b", qkv="b", state="b")


def _gdn_kernel(pb_ref, ga_ref, hist_ref, s0_ref, cw_ref, alog_ref, dtb_ref, nw_ref, o_ref, sout_ref,
                xbuf, s_scr, *, chunk, nseq, shared_init):
    c = pl.program_id(1)
    C = chunk
    P = GDN_PREC

    @pl.when(c == 0)
    def _():
        for i in range(nseq):
            j = 0 if shared_init else i
            xbuf[i, 0:SUBLANE, :] = hist_ref[j]
            s_scr[i] = s0_ref[j]

    row = _iota2((C, C), 0)
    col = _iota2((C, C), 1)
    incl = row >= col
    strict = row > col
    tri = incl.astype(F32)
    sel = (_iota2((SUBLANE, LANE), 0) + SMALL_GDN_A == _iota2((SUBLANE, LANE), 1)).astype(F32)

    def prep(i):
        x = pb_ref[i]
        qkv = x[:, 0:3 * W_B]
        full = jnp.concatenate([xbuf[i], qkv], axis=0)
        conv = qkv * cw_ref[CONV_W - 1:CONV_W, :]
        for j in range(1, CONV_W):
            conv = conv + pltpu.roll(full, j, axis=0)[SUBLANE:, :] * cw_ref[CONV_W - 1 - j:CONV_W - j, :]
        xbuf[i] = qkv[C - SUBLANE:, :]
        cs = conv * _sigmoid(conv)
        gates = ga_ref[i]
        glog = -jnp.exp(alog_ref[...]) * _softplus(gates + dtb_ref[...])
        beta = _sigmoid(gates)
        G = _mm(tri, glog, NN, P["cum"])
        Gt = _mm(sel, G, NT, P["tr"])
        return dict(cs=cs, z=x[:, PB_Z:PB_Z + W_B], beta=beta, G=G, Gt=Gt)

    seqs = [prep(i) for i in range(nseq)]

    inst = [(i, h) for i in range(nseq) for h in range(H_B)]
    n = range(len(inst))
    l2n = lambda t, c: t * (lax.rsqrt(jnp.sum(t * t, axis=-1, keepdims=True) + 1e-12) * c)
    head = lambda i, h, part: seqs[i]["cs"][:, part * W_B + h * LANE:part * W_B + (h + 1) * LANE]
    qs = [l2n(head(i, h, 0), DK_B ** -0.5) for i, h in inst]
    ks = [l2n(head(i, h, 1), 1.0) for i, h in inst]
    vs = [head(i, h, 2) for i, h in inst]
    gcols = [seqs[i]["G"][:, SMALL_GDN_A + h:SMALL_GDN_A + h + 1] for i, h in inst]
    grows = [seqs[i]["Gt"][h:h + 1, :] for i, h in inst]
    bcols = [seqs[i]["beta"][:, SMALL_GDN_B + h:SMALL_GDN_B + h + 1] for i, h in inst]
    Ss = [s_scr[i, h] for i, h in inst]

    dmats = [jnp.exp(jnp.where(incl, gcols[j] - grows[j], NEG_BIG)) for j in n]
    kbs = [ks[j] * bcols[j] for j in n]
    lows = [jnp.where(strict, _mm(kbs[j], ks[j], NT, P["kk"]) * dmats[j], 0.0) for j in n]
    ainv = _inv_unit_lower(lows, C, P["invm"])
    egs = [jnp.exp(t) for t in gcols]
    wus = [_mm(ainv[j], jnp.concatenate([kbs[j] * egs[j], vs[j] * bcols[j]], axis=1), NN, P["wu"])
           for j in n]
    qks = [_mm(qs[j], ks[j], NT, P["qk"]) * dmats[j] for j in n]
    srs = [_mm(jnp.concatenate([wus[j][:, :DK_B], qs[j] * egs[j]], axis=0), Ss[j], NN, P["sread"])
           for j in n]
    v_new = [wus[j][:, DK_B:] - srs[j][:C] for j in n]
    glast = [t[C - 1:C, :] for t in gcols]
    s_new = [Ss[j] * jnp.exp(glast[j])
             + _mm(ks[j] * jnp.exp(glast[j] - gcols[j]), v_new[j], TN, P["state"]) for j in n]
    for j, (i, h) in enumerate(inst):
        s_scr[i, h] = s_new[j]
    os_ = [srs[j][C:] + _mm(qks[j], v_new[j], NN, P["qkv"]) for j in n]
    for j, (i, h) in enumerate(inst):
        o = os_[j]
        o = o * lax.rsqrt(jnp.mean(o * o, axis=-1, keepdims=True) + NORM_EPS) * nw_ref[...]
        zh = seqs[i]["z"][:, h * LANE:(h + 1) * LANE]
        o_ref[i, :, h * LANE:(h + 1) * LANE] = o * (zh * _sigmoid(zh))

    @pl.when(c == pl.num_programs(1) - 1)
    def _():
        sout_ref[...] = s_scr[...]


def _gdn(pb, pa, hist8, s0, consts, chunk, nseq):
    B, L, _ = pb.shape
    shared_init = hist8.shape[0] == 1 and B > 1
    state_shape = (H_B, DK_B, DV_B)
    return pl.pallas_call(
        functools.partial(_gdn_kernel, chunk=chunk, nseq=nseq, shared_init=shared_init),
        grid=(B // nseq, L // chunk),
        in_specs=[pl.BlockSpec((nseq, chunk, PB_W), lambda b, c: (b, c, 0)),
                  pl.BlockSpec((nseq, chunk, LANE), lambda b, c: (b, c, PA_SMALL // LANE))]
        + _seq_specs(nseq, shared_init, 3 * W_B, state_shape)
        + [pl.BlockSpec(t.shape, lambda b, c: (0, 0)) for t in consts],
        out_specs=[pl.BlockSpec((nseq, chunk, W_B), lambda b, c: (b, c, 0)),
                   pl.BlockSpec((nseq,) + state_shape, lambda b, c: (b, 0, 0, 0))],
        out_shape=[jax.ShapeDtypeStruct((B, L, W_B), F32),
                   jax.ShapeDtypeStruct((B,) + state_shape, F32)],
        scratch_shapes=[pltpu.VMEM((nseq, SUBLANE, 3 * W_B), F32),
                        pltpu.VMEM((nseq,) + state_shape, F32)],
        compiler_params=pltpu.CompilerParams(dimension_semantics=("arbitrary", "arbitrary"),
                                             vmem_limit_bytes=VMEM_LIMIT),
        name="gdn_chunk%d" % chunk,
    )(pb, pa, hist8, s0, *consts)


def _row_mask(nrow, i):
    return _iota2((nrow, 1), 0) == i


STEP_VECS = ("w", "kk", "kka", "k2", "v", "r", "g", "bon")


def _rwkv_step_prep_kernel(pa_ref, prev_ref, mu_ref, w0_ref, wdu_ref, a0_ref, wau_ref, wgu_ref,
                           kk_ref, ka_ref, rk_ref, vec_ref):
    x = pa_ref[...]
    xm = x + (prev_ref[...] - x) * mu_ref[...]
    r = xm[:, 0:W_A]
    k = xm[:, W_A:2 * W_A]
    v = xm[:, 2 * W_A:3 * W_A]
    sm = xm[:, PA_SMALL:PA_SMALL + LANE]
    gd = xm[:, PA_GATE:PA_GATE + LANE]
    wl = w0_ref[...] + _dot(jnp.tanh(sm).astype(BF16), wdu_ref[...])
    w = jnp.exp(-jnp.exp(-_softplus(-wl) - 0.5))
    a = _sigmoid(a0_ref[...] + _dot(sm.astype(BF16), wau_ref[...]))
    g = _dot(_sigmoid(gd).astype(BF16), wgu_ref[...])
    kkr = k * kk_ref[...]
    k2 = k * (1.0 + (a - 1.0) * ka_ref[...])
    bon = r * k2 * rk_ref[...]
    seg = ((_iota2((LANE, LANE), 0) // N_A) == (_iota2((LANE, LANE), 1) // N_A)).astype(F32)
    ssq = jnp.concatenate([_mm(kkr[:, p * LANE:(p + 1) * LANE] ** 2, seg, NN, RWKV_PREC["seg"])
                           for p in range(H_A // 2)], axis=-1)
    kk = kkr * lax.rsqrt(ssq + 1e-12)
    vecs = dict(w=w, kk=kk, kka=kk * a, k2=k2, v=v, r=r, g=g, bon=bon)
    for j, name in enumerate(STEP_VECS):
        vec_ref[j] = vecs[name].T


def _rwkv_step_state_kernel(vec_ref, s_ref, lnw_ref, lnb_ref, o_ref, sout_ref, o_scr):
    w, kk, kka, k2, v, r, g, bon = [vec_ref[j] for j in range(len(STEP_VECS))]
    for i in range(N_A):
        S = s_ref[0, i]
        sa = jnp.sum(S * kk, axis=0, keepdims=True)
        s_new = S * w - sa * kka + v[i:i + 1, :] * k2
        sout_ref[0, i] = s_new
        o_scr[i:i + 1, :] = jnp.sum(s_new * r, axis=0, keepdims=True)
    o = o_scr[...]
    mean = jnp.mean(o, axis=0, keepdims=True)
    dlt = o - mean
    var = jnp.mean(dlt * dlt, axis=0, keepdims=True)
    on = dlt * lax.rsqrt(var + LNX_EPS) * lnw_ref[...] + lnb_ref[...]
    o_ref[...] = (on + jnp.sum(bon, axis=0, keepdims=True) * v) * g


def _gdn_step_kernel(pb_ref, ga_ref, hist_ref, s_ref, cw_ref, alog_ref, dtb_ref, nw_ref, o_ref, sout_ref):
    ns = pb_ref.shape[0]
    x = pb_ref[...]
    conv = x[:, 0:3 * W_B] * cw_ref[CONV_W - 1:CONV_W, :]
    for j in range(CONV_W - 1):
        conv = conv + hist_ref[j] * cw_ref[j:j + 1, :]
    cs = conv * _sigmoid(conv)
    z = x[:, PB_Z:PB_Z + W_B]
    gates = ga_ref[...]
    eg = jnp.exp(-jnp.exp(alog_ref[...]) * _softplus(gates + dtb_ref[...]))
    beta = _sigmoid(gates)

    l2n = lambda t: t * lax.rsqrt(jnp.sum(t * t, axis=-1, keepdims=True) + 1e-12)
    for h in range(H_B):
        hs = slice(h * LANE, (h + 1) * LANE)
        q_h = l2n(cs[:, hs]) * (DK_B ** -0.5)
        k_h = l2n(cs[:, W_B + h * LANE:W_B + (h + 1) * LANE])
        v_h = cs[:, 2 * W_B + h * LANE:2 * W_B + (h + 1) * LANE]
        eg_h = eg[:, SMALL_GDN_A + h:SMALL_GDN_A + h + 1]
        b_h = beta[:, SMALL_GDN_B + h:SMALL_GDN_B + h + 1]
        seq = range(ns)
        Ss = [s_ref[i, h] for i in seq]
        kss = [_mm(k_h, Ss[i], NN, "xr") for i in seq]
        v_new = [b_h * (v_h - eg_h * kss[i]) for i in seq]
        s_new = [Ss[i] * eg_h[i:i + 1, :] + _mm(jnp.where(_row_mask(ns, i), k_h, 0.0), v_new[i], TN, "b")
                 for i in seq]
        for i in seq:
            sout_ref[i, h] = s_new[i]
        o_all = [_mm(q_h, s_new[i], NN, "b") for i in seq]
        o_h = jnp.zeros((ns, DV_B), F32)
        for i in seq:
            o_h = jnp.where(_row_mask(ns, i), o_all[i], o_h)
        o_h = o_h * lax.rsqrt(jnp.mean(o_h * o_h, axis=-1, keepdims=True) + NORM_EPS) * nw_ref[...]
        zh = z[:, hs]
        o_ref[:, hs] = o_h * (zh * _sigmoid(zh))


def _step_const_specs(ts):
    return [pl.BlockSpec(t.shape, lambda b: (0,) * t.ndim) for t in ts]


def _rwkv_step(pa, prev, s_t, consts):
    B = s_t.shape[-1]
    prep_consts = consts[:-2]
    lnw_col, lnb_col = [c.reshape(W_A, 1) for c in consts[-2:]]
    vecs = pl.pallas_call(
        _rwkv_step_prep_kernel,
        grid=(1,),
        in_specs=[pl.BlockSpec((B, PA_W), lambda b: (0, 0)),
                  pl.BlockSpec((B, PA_W), lambda b: (0, 0))] + _step_const_specs(prep_consts),
        out_specs=pl.BlockSpec((len(STEP_VECS), W_A, B), lambda b: (0, 0, 0)),
        out_shape=jax.ShapeDtypeStruct((len(STEP_VECS), W_A, B), F32),
        compiler_params=pltpu.CompilerParams(dimension_semantics=("arbitrary",),
                                             vmem_limit_bytes=VMEM_LIMIT),
        name="rwkv_step_prep",
    )(pa, prev, *prep_consts)
    state_spec = pl.BlockSpec((1, N_A, N_A, B), lambda h: (h, 0, 0, 0))
    return pl.pallas_call(
        _rwkv_step_state_kernel,
        grid=(H_A,),
        in_specs=[pl.BlockSpec((len(STEP_VECS), N_A, B), lambda h: (0, h, 0)), state_spec,
                  pl.BlockSpec((N_A, 1), lambda h: (h, 0)), pl.BlockSpec((N_A, 1), lambda h: (h, 0))],
        out_specs=[pl.BlockSpec((N_A, B), lambda h: (h, 0)), state_spec],
        out_shape=[jax.ShapeDtypeStruct((W_A, B), F32), jax.ShapeDtypeStruct(s_t.shape, F32)],
        scratch_shapes=[pltpu.VMEM((N_A, B), F32)],
        compiler_params=pltpu.CompilerParams(dimension_semantics=("arbitrary",),
                                             vmem_limit_bytes=VMEM_LIMIT),
        name="rwkv_step_state",
    )(vecs, s_t, lnw_col, lnb_col)


def _gdn_step(pb, pa, hist, s, consts, ns):
    B = s.shape[0]
    return pl.pallas_call(
        _gdn_step_kernel,
        grid=(B // ns,),
        in_specs=[pl.BlockSpec((ns, PB_W), lambda b: (b, 0)),
                  pl.BlockSpec((ns, LANE), lambda b: (b, PA_SMALL // LANE)),
                  pl.BlockSpec((CONV_W - 1, ns, 3 * W_B), lambda b: (0, b, 0)),
                  pl.BlockSpec((ns, H_B, DK_B, DV_B), lambda b: (b, 0, 0, 0))] + _step_const_specs(consts),
        out_specs=[pl.BlockSpec((ns, W_B), lambda b: (b, 0)),
                   pl.BlockSpec((ns, H_B, DK_B, DV_B), lambda b: (b, 0, 0, 0))],
        out_shape=[jax.ShapeDtypeStruct((B, W_B), F32), jax.ShapeDtypeStruct(s.shape, F32)],
        compiler_params=pltpu.CompilerParams(dimension_semantics=("arbitrary",),
                                             vmem_limit_bytes=VMEM_LIMIT),
        name="gdn_step",
    )(pb, pa, hist, s, *consts)


def _pad_cols(t, width):
    return jnp.pad(t, [(0, 0)] * (t.ndim - 1) + [(0, width - t.shape[-1])])


def _pa_layout(t, gdn_gates=None):
    main = t[..., :3 * W_A]
    small = t[..., 3 * W_A:3 * W_A + D_DECAY + D_AAA]
    gate = t[..., 3 * W_A + D_DECAY + D_AAA:]
    if gdn_gates is not None:
        small = jnp.concatenate([small, gdn_gates], axis=-1)
    return jnp.concatenate([main, _pad_cols(small, LANE), _pad_cols(gate, LANE)], axis=-1)


def _pa_unlayout(t):
    return jnp.concatenate([t[..., :3 * W_A], t[..., PA_SMALL:PA_SMALL + D_DECAY + D_AAA],
                            t[..., PA_GATE:PA_GATE + D_GATE]], axis=-1)


def _small_rows(t, first_row):
    return jnp.pad(t, [(first_row, LANE - first_row - t.shape[0]), (0, 0)])


def _small_lanes(t, first_lane):
    return jnp.pad(t, [(0, 0), (first_lane, LANE - first_lane - t.shape[1])])


def _pair_unblock(s):
    B = s.shape[0]
    return jnp.stack([s[:, :, :N_A, :N_A], s[:, :, N_A:, N_A:]], axis=2).reshape(B, H_A, N_A, N_A)


def _history_rows(rows):
    B, n, w = rows.shape
    return jnp.concatenate([jnp.zeros((B, SUBLANE - n, w), F32), rows], axis=1)


def kernel(x_prompt, x_sample, state_rwkv, state_shift, state_gdn, state_conv, meta_tokens,
           g_ffn1, w_gate1, w_up1, w_down1, g_mix, w_in, mu_shift, w0, w_decay_up, a0, w_a_up,
           w_g_up, k_k, k_a, r_k, lnx_w, lnx_b, conv_w, a_log, dt_bias, gdn_norm_w, w_out,
           g_ffn2, w_gate2, w_up2, w_down2, g_final):
    assert g_ffn1.shape[0] == 1, "single trunk layer"
    bp, sp, _ = x_prompt.shape
    bs = x_sample.shape[0]
    assert x_sample.shape[1] == 1 and sp % CHUNK_RWKV == 0 and sp % CHUNK_GDN == 0
    assert (bp * sp) % TM_DENSE == 0 and (bp * sp) % TM_DENSE_OUT == 0
    assert bp % SEQS_PER_STEP == 0 and bs % DEC_SEQS_PER_STEP == 0
    row = lambda t: t.reshape(1, -1).astype(F32)

    ffn1 = (w_gate1[0].astype(BF16), w_up1[0].astype(BF16), w_down1[0].astype(BF16))
    ffn2 = (w_gate2[0].astype(BF16), w_up2[0].astype(BF16), w_down2[0].astype(BF16))
    win_a = _pa_layout(w_in[0][:, :N_A_IN], w_in[0][:, N_A_IN + PB_W:]).astype(BF16)
    win_b = w_in[0][:, N_A_IN:N_A_IN + PB_W].astype(BF16)
    wo_a = w_out[0][:W_A].astype(BF16)
    wo_b = w_out[0][W_A:].astype(BF16)
    dense_in_consts = (row(g_ffn1[0]), *ffn1, row(g_mix[0]), win_a, win_b)
    dense_out_consts = (wo_a, wo_b, row(g_ffn2[0]), *ffn2, row(g_final))
    rwkv_consts = (_pa_layout(row(mu_shift[0])), row(w0[0]),
                   _small_rows(w_decay_up[0], 0).astype(BF16), row(a0[0]),
                   _small_rows(w_a_up[0], SMALL_AAA).astype(BF16), _small_rows(w_g_up[0], 0).astype(BF16),
                   row(k_k[0]), row(k_a[0]), row(r_k[0]), row(lnx_w[0]), row(lnx_b[0]))
    gdn_consts = (conv_w[0].astype(F32), _small_lanes(row(a_log[0]), SMALL_GDN_A),
                  _small_lanes(row(dt_bias[0]), SMALL_GDN_A), row(gdn_norm_w[0]))

    xs = jnp.concatenate([x_sample[:, 0, :].astype(F32), meta_tokens.astype(F32)], axis=0)
    hs, pas, pbs = _dense_in(xs, *dense_in_consts, tm=xs.shape[0])

    pa_meta = pas[bs:][None]
    pb_meta = pbs[bs:][None]
    _, rw_meta = _rwkv(pa_meta, jnp.zeros((1, SUBLANE, PA_W), F32),
                       jnp.zeros((1, H_A // 2, LANE, LANE), F32), rwkv_consts, N_META, 1)
    _, gd_meta = _gdn(pb_meta, pa_meta, jnp.zeros((1, SUBLANE, 3 * W_B), F32),
                      jnp.zeros((1, H_B, DK_B, DV_B), F32), gdn_consts, N_META, 1)

    hp, pap, pbp = _dense_in(x_prompt.reshape(bp * sp, D_MODEL).astype(F32), *dense_in_consts,
                             tm=TM_DENSE)
    pap3 = pap.reshape(bp, sp, PA_W)
    pbp3 = pbp.reshape(bp, sp, PB_W)
    oa_p, rw_p = _rwkv(pap3, _history_rows(pa_meta[:, -1:, :]), rw_meta, rwkv_consts,
                       CHUNK_RWKV, SEQS_PER_STEP)
    ob_p, gd_p = _gdn(pbp3, pap3, _history_rows(pb_meta[:, -(CONV_W - 1):, :3 * W_B]), gd_meta, gdn_consts,
                      CHUNK_GDN, SEQS_PER_STEP)
    y_p = _dense_out(hp, oa_p.reshape(bp * sp, W_A), ob_p.reshape(bp * sp, W_B), *dense_out_consts,
                     tm=TM_DENSE_OUT)

    oa_t, rw_t = _rwkv_step(pas, _pa_layout(state_shift[0].astype(F32)),
                            jnp.transpose(state_rwkv[0].astype(F32), (1, 2, 3, 0)), rwkv_consts)
    oa_s = oa_t.T
    rw_s = jnp.transpose(rw_t, (3, 0, 1, 2))
    ob_s, gd_s = _gdn_step(pbs, pas, jnp.swapaxes(state_conv[0].astype(F32), 0, 1), state_gdn[0].astype(F32),
                           gdn_consts, DEC_SEQS_PER_STEP)
    y_s = _dense_out(hs[:bs], oa_s, ob_s, *dense_out_consts, tm=bs)

    new_conv_s = jnp.concatenate([state_conv[0].astype(F32)[:, 1:, :], pbs[:bs, None, :3 * W_B]], axis=1)
    return (y_p.reshape(bp, sp, D_MODEL).astype(x_prompt.dtype),
            y_s.reshape(bs, 1, D_MODEL).astype(x_sample.dtype),
            _pair_unblock(rw_p)[None],
            _pa_unlayout(pap3[:, -1, :])[None],
            gd_p[None],
            pbp3[:, -(CONV_W - 1):, :3 * W_B][None],
            rw_s[None],
            _pa_unlayout(pas[:bs])[None],
            gd_s[None],
            new_conv_s[None])
```

```python
import functools
import math

import jax
import jax.numpy as jnp
from jax import lax
from jax.experimental import pallas as pl
from jax.experimental.pallas import tpu as pltpu

F32 = jnp.float32
BF16 = jnp.bfloat16

LANE = 128
SUBLANE = 8
VMEM_LIMIT = 56 * 1024 * 1024

D_MODEL = 1024
D_FF = 2816
N_META = 16
H_A, N_A = 8, 64
W_A = H_A * N_A
D_DECAY, D_AAA, D_GATE = 32, 32, 96
N_A_IN = 3 * W_A + D_DECAY + D_AAA + D_GATE
H_B, DK_B, DV_B = 4, 128, 128
W_B = H_B * DV_B
CONV_W = 4
LNX_EPS = 64e-5
NORM_EPS = 1e-6
NEG_BIG = -1e30
DECAY_SCALE = math.exp(-0.5)

PA_W = 3 * W_A + 2 * LANE
PA_SMALL = 3 * W_A
PA_GATE = PA_SMALL + LANE
SMALL_AAA = D_DECAY
SMALL_GDN_A = D_DECAY + D_AAA
SMALL_GDN_B = SMALL_GDN_A + H_B
PB_W = 4 * W_B
PB_Z = 3 * W_B

TM_DENSE = 512
TM_DENSE_OUT = 512
CHUNK_RWKV = 64
CHUNK_GDN = 128
SEQS_PER_STEP = 4
DEC_SEQS_PER_STEP = 2 * SUBLANE

NN = (((1,), (0,)), ((), ()))
NT = (((1,), (1,)), ((), ()))
TN = (((0,), (0,)), ((), ()))


def _dot(a, b):
    return jnp.dot(a, b, preferred_element_type=F32)


def _sigmoid(x):
    return 1.0 / (1.0 + jnp.exp(-x))


def _softplus(x):
    return jnp.maximum(x, 0.0) + jnp.log(1.0 + jnp.exp(-jnp.abs(x)))


def _rms(x, g):
    return x * lax.rsqrt(jnp.mean(x * x, axis=-1, keepdims=True) + NORM_EPS) * g


def _iota2(shape, dim):
    return lax.broadcasted_iota(jnp.int32, shape, dim)


def _split(x):
    hi = x.astype(BF16)
    return hi, (x - hi.astype(F32)).astype(BF16)


def _dg(a, b, dims):
    return lax.dot_general(a, b, dims, preferred_element_type=F32)


def _mm(a, b, dims=NN, mode="b"):
    if mode == "b":
        return _dg(a.astype(BF16), b.astype(BF16), dims)
    if mode == "xl":
        ah, al = _split(a)
        bh = b.astype(BF16)
        return _dg(ah, bh, dims) + _dg(al, bh, dims)
    if mode == "xr":
        ah = a.astype(BF16)
        bh, bl = _split(b)
        return _dg(ah, bh, dims) + _dg(ah, bl, dims)
    assert mode == "x3"
    ah, al = _split(a)
    bh, bl = _split(b)
    return _dg(ah, bh, dims) + (_dg(ah, bl, dims) + _dg(al, bh, dims))


def _inv_unit_lower(lows, nil, merge_mode, expand=None):
    shape = lows[0].shape
    row = _iota2(shape, 0)
    col = _iota2(shape, 1) % nil
    same_block = lambda s: (row // s) == (col // s)
    eye = (row == col).astype(F32)
    if expand is None:
        expand = lambda t: t
    narrow = (lambda t: t.astype(BF16)) if merge_mode == "b" else (lambda t: t)
    s = 2
    in_base = same_block(s)
    invs = [eye - jnp.where(in_base, low, 0.0) for low in lows]
    while s < nil:
        newly = jnp.logical_and(same_block(2 * s), jnp.logical_not(same_block(s)))
        ts = [_mm(inv, expand(narrow(jnp.where(newly, low, 0.0))), NN, merge_mode)
              for inv, low in zip(invs, lows)]
        invs = [inv - _mm(t, expand(narrow(inv)), NN, merge_mode) for inv, t in zip(invs, ts)]
        s *= 2
    return invs


FF_SPLIT = (0, 1536, D_FF)


def _swiglu(n, wg_ref, wu_ref, wd_ref):
    out = None
    for lo, hi in zip(FF_SPLIT[:-1], FF_SPLIT[1:]):
        gate = _dot(n, wg_ref[:, lo:hi])
        up = _dot(n, wu_ref[:, lo:hi])
        act = (gate * _sigmoid(gate) * up).astype(BF16)
        part = _dot(act, wd_ref[lo:hi, :])
        out = part if out is None else out + part
    return out


def _dense_in_kernel(x_ref, g1_ref, wg_ref, wu_ref, wd_ref, gm_ref, wa_ref, wb_ref,
                     h_ref, pa_ref, pb_ref):
    x = x_ref[...]
    h = x + 0.5 * _swiglu(_rms(x, g1_ref[...]).astype(BF16), wg_ref, wu_ref, wd_ref)
    h_ref[...] = h
    n = _rms(h, gm_ref[...]).astype(BF16)
    pa_ref[...] = _dot(n, wa_ref[...])
    pb_ref[...] = _dot(n, wb_ref[...])


def _dense_out_kernel(h_ref, oa_ref, ob_ref, woa_ref, wob_ref, g2_ref, wg_ref, wu_ref, wd_ref,
                      gf_ref, y_ref):
    h = (h_ref[...] + _dot(oa_ref[...].astype(BF16), woa_ref[...])
         + _dot(ob_ref[...].astype(BF16), wob_ref[...]))
    h = h + 0.5 * _swiglu(_rms(h, g2_ref[...]).astype(BF16), wg_ref, wu_ref, wd_ref)
    y_ref[...] = _rms(h, gf_ref[...])


def _const_spec(shape):
    return pl.BlockSpec(shape, lambda *_: (0,) * len(shape), pipeline_mode=pl.Buffered(1))


def _row_spec(tm, width):
    return pl.BlockSpec((tm, width), lambda i: (i, 0))


def _dense_in(x, g1, wg, wu, wd, gm, wa, wb, tm):
    n = x.shape[0]
    consts = (g1, wg, wu, wd, gm, wa, wb)
    return pl.pallas_call(
        _dense_in_kernel,
        grid=(n // tm,),
        in_specs=[_row_spec(tm, D_MODEL)] + [_const_spec(c.shape) for c in consts],
        out_specs=[_row_spec(tm, D_MODEL), _row_spec(tm, PA_W), _row_spec(tm, PB_W)],
        out_shape=[jax.ShapeDtypeStruct((n, D_MODEL), F32),
                   jax.ShapeDtypeStruct((n, PA_W), F32),
                   jax.ShapeDtypeStruct((n, PB_W), F32)],
        compiler_params=pltpu.CompilerParams(dimension_semantics=("arbitrary",),
                                             vmem_limit_bytes=VMEM_LIMIT),
        name="dense_in",
    )(x, *consts)


def _dense_out(h, oa, ob, woa, wob, g2, wg, wu, wd, gf, tm):
    n = h.shape[0]
    consts = (woa, wob, g2, wg, wu, wd, gf)
    return pl.pallas_call(
        _dense_out_kernel,
        grid=(n // tm,),
        in_specs=[_row_spec(tm, D_MODEL), _row_spec(tm, W_A), _row_spec(tm, W_B)]
        + [_const_spec(c.shape) for c in consts],
        out_specs=_row_spec(tm, D_MODEL),
        out_shape=jax.ShapeDtypeStruct((n, D_MODEL), F32),
        compiler_params=pltpu.CompilerParams(dimension_semantics=("arbitrary",),
                                             vmem_limit_bytes=VMEM_LIMIT),
        name="dense_out",
    )(h, oa, ob, *consts)


RWKV_PREC = dict(cum="b", seg="b", pair="b", invm="b", sread="b", akv="b", solve="b", inter="b",
                 state="x3")


def _rwkv_kernel(pa_ref, prev_ref, s0_ref, mu_ref, w0_ref, wdu_ref, a0_ref, wau_ref, wgu_ref,
                 kk_ref, ka_ref, rk_ref, lnw_ref, lnb_ref, o_ref, sout_ref, xbuf, s_scr,
                 *, chunk, nseq, shared_init):
    c = pl.program_id(1)
    C = chunk
    P = RWKV_PREC

    @pl.when(c == 0)
    def _():
        for i in range(nseq):
            j = 0 if shared_init else i
            xbuf[i, 0:SUBLANE, :] = prev_ref[j]
            s_scr[i] = s0_ref[j]

    tri = (_iota2((C, C), 0) >= _iota2((C, C), 1)).astype(F32)
    mid = C // 2 - 1

    def prep(i):
        x = pa_ref[i]
        full = jnp.concatenate([xbuf[i], x], axis=0)
        prev = pltpu.roll(full, 1, axis=0)[SUBLANE:, :]
        xm = x + (prev - x) * mu_ref[...]
        xbuf[i] = x[C - SUBLANE:, :]

        r = xm[:, 0:W_A]
        k = xm[:, W_A:2 * W_A]
        v = xm[:, 2 * W_A:3 * W_A]
        sm = xm[:, PA_SMALL:PA_SMALL + LANE]
        gd = xm[:, PA_GATE:PA_GATE + LANE]

        wl = w0_ref[...] + _dot(jnp.tanh(sm).astype(BF16), wdu_ref[...])
        lw = -DECAY_SCALE * _sigmoid(wl)
        a = _sigmoid(a0_ref[...] + _dot(sm.astype(BF16), wau_ref[...]))
        g = _dot(_sigmoid(gd).astype(BF16), wgu_ref[...])
        kkr = k * kk_ref[...]
        k2 = k * (1.0 + (a - 1.0) * ka_ref[...])
        G = _mm(tri, lw, NN, P["cum"])
        return dict(r=r, v=v, a=a, g=g, kkr=kkr, k2=k2, G=G, Gx=G - lw, Gm=G[mid:mid + 1, :],
                    Gc=G[C - 1:C, :])

    seqs = [prep(i) for i in range(nseq)]

    C2 = 2 * C
    trow = _iota2((C, C2), 0)
    tcol = _iota2((C, C2), 1) % C
    incl = trow >= tcol
    strict = trow > tcol
    lane = _iota2((1, LANE), 1)
    m0 = (lane < N_A).astype(F32)
    m1 = (lane >= N_A).astype(F32)
    col2 = _iota2((1, C2), 1)
    c0 = (col2 < C).astype(F32)
    c1 = (col2 >= C).astype(F32)
    seg = ((_iota2((LANE, LANE), 0) // N_A) == (_iota2((LANE, LANE), 1) // N_A)).astype(F32)
    stack = lambda t: jnp.concatenate([t * m0.astype(t.dtype), t * m1.astype(t.dtype)], axis=0)
    expand = lambda t: jnp.concatenate([t * c0.astype(t.dtype), t * c1.astype(t.dtype)], axis=0)
    narrow = lambda t, site: t.astype(BF16) if P[site] == "b" else t
    merged = C2 % LANE == 0

    inst = [(i, p) for i in range(nseq) for p in range(H_A // 2)]
    n = range(len(inst))
    sls = [slice(p * LANE, (p + 1) * LANE) for _, p in inst]
    take = lambda name: [seqs[i][name][:, sls[j]] for j, (i, _) in enumerate(inst)]
    rs, vs_, k2s, kkrs, as_, gs = take("r"), take("v"), take("k2"), take("kkr"), take("a"), take("g")
    Gs, Gxs, Gms, Gcs = take("G"), take("Gx"), take("Gm"), take("Gc")
    Ss = [s_scr[i, p] for i, p in inst]

    ssq = [_mm(t * t, seg, NN, P["seg"]) for t in kkrs]
    kks = [t * lax.rsqrt(q + 1e-12) for t, q in zip(kkrs, ssq)]
    kkas = [kks[j] * as_[j] for j in n]
    inv_rel = [jnp.exp(Gms[j] - Gs[j]) for j in n]
    dec = [jnp.exp(Gcs[j] - Gs[j]) for j in n]
    lhs = [jnp.concatenate([kks[j] * jnp.exp(Gxs[j] - Gms[j]), rs[j] * jnp.exp(Gs[j] - Gms[j])], axis=0)
           for j in n]
    ais = [stack(narrow(kkas[j] * inv_rel[j], "pair")) for j in n]
    kis = [stack(narrow(k2s[j] * inv_rel[j], "pair")) for j in n]
    if merged:
        pm = [_mm(lhs[j], jnp.concatenate([ais[j], kis[j]], axis=0), NT, P["pair"]) for j in n]
        pas_, pks_ = [t[:, :C2] for t in pm], [t[:, C2:] for t in pm]
    else:
        pas_ = [_mm(lhs[j], ais[j], NT, P["pair"]) for j in n]
        pks_ = [_mm(lhs[j], kis[j], NT, P["pair"]) for j in n]
    a_aa = [jnp.where(strict, t[:C], 0.0) for t in pas_]
    a_ak = [jnp.where(strict, t[:C], 0.0) for t in pks_]
    a_ra = [jnp.where(incl, t[C:], 0.0) for t in pas_]
    a_rk = [jnp.where(incl, t[C:], 0.0) for t in pks_]
    ainv = _inv_unit_lower(a_aa, C, P["invm"], expand)

    assert P["akv"] == "b" and P["inter"] == "b" and P["state"] == "x3"
    v_hl = [_split(t) for t in vs_]
    vst = [stack(hl[0]) for hl in v_hl]
    akv = [_mm(a_ak[j], vst[j], NN, "b") for j in n]
    sx = [_mm(jnp.concatenate([kks[j] * jnp.exp(Gxs[j]), rs[j] * jnp.exp(Gs[j])], axis=0), Ss[j], NT,
              P["sread"]) for j in n]
    u_hl = [_split(-_mm(ainv[j], stack(narrow(sx[j][:C] + akv[j], "solve")), NN, P["solve"])) for j in n]
    ust = [stack(hl[0]) for hl in u_hl]
    cat2 = lambda a, b, q: jnp.concatenate([a[q], b[q]], axis=0)
    ad_hl = [_split(kkas[j] * dec[j]) for j in n]
    kd_hl = [_split(k2s[j] * dec[j]) for j in n]
    s_new = [Ss[j] * jnp.exp(Gcs[j])
             + seg * (_dg(cat2(u_hl[j], v_hl[j], 0), cat2(ad_hl[j], kd_hl[j], 0), TN)
                      + (_dg(cat2(u_hl[j], v_hl[j], 0), cat2(ad_hl[j], kd_hl[j], 1), TN)
                         + _dg(cat2(u_hl[j], v_hl[j], 1), cat2(ad_hl[j], kd_hl[j], 0), TN)))
             for j in n]
    for j, (i, p) in enumerate(inst):
        s_scr[i, p] = s_new[j]
    if merged:
        inter = [_mm(jnp.concatenate([a_ra[j], a_rk[j]], axis=1),
                     jnp.concatenate([ust[j], vst[j]], axis=0), NN, "b") for j in n]
    else:
        inter = [_mm(a_ra[j], ust[j], NN, "b") + _mm(a_rk[j], vst[j], NN, "b") for j in n]
    Os = [sx[j][C:] + inter[j] for j in n]
    means = [_mm(t, seg, NN, P["seg"]) * (1.0 / N_A) for t in Os]
    dlts = [t - m for t, m in zip(Os, means)]
    vars_ = [_mm(t * t, seg, NN, P["seg"]) * (1.0 / N_A) for t in dlts]
    bonus = [_mm(rs[j] * k2s[j] * rk_ref[:, sls[j]], seg, NN, P["seg"]) * vs_[j] for j in n]
    for j, (i, p) in enumerate(inst):
        on = dlts[j] * lax.rsqrt(vars_[j] + LNX_EPS) * lnw_ref[:, sls[j]] + lnb_ref[:, sls[j]]
        o_ref[i, :, sls[j]] = (on + bonus[j]) * gs[j]

    @pl.when(c == pl.num_programs(1) - 1)
    def _():
        sout_ref[...] = s_scr[...]


def _seq_specs(nseq, shared_init, hist_w, state_shape):
    zeros = (0,) * len(state_shape)
    if shared_init:
        return [pl.BlockSpec((1, SUBLANE, hist_w), lambda b, c: (0, 0, 0)),
                pl.BlockSpec((1,) + state_shape, lambda b, c: (0,) + zeros)]
    return [pl.BlockSpec((nseq, SUBLANE, hist_w), lambda b, c: (b, 0, 0)),
            pl.BlockSpec((nseq,) + state_shape, lambda b, c: (b,) + zeros)]


def _rwkv(pa, prev8, s0, consts, chunk, nseq):
    B, L, _ = pa.shape
    shared_init = prev8.shape[0] == 1 and B > 1
    state_shape = (H_A // 2, LANE, LANE)
    return pl.pallas_call(
        functools.partial(_rwkv_kernel, chunk=chunk, nseq=nseq, shared_init=shared_init),
        grid=(B // nseq, L // chunk),
        in_specs=[pl.BlockSpec((nseq, chunk, PA_W), lambda b, c: (b, c, 0))]
        + _seq_specs(nseq, shared_init, PA_W, state_shape)
        + [pl.BlockSpec(t.shape, lambda b, c: (0, 0)) for t in consts],
        out_specs=[pl.BlockSpec((nseq, chunk, W_A), lambda b, c: (b, c, 0)),
                   pl.BlockSpec((nseq,) + state_shape, lambda b, c: (b, 0, 0, 0))],
        out_shape=[jax.ShapeDtypeStruct((B, L, W_A), F32),
                   jax.ShapeDtypeStruct((B,) + state_shape, F32)],
        scratch_shapes=[pltpu.VMEM((nseq, SUBLANE, PA_W), F32),
                        pltpu.VMEM((nseq,) + state_shape, F32)],
        compiler_params=pltpu.CompilerParams(dimension_semantics=("arbitrary", "arbitrary"),
                                             vmem_limit_bytes=VMEM_LIMIT),
        name="rwkv_chunk%d" % chunk,
    )(pa, prev8, s0, *consts)


GDN_PREC = dict(cum="xr", tr="xr", kk="b", invm="b", wu="b", qk="b", sread="b", qkv="b", state="b")


def _gdn_kernel(pb_ref, ga_ref, hist_ref, s0_ref, cw_ref, alog_ref, dtb_ref, nw_ref, o_ref, sout_ref,
                xbuf, s_scr, *, chunk, nseq, shared_init):
    c = pl.program_id(1)
    C = chunk
    P = GDN_PREC

    @pl.when(c == 0)
    def _():
        for i in range(nseq):
            j = 0 if shared_init else i
            xbuf[i, 0:SUBLANE, :] = hist_ref[j]
            s_scr[i] = s0_ref[j]

    row = _iota2((C, C), 0)
    col = _iota2((C, C), 1)
    incl = row >= col
    strict = row > col
    tri = incl.astype(F32)
    sel = (_iota2((SUBLANE, LANE), 0) + SMALL_GDN_A == _iota2((SUBLANE, LANE), 1)).astype(F32)

    def prep(i):
        x = pb_ref[i]
        qkv = x[:, 0:3 * W_B]
        full = jnp.concatenate([xbuf[i], qkv], axis=0)
        conv = qkv * cw_ref[CONV_W - 1:CONV_W, :]
        for j in range(1, CONV_W):
            conv = conv + pltpu.roll(full, j, axis=0)[SUBLANE:, :] * cw_ref[CONV_W - 1 - j:CONV_W - j, :]
        xbuf[i] = qkv[C - SUBLANE:, :]
        cs = conv * _sigmoid(conv)
        gates = ga_ref[i]
        glog = -jnp.exp(alog_ref[...]) * _softplus(gates + dtb_ref[...])
        beta = _sigmoid(gates)
        G = _mm(tri, glog, NN, P["cum"])
        Gt = _mm(sel, G, NT, P["tr"])
        return dict(cs=cs, z=x[:, PB_Z:PB_Z + W_B], beta=beta, G=G, Gt=Gt)

    seqs = [prep(i) for i in range(nseq)]

    inst = [(i, h) for i in range(nseq) for h in range(H_B)]
    n = range(len(inst))
    l2n = lambda t, c: t * (lax.rsqrt(jnp.sum(t * t, axis=-1, keepdims=True) + 1e-12) * c)
    head = lambda i, h, part: seqs[i]["cs"][:, part * W_B + h * LANE:part * W_B + (h + 1) * LANE]
    qs = [l2n(head(i, h, 0), DK_B ** -0.5) for i, h in inst]
    ks = [l2n(head(i, h, 1), 1.0) for i, h in inst]
    vs = [head(i, h, 2) for i, h in inst]
    gcols = [seqs[i]["G"][:, SMALL_GDN_A + h:SMALL_GDN_A + h + 1] for i, h in inst]
    grows = [seqs[i]["Gt"][h:h + 1, :] for i, h in inst]
    bcols = [seqs[i]["beta"][:, SMALL_GDN_B + h:SMALL_GDN_B + h + 1] for i, h in inst]
    Ss = [s_scr[i, h] for i, h in inst]

    dmats = [jnp.exp(jnp.where(incl, gcols[j] - grows[j], NEG_BIG)) for j in n]
    kbs = [ks[j] * bcols[j] for j in n]
    lows = [jnp.where(strict, _mm(kbs[j], ks[j], NT, P["kk"]) * dmats[j], 0.0) for j in n]
    ainv = _inv_unit_lower(lows, C, P["invm"])
    egs = [jnp.exp(t) for t in gcols]
    wus = [_mm(ainv[j], jnp.concatenate([kbs[j] * egs[j], vs[j] * bcols[j]], axis=1), NN, P["wu"])
           for j in n]
    qks = [_mm(qs[j], ks[j], NT, P["qk"]) * dmats[j] for j in n]
    srs = [_mm(jnp.concatenate([wus[j][:, :DK_B], qs[j] * egs[j]], axis=0), Ss[j], NN, P["sread"])
           for j in n]
    v_new = [wus[j][:, DK_B:] - srs[j][:C] for j in n]
    glast = [t[C - 1:C, :] for t in gcols]
    s_new = [Ss[j] * jnp.exp(glast[j])
             + _mm(ks[j] * jnp.exp(glast[j] - gcols[j]), v_new[j], TN, P["state"]) for j in n]
    for j, (i, h) in enumerate(inst):
        s_scr[i, h] = s_new[j]
    os_ = [srs[j][C:] + _mm(qks[j], v_new[j], NN, P["qkv"]) for j in n]
    for j, (i, h) in enumerate(inst):
        o = os_[j]
        o = o * lax.rsqrt(jnp.mean(o * o, axis=-1, keepdims=True) + NORM_EPS) * nw_ref[...]
        zh = seqs[i]["z"][:, h * LANE:(h + 1) * LANE]
        o_ref[i, :, h * LANE:(h + 1) * LANE] = o * (zh * _sigmoid(zh))

    @pl.when(c == pl.num_programs(1) - 1)
    def _():
        sout_ref[...] = s_scr[...]


def _gdn(pb, pa, hist8, s0, consts, chunk, nseq):
    B, L, _ = pb.shape
    shared_init = hist8.shape[0] == 1 and B > 1
    state_shape = (H_B, DK_B, DV_B)
    return pl.pallas_call(
        functools.partial(_gdn_kernel, chunk=chunk, nseq=nseq, shared_init=shared_init),
        grid=(B // nseq, L // chunk),
        in_specs=[pl.BlockSpec((nseq, chunk, PB_W), lambda b, c: (b, c, 0)),
                  pl.BlockSpec((nseq, chunk, LANE), lambda b, c: (b, c, PA_SMALL // LANE))]
        + _seq_specs(nseq, shared_init, 3 * W_B, state_shape)
        + [pl.BlockSpec(t.shape, lambda b, c: (0, 0)) for t in consts],
        out_specs=[pl.BlockSpec((nseq, chunk, W_B), lambda b, c: (b, c, 0)),
                   pl.BlockSpec((nseq,) + state_shape, lambda b, c: (b, 0, 0, 0))],
        out_shape=[jax.ShapeDtypeStruct((B, L, W_B), F32),
                   jax.ShapeDtypeStruct((B,) + state_shape, F32)],
        scratch_shapes=[pltpu.VMEM((nseq, SUBLANE, 3 * W_B), F32),
                        pltpu.VMEM((nseq,) + state_shape, F32)],
        compiler_params=pltpu.CompilerParams(dimension_semantics=("arbitrary", "arbitrary"),
                                             vmem_limit_bytes=VMEM_LIMIT),
        name="gdn_chunk%d" % chunk,
    )(pb, pa, hist8, s0, *consts)


def _row_mask(nrow, i):
    return _iota2((nrow, 1), 0) == i


STEP_VECS = ("w", "kk", "kka", "k2", "v", "r", "g", "bon")


def _rwkv_step_prep_kernel(pa_ref, prev_ref, mu_ref, w0_ref, wdu_ref, a0_ref, wau_ref, wgu_ref,
                           kk_ref, ka_ref, rk_ref, vec_ref):
    x = pa_ref[...]
    xm = x + (prev_ref[...] - x) * mu_ref[...]
    r = xm[:, 0:W_A]
    k = xm[:, W_A:2 * W_A]
    v = xm[:, 2 * W_A:3 * W_A]
    sm = xm[:, PA_SMALL:PA_SMALL + LANE]
    gd = xm[:, PA_GATE:PA_GATE + LANE]
    wl = w0_ref[...] + _dot(jnp.tanh(sm).astype(BF16), wdu_ref[...])
    w = jnp.exp(-DECAY_SCALE * _sigmoid(wl))
    a = _sigmoid(a0_ref[...] + _dot(sm.astype(BF16), wau_ref[...]))
    g = _dot(_sigmoid(gd).astype(BF16), wgu_ref[...])
    kkr = k * kk_ref[...]
    k2 = k * (1.0 + (a - 1.0) * ka_ref[...])
    bon = r * k2 * rk_ref[...]
    seg = ((_iota2((LANE, LANE), 0) // N_A) == (_iota2((LANE, LANE), 1) // N_A)).astype(F32)
    ssq = jnp.concatenate([_mm(kkr[:, p * LANE:(p + 1) * LANE] ** 2, seg, NN, RWKV_PREC["seg"])
                           for p in range(H_A // 2)], axis=-1)
    kk = kkr * lax.rsqrt(ssq + 1e-12)
    vecs = dict(w=w, kk=kk, kka=kk * a, k2=k2, v=v, r=r, g=g, bon=bon)
    for j, name in enumerate(STEP_VECS):
        vec_ref[j] = vecs[name].T


def _rwkv_step_state_kernel(vec_ref, s_ref, lnw_ref, lnb_ref, o_ref, sout_ref, o_scr):
    w, kk, kka, k2, v, r, g, bon = [vec_ref[j] for j in range(len(STEP_VECS))]
    for i in range(N_A):
        S = s_ref[0, i]
        sa = jnp.sum(S * kk, axis=0, keepdims=True)
        s_new = S * w - sa * kka + v[i:i + 1, :] * k2
        sout_ref[0, i] = s_new
        o_scr[i:i + 1, :] = jnp.sum(s_new * r, axis=0, keepdims=True)
    o = o_scr[...]
    mean = jnp.mean(o, axis=0, keepdims=True)
    dlt = o - mean
    var = jnp.mean(dlt * dlt, axis=0, keepdims=True)
    on = dlt * lax.rsqrt(var + LNX_EPS) * lnw_ref[...] + lnb_ref[...]
    o_ref[...] = (on + jnp.sum(bon, axis=0, keepdims=True) * v) * g


def _gdn_step_kernel(pb_ref, ga_ref, hist_ref, s_ref, cw_ref, alog_ref, dtb_ref, nw_ref, o_ref, sout_ref):
    ns = pb_ref.shape[0]
    x = pb_ref[...]
    conv = x[:, 0:3 * W_B] * cw_ref[CONV_W - 1:CONV_W, :]
    for j in range(CONV_W - 1):
        conv = conv + hist_ref[j] * cw_ref[j:j + 1, :]
    cs = conv * _sigmoid(conv)
    z = x[:, PB_Z:PB_Z + W_B]
    gates = ga_ref[...]
    eg = jnp.exp(-jnp.exp(alog_ref[...]) * _softplus(gates + dtb_ref[...]))
    beta = _sigmoid(gates)

    l2n = lambda t: t * lax.rsqrt(jnp.sum(t * t, axis=-1, keepdims=True) + 1e-12)
    for h in range(H_B):
        hs = slice(h * LANE, (h + 1) * LANE)
        q_h = l2n(cs[:, hs]) * (DK_B ** -0.5)
        k_h = l2n(cs[:, W_B + h * LANE:W_B + (h + 1) * LANE])
        v_h = cs[:, 2 * W_B + h * LANE:2 * W_B + (h + 1) * LANE]
        eg_h = eg[:, SMALL_GDN_A + h:SMALL_GDN_A + h + 1]
        b_h = beta[:, SMALL_GDN_B + h:SMALL_GDN_B + h + 1]
        seq = range(ns)
        Ss = [s_ref[i, h] for i in seq]
        kss = [_mm(k_h, Ss[i], NN, "xr") for i in seq]
        v_new = [b_h * (v_h - eg_h * kss[i]) for i in seq]
        s_new = [Ss[i] * eg_h[i:i + 1, :] + _mm(jnp.where(_row_mask(ns, i), k_h, 0.0), v_new[i], TN, "b")
                 for i in seq]
        for i in seq:
            sout_ref[i, h] = s_new[i]
        o_all = [_mm(q_h, s_new[i], NN, "b") for i in seq]
        o_h = jnp.zeros((ns, DV_B), F32)
        for i in seq:
            o_h = jnp.where(_row_mask(ns, i), o_all[i], o_h)
        o_h = o_h * lax.rsqrt(jnp.mean(o_h * o_h, axis=-1, keepdims=True) + NORM_EPS) * nw_ref[...]
        zh = z[:, hs]
        o_ref[:, hs] = o_h * (zh * _sigmoid(zh))


def _step_const_specs(ts):
    return [pl.BlockSpec(t.shape, lambda b: (0,) * t.ndim) for t in ts]


def _rwkv_step(pa, prev, s_t, consts):
    B = s_t.shape[-1]
    prep_consts = consts[:-2]
    lnw_col, lnb_col = [c.reshape(W_A, 1) for c in consts[-2:]]
    vecs = pl.pallas_call(
        _rwkv_step_prep_kernel,
        grid=(1,),
        in_specs=[pl.BlockSpec((B, PA_W), lambda b: (0, 0)),
                  pl.BlockSpec((B, PA_W), lambda b: (0, 0))] + _step_const_specs(prep_consts),
        out_specs=pl.BlockSpec((len(STEP_VECS), W_A, B), lambda b: (0, 0, 0)),
        out_shape=jax.ShapeDtypeStruct((len(STEP_VECS), W_A, B), F32),
        compiler_params=pltpu.CompilerParams(dimension_semantics=("arbitrary",),
                                             vmem_limit_bytes=VMEM_LIMIT),
        name="rwkv_step_prep",
    )(pa, prev, *prep_consts)
    state_spec = pl.BlockSpec((1, N_A, N_A, B), lambda h: (h, 0, 0, 0))
    return pl.pallas_call(
        _rwkv_step_state_kernel,
        grid=(H_A,),
        in_specs=[pl.BlockSpec((len(STEP_VECS), N_A, B), lambda h: (0, h, 0)), state_spec,
                  pl.BlockSpec((N_A, 1), lambda h: (h, 0)), pl.BlockSpec((N_A, 1), lambda h: (h, 0))],
        out_specs=[pl.BlockSpec((N_A, B), lambda h: (h, 0)), state_spec],
        out_shape=[jax.ShapeDtypeStruct((W_A, B), F32), jax.ShapeDtypeStruct(s_t.shape, F32)],
        scratch_shapes=[pltpu.VMEM((N_A, B), F32)],
        compiler_params=pltpu.CompilerParams(dimension_semantics=("arbitrary",),
                                             vmem_limit_bytes=VMEM_LIMIT),
        name="rwkv_step_state",
    )(vecs, s_t, lnw_col, lnb_col)


def _gdn_step(pb, pa, hist, s, consts, ns):
    B = s.shape[0]
    return pl.pallas_call(
        _gdn_step_kernel,
        grid=(B // ns,),
        in_specs=[pl.BlockSpec((ns, PB_W), lambda b: (b, 0)),
                  pl.BlockSpec((ns, LANE), lambda b: (b, PA_SMALL // LANE)),
                  pl.BlockSpec((CONV_W - 1, ns, 3 * W_B), lambda b: (0, b, 0)),
                  pl.BlockSpec((ns, H_B, DK_B, DV_B), lambda b: (b, 0, 0, 0))] + _step_const_specs(consts),
        out_specs=[pl.BlockSpec((ns, W_B), lambda b: (b, 0)),
                   pl.BlockSpec((ns, H_B, DK_B, DV_B), lambda b: (b, 0, 0, 0))],
        out_shape=[jax.ShapeDtypeStruct((B, W_B), F32), jax.ShapeDtypeStruct(s.shape, F32)],
        compiler_params=pltpu.CompilerParams(dimension_semantics=("arbitrary",),
                                             vmem_limit_bytes=VMEM_LIMIT),
        name="gdn_step",
    )(pb, pa, hist, s, *consts)


def _pad_cols(t, width):
    return jnp.pad(t, [(0, 0)] * (t.ndim - 1) + [(0, width - t.shape[-1])])


def _pa_layout(t, gdn_gates=None):
    main = t[..., :3 * W_A]
    small = t[..., 3 * W_A:3 * W_A + D_DECAY + D_AAA]
    gate = t[..., 3 * W_A + D_DECAY + D_AAA:]
    if gdn_gates is not None:
        small = jnp.concatenate([small, gdn_gates], axis=-1)
    return jnp.concatenate([main, _pad_cols(small, LANE), _pad_cols(gate, LANE)], axis=-1)


def _pa_unlayout(t):
    return jnp.concatenate([t[..., :3 * W_A], t[..., PA_SMALL:PA_SMALL + D_DECAY + D_AAA],
                            t[..., PA_GATE:PA_GATE + D_GATE]], axis=-1)


def _small_rows(t, first_row):
    return jnp.pad(t, [(first_row, LANE - first_row - t.shape[0]), (0, 0)])


def _small_lanes(t, first_lane):
    return jnp.pad(t, [(0, 0), (first_lane, LANE - first_lane - t.shape[1])])


def _pair_unblock(s):
    B = s.shape[0]
    return jnp.stack([s[:, :, :N_A, :N_A], s[:, :, N_A:, N_A:]], axis=2).reshape(B, H_A, N_A, N_A)


def _history_rows(rows):
    B, n, w = rows.shape
    return jnp.concatenate([jnp.zeros((B, SUBLANE - n, w), F32), rows], axis=1)


def kernel(x_prompt, x_sample, state_rwkv, state_shift, state_gdn, state_conv, meta_tokens,
           g_ffn1, w_gate1, w_up1, w_down1, g_mix, w_in, mu_shift, w0, w_decay_up, a0, w_a_up,
           w_g_up, k_k, k_a, r_k, lnx_w, lnx_b, conv_w, a_log, dt_bias, gdn_norm_w, w_out,
           g_ffn2, w_gate2, w_up2, w_down2, g_final):
    assert g_ffn1.shape[0] == 1, "single trunk layer"
    bp, sp, _ = x_prompt.shape
    bs = x_sample.shape[0]
    assert x_sample.shape[1] == 1 and sp % CHUNK_RWKV == 0 and sp % CHUNK_GDN == 0
    assert (bp * sp) % TM_DENSE == 0 and (bp * sp) % TM_DENSE_OUT == 0
    assert bp % SEQS_PER_STEP == 0 and bs % DEC_SEQS_PER_STEP == 0
    row = lambda t: t.reshape(1, -1).astype(F32)

    ffn1 = (w_gate1[0].astype(BF16), w_up1[0].astype(BF16), w_down1[0].astype(BF16))
    ffn2 = (w_gate2[0].astype(BF16), w_up2[0].astype(BF16), w_down2[0].astype(BF16))
    win_a = _pa_layout(w_in[0][:, :N_A_IN], w_in[0][:, N_A_IN + PB_W:]).astype(BF16)
    win_b = w_in[0][:, N_A_IN:N_A_IN + PB_W].astype(BF16)
    wo_a = w_out[0][:W_A].astype(BF16)
    wo_b = w_out[0][W_A:].astype(BF16)
    dense_in_consts = (row(g_ffn1[0]), *ffn1, row(g_mix[0]), win_a, win_b)
    dense_out_consts = (wo_a, wo_b, row(g_ffn2[0]), *ffn2, row(g_final))
    rwkv_consts = (_pa_layout(row(mu_shift[0])), row(w0[0]),
                   _small_rows(w_decay_up[0], 0).astype(BF16), row(a0[0]),
                   _small_rows(w_a_up[0], SMALL_AAA).astype(BF16), _small_rows(w_g_up[0], 0).astype(BF16),
                   row(k_k[0]), row(k_a[0]), row(r_k[0]), row(lnx_w[0]), row(lnx_b[0]))
    gdn_consts = (conv_w[0].astype(F32), _small_lanes(row(a_log[0]), SMALL_GDN_A),
                  _small_lanes(row(dt_bias[0]), SMALL_GDN_A), row(gdn_norm_w[0]))

    xs = jnp.concatenate([x_sample[:, 0, :].astype(F32), meta_tokens.astype(F32)], axis=0)
    hs, pas, pbs = _dense_in(xs, *dense_in_consts, tm=xs.shape[0])

    pa_meta = pas[bs:][None]
    pb_meta = pbs[bs:][None]
    _, rw_meta = _rwkv(pa_meta, jnp.zeros((1, SUBLANE, PA_W), F32),
                       jnp.zeros((1, H_A // 2, LANE, LANE), F32), rwkv_consts, N_META, 1)
    _, gd_meta = _gdn(pb_meta, pa_meta, jnp.zeros((1, SUBLANE, 3 * W_B), F32),
                      jnp.zeros((1, H_B, DK_B, DV_B), F32), gdn_consts, N_META, 1)

    hp, pap, pbp = _dense_in(x_prompt.reshape(bp * sp, D_MODEL).astype(F32), *dense_in_consts,
                             tm=TM_DENSE)
    pap3 = pap.reshape(bp, sp, PA_W)
    pbp3 = pbp.reshape(bp, sp, PB_W)
    oa_p, rw_p = _rwkv(pap3, _history_rows(pa_meta[:, -1:, :]), rw_meta, rwkv_consts,
                       CHUNK_RWKV, SEQS_PER_STEP)
    ob_p, gd_p = _gdn(pbp3, pap3, _history_rows(pb_meta[:, -(CONV_W - 1):, :3 * W_B]), gd_meta, gdn_consts,
                      CHUNK_GDN, SEQS_PER_STEP)
    y_p = _dense_out(hp, oa_p.reshape(bp * sp, W_A), ob_p.reshape(bp * sp, W_B), *dense_out_consts,
                     tm=TM_DENSE_OUT)

    oa_t, rw_t = _rwkv_step(pas, _pa_layout(state_shift[0].astype(F32)),
                            jnp.transpose(state_rwkv[0].astype(F32), (1, 2, 3, 0)), rwkv_consts)
    oa_s = oa_t.T
    rw_s = jnp.transpose(rw_t, (3, 0, 1, 2))
    ob_s, gd_s = _gdn_step(pbs, pas, jnp.swapaxes(state_conv[0].astype(F32), 0, 1), state_gdn[0].astype(F32),
                           gdn_consts, DEC_SEQS_PER_STEP)
    y_s = _dense_out(hs[:bs], oa_s, ob_s, *dense_out_consts, tm=bs)

    new_conv_s = jnp.concatenate([state_conv[0].astype(F32)[:, 1:, :], pbs[:bs, None, :3 * W_B]], axis=1)
    return (y_p.reshape(bp, sp, D_MODEL).astype(x_prompt.dtype),
            y_s.reshape(bs, 1, D_MODEL).astype(x_sample.dtype),
            _pair_unblock(rw_p)[None],
            _pa_unlayout(pap3[:, -1, :])[None],
            gd_p[None],
            pbp3[:, -(CONV_W - 1):, :3 * W_B][None],
            rw_s[None],
            _pa_unlayout(pas[:bs])[None],
            gd_s[None],
            new_conv_s[None])
```

```python
import functools
import math

import jax
import jax.numpy as jnp
from jax import lax
from jax.experimental import pallas as pl
from jax.experimental.pallas import tpu as pltpu

F32 = jnp.float32
BF16 = jnp.bfloat16

LANE = 128
SUBLANE = 8
VMEM_LIMIT = 56 * 1024 * 1024

D_MODEL = 1024
D_FF = 2816
N_META = 16
H_A, N_A = 8, 64
W_A = H_A * N_A
D_DECAY, D_AAA, D_GATE = 32, 32, 96
N_A_IN = 3 * W_A + D_DECAY + D_AAA + D_GATE
H_B, DK_B, DV_B = 4, 128, 128
W_B = H_B * DV_B
CONV_W = 4
LNX_EPS = 64e-5
NORM_EPS = 1e-6
NEG_BIG = -1e30
DECAY_SCALE = math.exp(-0.5)

PA_W = 3 * W_A + 2 * LANE
PA_SMALL = 3 * W_A
PA_GATE = PA_SMALL + LANE
SMALL_AAA = D_DECAY
SMALL_GDN_A = D_DECAY + D_AAA
SMALL_GDN_B = SMALL_GDN_A + H_B
PB_W = 4 * W_B
PB_Z = 3 * W_B

TM_DENSE = 512
TM_DENSE_OUT = 512
CHUNK_RWKV = 64
CHUNK_GDN = 128
SEQS_PER_STEP_RWKV = 8
SEQS_PER_STEP_GDN = 4
DEC_SEQS_PER_STEP = SUBLANE

NN = (((1,), (0,)), ((), ()))
NT = (((1,), (1,)), ((), ()))
TN = (((0,), (0,)), ((), ()))


def _dot(a, b):
    return jnp.dot(a, b, preferred_element_type=F32)


def _sigmoid(x):
    return 1.0 / (1.0 + jnp.exp(-x))


def _softplus(x):
    return jnp.maximum(x, 0.0) + jnp.log(1.0 + jnp.exp(-jnp.abs(x)))


def _rms(x, g):
    return x * lax.rsqrt(jnp.mean(x * x, axis=-1, keepdims=True) + NORM_EPS) * g


def _iota2(shape, dim):
    return lax.broadcasted_iota(jnp.int32, shape, dim)


def _split(x):
    hi = x.astype(BF16)
    return hi, (x - hi.astype(F32)).astype(BF16)


def _dg(a, b, dims):
    return lax.dot_general(a, b, dims, preferred_element_type=F32)


def _mm(a, b, dims=NN, mode="b"):
    if mode == "b":
        return _dg(a.astype(BF16), b.astype(BF16), dims)
    if mode == "xl":
        ah, al = _split(a)
        bh = b.astype(BF16)
        return _dg(ah, bh, dims) + _dg(al, bh, dims)
    if mode == "xr":
        ah = a.astype(BF16)
        bh, bl = _split(b)
        return _dg(ah, bh, dims) + _dg(ah, bl, dims)
    assert mode == "x3"
    ah, al = _split(a)
    bh, bl = _split(b)
    return _dg(ah, bh, dims) + (_dg(ah, bl, dims) + _dg(al, bh, dims))


def _inv_unit_lower(lows, nil, merge_mode, expand=None):
    shape = lows[0].shape
    row = _iota2(shape, 0)
    col = _iota2(shape, 1) % nil
    same_block = lambda s: (row // s) == (col // s)
    eye = (row == col).astype(F32)
    if expand is None:
        expand = lambda t: t
    narrow = (lambda t: t.astype(BF16)) if merge_mode == "b" else (lambda t: t)
    s = 2
    in_base = same_block(s)
    invs = [eye - jnp.where(in_base, low, 0.0) for low in lows]
    while s < nil:
        newly = jnp.logical_and(same_block(2 * s), jnp.logical_not(same_block(s)))
        ts = [_mm(inv, expand(narrow(jnp.where(newly, low, 0.0))), NN, merge_mode)
              for inv, low in zip(invs, lows)]
        invs = [inv - _mm(t, expand(narrow(inv)), NN, merge_mode) for inv, t in zip(invs, ts)]
        s *= 2
    return invs


FF_SPLIT = (0, 1536, D_FF)


def _swiglu(n, wg_ref, wu_ref, wd_ref):
    out = None
    for lo, hi in zip(FF_SPLIT[:-1], FF_SPLIT[1:]):
        gate = _dot(n, wg_ref[:, lo:hi])
        up = _dot(n, wu_ref[:, lo:hi])
        act = (gate * _sigmoid(gate) * up).astype(BF16)
        part = _dot(act, wd_ref[lo:hi, :])
        out = part if out is None else out + part
    return out


def _dense_in_kernel(x_ref, g1_ref, wg_ref, wu_ref, wd_ref, gm_ref, wa_ref, wb_ref,
                     h_ref, pa_ref, pb_ref):
    x = x_ref[...]
    h = x + 0.5 * _swiglu(_rms(x, g1_ref[...]).astype(BF16), wg_ref, wu_ref, wd_ref)
    h_ref[...] = h
    n = _rms(h, gm_ref[...]).astype(BF16)
    pa_ref[...] = _dot(n, wa_ref[...])
    pb_ref[...] = _dot(n, wb_ref[...])


def _dense_out_kernel(h_ref, oa_ref, ob_ref, woa_ref, wob_ref, g2_ref, wg_ref, wu_ref, wd_ref,
                      gf_ref, y_ref):
    h = (h_ref[...] + _dot(oa_ref[...].astype(BF16), woa_ref[...])
         + _dot(ob_ref[...].astype(BF16), wob_ref[...]))
    h = h + 0.5 * _swiglu(_rms(h, g2_ref[...]).astype(BF16), wg_ref, wu_ref, wd_ref)
    y_ref[...] = _rms(h, gf_ref[...])


def _const_spec(shape):
    return pl.BlockSpec(shape, lambda *_: (0,) * len(shape), pipeline_mode=pl.Buffered(1))


def _row_spec(tm, width):
    return pl.BlockSpec((tm, width), lambda i: (i, 0))


def _dense_in(x, g1, wg, wu, wd, gm, wa, wb, tm):
    n = x.shape[0]
    consts = (g1, wg, wu, wd, gm, wa, wb)
    return pl.pallas_call(
        _dense_in_kernel,
        grid=(n // tm,),
        in_specs=[_row_spec(tm, D_MODEL)] + [_const_spec(c.shape) for c in consts],
        out_specs=[_row_spec(tm, D_MODEL), _row_spec(tm, PA_W), _row_spec(tm, PB_W)],
        out_shape=[jax.ShapeDtypeStruct((n, D_MODEL), F32),
                   jax.ShapeDtypeStruct((n, PA_W), F32),
                   jax.ShapeDtypeStruct((n, PB_W), F32)],
        compiler_params=pltpu.CompilerParams(dimension_semantics=("arbitrary",),
                                             vmem_limit_bytes=VMEM_LIMIT),
        name="dense_in",
    )(x, *consts)


def _dense_out(h, oa, ob, woa, wob, g2, wg, wu, wd, gf, tm):
    n = h.shape[0]
    consts = (woa, wob, g2, wg, wu, wd, gf)
    return pl.pallas_call(
        _dense_out_kernel,
        grid=(n // tm,),
        in_specs=[_row_spec(tm, D_MODEL), _row_spec(tm, W_A), _row_spec(tm, W_B)]
        + [_const_spec(c.shape) for c in consts],
        out_specs=_row_spec(tm, D_MODEL),
        out_shape=jax.ShapeDtypeStruct((n, D_MODEL), F32),
        compiler_params=pltpu.CompilerParams(dimension_semantics=("arbitrary",),
                                             vmem_limit_bytes=VMEM_LIMIT),
        name="dense_out",
    )(h, oa, ob, *consts)


RWKV_PREC = dict(cum="b", seg="b", pair="b", invm="b", sread="b", akv="b", solve="b", inter="b",
                 state="x3")


def _rwkv_kernel(pa_ref, prev_ref, s0_ref, mu_ref, w0_ref, wdu_ref, a0_ref, wau_ref, wgu_ref,
                 kk_ref, ka_ref, rk_ref, lnw_ref, lnb_ref, o_ref, sout_ref, xbuf, s_scr,
                 *, chunk, nseq, shared_init):
    c = pl.program_id(1)
    C = chunk
    P = RWKV_PREC

    @pl.when(c == 0)
    def _():
        for i in range(nseq):
            j = 0 if shared_init else i
            xbuf[i, 0:SUBLANE, :] = prev_ref[j]
            s_scr[i] = s0_ref[j]

    tri = (_iota2((C, C), 0) >= _iota2((C, C), 1)).astype(F32)
    mid = C // 2 - 1

    def prep(i):
        x = pa_ref[i]
        full = jnp.concatenate([xbuf[i], x], axis=0)
        prev = pltpu.roll(full, 1, axis=0)[SUBLANE:, :]
        xm = x + (prev - x) * mu_ref[...]
        xbuf[i] = x[C - SUBLANE:, :]

        r = xm[:, 0:W_A]
        k = xm[:, W_A:2 * W_A]
        v = xm[:, 2 * W_A:3 * W_A]
        sm = xm[:, PA_SMALL:PA_SMALL + LANE]
        gd = xm[:, PA_GATE:PA_GATE + LANE]

        wl = w0_ref[...] + _dot(jnp.tanh(sm).astype(BF16), wdu_ref[...])
        lw = -DECAY_SCALE * _sigmoid(wl)
        a = _sigmoid(a0_ref[...] + _dot(sm.astype(BF16), wau_ref[...]))
        g = _dot(_sigmoid(gd).astype(BF16), wgu_ref[...])
        kkr = k * kk_ref[...]
        k2 = k * (1.0 + (a - 1.0) * ka_ref[...])
        G = _mm(tri, lw, NN, P["cum"])
        return dict(r=r, v=v, a=a, g=g, kkr=kkr, k2=k2, G=G, Gx=G - lw, Gm=G[mid:mid + 1, :],
                    Gc=G[C - 1:C, :])

    seqs = [prep(i) for i in range(nseq)]

    C2 = 2 * C
    trow = _iota2((C, C2), 0)
    tcol = _iota2((C, C2), 1) % C
    incl = trow >= tcol
    strict = trow > tcol
    lane = _iota2((1, LANE), 1)
    m0 = (lane < N_A).astype(F32)
    m1 = (lane >= N_A).astype(F32)
    col2 = _iota2((1, C2), 1)
    c0 = (col2 < C).astype(F32)
    c1 = (col2 >= C).astype(F32)
    seg = ((_iota2((LANE, LANE), 0) // N_A) == (_iota2((LANE, LANE), 1) // N_A)).astype(F32)
    stack = lambda t: jnp.concatenate([t * m0.astype(t.dtype), t * m1.astype(t.dtype)], axis=0)
    expand = lambda t: jnp.concatenate([t * c0.astype(t.dtype), t * c1.astype(t.dtype)], axis=0)
    narrow = lambda t, site: t.astype(BF16) if P[site] == "b" else t
    merged = C2 % LANE == 0

    inst = [(i, p) for i in range(nseq) for p in range(H_A // 2)]
    n = range(len(inst))
    sls = [slice(p * LANE, (p + 1) * LANE) for _, p in inst]
    take = lambda name: [seqs[i][name][:, sls[j]] for j, (i, _) in enumerate(inst)]
    rs, vs_, k2s, kkrs, as_, gs = take("r"), take("v"), take("k2"), take("kkr"), take("a"), take("g")
    Gs, Gxs, Gms, Gcs = take("G"), take("Gx"), take("Gm"), take("Gc")
    Ss = [s_scr[i, p] for i, p in inst]

    ssq = [_mm(t * t, seg, NN, P["seg"]) for t in kkrs]
    kks = [t * lax.rsqrt(q + 1e-12) for t, q in zip(kkrs, ssq)]
    kkas = [kks[j] * as_[j] for j in n]
    inv_rel = [jnp.exp(Gms[j] - Gs[j]) for j in n]
    dec = [jnp.exp(Gcs[j] - Gs[j]) for j in n]
    lhs = [jnp.concatenate([kks[j] * jnp.exp(Gxs[j] - Gms[j]), rs[j] * jnp.exp(Gs[j] - Gms[j])], axis=0)
           for j in n]
    ais = [stack(narrow(kkas[j] * inv_rel[j], "pair")) for j in n]
    kis = [stack(narrow(k2s[j] * inv_rel[j], "pair")) for j in n]
    if merged:
        pm = [_mm(lhs[j], jnp.concatenate([ais[j], kis[j]], axis=0), NT, P["pair"]) for j in n]
        pas_, pks_ = [t[:, :C2] for t in pm], [t[:, C2:] for t in pm]
    else:
        pas_ = [_mm(lhs[j], ais[j], NT, P["pair"]) for j in n]
        pks_ = [_mm(lhs[j], kis[j], NT, P["pair"]) for j in n]
    a_aa = [jnp.where(strict, t[:C], 0.0) for t in pas_]
    a_ak = [jnp.where(strict, t[:C], 0.0) for t in pks_]
    a_ra = [jnp.where(incl, t[C:], 0.0) for t in pas_]
    a_rk = [jnp.where(incl, t[C:], 0.0) for t in pks_]
    ainv = _inv_unit_lower(a_aa, C, P["invm"], expand)

    assert P["akv"] == "b" and P["inter"] == "b" and P["state"] == "x3"
    v_hl = [_split(t) for t in vs_]
    vst = [stack(hl[0]) for hl in v_hl]
    akv = [_mm(a_ak[j], vst[j], NN, "b") for j in n]
    sx = [_mm(jnp.concatenate([kks[j] * jnp.exp(Gxs[j]), rs[j] * jnp.exp(Gs[j])], axis=0), Ss[j], NT,
              P["sread"]) for j in n]
    u_hl = [_split(-_mm(ainv[j], stack(narrow(sx[j][:C] + akv[j], "solve")), NN, P["solve"])) for j in n]
    ust = [stack(hl[0]) for hl in u_hl]
    cat2 = lambda a, b, q: jnp.concatenate([a[q], b[q]], axis=0)
    ad_hl = [_split(kkas[j] * dec[j]) for j in n]
    kd_hl = [_split(k2s[j] * dec[j]) for j in n]
    s_new = [Ss[j] * jnp.exp(Gcs[j])
             + seg * (_dg(cat2(u_hl[j], v_hl[j], 0), cat2(ad_hl[j], kd_hl[j], 0), TN)
                      + (_dg(cat2(u_hl[j], v_hl[j], 0), cat2(ad_hl[j], kd_hl[j], 1), TN)
                         + _dg(cat2(u_hl[j], v_hl[j], 1), cat2(ad_hl[j], kd_hl[j], 0), TN)))
             for j in n]
    for j, (i, p) in enumerate(inst):
        s_scr[i, p] = s_new[j]
    if merged:
        inter = [_mm(jnp.concatenate([a_ra[j], a_rk[j]], axis=1),
                     jnp.concatenate([ust[j], vst[j]], axis=0), NN, "b") for j in n]
    else:
        inter = [_mm(a_ra[j], ust[j], NN, "b") + _mm(a_rk[j], vst[j], NN, "b") for j in n]
    Os = [sx[j][C:] + inter[j] for j in n]
    means = [_mm(t, seg, NN, P["seg"]) * (1.0 / N_A) for t in Os]
    dlts = [t - m for t, m in zip(Os, means)]
    vars_ = [_mm(t * t, seg, NN, P["seg"]) * (1.0 / N_A) for t in dlts]
    bonus = [_mm(rs[j] * k2s[j] * rk_ref[:, sls[j]], seg, NN, P["seg"]) * vs_[j] for j in n]
    for j, (i, p) in enumerate(inst):
        on = dlts[j] * lax.rsqrt(vars_[j] + LNX_EPS) * lnw_ref[:, sls[j]] + lnb_ref[:, sls[j]]
        o_ref[i, :, sls[j]] = (on + bonus[j]) * gs[j]

    @pl.when(c == pl.num_programs(1) - 1)
    def _():
        sout_ref[...] = s_scr[...]


def _seq_specs(nseq, shared_init, hist_w, state_shape):
    zeros = (0,) * len(state_shape)
    if shared_init:
        return [pl.BlockSpec((1, SUBLANE, hist_w), lambda b, c: (0, 0, 0)),
                pl.BlockSpec((1,) + state_shape, lambda b, c: (0,) + zeros)]
    return [pl.BlockSpec((nseq, SUBLANE, hist_w), lambda b, c: (b, 0, 0)),
            pl.BlockSpec((nseq,) + state_shape, lambda b, c: (b,) + zeros)]


def _rwkv(pa, prev8, s0, consts, chunk, nseq):
    B, L, _ = pa.shape
    shared_init = prev8.shape[0] == 1 and B > 1
    state_shape = (H_A // 2, LANE, LANE)
    return pl.pallas_call(
        functools.partial(_rwkv_kernel, chunk=chunk, nseq=nseq, shared_init=shared_init),
        grid=(B // nseq, L // chunk),
        in_specs=[pl.BlockSpec((nseq, chunk, PA_W), lambda b, c: (b, c, 0))]
        + _seq_specs(nseq, shared_init, PA_W, state_shape)
        + [pl.BlockSpec(t.shape, lambda b, c: (0, 0)) for t in consts],
        out_specs=[pl.BlockSpec((nseq, chunk, W_A), lambda b, c: (b, c, 0)),
                   pl.BlockSpec((nseq,) + state_shape, lambda b, c: (b, 0, 0, 0))],
        out_shape=[jax.ShapeDtypeStruct((B, L, W_A), F32),
                   jax.ShapeDtypeStruct((B,) + state_shape, F32)],
        scratch_shapes=[pltpu.VMEM((nseq, SUBLANE, PA_W), F32),
                        pltpu.VMEM((nseq,) + state_shape, F32)],
        compiler_params=pltpu.CompilerParams(dimension_semantics=("arbitrary", "arbitrary"),
                                             vmem_limit_bytes=VMEM_LIMIT),
        name="rwkv_chunk%d" % chunk,
    )(pa, prev8, s0, *consts)


GDN_PREC = dict(cum="xr", tr="xr", kk="b", invm="b", wu="b", qk="b", sread="b", qkv="b", state="b")


def _gdn_kernel(pb_ref, ga_ref, hist_ref, s0_ref, cw_ref, alog_ref, dtb_ref, nw_ref, o_ref, sout_ref,
                xbuf, s_scr, *, chunk, nseq, shared_init):
    c = pl.program_id(1)
    C = chunk
    P = GDN_PREC

    @pl.when(c == 0)
    def _():
        for i in range(nseq):
            j = 0 if shared_init else i
            xbuf[i, 0:SUBLANE, :] = hist_ref[j]
            s_scr[i] = s0_ref[j]

    row = _iota2((C, C), 0)
    col = _iota2((C, C), 1)
    incl = row >= col
    strict = row > col
    tri = incl.astype(F32)
    sel = (_iota2((SUBLANE, LANE), 0) + SMALL_GDN_A == _iota2((SUBLANE, LANE), 1)).astype(F32)

    def prep(i):
        x = pb_ref[i]
        qkv = x[:, 0:3 * W_B]
        full = jnp.concatenate([xbuf[i], qkv], axis=0)
        conv = qkv * cw_ref[CONV_W - 1:CONV_W, :]
        for j in range(1, CONV_W):
            conv = conv + pltpu.roll(full, j, axis=0)[SUBLANE:, :] * cw_ref[CONV_W - 1 - j:CONV_W - j, :]
        xbuf[i] = qkv[C - SUBLANE:, :]
        cs = conv * _sigmoid(conv)
        gates = ga_ref[i]
        glog = -jnp.exp(alog_ref[...]) * _softplus(gates + dtb_ref[...])
        beta = _sigmoid(gates)
        G = _mm(tri, glog, NN, P["cum"])
        Gt = _mm(sel, G, NT, P["tr"])
        return dict(cs=cs, z=x[:, PB_Z:PB_Z + W_B], beta=beta, G=G, Gt=Gt)

    seqs = [prep(i) for i in range(nseq)]

    inst = [(i, h) for i in range(nseq) for h in range(H_B)]
    n = range(len(inst))
    l2n = lambda t, c: t * (lax.rsqrt(jnp.sum(t * t, axis=-1, keepdims=True) + 1e-12) * c)
    head = lambda i, h, part: seqs[i]["cs"][:, part * W_B + h * LANE:part * W_B + (h + 1) * LANE]
    qs = [l2n(head(i, h, 0), DK_B ** -0.5) for i, h in inst]
    ks = [l2n(head(i, h, 1), 1.0) for i, h in inst]
    vs = [head(i, h, 2) for i, h in inst]
    gcols = [seqs[i]["G"][:, SMALL_GDN_A + h:SMALL_GDN_A + h + 1] for i, h in inst]
    grows = [seqs[i]["Gt"][h:h + 1, :] for i, h in inst]
    bcols = [seqs[i]["beta"][:, SMALL_GDN_B + h:SMALL_GDN_B + h + 1] for i, h in inst]
    Ss = [s_scr[i, h] for i, h in inst]

    dmats = [jnp.exp(jnp.where(incl, gcols[j] - grows[j], NEG_BIG)) for j in n]
    kbs = [ks[j] * bcols[j] for j in n]
    lows = [jnp.where(strict, _mm(kbs[j], ks[j], NT, P["kk"]) * dmats[j], 0.0) for j in n]
    ainv = _inv_unit_lower(lows, C, P["invm"])
    egs = [jnp.exp(t) for t in gcols]
    wus = [_mm(ainv[j], jnp.concatenate([kbs[j] * egs[j], vs[j] * bcols[j]], axis=1), NN, P["wu"])
           for j in n]
    qks = [_mm(qs[j], ks[j], NT, P["qk"]) * dmats[j] for j in n]
    srs = [_mm(jnp.concatenate([wus[j][:, :DK_B], qs[j] * egs[j]], axis=0), Ss[j], NN, P["sread"])
           for j in n]
    v_new = [wus[j][:, DK_B:] - srs[j][:C] for j in n]
    glast = [t[C - 1:C, :] for t in gcols]
    s_new = [Ss[j] * jnp.exp(glast[j])
             + _mm(ks[j] * jnp.exp(glast[j] - gcols[j]), v_new[j], TN, P["state"]) for j in n]
    for j, (i, h) in enumerate(inst):
        s_scr[i, h] = s_new[j]
    os_ = [srs[j][C:] + _mm(qks[j], v_new[j], NN, P["qkv"]) for j in n]
    for j, (i, h) in enumerate(inst):
        o = os_[j]
        o = o * lax.rsqrt(jnp.mean(o * o, axis=-1, keepdims=True) + NORM_EPS) * nw_ref[...]
        zh = seqs[i]["z"][:, h * LANE:(h + 1) * LANE]
        o_ref[i, :, h * LANE:(h + 1) * LANE] = o * (zh * _sigmoid(zh))

    @pl.when(c == pl.num_programs(1) - 1)
    def _():
        sout_ref[...] = s_scr[...]


def _gdn(pb, pa, hist8, s0, consts, chunk, nseq):
    B, L, _ = pb.shape
    shared_init = hist8.shape[0] == 1 and B > 1
    state_shape = (H_B, DK_B, DV_B)
    return pl.pallas_call(
        functools.partial(_gdn_kernel, chunk=chunk, nseq=nseq, shared_init=shared_init),
        grid=(B // nseq, L // chunk),
        in_specs=[pl.BlockSpec((nseq, chunk, PB_W), lambda b, c: (b, c, 0)),
                  pl.BlockSpec((nseq, chunk, LANE), lambda b, c: (b, c, PA_SMALL // LANE))]
        + _seq_specs(nseq, shared_init, 3 * W_B, state_shape)
        + [pl.BlockSpec(t.shape, lambda b, c: (0, 0)) for t in consts],
        out_specs=[pl.BlockSpec((nseq, chunk, W_B), lambda b, c: (b, c, 0)),
                   pl.BlockSpec((nseq,) + state_shape, lambda b, c: (b, 0, 0, 0))],
        out_shape=[jax.ShapeDtypeStruct((B, L, W_B), F32),
                   jax.ShapeDtypeStruct((B,) + state_shape, F32)],
        scratch_shapes=[pltpu.VMEM((nseq, SUBLANE, 3 * W_B), F32),
                        pltpu.VMEM((nseq,) + state_shape, F32)],
        compiler_params=pltpu.CompilerParams(dimension_semantics=("arbitrary", "arbitrary"),
                                             vmem_limit_bytes=VMEM_LIMIT),
        name="gdn_chunk%d" % chunk,
    )(pb, pa, hist8, s0, *consts)


def _row_mask(nrow, i):
    return _iota2((nrow, 1), 0) == i


STEP_VECS = ("w", "kk", "kka", "k2", "v", "r", "g", "bon")


def _rwkv_step_prep_kernel(pa_ref, prev_ref, mu_ref, w0_ref, wdu_ref, a0_ref, wau_ref, wgu_ref,
                           kk_ref, ka_ref, rk_ref, vec_ref):
    x = pa_ref[...]
    xm = x + (prev_ref[...] - x) * mu_ref[...]
    r = xm[:, 0:W_A]
    k = xm[:, W_A:2 * W_A]
    v = xm[:, 2 * W_A:3 * W_A]
    sm = xm[:, PA_SMALL:PA_SMALL + LANE]
    gd = xm[:, PA_GATE:PA_GATE + LANE]
    wl = w0_ref[...] + _dot(jnp.tanh(sm).astype(BF16), wdu_ref[...])
    w = jnp.exp(-DECAY_SCALE * _sigmoid(wl))
    a = _sigmoid(a0_ref[...] + _dot(sm.astype(BF16), wau_ref[...]))
    g = _dot(_sigmoid(gd).astype(BF16), wgu_ref[...])
    kkr = k * kk_ref[...]
    k2 = k * (1.0 + (a - 1.0) * ka_ref[...])
    bon = r * k2 * rk_ref[...]
    seg = ((_iota2((LANE, LANE), 0) // N_A) == (_iota2((LANE, LANE), 1) // N_A)).astype(F32)
    ssq = jnp.concatenate([_mm(kkr[:, p * LANE:(p + 1) * LANE] ** 2, seg, NN, RWKV_PREC["seg"])
                           for p in range(H_A // 2)], axis=-1)
    kk = kkr * lax.rsqrt(ssq + 1e-12)
    vecs = dict(w=w, kk=kk, kka=kk * a, k2=k2, v=v, r=r, g=g, bon=bon)
    for j, name in enumerate(STEP_VECS):
        vec_ref[j] = vecs[name].T


def _rwkv_step_state_kernel(vec_ref, s_ref, lnw_ref, lnb_ref, o_ref, sout_ref, o_scr):
    w, kk, kka, k2, v, r, g, bon = [vec_ref[j] for j in range(len(STEP_VECS))]
    for i in range(N_A):
        S = s_ref[0, i]
        sa = jnp.sum(S * kk, axis=0, keepdims=True)
        s_new = S * w - sa * kka + v[i:i + 1, :] * k2
        sout_ref[0, i] = s_new
        o_scr[i:i + 1, :] = jnp.sum(s_new * r, axis=0, keepdims=True)
    o = o_scr[...]
    mean = jnp.mean(o, axis=0, keepdims=True)
    dlt = o - mean
    var = jnp.mean(dlt * dlt, axis=0, keepdims=True)
    on = dlt * lax.rsqrt(var + LNX_EPS) * lnw_ref[...] + lnb_ref[...]
    o_ref[...] = (on + jnp.sum(bon, axis=0, keepdims=True) * v) * g


def _gdn_step_kernel(pb_ref, ga_ref, hist_ref, s_ref, cw_ref, alog_ref, dtb_ref, nw_ref, o_ref, sout_ref):
    ns = pb_ref.shape[0]
    x = pb_ref[...]
    conv = x[:, 0:3 * W_B] * cw_ref[CONV_W - 1:CONV_W, :]
    for j in range(CONV_W - 1):
        conv = conv + hist_ref[j] * cw_ref[j:j + 1, :]
    cs = conv * _sigmoid(conv)
    z = x[:, PB_Z:PB_Z + W_B]
    gates = ga_ref[...]
    eg = jnp.exp(-jnp.exp(alog_ref[...]) * _softplus(gates + dtb_ref[...]))
    beta = _sigmoid(gates)

    l2n = lambda t: t * lax.rsqrt(jnp.sum(t * t, axis=-1, keepdims=True) + 1e-12)
    for h in range(H_B):
        hs = slice(h * LANE, (h + 1) * LANE)
        q_h = l2n(cs[:, hs]) * (DK_B ** -0.5)
        k_h = l2n(cs[:, W_B + h * LANE:W_B + (h + 1) * LANE])
        v_h = cs[:, 2 * W_B + h * LANE:2 * W_B + (h + 1) * LANE]
        eg_h = eg[:, SMALL_GDN_A + h:SMALL_GDN_A + h + 1]
        b_h = beta[:, SMALL_GDN_B + h:SMALL_GDN_B + h + 1]
        seq = range(ns)
        Ss = [s_ref[i, h] for i in seq]
        kss = [_mm(k_h, Ss[i], NN, "xr") for i in seq]
        v_new = [b_h * (v_h - eg_h * kss[i]) for i in seq]
        s_new = [Ss[i] * eg_h[i:i + 1, :] + _mm(jnp.where(_row_mask(ns, i), k_h, 0.0), v_new[i], TN, "b")
                 for i in seq]
        for i in seq:
            sout_ref[i, h] = s_new[i]
        o_all = [_mm(q_h, s_new[i], NN, "b") for i in seq]
        o_h = jnp.zeros((ns, DV_B), F32)
        for i in seq:
            o_h = jnp.where(_row_mask(ns, i), o_all[i], o_h)
        o_h = o_h * lax.rsqrt(jnp.mean(o_h * o_h, axis=-1, keepdims=True) + NORM_EPS) * nw_ref[...]
        zh = z[:, hs]
        o_ref[:, hs] = o_h * (zh * _sigmoid(zh))


def _step_const_specs(ts):
    return [pl.BlockSpec(t.shape, lambda b: (0,) * t.ndim) for t in ts]


def _rwkv_step(pa, prev, s_t, consts):
    B = s_t.shape[-1]
    prep_consts = consts[:-2]
    lnw_col, lnb_col = [c.reshape(W_A, 1) for c in consts[-2:]]
    vecs = pl.pallas_call(
        _rwkv_step_prep_kernel,
        grid=(1,),
        in_specs=[pl.BlockSpec((B, PA_W), lambda b: (0, 0)),
                  pl.BlockSpec((B, PA_W), lambda b: (0, 0))] + _step_const_specs(prep_consts),
        out_specs=pl.BlockSpec((len(STEP_VECS), W_A, B), lambda b: (0, 0, 0)),
        out_shape=jax.ShapeDtypeStruct((len(STEP_VECS), W_A, B), F32),
        compiler_params=pltpu.CompilerParams(dimension_semantics=("arbitrary",),
                                             vmem_limit_bytes=VMEM_LIMIT),
        name="rwkv_step_prep",
    )(pa, prev, *prep_consts)
    state_spec = pl.BlockSpec((1, N_A, N_A, B), lambda h: (h, 0, 0, 0))
    return pl.pallas_call(
        _rwkv_step_state_kernel,
        grid=(H_A,),
        in_specs=[pl.BlockSpec((len(STEP_VECS), N_A, B), lambda h: (0, h, 0)), state_spec,
                  pl.BlockSpec((N_A, 1), lambda h: (h, 0)), pl.BlockSpec((N_A, 1), lambda h: (h, 0))],
        out_specs=[pl.BlockSpec((N_A, B), lambda h: (h, 0)), state_spec],
        out_shape=[jax.ShapeDtypeStruct((W_A, B), F32), jax.ShapeDtypeStruct(s_t.shape, F32)],
        scratch_shapes=[pltpu.VMEM((N_A, B), F32)],
        compiler_params=pltpu.CompilerParams(dimension_semantics=("arbitrary",),
                                             vmem_limit_bytes=VMEM_LIMIT),
        name="rwkv_step_state",
    )(vecs, s_t, lnw_col, lnb_col)


def _gdn_step(pb, pa, hist, s, consts, ns):
    B = s.shape[0]
    return pl.pallas_call(
        _gdn_step_kernel,
        grid=(B // ns,),
        in_specs=[pl.BlockSpec((ns, PB_W), lambda b: (b, 0)),
                  pl.BlockSpec((ns, LANE), lambda b: (b, PA_SMALL // LANE)),
                  pl.BlockSpec((CONV_W - 1, ns, 3 * W_B), lambda b: (0, b, 0)),
                  pl.BlockSpec((ns, H_B, DK_B, DV_B), lambda b: (b, 0, 0, 0))] + _step_const_specs(consts),
        out_specs=[pl.BlockSpec((ns, W_B), lambda b: (b, 0)),
                   pl.BlockSpec((ns, H_B, DK_B, DV_B), lambda b: (b, 0, 0, 0))],
        out_shape=[jax.ShapeDtypeStruct((B, W_B), F32), jax.ShapeDtypeStruct(s.shape, F32)],
        compiler_params=pltpu.CompilerParams(dimension_semantics=("arbitrary",),
                                             vmem_limit_bytes=VMEM_LIMIT),
        name="gdn_step",
    )(pb, pa, hist, s, *consts)


def _pad_cols(t, width):
    return jnp.pad(t, [(0, 0)] * (t.ndim - 1) + [(0, width - t.shape[-1])])


def _pa_layout(t, gdn_gates=None):
    main = t[..., :3 * W_A]
    small = t[..., 3 * W_A:3 * W_A + D_DECAY + D_AAA]
    gate = t[..., 3 * W_A + D_DECAY + D_AAA:]
    if gdn_gates is not None:
        small = jnp.concatenate([small, gdn_gates], axis=-1)
    return jnp.concatenate([main, _pad_cols(small, LANE), _pad_cols(gate, LANE)], axis=-1)


def _pa_unlayout(t):
    return jnp.concatenate([t[..., :3 * W_A], t[..., PA_SMALL:PA_SMALL + D_DECAY + D_AAA],
                            t[..., PA_GATE:PA_GATE + D_GATE]], axis=-1)


def _small_rows(t, first_row):
    return jnp.pad(t, [(first_row, LANE - first_row - t.shape[0]), (0, 0)])


def _small_lanes(t, first_lane):
    return jnp.pad(t, [(0, 0), (first_lane, LANE - first_lane - t.shape[1])])


def _pair_unblock(s):
    B = s.shape[0]
    return jnp.stack([s[:, :, :N_A, :N_A], s[:, :, N_A:, N_A:]], axis=2).reshape(B, H_A, N_A, N_A)


def _history_rows(rows):
    B, n, w = rows.shape
    return jnp.concatenate([jnp.zeros((B, SUBLANE - n, w), F32), rows], axis=1)


def kernel(x_prompt, x_sample, state_rwkv, state_shift, state_gdn, state_conv, meta_tokens,
           g_ffn1, w_gate1, w_up1, w_down1, g_mix, w_in, mu_shift, w0, w_decay_up, a0, w_a_up,
           w_g_up, k_k, k_a, r_k, lnx_w, lnx_b, conv_w, a_log, dt_bias, gdn_norm_w, w_out,
           g_ffn2, w_gate2, w_up2, w_down2, g_final):
    assert g_ffn1.shape[0] == 1, "single trunk layer"
    bp, sp, _ = x_prompt.shape
    bs = x_sample.shape[0]
    assert x_sample.shape[1] == 1 and sp % CHUNK_RWKV == 0 and sp % CHUNK_GDN == 0
    assert (bp * sp) % TM_DENSE == 0 and (bp * sp) % TM_DENSE_OUT == 0
    assert bp % SEQS_PER_STEP_RWKV == 0 and bp % SEQS_PER_STEP_GDN == 0 and bs % DEC_SEQS_PER_STEP == 0
    row = lambda t: t.reshape(1, -1).astype(F32)

    ffn1 = (w_gate1[0].astype(BF16), w_up1[0].astype(BF16), w_down1[0].astype(BF16))
    ffn2 = (w_gate2[0].astype(BF16), w_up2[0].astype(BF16), w_down2[0].astype(BF16))
    win_a = _pa_layout(w_in[0][:, :N_A_IN], w_in[0][:, N_A_IN + PB_W:]).astype(BF16)
    win_b = w_in[0][:, N_A_IN:N_A_IN + PB_W].astype(BF16)
    wo_a = w_out[0][:W_A].astype(BF16)
    wo_b = w_out[0][W_A:].astype(BF16)
    dense_in_consts = (row(g_ffn1[0]), *ffn1, row(g_mix[0]), win_a, win_b)
    dense_out_consts = (wo_a, wo_b, row(g_ffn2[0]), *ffn2, row(g_final))
    rwkv_consts = (_pa_layout(row(mu_shift[0])), row(w0[0]),
                   _small_rows(w_decay_up[0], 0).astype(BF16), row(a0[0]),
                   _small_rows(w_a_up[0], SMALL_AAA).astype(BF16), _small_rows(w_g_up[0], 0).astype(BF16),
                   row(k_k[0]), row(k_a[0]), row(r_k[0]), row(lnx_w[0]), row(lnx_b[0]))
    gdn_consts = (conv_w[0].astype(F32), _small_lanes(row(a_log[0]), SMALL_GDN_A),
                  _small_lanes(row(dt_bias[0]), SMALL_GDN_A), row(gdn_norm_w[0]))

    xs = jnp.concatenate([x_sample[:, 0, :].astype(F32), meta_tokens.astype(F32)], axis=0)
    hs, pas, pbs = _dense_in(xs, *dense_in_consts, tm=xs.shape[0])

    pa_meta = pas[bs:][None]
    pb_meta = pbs[bs:][None]
    _, rw_meta = _rwkv(pa_meta, jnp.zeros((1, SUBLANE, PA_W), F32),
                       jnp.zeros((1, H_A // 2, LANE, LANE), F32), rwkv_consts, N_META, 1)
    _, gd_meta = _gdn(pb_meta, pa_meta, jnp.zeros((1, SUBLANE, 3 * W_B), F32),
                      jnp.zeros((1, H_B, DK_B, DV_B), F32), gdn_consts, N_META, 1)

    hp, pap, pbp = _dense_in(x_prompt.reshape(bp * sp, D_MODEL).astype(F32), *dense_in_consts,
                             tm=TM_DENSE)
    pap3 = pap.reshape(bp, sp, PA_W)
    pbp3 = pbp.reshape(bp, sp, PB_W)
    oa_p, rw_p = _rwkv(pap3, _history_rows(pa_meta[:, -1:, :]), rw_meta, rwkv_consts,
                       CHUNK_RWKV, SEQS_PER_STEP_RWKV)
    ob_p, gd_p = _gdn(pbp3, pap3, _history_rows(pb_meta[:, -(CONV_W - 1):, :3 * W_B]), gd_meta, gdn_consts,
                      CHUNK_GDN, SEQS_PER_STEP_GDN)
    y_p = _dense_out(hp, oa_p.reshape(bp * sp, W_A), ob_p.reshape(bp * sp, W_B), *dense_out_consts,
                     tm=TM_DENSE_OUT)

    oa_t, rw_t = _rwkv_step(pas, _pa_layout(state_shift[0].astype(F32)),
                            jnp.transpose(state_rwkv[0].astype(F32), (1, 2, 3, 0)), rwkv_consts)
    oa_s = oa_t.T
    rw_s = jnp.transpose(rw_t, (3, 0, 1, 2))
    ob_s, gd_s = _gdn_step(pbs, pas, jnp.swapaxes(state_conv[0].astype(F32), 0, 1), state_gdn[0].astype(F32),
                           gdn_consts, DEC_SEQS_PER_STEP)
    y_s = _dense_out(hs[:bs], oa_s, ob_s, *dense_out_consts, tm=bs)

    new_conv_s = jnp.concatenate([state_conv[0].astype(F32)[:, 1:, :], pbs[:bs, None, :3 * W_B]], axis=1)
    return (y_p.reshape(bp, sp, D_MODEL).astype(x_prompt.dtype),
            y_s.reshape(bs, 1, D_MODEL).astype(x_sample.dtype),
            _pair_unblock(rw_p)[None],
            _pa_unlayout(pap3[:, -1, :])[None],
            gd_p[None],
            pbp3[:, -(CONV_W - 1):, :3 * W_B][None],
            rw_s[None],
            _pa_unlayout(pas[:bs])[None],
            gd_s[None],
            new_conv_s[None])
```

```python
import functools
import math

import jax
import jax.numpy as jnp
from jax import lax
from jax.experimental import pallas as pl
from jax.experimental.pallas import tpu as pltpu

F32 = jnp.float32
BF16 = jnp.bfloat16

LANE = 128
SUBLANE = 8
VMEM_LIMIT = 56 * 1024 * 1024

D_MODEL = 1024
D_FF = 2816
N_META = 16
H_A, N_A = 8, 64
W_A = H_A * N_A
D_DECAY, D_AAA, D_GATE = 32, 32, 96
N_A_IN = 3 * W_A + D_DECAY + D_AAA + D_GATE
H_B, DK_B, DV_B = 4, 128, 128
W_B = H_B * DV_B
CONV_W = 4
LNX_EPS = 64e-5
NORM_EPS = 1e-6
NEG_BIG = -1e30
DECAY_SCALE = math.exp(-0.5)

PA_W = 3 * W_A + 2 * LANE
PA_SMALL = 3 * W_A
PA_GATE = PA_SMALL + LANE
SMALL_AAA = D_DECAY
SMALL_GDN_A = D_DECAY + D_AAA
SMALL_GDN_B = SMALL_GDN_A + H_B
PB_W = 4 * W_B
PB_Z = 3 * W_B

TM_DENSE = 512
TM_DENSE_OUT = 512
CHUNK_RWKV = 64
CHUNK_GDN = 128
SEQS_PER_STEP_RWKV = 8
SEQS_PER_STEP_GDN = 4
DEC_SEQS_PER_STEP = SUBLANE

NN = (((1,), (0,)), ((), ()))
NT = (((1,), (1,)), ((), ()))
TN = (((0,), (0,)), ((), ()))


def _dot(a, b):
    return jnp.dot(a, b, preferred_element_type=F32)


def _sigmoid(x):
    return 1.0 / (1.0 + jnp.exp(-x))


def _softplus(x):
    return jnp.maximum(x, 0.0) + jnp.log(1.0 + jnp.exp(-jnp.abs(x)))


def _rms(x, g):
    return x * lax.rsqrt(jnp.mean(x * x, axis=-1, keepdims=True) + NORM_EPS) * g


def _iota2(shape, dim):
    return lax.broadcasted_iota(jnp.int32, shape, dim)


def _split(x):
    hi = x.astype(BF16)
    return hi, (x - hi.astype(F32)).astype(BF16)


def _dg(a, b, dims):
    return lax.dot_general(a, b, dims, preferred_element_type=F32)


def _mm(a, b, dims=NN, mode="b"):
    if mode == "b":
        return _dg(a.astype(BF16), b.astype(BF16), dims)
    if mode == "xl":
        ah, al = _split(a)
        bh = b.astype(BF16)
        return _dg(ah, bh, dims) + _dg(al, bh, dims)
    if mode == "xr":
        ah = a.astype(BF16)
        bh, bl = _split(b)
        return _dg(ah, bh, dims) + _dg(ah, bl, dims)
    assert mode == "x3"
    ah, al = _split(a)
    bh, bl = _split(b)
    return _dg(ah, bh, dims) + (_dg(ah, bl, dims) + _dg(al, bh, dims))


def _inv_unit_lower(lows, nil, merge_mode, expand=None):
    shape = lows[0].shape
    row = _iota2(shape, 0)
    col = _iota2(shape, 1) % nil
    same_block = lambda s: (row // s) == (col // s)
    eye = (row == col).astype(F32)
    if expand is None:
        expand = lambda t: t
    narrow = (lambda t: t.astype(BF16)) if merge_mode == "b" else (lambda t: t)
    s = 2
    in_base = same_block(s)
    invs = [eye - jnp.where(in_base, low, 0.0) for low in lows]
    while s < nil:
        newly = jnp.logical_and(same_block(2 * s), jnp.logical_not(same_block(s)))
        ts = [_mm(inv, expand(narrow(jnp.where(newly, low, 0.0))), NN, merge_mode)
              for inv, low in zip(invs, lows)]
        invs = [inv - _mm(t, expand(narrow(inv)), NN, merge_mode) for inv, t in zip(invs, ts)]
        s *= 2
    return invs


FF_SPLIT = (0, 1536, D_FF)


def _swiglu(n, wg_ref, wu_ref, wd_ref):
    out = None
    for lo, hi in zip(FF_SPLIT[:-1], FF_SPLIT[1:]):
        gate = _dot(n, wg_ref[:, lo:hi])
        up = _dot(n, wu_ref[:, lo:hi])
        act = (gate * _sigmoid(gate) * up).astype(BF16)
        part = _dot(act, wd_ref[lo:hi, :])
        out = part if out is None else out + part
    return out


def _dense_in_kernel(x_ref, g1_ref, wg_ref, wu_ref, wd_ref, gm_ref, wa_ref, wb_ref,
                     h_ref, pa_ref, pb_ref):
    x = x_ref[...]
    h = x + 0.5 * _swiglu(_rms(x, g1_ref[...]).astype(BF16), wg_ref, wu_ref, wd_ref)
    h_ref[...] = h
    n = _rms(h, gm_ref[...]).astype(BF16)
    pa_ref[...] = _dot(n, wa_ref[...])
    pb_ref[...] = _dot(n, wb_ref[...])


def _dense_out_kernel(h_ref, oa_ref, ob_ref, woa_ref, wob_ref, g2_ref, wg_ref, wu_ref, wd_ref,
                      gf_ref, y_ref):
    h = (h_ref[...] + _dot(oa_ref[...].astype(BF16), woa_ref[...])
         + _dot(ob_ref[...].astype(BF16), wob_ref[...]))
    h = h + 0.5 * _swiglu(_rms(h, g2_ref[...]).astype(BF16), wg_ref, wu_ref, wd_ref)
    y_ref[...] = _rms(h, gf_ref[...])


def _const_spec(shape):
    return pl.BlockSpec(shape, lambda *_: (0,) * len(shape), pipeline_mode=pl.Buffered(1))


def _row_spec(tm, width):
    return pl.BlockSpec((tm, width), lambda i: (i, 0))


def _dense_in(x, g1, wg, wu, wd, gm, wa, wb, tm):
    n = x.shape[0]
    consts = (g1, wg, wu, wd, gm, wa, wb)
    return pl.pallas_call(
        _dense_in_kernel,
        grid=(n // tm,),
        in_specs=[_row_spec(tm, D_MODEL)] + [_const_spec(c.shape) for c in consts],
        out_specs=[_row_spec(tm, D_MODEL), _row_spec(tm, PA_W), _row_spec(tm, PB_W)],
        out_shape=[jax.ShapeDtypeStruct((n, D_MODEL), F32),
                   jax.ShapeDtypeStruct((n, PA_W), F32),
                   jax.ShapeDtypeStruct((n, PB_W), F32)],
        compiler_params=pltpu.CompilerParams(dimension_semantics=("arbitrary",),
                                             vmem_limit_bytes=VMEM_LIMIT),
        name="dense_in",
    )(x, *consts)


def _dense_out(h, oa, ob, woa, wob, g2, wg, wu, wd, gf, tm):
    n = h.shape[0]
    consts = (woa, wob, g2, wg, wu, wd, gf)
    return pl.pallas_call(
        _dense_out_kernel,
        grid=(n // tm,),
        in_specs=[_row_spec(tm, D_MODEL), _row_spec(tm, W_A), _row_spec(tm, W_B)]
        + [_const_spec(c.shape) for c in consts],
        out_specs=_row_spec(tm, D_MODEL),
        out_shape=jax.ShapeDtypeStruct((n, D_MODEL), F32),
        compiler_params=pltpu.CompilerParams(dimension_semantics=("arbitrary",),
                                             vmem_limit_bytes=VMEM_LIMIT),
        name="dense_out",
    )(h, oa, ob, *consts)


RWKV_PREC = dict(cum="b", seg="b", pair="b", invm="b", sread="b", akv="b", solve="b", inter="b",
                 state="x3")


def _rwkv_kernel(pa_ref, prev_ref, s0_ref, mu_ref, w0_ref, wdu_ref, a0_ref, wau_ref, wgu_ref,
                 kk_ref, ka_ref, rk_ref, lnw_ref, lnb_ref, o_ref, sout_ref, xbuf, s_scr,
                 *, chunk, nseq, shared_init):
    c = pl.program_id(1)
    C = chunk
    P = RWKV_PREC

    @pl.when(c == 0)
    def _():
        for i in range(nseq):
            j = 0 if shared_init else i
            xbuf[i, 0:SUBLANE, :] = prev_ref[j]
            s_scr[i] = s0_ref[j]

    tri = (_iota2((C, C), 0) >= _iota2((C, C), 1)).astype(F32)
    mid = C // 2 - 1

    def prep(i):
        x = pa_ref[i]
        full = jnp.concatenate([xbuf[i], x], axis=0)
        prev = pltpu.roll(full, 1, axis=0)[SUBLANE:, :]
        xm = x + (prev - x) * mu_ref[...]
        xbuf[i] = x[C - SUBLANE:, :]

        r = xm[:, 0:W_A]
        k = xm[:, W_A:2 * W_A]
        v = xm[:, 2 * W_A:3 * W_A]
        sm = xm[:, PA_SMALL:PA_SMALL + LANE]
        gd = xm[:, PA_GATE:PA_GATE + LANE]

        wl = w0_ref[...] + _dot(jnp.tanh(sm).astype(BF16), wdu_ref[...])
        lw = -DECAY_SCALE * _sigmoid(wl)
        a = _sigmoid(a0_ref[...] + _dot(sm.astype(BF16), wau_ref[...]))
        g = _dot(_sigmoid(gd).astype(BF16), wgu_ref[...])
        kkr = k * kk_ref[...]
        k2 = k * (1.0 + (a - 1.0) * ka_ref[...])
        G = _mm(tri, lw, NN, P["cum"])
        return dict(r=r, v=v, a=a, g=g, kkr=kkr, k2=k2, G=G, Gx=G - lw, Gm=G[mid:mid + 1, :],
                    Gc=G[C - 1:C, :])

    seqs = [prep(i) for i in range(nseq)]

    C2 = 2 * C
    trow = _iota2((C, C2), 0)
    tcol = _iota2((C, C2), 1) % C
    incl = trow >= tcol
    strict = trow > tcol
    lane = _iota2((1, LANE), 1)
    m0 = (lane < N_A).astype(F32)
    m1 = (lane >= N_A).astype(F32)
    col2 = _iota2((1, C2), 1)
    c0 = (col2 < C).astype(F32)
    c1 = (col2 >= C).astype(F32)
    seg = ((_iota2((LANE, LANE), 0) // N_A) == (_iota2((LANE, LANE), 1) // N_A)).astype(F32)
    stack = lambda t: jnp.concatenate([t * m0.astype(t.dtype), t * m1.astype(t.dtype)], axis=0)
    expand = lambda t: jnp.concatenate([t * c0.astype(t.dtype), t * c1.astype(t.dtype)], axis=0)
    narrow = lambda t, site: t.astype(BF16) if P[site] == "b" else t
    merged = C2 % LANE == 0

    inst = [(i, p) for i in range(nseq) for p in range(H_A // 2)]
    n = range(len(inst))
    sls = [slice(p * LANE, (p + 1) * LANE) for _, p in inst]
    take = lambda name: [seqs[i][name][:, sls[j]] for j, (i, _) in enumerate(inst)]
    rs, vs_, k2s, kkrs, as_, gs = take("r"), take("v"), take("k2"), take("kkr"), take("a"), take("g")
    Gs, Gxs, Gms, Gcs = take("G"), take("Gx"), take("Gm"), take("Gc")
    Ss = [s_scr[i, p] for i, p in inst]

    ssq = [_mm(t * t, seg, NN, P["seg"]) for t in kkrs]
    kks = [t * lax.rsqrt(q + 1e-12) for t, q in zip(kkrs, ssq)]
    kkas = [kks[j] * as_[j] for j in n]
    inv_rel = [jnp.exp(Gms[j] - Gs[j]) for j in n]
    dec = [jnp.exp(Gcs[j] - Gs[j]) for j in n]
    lhs = [jnp.concatenate([kks[j] * jnp.exp(Gxs[j] - Gms[j]), rs[j] * jnp.exp(Gs[j] - Gms[j])], axis=0)
           for j in n]
    ais = [stack(narrow(kkas[j] * inv_rel[j], "pair")) for j in n]
    kis = [stack(narrow(k2s[j] * inv_rel[j], "pair")) for j in n]
    if merged:
        pm = [_mm(lhs[j], jnp.concatenate([ais[j], kis[j]], axis=0), NT, P["pair"]) for j in n]
        pas_, pks_ = [t[:, :C2] for t in pm], [t[:, C2:] for t in pm]
    else:
        pas_ = [_mm(lhs[j], ais[j], NT, P["pair"]) for j in n]
        pks_ = [_mm(lhs[j], kis[j], NT, P["pair"]) for j in n]
    a_aa = [jnp.where(strict, t[:C], 0.0) for t in pas_]
    a_ak = [jnp.where(strict, t[:C], 0.0) for t in pks_]
    a_ra = [jnp.where(incl, t[C:], 0.0) for t in pas_]
    a_rk = [jnp.where(incl, t[C:], 0.0) for t in pks_]
    ainv = _inv_unit_lower(a_aa, C, P["invm"], expand)

    assert P["akv"] == "b" and P["inter"] == "b" and P["state"] == "x3"
    v_hl = [_split(t) for t in vs_]
    vst = [stack(hl[0]) for hl in v_hl]
    akv = [_mm(a_ak[j], vst[j], NN, "b") for j in n]
    sx = [_mm(jnp.concatenate([kks[j] * jnp.exp(Gxs[j]), rs[j] * jnp.exp(Gs[j])], axis=0), Ss[j], NT,
              P["sread"]) for j in n]
    u_hl = [_split(-_mm(ainv[j], stack(narrow(sx[j][:C] + akv[j], "solve")), NN, P["solve"])) for j in n]
    ust = [stack(hl[0]) for hl in u_hl]
    cat2 = lambda a, b, q: jnp.concatenate([a[q], b[q]], axis=0)
    ad_hl = [_split(kkas[j] * dec[j]) for j in n]
    kd_hl = [_split(k2s[j] * dec[j]) for j in n]
    s_new = [Ss[j] * jnp.exp(Gcs[j])
             + seg * (_dg(cat2(u_hl[j], v_hl[j], 0), cat2(ad_hl[j], kd_hl[j], 0), TN)
                      + (_dg(cat2(u_hl[j], v_hl[j], 0), cat2(ad_hl[j], kd_hl[j], 1), TN)
                         + _dg(cat2(u_hl[j], v_hl[j], 1), cat2(ad_hl[j], kd_hl[j], 0), TN)))
             for j in n]
    for j, (i, p) in enumerate(inst):
        s_scr[i, p] = s_new[j]
    if merged:
        inter = [_mm(jnp.concatenate([a_ra[j], a_rk[j]], axis=1),
                     jnp.concatenate([ust[j], vst[j]], axis=0), NN, "b") for j in n]
    else:
        inter = [_mm(a_ra[j], ust[j], NN, "b") + _mm(a_rk[j], vst[j], NN, "b") for j in n]
    Os = [sx[j][C:] + inter[j] for j in n]
    means = [_mm(t, seg, NN, P["seg"]) * (1.0 / N_A) for t in Os]
    dlts = [t - m for t, m in zip(Os, means)]
    vars_ = [_mm(t * t, seg, NN, P["seg"]) * (1.0 / N_A) for t in dlts]
    bonus = [_mm(rs[j] * k2s[j] * rk_ref[:, sls[j]], seg, NN, P["seg"]) * vs_[j] for j in n]
    for j, (i, p) in enumerate(inst):
        on = dlts[j] * lax.rsqrt(vars_[j] + LNX_EPS) * lnw_ref[:, sls[j]] + lnb_ref[:, sls[j]]
        o_ref[i, :, sls[j]] = (on + bonus[j]) * gs[j]

    @pl.when(c == pl.num_programs(1) - 1)
    def _():
        sout_ref[...] = s_scr[...]


def _seq_specs(nseq, shared_init, hist_w, state_shape):
    zeros = (0,) * len(state_shape)
    if shared_init:
        return [pl.BlockSpec((1, SUBLANE, hist_w), lambda b, c: (0, 0, 0)),
                pl.BlockSpec((1,) + state_shape, lambda b, c: (0,) + zeros)]
    return [pl.BlockSpec((nseq, SUBLANE, hist_w), lambda b, c: (b, 0, 0)),
            pl.BlockSpec((nseq,) + state_shape, lambda b, c: (b,) + zeros)]


def _rwkv(pa, prev8, s0, consts, chunk, nseq):
    B, L, _ = pa.shape
    shared_init = prev8.shape[0] == 1 and B > 1
    state_shape = (H_A // 2, LANE, LANE)
    return pl.pallas_call(
        functools.partial(_rwkv_kernel, chunk=chunk, nseq=nseq, shared_init=shared_init),
        grid=(B // nseq, L // chunk),
        in_specs=[pl.BlockSpec((nseq, chunk, PA_W), lambda b, c: (b, c, 0))]
        + _seq_specs(nseq, shared_init, PA_W, state_shape)
        + [pl.BlockSpec(t.shape, lambda b, c: (0, 0)) for t in consts],
        out_specs=[pl.BlockSpec((nseq, chunk, W_A), lambda b, c: (b, c, 0)),
                   pl.BlockSpec((nseq,) + state_shape, lambda b, c: (b, 0, 0, 0))],
        out_shape=[jax.ShapeDtypeStruct((B, L, W_A), F32),
                   jax.ShapeDtypeStruct((B,) + state_shape, F32)],
        scratch_shapes=[pltpu.VMEM((nseq, SUBLANE, PA_W), F32),
                        pltpu.VMEM((nseq,) + state_shape, F32)],
        compiler_params=pltpu.CompilerParams(dimension_semantics=("arbitrary", "arbitrary"),
                                             vmem_limit_bytes=VMEM_LIMIT),
        name="rwkv_chunk%d" % chunk,
    )(pa, prev8, s0, *consts)


GDN_PREC = dict(cum="xr", tr="xr", kk="b", invm="b", wu="b", qk="b", sread="b", qkv="b", state="b")


def _gdn_kernel(pb_ref, ga_ref, hist_ref, s0_ref, cw_ref, alog_ref, dtb_ref, nw_ref, o_ref, sout_ref,
                xbuf, s_scr, *, chunk, nseq, shared_init):
    c = pl.program_id(1)
    C = chunk
    P = GDN_PREC

    @pl.when(c == 0)
    def _():
        for i in range(nseq):
            j = 0 if shared_init else i
            xbuf[i, 0:SUBLANE, :] = hist_ref[j]
            s_scr[i] = s0_ref[j]

    row = _iota2((C, C), 0)
    col = _iota2((C, C), 1)
    incl = row >= col
    strict = row > col
    tri = incl.astype(F32)
    sel = (_iota2((SUBLANE, LANE), 0) + SMALL_GDN_A == _iota2((SUBLANE, LANE), 1)).astype(F32)

    def prep(i):
        x = pb_ref[i]
        qkv = x[:, 0:3 * W_B]
        full = jnp.concatenate([xbuf[i], qkv], axis=0)
        conv = qkv * cw_ref[CONV_W - 1:CONV_W, :]
        for j in range(1, CONV_W):
            conv = conv + pltpu.roll(full, j, axis=0)[SUBLANE:, :] * cw_ref[CONV_W - 1 - j:CONV_W - j, :]
        xbuf[i] = qkv[C - SUBLANE:, :]
        cs = conv * _sigmoid(conv)
        gates = ga_ref[i]
        glog = -jnp.exp(alog_ref[...]) * _softplus(gates + dtb_ref[...])
        beta = _sigmoid(gates)
        G = _mm(tri, glog, NN, P["cum"])
        Gt = _mm(sel, G, NT, P["tr"])
        return dict(cs=cs, z=x[:, PB_Z:PB_Z + W_B], beta=beta, G=G, Gt=Gt)

    seqs = [prep(i) for i in range(nseq)]

    inst = [(i, h) for i in range(nseq) for h in range(H_B)]
    n = range(len(inst))
    l2n = lambda t, c: t * (lax.rsqrt(jnp.sum(t * t, axis=-1, keepdims=True) + 1e-12) * c)
    head = lambda i, h, part: seqs[i]["cs"][:, part * W_B + h * LANE:part * W_B + (h + 1) * LANE]
    qs = [l2n(head(i, h, 0), DK_B ** -0.5) for i, h in inst]
    ks = [l2n(head(i, h, 1), 1.0) for i, h in inst]
    vs = [head(i, h, 2) for i, h in inst]
    gcols = [seqs[i]["G"][:, SMALL_GDN_A + h:SMALL_GDN_A + h + 1] for i, h in inst]
    grows = [seqs[i]["Gt"][h:h + 1, :] for i, h in inst]
    bcols = [seqs[i]["beta"][:, SMALL_GDN_B + h:SMALL_GDN_B + h + 1] for i, h in inst]
    Ss = [s_scr[i, h] for i, h in inst]

    dmats = [jnp.exp(jnp.where(incl, gcols[j] - grows[j], NEG_BIG)) for j in n]
    kbs = [ks[j] * bcols[j] for j in n]
    lows = [jnp.where(strict, _mm(kbs[j], ks[j], NT, P["kk"]) * dmats[j], 0.0) for j in n]
    ainv = _inv_unit_lower(lows, C, P["invm"])
    egs = [jnp.exp(t) for t in gcols]
    wus = [_mm(ainv[j], jnp.concatenate([kbs[j] * egs[j], vs[j] * bcols[j]], axis=1), NN, P["wu"])
           for j in n]
    qks = [_mm(qs[j], ks[j], NT, P["qk"]) * dmats[j] for j in n]
    srs = [_mm(jnp.concatenate([wus[j][:, :DK_B], qs[j] * egs[j]], axis=0), Ss[j], NN, P["sread"])
           for j in n]
    v_new = [wus[j][:, DK_B:] - srs[j][:C] for j in n]
    glast = [t[C - 1:C, :] for t in gcols]
    s_new = [Ss[j] * jnp.exp(glast[j])
             + _mm(ks[j] * jnp.exp(glast[j] - gcols[j]), v_new[j], TN, P["state"]) for j in n]
    for j, (i, h) in enumerate(inst):
        s_scr[i, h] = s_new[j]
    os_ = [srs[j][C:] + _mm(qks[j], v_new[j], NN, P["qkv"]) for j in n]
    for j, (i, h) in enumerate(inst):
        o = os_[j]
        o = o * lax.rsqrt(jnp.mean(o * o, axis=-1, keepdims=True) + NORM_EPS) * nw_ref[...]
        zh = seqs[i]["z"][:, h * LANE:(h + 1) * LANE]
        o_ref[i, :, h * LANE:(h + 1) * LANE] = o * (zh * _sigmoid(zh))

    @pl.when(c == pl.num_programs(1) - 1)
    def _():
        sout_ref[...] = s_scr[...]


def _gdn(pb, pa, hist8, s0, consts, chunk, nseq):
    B, L, _ = pb.shape
    shared_init = hist8.shape[0] == 1 and B > 1
    state_shape = (H_B, DK_B, DV_B)
    return pl.pallas_call(
        functools.partial(_gdn_kernel, chunk=chunk, nseq=nseq, shared_init=shared_init),
        grid=(B // nseq, L // chunk),
        in_specs=[pl.BlockSpec((nseq, chunk, PB_W), lambda b, c: (b, c, 0)),
                  pl.BlockSpec((nseq, chunk, LANE), lambda b, c: (b, c, PA_SMALL // LANE))]
        + _seq_specs(nseq, shared_init, 3 * W_B, state_shape)
        + [pl.BlockSpec(t.shape, lambda b, c: (0, 0)) for t in consts],
        out_specs=[pl.BlockSpec((nseq, chunk, W_B), lambda b, c: (b, c, 0)),
                   pl.BlockSpec((nseq,) + state_shape, lambda b, c: (b, 0, 0, 0))],
        out_shape=[jax.ShapeDtypeStruct((B, L, W_B), F32),
                   jax.ShapeDtypeStruct((B,) + state_shape, F32)],
        scratch_shapes=[pltpu.VMEM((nseq, SUBLANE, 3 * W_B), F32),
                        pltpu.VMEM((nseq,) + state_shape, F32)],
        compiler_params=pltpu.CompilerParams(dimension_semantics=("arbitrary", "arbitrary"),
                                             vmem_limit_bytes=VMEM_LIMIT),
        name="gdn_chunk%d" % chunk,
    )(pb, pa, hist8, s0, *consts)


def _row_mask(nrow, i):
    return _iota2((nrow, 1), 0) == i


STEP_VECS = ("w", "kk", "kka", "k2", "v", "r", "g", "bon")


def _rwkv_step_prep_kernel(pa_ref, prev_ref, mu_ref, w0_ref, wdu_ref, a0_ref, wau_ref, wgu_ref,
                           kk_ref, ka_ref, rk_ref, vec_ref):
    x = pa_ref[...]
    xm = x + (prev_ref[...] - x) * mu_ref[...]
    r = xm[:, 0:W_A]
    k = xm[:, W_A:2 * W_A]
    v = xm[:, 2 * W_A:3 * W_A]
    sm = xm[:, PA_SMALL:PA_SMALL + LANE]
    gd = xm[:, PA_GATE:PA_GATE + LANE]
    wl = w0_ref[...] + _dot(jnp.tanh(sm).astype(BF16), wdu_ref[...])
    w = jnp.exp(-DECAY_SCALE * _sigmoid(wl))
    a = _sigmoid(a0_ref[...] + _dot(sm.astype(BF16), wau_ref[...]))
    g = _dot(_sigmoid(gd).astype(BF16), wgu_ref[...])
    kkr = k * kk_ref[...]
    k2 = k * (1.0 + (a - 1.0) * ka_ref[...])
    bon = r * k2 * rk_ref[...]
    seg = ((_iota2((LANE, LANE), 0) // N_A) == (_iota2((LANE, LANE), 1) // N_A)).astype(F32)
    ssq = jnp.concatenate([_mm(kkr[:, p * LANE:(p + 1) * LANE] ** 2, seg, NN, RWKV_PREC["seg"])
                           for p in range(H_A // 2)], axis=-1)
    kk = kkr * lax.rsqrt(ssq + 1e-12)
    vecs = dict(w=w, kk=kk, kka=kk * a, k2=k2, v=v, r=r, g=g, bon=bon)
    for j, name in enumerate(STEP_VECS):
        vec_ref[j] = vecs[name].T


def _rwkv_step_state_kernel(vec_ref, s_ref, lnw_ref, lnb_ref, o_ref, sout_ref, o_scr):
    w, kk, kka, k2, v, r, g, bon = [vec_ref[j] for j in range(len(STEP_VECS))]
    for i in range(N_A):
        S = s_ref[0, i]
        sa = jnp.sum(S * kk, axis=0, keepdims=True)
        s_new = S * w - sa * kka + v[i:i + 1, :] * k2
        sout_ref[0, i] = s_new
        o_scr[i:i + 1, :] = jnp.sum(s_new * r, axis=0, keepdims=True)
    o = o_scr[...]
    mean = jnp.mean(o, axis=0, keepdims=True)
    dlt = o - mean
    var = jnp.mean(dlt * dlt, axis=0, keepdims=True)
    on = dlt * lax.rsqrt(var + LNX_EPS) * lnw_ref[...] + lnb_ref[...]
    o_ref[...] = (on + jnp.sum(bon, axis=0, keepdims=True) * v) * g


def _gdn_step_kernel(pb_ref, ga_ref, hist_ref, s_ref, cw_ref, alog_ref, dtb_ref, nw_ref, o_ref, sout_ref):
    ns = pb_ref.shape[0]
    x = pb_ref[...]
    conv = x[:, 0:3 * W_B] * cw_ref[CONV_W - 1:CONV_W, :]
    for j in range(CONV_W - 1):
        conv = conv + hist_ref[j] * cw_ref[j:j + 1, :]
    cs = conv * _sigmoid(conv)
    z = x[:, PB_Z:PB_Z + W_B]
    gates = ga_ref[...]
    eg = jnp.exp(-jnp.exp(alog_ref[...]) * _softplus(gates + dtb_ref[...]))
    beta = _sigmoid(gates)

    l2n = lambda t: t * lax.rsqrt(jnp.sum(t * t, axis=-1, keepdims=True) + 1e-12)
    heads = range(H_B)
    q = [l2n(cs[:, h * LANE:(h + 1) * LANE]) * (DK_B ** -0.5) for h in heads]
    k = [l2n(cs[:, W_B + h * LANE:W_B + (h + 1) * LANE]) for h in heads]
    v = [cs[:, 2 * W_B + h * LANE:2 * W_B + (h + 1) * LANE] for h in heads]
    egh = [eg[:, SMALL_GDN_A + h:SMALL_GDN_A + h + 1] for h in heads]
    bh = [beta[:, SMALL_GDN_B + h:SMALL_GDN_B + h + 1] for h in heads]
    inst = [(i, h) for h in heads for i in range(ns)]
    Ss = [s_ref[i, h] for i, h in inst]
    kss = [_mm(k[h], S, NN, "xr") for (i, h), S in zip(inst, Ss)]
    v_new = [bh[h] * (v[h] - egh[h] * ks) for (i, h), ks in zip(inst, kss)]
    s_new = [S * egh[h][i:i + 1, :] + _mm(jnp.where(_row_mask(ns, i), k[h], 0.0), vn, TN, "b")
             for (i, h), S, vn in zip(inst, Ss, v_new)]
    for (i, h), sn in zip(inst, s_new):
        sout_ref[i, h] = sn
    o_all = [_mm(q[h], sn, NN, "b") for (i, h), sn in zip(inst, s_new)]
    for h in heads:
        o_h = jnp.zeros((ns, DV_B), F32)
        for (i, hh), oa in zip(inst, o_all):
            if hh == h:
                o_h = jnp.where(_row_mask(ns, i), oa, o_h)
        o_h = o_h * lax.rsqrt(jnp.mean(o_h * o_h, axis=-1, keepdims=True) + NORM_EPS) * nw_ref[...]
        zh = z[:, h * LANE:(h + 1) * LANE]
        o_ref[:, h * LANE:(h + 1) * LANE] = o_h * (zh * _sigmoid(zh))


def _step_const_specs(ts):
    return [pl.BlockSpec(t.shape, lambda b: (0,) * t.ndim) for t in ts]


def _rwkv_step(pa, prev, s_t, consts):
    B = s_t.shape[-1]
    prep_consts = consts[:-2]
    lnw_col, lnb_col = [c.reshape(W_A, 1) for c in consts[-2:]]
    vecs = pl.pallas_call(
        _rwkv_step_prep_kernel,
        grid=(1,),
        in_specs=[pl.BlockSpec((B, PA_W), lambda b: (0, 0)),
                  pl.BlockSpec((B, PA_W), lambda b: (0, 0))] + _step_const_specs(prep_consts),
        out_specs=pl.BlockSpec((len(STEP_VECS), W_A, B), lambda b: (0, 0, 0)),
        out_shape=jax.ShapeDtypeStruct((len(STEP_VECS), W_A, B), F32),
        compiler_params=pltpu.CompilerParams(dimension_semantics=("arbitrary",),
                                             vmem_limit_bytes=VMEM_LIMIT),
        name="rwkv_step_prep",
    )(pa, prev, *prep_consts)
    state_spec = pl.BlockSpec((1, N_A, N_A, B), lambda h: (h, 0, 0, 0))
    return pl.pallas_call(
        _rwkv_step_state_kernel,
        grid=(H_A,),
        in_specs=[pl.BlockSpec((len(STEP_VECS), N_A, B), lambda h: (0, h, 0)), state_spec,
                  pl.BlockSpec((N_A, 1), lambda h: (h, 0)), pl.BlockSpec((N_A, 1), lambda h: (h, 0))],
        out_specs=[pl.BlockSpec((N_A, B), lambda h: (h, 0)), state_spec],
        out_shape=[jax.ShapeDtypeStruct((W_A, B), F32), jax.ShapeDtypeStruct(s_t.shape, F32)],
        scratch_shapes=[pltpu.VMEM((N_A, B), F32)],
        compiler_params=pltpu.CompilerParams(dimension_semantics=("arbitrary",),
                                             vmem_limit_bytes=VMEM_LIMIT),
        name="rwkv_step_state",
    )(vecs, s_t, lnw_col, lnb_col)


def _gdn_step(pb, pa, hist, s, consts, ns):
    B = s.shape[0]
    return pl.pallas_call(
        _gdn_step_kernel,
        grid=(B // ns,),
        in_specs=[pl.BlockSpec((ns, PB_W), lambda b: (b, 0)),
                  pl.BlockSpec((ns, LANE), lambda b: (b, PA_SMALL // LANE)),
                  pl.BlockSpec((CONV_W - 1, ns, 3 * W_B), lambda b: (0, b, 0)),
                  pl.BlockSpec((ns, H_B, DK_B, DV_B), lambda b: (b, 0, 0, 0))] + _step_const_specs(consts),
        out_specs=[pl.BlockSpec((ns, W_B), lambda b: (b, 0)),
                   pl.BlockSpec((ns, H_B, DK_B, DV_B), lambda b: (b, 0, 0, 0))],
        out_shape=[jax.ShapeDtypeStruct((B, W_B), F32), jax.ShapeDtypeStruct(s.shape, F32)],
        compiler_params=pltpu.CompilerParams(dimension_semantics=("arbitrary",),
                                             vmem_limit_bytes=VMEM_LIMIT),
        name="gdn_step",
    )(pb, pa, hist, s, *consts)


def _pad_cols(t, width):
    return jnp.pad(t, [(0, 0)] * (t.ndim - 1) + [(0, width - t.shape[-1])])


def _pa_layout(t, gdn_gates=None):
    main = t[..., :3 * W_A]
    small = t[..., 3 * W_A:3 * W_A + D_DECAY + D_AAA]
    gate = t[..., 3 * W_A + D_DECAY + D_AAA:]
    if gdn_gates is not None:
        small = jnp.concatenate([small, gdn_gates], axis=-1)
    return jnp.concatenate([main, _pad_cols(small, LANE), _pad_cols(gate, LANE)], axis=-1)


def _pa_unlayout(t):
    return jnp.concatenate([t[..., :3 * W_A], t[..., PA_SMALL:PA_SMALL + D_DECAY + D_AAA],
                            t[..., PA_GATE:PA_GATE + D_GATE]], axis=-1)


def _small_rows(t, first_row):
    return jnp.pad(t, [(first_row, LANE - first_row - t.shape[0]), (0, 0)])


def _small_lanes(t, first_lane):
    return jnp.pad(t, [(0, 0), (first_lane, LANE - first_lane - t.shape[1])])


def _pair_unblock(s):
    B = s.shape[0]
    return jnp.stack([s[:, :, :N_A, :N_A], s[:, :, N_A:, N_A:]], axis=2).reshape(B, H_A, N_A, N_A)


def _history_rows(rows):
    B, n, w = rows.shape
    return jnp.concatenate([jnp.zeros((B, SUBLANE - n, w), F32), rows], axis=1)


def kernel(x_prompt, x_sample, state_rwkv, state_shift, state_gdn, state_conv, meta_tokens,
           g_ffn1, w_gate1, w_up1, w_down1, g_mix, w_in, mu_shift, w0, w_decay_up, a0, w_a_up,
           w_g_up, k_k, k_a, r_k, lnx_w, lnx_b, conv_w, a_log, dt_bias, gdn_norm_w, w_out,
           g_ffn2, w_gate2, w_up2, w_down2, g_final):
    assert g_ffn1.shape[0] == 1, "single trunk layer"
    bp, sp, _ = x_prompt.shape
    bs = x_sample.shape[0]
    assert x_sample.shape[1] == 1 and sp % CHUNK_RWKV == 0 and sp % CHUNK_GDN == 0
    assert (bp * sp) % TM_DENSE == 0 and (bp * sp) % TM_DENSE_OUT == 0
    assert bp % SEQS_PER_STEP_RWKV == 0 and bp % SEQS_PER_STEP_GDN == 0 and bs % DEC_SEQS_PER_STEP == 0
    row = lambda t: t.reshape(1, -1).astype(F32)

    ffn1 = (w_gate1[0].astype(BF16), w_up1[0].astype(BF16), w_down1[0].astype(BF16))
    ffn2 = (w_gate2[0].astype(BF16), w_up2[0].astype(BF16), w_down2[0].astype(BF16))
    win_a = _pa_layout(w_in[0][:, :N_A_IN], w_in[0][:, N_A_IN + PB_W:]).astype(BF16)
    win_b = w_in[0][:, N_A_IN:N_A_IN + PB_W].astype(BF16)
    wo_a = w_out[0][:W_A].astype(BF16)
    wo_b = w_out[0][W_A:].astype(BF16)
    dense_in_consts = (row(g_ffn1[0]), *ffn1, row(g_mix[0]), win_a, win_b)
    dense_out_consts = (wo_a, wo_b, row(g_ffn2[0]), *ffn2, row(g_final))
    rwkv_consts = (_pa_layout(row(mu_shift[0])), row(w0[0]),
                   _small_rows(w_decay_up[0], 0).astype(BF16), row(a0[0]),
                   _small_rows(w_a_up[0], SMALL_AAA).astype(BF16), _small_rows(w_g_up[0], 0).astype(BF16),
                   row(k_k[0]), row(k_a[0]), row(r_k[0]), row(lnx_w[0]), row(lnx_b[0]))
    gdn_consts = (conv_w[0].astype(F32), _small_lanes(row(a_log[0]), SMALL_GDN_A),
                  _small_lanes(row(dt_bias[0]), SMALL_GDN_A), row(gdn_norm_w[0]))

    xs = jnp.concatenate([x_sample[:, 0, :].astype(F32), meta_tokens.astype(F32)], axis=0)
    hs, pas, pbs = _dense_in(xs, *dense_in_consts, tm=xs.shape[0])

    pa_meta = pas[bs:][None]
    pb_meta = pbs[bs:][None]
    _, rw_meta = _rwkv(pa_meta, jnp.zeros((1, SUBLANE, PA_W), F32),
                       jnp.zeros((1, H_A // 2, LANE, LANE), F32), rwkv_consts, N_META, 1)
    _, gd_meta = _gdn(pb_meta, pa_meta, jnp.zeros((1, SUBLANE, 3 * W_B), F32),
                      jnp.zeros((1, H_B, DK_B, DV_B), F32), gdn_consts, N_META, 1)

    hp, pap, pbp = _dense_in(x_prompt.reshape(bp * sp, D_MODEL).astype(F32), *dense_in_consts,
                             tm=TM_DENSE)
    pap3 = pap.reshape(bp, sp, PA_W)
    pbp3 = pbp.reshape(bp, sp, PB_W)
    oa_p, rw_p = _rwkv(pap3, _history_rows(pa_meta[:, -1:, :]), rw_meta, rwkv_consts,
                       CHUNK_RWKV, SEQS_PER_STEP_RWKV)
    ob_p, gd_p = _gdn(pbp3, pap3, _history_rows(pb_meta[:, -(CONV_W - 1):, :3 * W_B]), gd_meta, gdn_consts,
                      CHUNK_GDN, SEQS_PER_STEP_GDN)
    y_p = _dense_out(hp, oa_p.reshape(bp * sp, W_A), ob_p.reshape(bp * sp, W_B), *dense_out_consts,
                     tm=TM_DENSE_OUT)

    oa_t, rw_t = _rwkv_step(pas, _pa_layout(state_shift[0].astype(F32)),
                            jnp.transpose(state_rwkv[0].astype(F32), (1, 2, 3, 0)), rwkv_consts)
    oa_s = oa_t.T
    rw_s = jnp.transpose(rw_t, (3, 0, 1, 2))
    ob_s, gd_s = _gdn_step(pbs, pas, jnp.swapaxes(state_conv[0].astype(F32), 0, 1), state_gdn[0].astype(F32),
                           gdn_consts, DEC_SEQS_PER_STEP)
    y_s = _dense_out(hs[:bs], oa_s, ob_s, *dense_out_consts, tm=bs)

    new_conv_s = jnp.concatenate([state_conv[0].astype(F32)[:, 1:, :], pbs[:bs, None, :3 * W_B]], axis=1)
    return (y_p.reshape(bp, sp, D_MODEL).astype(x_prompt.dtype),
            y_s.reshape(bs, 1, D_MODEL).astype(x_sample.dtype),
            _pair_unblock(rw_p)[None],
            _pa_unlayout(pap3[:, -1, :])[None],
            gd_p[None],
            pbp3[:, -(CONV_W - 1):, :3 * W_B][None],
            rw_s[None],
            _pa_unlayout(pas[:bs])[None],
            gd_s[None],
            new_conv_s[None])
```

```python
import functools
import math

import jax
import jax.numpy as jnp
from jax import lax
from jax.experimental import pallas as pl
from jax.experimental.pallas import tpu as pltpu

F32 = jnp.float32
BF16 = jnp.bfloat16

LANE = 128
SUBLANE = 8
VMEM_LIMIT = 56 * 1024 * 1024

D_MODEL = 1024
D_FF = 2816
N_META = 16
H_A, N_A = 8, 64
W_A = H_A * N_A
D_DECAY, D_AAA, D_GATE = 32, 32, 96
N_A_IN = 3 * W_A + D_DECAY + D_AAA + D_GATE
H_B, DK_B, DV_B = 4, 128, 128
W_B = H_B * DV_B
CONV_W = 4
LNX_EPS = 64e-5
NORM_EPS = 1e-6
NEG_BIG = -1e30
DECAY_SCALE = math.exp(-0.5)

PA_W = 3 * W_A + 2 * LANE
PA_SMALL = 3 * W_A
PA_GATE = PA_SMALL + LANE
SMALL_AAA = D_DECAY
SMALL_GDN_A = D_DECAY + D_AAA
SMALL_GDN_B = SMALL_GDN_A + H_B
PB_W = 4 * W_B
PB_Z = 3 * W_B

TM_DENSE = 512
TM_DENSE_OUT = 512
CHUNK_RWKV = 64
CHUNK_GDN = 128
SEQS_PER_STEP_RWKV = 8
SEQS_PER_STEP_GDN = 4
DEC_SEQS_PER_STEP = SUBLANE

NN = (((1,), (0,)), ((), ()))
NT = (((1,), (1,)), ((), ()))
TN = (((0,), (0,)), ((), ()))


def _dot(a, b):
    return jnp.dot(a, b, preferred_element_type=F32)


def _sigmoid(x):
    return 1.0 / (1.0 + jnp.exp(-x))


def _softplus(x):
    return jnp.maximum(x, 0.0) + jnp.log(1.0 + jnp.exp(-jnp.abs(x)))


def _rms(x, g):
    return x * lax.rsqrt(jnp.mean(x * x, axis=-1, keepdims=True) + NORM_EPS) * g


def _iota2(shape, dim):
    return lax.broadcasted_iota(jnp.int32, shape, dim)


def _split(x):
    hi = x.astype(BF16)
    return hi, (x - hi.astype(F32)).astype(BF16)


def _dg(a, b, dims):
    return lax.dot_general(a, b, dims, preferred_element_type=F32)


def _mm(a, b, dims=NN, mode="b"):
    if mode == "b":
        return _dg(a.astype(BF16), b.astype(BF16), dims)
    if mode == "xl":
        ah, al = _split(a)
        bh = b.astype(BF16)
        return _dg(ah, bh, dims) + _dg(al, bh, dims)
    if mode == "xr":
        ah = a.astype(BF16)
        bh, bl = _split(b)
        return _dg(ah, bh, dims) + _dg(ah, bl, dims)
    assert mode == "x3"
    ah, al = _split(a)
    bh, bl = _split(b)
    return _dg(ah, bh, dims) + (_dg(ah, bl, dims) + _dg(al, bh, dims))


def _inv_unit_lower(lows, nil, merge_mode, expand=None):
    shape = lows[0].shape
    row = _iota2(shape, 0)
    col = _iota2(shape, 1) % nil
    same_block = lambda s: (row // s) == (col // s)
    eye = (row == col).astype(F32)
    if expand is None:
        expand = lambda t: t
    narrow = (lambda t: t.astype(BF16)) if merge_mode == "b" else (lambda t: t)
    s = 2
    in_base = same_block(s)
    invs = [eye - jnp.where(in_base, low, 0.0) for low in lows]
    while s < nil:
        newly = jnp.logical_and(same_block(2 * s), jnp.logical_not(same_block(s)))
        ts = [_mm(inv, expand(narrow(jnp.where(newly, low, 0.0))), NN, merge_mode)
              for inv, low in zip(invs, lows)]
        invs = [inv - _mm(t, expand(narrow(inv)), NN, merge_mode) for inv, t in zip(invs, ts)]
        s *= 2
    return invs


FF_SPLIT = (0, 1536, D_FF)


def _swiglu(n, wg_ref, wu_ref, wd_ref):
    out = None
    for lo, hi in zip(FF_SPLIT[:-1], FF_SPLIT[1:]):
        gate = _dot(n, wg_ref[:, lo:hi])
        up = _dot(n, wu_ref[:, lo:hi])
        act = (gate * _sigmoid(gate) * up).astype(BF16)
        part = _dot(act, wd_ref[lo:hi, :])
        out = part if out is None else out + part
    return out


def _dense_in_kernel(x_ref, g1_ref, wg_ref, wu_ref, wd_ref, gm_ref, wa_ref, wb_ref,
                     h_ref, pa_ref, pb_ref):
    x = x_ref[...]
    h = x + 0.5 * _swiglu(_rms(x, g1_ref[...]).astype(BF16), wg_ref, wu_ref, wd_ref)
    h_ref[...] = h
    n = _rms(h, gm_ref[...]).astype(BF16)
    pa_ref[...] = _dot(n, wa_ref[...])
    pb_ref[...] = _dot(n, wb_ref[...])


def _dense_out_kernel(h_ref, oa_ref, ob_ref, woa_ref, wob_ref, g2_ref, wg_ref, wu_ref, wd_ref,
                      gf_ref, y_ref):
    h = (h_ref[...] + _dot(oa_ref[...].astype(BF16), woa_ref[...])
         + _dot(ob_ref[...].astype(BF16), wob_ref[...]))
    h = h + 0.5 * _swiglu(_rms(h, g2_ref[...]).astype(BF16), wg_ref, wu_ref, wd_ref)
    y_ref[...] = _rms(h, gf_ref[...])


def _const_spec(shape):
    return pl.BlockSpec(shape, lambda *_: (0,) * len(shape), pipeline_mode=pl.Buffered(1))


def _row_spec(tm, width):
    return pl.BlockSpec((tm, width), lambda i: (i, 0))


def _dense_in(x, g1, wg, wu, wd, gm, wa, wb, tm):
    n = x.shape[0]
    consts = (g1, wg, wu, wd, gm, wa, wb)
    return pl.pallas_call(
        _dense_in_kernel,
        grid=(n // tm,),
        in_specs=[_row_spec(tm, D_MODEL)] + [_const_spec(c.shape) for c in consts],
        out_specs=[_row_spec(tm, D_MODEL), _row_spec(tm, PA_W), _row_spec(tm, PB_W)],
        out_shape=[jax.ShapeDtypeStruct((n, D_MODEL), F32),
                   jax.ShapeDtypeStruct((n, PA_W), F32),
                   jax.ShapeDtypeStruct((n, PB_W), F32)],
        compiler_params=pltpu.CompilerParams(dimension_semantics=("arbitrary",),
                                             vmem_limit_bytes=VMEM_LIMIT),
        name="dense_in",
    )(x, *consts)


def _dense_out(h, oa, ob, woa, wob, g2, wg, wu, wd, gf, tm):
    n = h.shape[0]
    consts = (woa, wob, g2, wg, wu, wd, gf)
    return pl.pallas_call(
        _dense_out_kernel,
        grid=(n // tm,),
        in_specs=[_row_spec(tm, D_MODEL), _row_spec(tm, W_A), _row_spec(tm, W_B)]
        + [_const_spec(c.shape) for c in consts],
        out_specs=_row_spec(tm, D_MODEL),
        out_shape=jax.ShapeDtypeStruct((n, D_MODEL), F32),
        compiler_params=pltpu.CompilerParams(dimension_semantics=("arbitrary",),
                                             vmem_limit_bytes=VMEM_LIMIT),
        name="dense_out",
    )(h, oa, ob, *consts)


RWKV_PREC = dict(cum="b", seg="b", pair="b", invm="b", sread="b", akv="b", solve="b", inter="b",
                 state="x3")


def _rwkv_kernel(pa_ref, prev_ref, s0_ref, mu_ref, w0_ref, wdu_ref, a0_ref, wau_ref, wgu_ref,
                 kk_ref, ka_ref, rk_ref, lnw_ref, lnb_ref, o_ref, sout_ref, xbuf, s_scr,
                 *, chunk, nseq, shared_init):
    c = pl.program_id(1)
    C = chunk
    P = RWKV_PREC

    @pl.when(c == 0)
    def _():
        for i in range(nseq):
            j = 0 if shared_init else i
            xbuf[i, 0:SUBLANE, :] = prev_ref[j]
            s_scr[i] = s0_ref[j]

    tri = (_iota2((C, C), 0) >= _iota2((C, C), 1)).astype(F32)
    mid = C // 2 - 1

    def prep(i):
        x = pa_ref[i]
        full = jnp.concatenate([xbuf[i], x], axis=0)
        prev = pltpu.roll(full, 1, axis=0)[SUBLANE:, :]
        xm = x + (prev - x) * mu_ref[...]
        xbuf[i] = x[C - SUBLANE:, :]

        r = xm[:, 0:W_A]
        k = xm[:, W_A:2 * W_A]
        v = xm[:, 2 * W_A:3 * W_A]
        sm = xm[:, PA_SMALL:PA_SMALL + LANE]
        gd = xm[:, PA_GATE:PA_GATE + LANE]

        wl = w0_ref[...] + _dot(jnp.tanh(sm).astype(BF16), wdu_ref[...])
        lw = -DECAY_SCALE * _sigmoid(wl)
        a = _sigmoid(a0_ref[...] + _dot(sm.astype(BF16), wau_ref[...]))
        g = _dot(_sigmoid(gd).astype(BF16), wgu_ref[...])
        kkr = k * kk_ref[...]
        k2 = k * (1.0 + (a - 1.0) * ka_ref[...])
        G = _mm(tri, lw, NN, P["cum"])
        return dict(r=r, v=v, a=a, g=g, kkr=kkr, k2=k2, G=G, Gx=G - lw, Gm=G[mid:mid + 1, :],
                    Gc=G[C - 1:C, :])

    seqs = [prep(i) for i in range(nseq)]

    C2 = 2 * C
    trow = _iota2((C, C2), 0)
    tcol = _iota2((C, C2), 1) % C
    incl = trow >= tcol
    strict = trow > tcol
    lane = _iota2((1, LANE), 1)
    m0 = (lane < N_A).astype(F32)
    m1 = (lane >= N_A).astype(F32)
    col2 = _iota2((1, C2), 1)
    c0 = (col2 < C).astype(F32)
    c1 = (col2 >= C).astype(F32)
    seg = ((_iota2((LANE, LANE), 0) // N_A) == (_iota2((LANE, LANE), 1) // N_A)).astype(F32)
    stack = lambda t: jnp.concatenate([t * m0.astype(t.dtype), t * m1.astype(t.dtype)], axis=0)
    expand = lambda t: jnp.concatenate([t * c0.astype(t.dtype), t * c1.astype(t.dtype)], axis=0)
    narrow = lambda t, site: t.astype(BF16) if P[site] == "b" else t
    merged = C2 % LANE == 0

    inst = [(i, p) for i in range(nseq) for p in range(H_A // 2)]
    n = range(len(inst))
    sls = [slice(p * LANE, (p + 1) * LANE) for _, p in inst]
    take = lambda name: [seqs[i][name][:, sls[j]] for j, (i, _) in enumerate(inst)]
    rs, vs_, k2s, kkrs, as_, gs = take("r"), take("v"), take("k2"), take("kkr"), take("a"), take("g")
    Gs, Gxs, Gms, Gcs = take("G"), take("Gx"), take("Gm"), take("Gc")
    Ss = [s_scr[i, p] for i, p in inst]

    ssq = [_mm(t * t, seg, NN, P["seg"]) for t in kkrs]
    kks = [t * lax.rsqrt(q + 1e-12) for t, q in zip(kkrs, ssq)]
    kkas = [kks[j] * as_[j] for j in n]
    inv_rel = [jnp.exp(Gms[j] - Gs[j]) for j in n]
    dec = [jnp.exp(Gcs[j] - Gs[j]) for j in n]
    lhs = [jnp.concatenate([kks[j] * jnp.exp(Gxs[j] - Gms[j]), rs[j] * jnp.exp(Gs[j] - Gms[j])], axis=0)
           for j in n]
    ais = [stack(narrow(kkas[j] * inv_rel[j], "pair")) for j in n]
    kis = [stack(narrow(k2s[j] * inv_rel[j], "pair")) for j in n]
    if merged:
        pm = [_mm(lhs[j], jnp.concatenate([ais[j], kis[j]], axis=0), NT, P["pair"]) for j in n]
        pas_, pks_ = [t[:, :C2] for t in pm], [t[:, C2:] for t in pm]
    else:
        pas_ = [_mm(lhs[j], ais[j], NT, P["pair"]) for j in n]
        pks_ = [_mm(lhs[j], kis[j], NT, P["pair"]) for j in n]
    a_aa = [jnp.where(strict, t[:C], 0.0) for t in pas_]
    a_ak = [jnp.where(strict, t[:C], 0.0) for t in pks_]
    a_ra = [jnp.where(incl, t[C:], 0.0) for t in pas_]
    a_rk = [jnp.where(incl, t[C:], 0.0) for t in pks_]
    ainv = _inv_unit_lower(a_aa, C, P["invm"], expand)

    assert P["akv"] == "b" and P["inter"] == "b" and P["state"] == "x3"
    v_hl = [_split(t) for t in vs_]
    vst = [stack(hl[0]) for hl in v_hl]
    akv = [_mm(a_ak[j], vst[j], NN, "b") for j in n]
    sx = [_mm(jnp.concatenate([kks[j] * jnp.exp(Gxs[j]), rs[j] * jnp.exp(Gs[j])], axis=0), Ss[j], NT,
              P["sread"]) for j in n]
    u_hl = [_split(-_mm(ainv[j], stack(narrow(sx[j][:C] + akv[j], "solve")), NN, P["solve"])) for j in n]
    ust = [stack(hl[0]) for hl in u_hl]
    cat2 = lambda a, b, q: jnp.concatenate([a[q], b[q]], axis=0)
    ad_hl = [_split(kkas[j] * dec[j]) for j in n]
    kd_hl = [_split(k2s[j] * dec[j]) for j in n]
    s_new = [Ss[j] * jnp.exp(Gcs[j])
             + seg * (_dg(cat2(u_hl[j], v_hl[j], 0), cat2(ad_hl[j], kd_hl[j], 0), TN)
                      + (_dg(cat2(u_hl[j], v_hl[j], 0), cat2(ad_hl[j], kd_hl[j], 1), TN)
                         + _dg(cat2(u_hl[j], v_hl[j], 1), cat2(ad_hl[j], kd_hl[j], 0), TN)))
             for j in n]
    for j, (i, p) in enumerate(inst):
        s_scr[i, p] = s_new[j]
    if merged:
        inter = [_mm(jnp.concatenate([a_ra[j], a_rk[j]], axis=1),
                     jnp.concatenate([ust[j], vst[j]], axis=0), NN, "b") for j in n]
    else:
        inter = [_mm(a_ra[j], ust[j], NN, "b") + _mm(a_rk[j], vst[j], NN, "b") for j in n]
    Os = [sx[j][C:] + inter[j] for j in n]
    means = [_mm(t, seg, NN, P["seg"]) * (1.0 / N_A) for t in Os]
    dlts = [t - m for t, m in zip(Os, means)]
    vars_ = [_mm(t * t, seg, NN, P["seg"]) * (1.0 / N_A) for t in dlts]
    bonus = [_mm(rs[j] * k2s[j] * rk_ref[:, sls[j]], seg, NN, P["seg"]) * vs_[j] for j in n]
    for j, (i, p) in enumerate(inst):
        on = dlts[j] * lax.rsqrt(vars_[j] + LNX_EPS) * lnw_ref[:, sls[j]] + lnb_ref[:, sls[j]]
        o_ref[i, :, sls[j]] = (on + bonus[j]) * gs[j]

    @pl.when(c == pl.num_programs(1) - 1)
    def _():
        sout_ref[...] = s_scr[...]


def _seq_specs(nseq, shared_init, hist_w, state_shape):
    zeros = (0,) * len(state_shape)
    if shared_init:
        return [pl.BlockSpec((1, SUBLANE, hist_w), lambda b, c: (0, 0, 0)),
                pl.BlockSpec((1,) + state_shape, lambda b, c: (0,) + zeros)]
    return [pl.BlockSpec((nseq, SUBLANE, hist_w), lambda b, c: (b, 0, 0)),
            pl.BlockSpec((nseq,) + state_shape, lambda b, c: (b,) + zeros)]


def _rwkv(pa, prev8, s0, consts, chunk, nseq):
    B, L, _ = pa.shape
    shared_init = prev8.shape[0] == 1 and B > 1
    state_shape = (H_A // 2, LANE, LANE)
    return pl.pallas_call(
        functools.partial(_rwkv_kernel, chunk=chunk, nseq=nseq, shared_init=shared_init),
        grid=(B // nseq, L // chunk),
        in_specs=[pl.BlockSpec((nseq, chunk, PA_W), lambda b, c: (b, c, 0))]
        + _seq_specs(nseq, shared_init, PA_W, state_shape)
        + [pl.BlockSpec(t.shape, lambda b, c: (0, 0)) for t in consts],
        out_specs=[pl.BlockSpec((nseq, chunk, W_A), lambda b, c: (b, c, 0)),
                   pl.BlockSpec((nseq,) + state_shape, lambda b, c: (b, 0, 0, 0))],
        out_shape=[jax.ShapeDtypeStruct((B, L, W_A), F32),
                   jax.ShapeDtypeStruct((B,) + state_shape, F32)],
        scratch_shapes=[pltpu.VMEM((nseq, SUBLANE, PA_W), F32),
                        pltpu.VMEM((nseq,) + state_shape, F32)],
        compiler_params=pltpu.CompilerParams(dimension_semantics=("arbitrary", "arbitrary"),
                                             vmem_limit_bytes=VMEM_LIMIT),
        name="rwkv_chunk%d" % chunk,
    )(pa, prev8, s0, *consts)


GDN_PREC = dict(cum="xr", tr="xr", kk="b", invm="b", wu="b", qk="b", sread="b", qkv="b", state="b")


def _gdn_kernel(pb_ref, ga_ref, hist_ref, s0_ref, cw_ref, alog_ref, dtb_ref, nw_ref, o_ref, sout_ref,
                xbuf, s_scr, *, chunk, nseq, shared_init):
    c = pl.program_id(1)
    C = chunk
    P = GDN_PREC

    @pl.when(c == 0)
    def _():
        for i in range(nseq):
            j = 0 if shared_init else i
            xbuf[i, 0:SUBLANE, :] = hist_ref[j]
            s_scr[i] = s0_ref[j]

    row = _iota2((C, C), 0)
    col = _iota2((C, C), 1)
    incl = row >= col
    strict = row > col
    tri = incl.astype(F32)
    sel = (_iota2((SUBLANE, LANE), 0) + SMALL_GDN_A == _iota2((SUBLANE, LANE), 1)).astype(F32)

    def prep(i):
        x = pb_ref[i]
        qkv = x[:, 0:3 * W_B]
        full = jnp.concatenate([xbuf[i], qkv], axis=0)
        conv = qkv * cw_ref[CONV_W - 1:CONV_W, :]
        for j in range(1, CONV_W):
            conv = conv + pltpu.roll(full, j, axis=0)[SUBLANE:, :] * cw_ref[CONV_W - 1 - j:CONV_W - j, :]
        xbuf[i] = qkv[C - SUBLANE:, :]
        cs = conv * _sigmoid(conv)
        gates = ga_ref[i]
        glog = -jnp.exp(alog_ref[...]) * _softplus(gates + dtb_ref[...])
        beta = _sigmoid(gates)
        G = _mm(tri, glog, NN, P["cum"])
        Gt = _mm(sel, G, NT, P["tr"])
        return dict(cs=cs, z=x[:, PB_Z:PB_Z + W_B], beta=beta, G=G, Gt=Gt)

    seqs = [prep(i) for i in range(nseq)]

    inst = [(i, h) for i in range(nseq) for h in range(H_B)]
    n = range(len(inst))
    l2n = lambda t, c: t * (lax.rsqrt(jnp.sum(t * t, axis=-1, keepdims=True) + 1e-12) * c)
    head = lambda i, h, part: seqs[i]["cs"][:, part * W_B + h * LANE:part * W_B + (h + 1) * LANE]
    qs = [l2n(head(i, h, 0), DK_B ** -0.5) for i, h in inst]
    ks = [l2n(head(i, h, 1), 1.0) for i, h in inst]
    vs = [head(i, h, 2) for i, h in inst]
    gcols = [seqs[i]["G"][:, SMALL_GDN_A + h:SMALL_GDN_A + h + 1] for i, h in inst]
    grows = [seqs[i]["Gt"][h:h + 1, :] for i, h in inst]
    bcols = [seqs[i]["beta"][:, SMALL_GDN_B + h:SMALL_GDN_B + h + 1] for i, h in inst]
    Ss = [s_scr[i, h] for i, h in inst]

    dmats = [jnp.exp(jnp.where(incl, gcols[j] - grows[j], NEG_BIG)) for j in n]
    kbs = [ks[j] * bcols[j] for j in n]
    lows = [jnp.where(strict, _mm(kbs[j], ks[j], NT, P["kk"]) * dmats[j], 0.0) for j in n]
    ainv = _inv_unit_lower(lows, C, P["invm"])
    egs = [jnp.exp(t) for t in gcols]
    wus = [_mm(ainv[j], jnp.concatenate([kbs[j] * egs[j], vs[j] * bcols[j]], axis=1), NN, P["wu"])
           for j in n]
    qks = [_mm(qs[j], ks[j], NT, P["qk"]) * dmats[j] for j in n]
    srs = [_mm(jnp.concatenate([wus[j][:, :DK_B], qs[j] * egs[j]], axis=0), Ss[j], NN, P["sread"])
           for j in n]
    v_new = [wus[j][:, DK_B:] - srs[j][:C] for j in n]
    glast = [t[C - 1:C, :] for t in gcols]
    s_new = [Ss[j] * jnp.exp(glast[j])
             + _mm(ks[j] * jnp.exp(glast[j] - gcols[j]), v_new[j], TN, P["state"]) for j in n]
    for j, (i, h) in enumerate(inst):
        s_scr[i, h] = s_new[j]
    os_ = [srs[j][C:] + _mm(qks[j], v_new[j], NN, P["qkv"]) for j in n]
    for j, (i, h) in enumerate(inst):
        o = os_[j]
        o = o * lax.rsqrt(jnp.mean(o * o, axis=-1, keepdims=True) + NORM_EPS) * nw_ref[...]
        zh = seqs[i]["z"][:, h * LANE:(h + 1) * LANE]
        o_ref[i, :, h * LANE:(h + 1) * LANE] = o * (zh * _sigmoid(zh))

    @pl.when(c == pl.num_programs(1) - 1)
    def _():
        sout_ref[...] = s_scr[...]


def _gdn(pb, pa, hist8, s0, consts, chunk, nseq):
    B, L, _ = pb.shape
    shared_init = hist8.shape[0] == 1 and B > 1
    state_shape = (H_B, DK_B, DV_B)
    return pl.pallas_call(
        functools.partial(_gdn_kernel, chunk=chunk, nseq=nseq, shared_init=shared_init),
        grid=(B // nseq, L // chunk),
        in_specs=[pl.BlockSpec((nseq, chunk, PB_W), lambda b, c: (b, c, 0)),
                  pl.BlockSpec((nseq, chunk, LANE), lambda b, c: (b, c, PA_SMALL // LANE))]
        + _seq_specs(nseq, shared_init, 3 * W_B, state_shape)
        + [pl.BlockSpec(t.shape, lambda b, c: (0, 0)) for t in consts],
        out_specs=[pl.BlockSpec((nseq, chunk, W_B), lambda b, c: (b, c, 0)),
                   pl.BlockSpec((nseq,) + state_shape, lambda b, c: (b, 0, 0, 0))],
        out_shape=[jax.ShapeDtypeStruct((B, L, W_B), F32),
                   jax.ShapeDtypeStruct((B,) + state_shape, F32)],
        scratch_shapes=[pltpu.VMEM((nseq, SUBLANE, 3 * W_B), F32),
                        pltpu.VMEM((nseq,) + state_shape, F32)],
        compiler_params=pltpu.CompilerParams(dimension_semantics=("arbitrary", "arbitrary"),
                                             vmem_limit_bytes=VMEM_LIMIT),
        name="gdn_chunk%d" % chunk,
    )(pb, pa, hist8, s0, *consts)


def _row_mask(nrow, i):
    return _iota2((nrow, 1), 0) == i


STEP_VECS = ("w", "kk", "kka", "k2", "v", "r", "g", "bon")


def _rwkv_step_prep_kernel(pa_ref, prev_ref, mu_ref, w0_ref, wdu_ref, a0_ref, wau_ref, wgu_ref,
                           kk_ref, ka_ref, rk_ref, vec_ref):
    x = pa_ref[...]
    xm = x + (prev_ref[...] - x) * mu_ref[...]
    r = xm[:, 0:W_A]
    k = xm[:, W_A:2 * W_A]
    v = xm[:, 2 * W_A:3 * W_A]
    sm = xm[:, PA_SMALL:PA_SMALL + LANE]
    gd = xm[:, PA_GATE:PA_GATE + LANE]
    wl = w0_ref[...] + _dot(jnp.tanh(sm).astype(BF16), wdu_ref[...])
    w = jnp.exp(-DECAY_SCALE * _sigmoid(wl))
    a = _sigmoid(a0_ref[...] + _dot(sm.astype(BF16), wau_ref[...]))
    g = _dot(_sigmoid(gd).astype(BF16), wgu_ref[...])
    kkr = k * kk_ref[...]
    k2 = k * (1.0 + (a - 1.0) * ka_ref[...])
    bon = r * k2 * rk_ref[...]
    seg = ((_iota2((LANE, LANE), 0) // N_A) == (_iota2((LANE, LANE), 1) // N_A)).astype(F32)
    ssq = jnp.concatenate([_mm(kkr[:, p * LANE:(p + 1) * LANE] ** 2, seg, NN, RWKV_PREC["seg"])
                           for p in range(H_A // 2)], axis=-1)
    kk = kkr * lax.rsqrt(ssq + 1e-12)
    vecs = dict(w=w, kk=kk, kka=kk * a, k2=k2, v=v, r=r, g=g, bon=bon)
    for j, name in enumerate(STEP_VECS):
        vec_ref[j] = vecs[name].T


def _rwkv_step_state_kernel(vec_ref, s_ref, lnw_ref, lnb_ref, o_ref, sout_ref, o_scr):
    w, kk, kka, k2, v, r, g, bon = [vec_ref[j] for j in range(len(STEP_VECS))]
    for i in range(N_A):
        S = s_ref[0, i]
        sa = jnp.sum(S * kk, axis=0, keepdims=True)
        s_new = S * w - sa * kka + v[i:i + 1, :] * k2
        sout_ref[0, i] = s_new
        o_scr[i:i + 1, :] = jnp.sum(s_new * r, axis=0, keepdims=True)
    o = o_scr[...]
    mean = jnp.mean(o, axis=0, keepdims=True)
    dlt = o - mean
    var = jnp.mean(dlt * dlt, axis=0, keepdims=True)
    on = dlt * lax.rsqrt(var + LNX_EPS) * lnw_ref[...] + lnb_ref[...]
    o_ref[...] = (on + jnp.sum(bon, axis=0, keepdims=True) * v) * g


STATE_RING = 3


def _gdn_step_kernel(pb_ref, ga_ref, hist_ref, s_hbm, cw_ref, alog_ref, dtb_ref, nw_ref, o_ref, sout_ref,
                     sbuf, sems):
    ns = pb_ref.shape[0]
    step = pl.program_id(0)
    nsteps = pl.num_programs(0)
    ahead = STATE_RING - 1

    def fetch(j):
        slot = j % STATE_RING
        return pltpu.make_async_copy(s_hbm.at[pl.ds(j * ns, ns)], sbuf.at[slot], sems.at[slot])

    @pl.when(step == 0)
    def _():
        for j in range(ahead):
            fetch(j).start()

    @pl.when(step + ahead < nsteps)
    def _():
        fetch(step + ahead).start()

    fetch(step).wait()
    s_ref = sbuf.at[step % STATE_RING]
    x = pb_ref[...]
    conv = x[:, 0:3 * W_B] * cw_ref[CONV_W - 1:CONV_W, :]
    for j in range(CONV_W - 1):
        conv = conv + hist_ref[j] * cw_ref[j:j + 1, :]
    cs = conv * _sigmoid(conv)
    z = x[:, PB_Z:PB_Z + W_B]
    gates = ga_ref[...]
    eg = jnp.exp(-jnp.exp(alog_ref[...]) * _softplus(gates + dtb_ref[...]))
    beta = _sigmoid(gates)

    l2n = lambda t: t * lax.rsqrt(jnp.sum(t * t, axis=-1, keepdims=True) + 1e-12)
    heads = range(H_B)
    q = [l2n(cs[:, h * LANE:(h + 1) * LANE]) * (DK_B ** -0.5) for h in heads]
    k = [l2n(cs[:, W_B + h * LANE:W_B + (h + 1) * LANE]) for h in heads]
    v = [cs[:, 2 * W_B + h * LANE:2 * W_B + (h + 1) * LANE] for h in heads]
    egh = [eg[:, SMALL_GDN_A + h:SMALL_GDN_A + h + 1] for h in heads]
    bh = [beta[:, SMALL_GDN_B + h:SMALL_GDN_B + h + 1] for h in heads]
    inst = [(i, h) for h in heads for i in range(ns)]
    Ss = [s_ref[i, h] for i, h in inst]
    kss = [_mm(k[h], S, NN, "xr") for (i, h), S in zip(inst, Ss)]
    v_new = [bh[h] * (v[h] - egh[h] * ks) for (i, h), ks in zip(inst, kss)]
    s_new = [S * egh[h][i:i + 1, :] + _mm(jnp.where(_row_mask(ns, i), k[h], 0.0), vn, TN, "b")
             for (i, h), S, vn in zip(inst, Ss, v_new)]
    for (i, h), sn in zip(inst, s_new):
        sout_ref[i, h] = sn
    o_all = [_mm(q[h], sn, NN, "b") for (i, h), sn in zip(inst, s_new)]
    for h in heads:
        o_h = jnp.zeros((ns, DV_B), F32)
        for (i, hh), oa in zip(inst, o_all):
            if hh == h:
                o_h = jnp.where(_row_mask(ns, i), oa, o_h)
        o_h = o_h * lax.rsqrt(jnp.mean(o_h * o_h, axis=-1, keepdims=True) + NORM_EPS) * nw_ref[...]
        zh = z[:, h * LANE:(h + 1) * LANE]
        o_ref[:, h * LANE:(h + 1) * LANE] = o_h * (zh * _sigmoid(zh))


def _step_const_specs(ts):
    return [pl.BlockSpec(t.shape, lambda b: (0,) * t.ndim) for t in ts]


def _rwkv_step(pa, prev, s_t, consts):
    B = s_t.shape[-1]
    prep_consts = consts[:-2]
    lnw_col, lnb_col = [c.reshape(W_A, 1) for c in consts[-2:]]
    vecs = pl.pallas_call(
        _rwkv_step_prep_kernel,
        grid=(1,),
        in_specs=[pl.BlockSpec((B, PA_W), lambda b: (0, 0)),
                  pl.BlockSpec((B, PA_W), lambda b: (0, 0))] + _step_const_specs(prep_consts),
        out_specs=pl.BlockSpec((len(STEP_VECS), W_A, B), lambda b: (0, 0, 0)),
        out_shape=jax.ShapeDtypeStruct((len(STEP_VECS), W_A, B), F32),
        compiler_params=pltpu.CompilerParams(dimension_semantics=("arbitrary",),
                                             vmem_limit_bytes=VMEM_LIMIT),
        name="rwkv_step_prep",
    )(pa, prev, *prep_consts)
    state_spec = pl.BlockSpec((1, N_A, N_A, B), lambda h: (h, 0, 0, 0))
    return pl.pallas_call(
        _rwkv_step_state_kernel,
        grid=(H_A,),
        in_specs=[pl.BlockSpec((len(STEP_VECS), N_A, B), lambda h: (0, h, 0)), state_spec,
                  pl.BlockSpec((N_A, 1), lambda h: (h, 0)), pl.BlockSpec((N_A, 1), lambda h: (h, 0))],
        out_specs=[pl.BlockSpec((N_A, B), lambda h: (h, 0)), state_spec],
        out_shape=[jax.ShapeDtypeStruct((W_A, B), F32), jax.ShapeDtypeStruct(s_t.shape, F32)],
        scratch_shapes=[pltpu.VMEM((N_A, B), F32)],
        compiler_params=pltpu.CompilerParams(dimension_semantics=("arbitrary",),
                                             vmem_limit_bytes=VMEM_LIMIT),
        name="rwkv_step_state",
    )(vecs, s_t, lnw_col, lnb_col)


def _gdn_step(pb, pa, hist, s, consts, ns):
    B = s.shape[0]
    assert B // ns >= STATE_RING
    return pl.pallas_call(
        _gdn_step_kernel,
        grid=(B // ns,),
        in_specs=[pl.BlockSpec((ns, PB_W), lambda b: (b, 0)),
                  pl.BlockSpec((ns, LANE), lambda b: (b, PA_SMALL // LANE)),
                  pl.BlockSpec((CONV_W - 1, ns, 3 * W_B), lambda b: (0, b, 0)),
                  pl.BlockSpec(memory_space=pl.ANY)] + _step_const_specs(consts),
        out_specs=[pl.BlockSpec((ns, W_B), lambda b: (b, 0)),
                   pl.BlockSpec((ns, H_B, DK_B, DV_B), lambda b: (b, 0, 0, 0))],
        out_shape=[jax.ShapeDtypeStruct((B, W_B), F32), jax.ShapeDtypeStruct(s.shape, F32)],
        scratch_shapes=[pltpu.VMEM((STATE_RING, ns, H_B, DK_B, DV_B), F32),
                        pltpu.SemaphoreType.DMA((STATE_RING,))],
        compiler_params=pltpu.CompilerParams(dimension_semantics=("arbitrary",),
                                             vmem_limit_bytes=VMEM_LIMIT),
        name="gdn_step",
    )(pb, pa, hist, s, *consts)


def _pad_cols(t, width):
    return jnp.pad(t, [(0, 0)] * (t.ndim - 1) + [(0, width - t.shape[-1])])


def _pa_layout(t, gdn_gates=None):
    main = t[..., :3 * W_A]
    small = t[..., 3 * W_A:3 * W_A + D_DECAY + D_AAA]
    gate = t[..., 3 * W_A + D_DECAY + D_AAA:]
    if gdn_gates is not None:
        small = jnp.concatenate([small, gdn_gates], axis=-1)
    return jnp.concatenate([main, _pad_cols(small, LANE), _pad_cols(gate, LANE)], axis=-1)


def _pa_unlayout(t):
    return jnp.concatenate([t[..., :3 * W_A], t[..., PA_SMALL:PA_SMALL + D_DECAY + D_AAA],
                            t[..., PA_GATE:PA_GATE + D_GATE]], axis=-1)


def _small_rows(t, first_row):
    return jnp.pad(t, [(first_row, LANE - first_row - t.shape[0]), (0, 0)])


def _small_lanes(t, first_lane):
    return jnp.pad(t, [(0, 0), (first_lane, LANE - first_lane - t.shape[1])])


def _pair_unblock(s):
    B = s.shape[0]
    return jnp.stack([s[:, :, :N_A, :N_A], s[:, :, N_A:, N_A:]], axis=2).reshape(B, H_A, N_A, N_A)


def _history_rows(rows):
    B, n, w = rows.shape
    return jnp.concatenate([jnp.zeros((B, SUBLANE - n, w), F32), rows], axis=1)


def kernel(x_prompt, x_sample, state_rwkv, state_shift, state_gdn, state_conv, meta_tokens,
           g_ffn1, w_gate1, w_up1, w_down1, g_mix, w_in, mu_shift, w0, w_decay_up, a0, w_a_up,
           w_g_up, k_k, k_a, r_k, lnx_w, lnx_b, conv_w, a_log, dt_bias, gdn_norm_w, w_out,
           g_ffn2, w_gate2, w_up2, w_down2, g_final):
    assert g_ffn1.shape[0] == 1, "single trunk layer"
    bp, sp, _ = x_prompt.shape
    bs = x_sample.shape[0]
    assert x_sample.shape[1] == 1 and sp % CHUNK_RWKV == 0 and sp % CHUNK_GDN == 0
    assert (bp * sp) % TM_DENSE == 0 and (bp * sp) % TM_DENSE_OUT == 0
    assert bp % SEQS_PER_STEP_RWKV == 0 and bp % SEQS_PER_STEP_GDN == 0 and bs % DEC_SEQS_PER_STEP == 0
    row = lambda t: t.reshape(1, -1).astype(F32)

    ffn1 = (w_gate1[0].astype(BF16), w_up1[0].astype(BF16), w_down1[0].astype(BF16))
    ffn2 = (w_gate2[0].astype(BF16), w_up2[0].astype(BF16), w_down2[0].astype(BF16))
    win_a = _pa_layout(w_in[0][:, :N_A_IN], w_in[0][:, N_A_IN + PB_W:]).astype(BF16)
    win_b = w_in[0][:, N_A_IN:N_A_IN + PB_W].astype(BF16)
    wo_a = w_out[0][:W_A].astype(BF16)
    wo_b = w_out[0][W_A:].astype(BF16)
    dense_in_consts = (row(g_ffn1[0]), *ffn1, row(g_mix[0]), win_a, win_b)
    dense_out_consts = (wo_a, wo_b, row(g_ffn2[0]), *ffn2, row(g_final))
    rwkv_consts = (_pa_layout(row(mu_shift[0])), row(w0[0]),
                   _small_rows(w_decay_up[0], 0).astype(BF16), row(a0[0]),
                   _small_rows(w_a_up[0], SMALL_AAA).astype(BF16), _small_rows(w_g_up[0], 0).astype(BF16),
                   row(k_k[0]), row(k_a[0]), row(r_k[0]), row(lnx_w[0]), row(lnx_b[0]))
    gdn_consts = (conv_w[0].astype(F32), _small_lanes(row(a_log[0]), SMALL_GDN_A),
                  _small_lanes(row(dt_bias[0]), SMALL_GDN_A), row(gdn_norm_w[0]))

    xs = jnp.concatenate([x_sample[:, 0, :].astype(F32), meta_tokens.astype(F32)], axis=0)
    hs, pas, pbs = _dense_in(xs, *dense_in_consts, tm=xs.shape[0])

    pa_meta = pas[bs:][None]
    pb_meta = pbs[bs:][None]
    _, rw_meta = _rwkv(pa_meta, jnp.zeros((1, SUBLANE, PA_W), F32),
                       jnp.zeros((1, H_A // 2, LANE, LANE), F32), rwkv_consts, N_META, 1)
    _, gd_meta = _gdn(pb_meta, pa_meta, jnp.zeros((1, SUBLANE, 3 * W_B), F32),
                      jnp.zeros((1, H_B, DK_B, DV_B), F32), gdn_consts, N_META, 1)

    hp, pap, pbp = _dense_in(x_prompt.reshape(bp * sp, D_MODEL).astype(F32), *dense_in_consts,
                             tm=TM_DENSE)
    pap3 = pap.reshape(bp, sp, PA_W)
    pbp3 = pbp.reshape(bp, sp, PB_W)
    oa_p, rw_p = _rwkv(pap3, _history_rows(pa_meta[:, -1:, :]), rw_meta, rwkv_consts,
                       CHUNK_RWKV, SEQS_PER_STEP_RWKV)
    ob_p, gd_p = _gdn(pbp3, pap3, _history_rows(pb_meta[:, -(CONV_W - 1):, :3 * W_B]), gd_meta, gdn_consts,
                      CHUNK_GDN, SEQS_PER_STEP_GDN)
    y_p = _dense_out(hp, oa_p.reshape(bp * sp, W_A), ob_p.reshape(bp * sp, W_B), *dense_out_consts,
                     tm=TM_DENSE_OUT)

    oa_t, rw_t = _rwkv_step(pas, _pa_layout(state_shift[0].astype(F32)),
                            jnp.transpose(state_rwkv[0].astype(F32), (1, 2, 3, 0)), rwkv_consts)
    oa_s = oa_t.T
    rw_s = jnp.transpose(rw_t, (3, 0, 1, 2))
    ob_s, gd_s = _gdn_step(pbs, pas, jnp.swapaxes(state_conv[0].astype(F32), 0, 1), state_gdn[0].astype(F32),
                           gdn_consts, DEC_SEQS_PER_STEP)
    y_s = _dense_out(hs[:bs], oa_s, ob_s, *dense_out_consts, tm=bs)

    new_conv_s = jnp.concatenate([state_conv[0].astype(F32)[:, 1:, :], pbs[:bs, None, :3 * W_B]], axis=1)
    return (y_p.reshape(bp, sp, D_MODEL).astype(x_prompt.dtype),
            y_s.reshape(bs, 1, D_MODEL).astype(x_sample.dtype),
            _pair_unblock(rw_p)[None],
            _pa_unlayout(pap3[:, -1, :])[None],
            gd_p[None],
            pbp3[:, -(CONV_W - 1):, :3 * W_B][None],
            rw_s[None],
            _pa_unlayout(pas[:bs])[None],
            gd_s[None],
            new_conv_s[None])
```
